```python
import jax, jax.numpy as jnp
from jax import lax
import numpy as np

D_MODEL = 1024
BATCH = 2
SEQ = 8192
DEPTH = 2
DEC_BATCH = 128
DEC_SEQ = 4
PAST_LEN = 16384
PAGE_SIZE = 128

N_HEADS = 8
HEAD_DIM = 64
N_KV_HEADS = 2
GROUP = N_HEADS // N_KV_HEADS
ATTN_WIDTH = N_HEADS * HEAD_DIM
KV_WIDTH = N_KV_HEADS * HEAD_DIM
WINDOW = 128
ROPE_THETA = 500000.0
ROT_DIM = HEAD_DIM // 4
LRU_WIDTH = D_MODEL // 2
LRU_HEADS = 8
LRU_BLOCK = LRU_WIDTH // LRU_HEADS
CONV_W = 4
LRU_C = 8.0
D_FF = 2816
MIX_WIDTH = ATTN_WIDTH + LRU_WIDTH
IN_WIDTH = ATTN_WIDTH + 2 * KV_WIDTH + 2 * LRU_WIDTH
EPS = 1e-6

kernel_name = 'hymba_swa_sink_rglru_macaron_step'


def rmsnorm(x, g):
    xf = x.astype(jnp.float32)
    y = xf * lax.rsqrt(jnp.mean(xf * xf, axis=-1, keepdims=True) + EPS)
    return (y * g.astype(jnp.float32)).astype(x.dtype)


def rope(x, pos):
    half = ROT_DIM // 2
    inv = ROPE_THETA ** (-(jnp.arange(half, dtype=jnp.float32) * 2.0) / ROT_DIM)
    ang = pos.astype(jnp.float32)[:, None] * inv[None, :]
    cos = jnp.cos(ang)[:, None, :]
    sin = jnp.sin(ang)[:, None, :]
    xr = x[..., :ROT_DIM].astype(jnp.float32)
    x1, x2 = xr[..., :half], xr[..., half:]
    rot = jnp.concatenate([x1 * cos - x2 * sin, x2 * cos + x1 * sin], axis=-1).astype(x.dtype)
    return jnp.concatenate([rot, x[..., ROT_DIM:]], axis=-1)


def sink_attention(q, k, v, mask, sinks):
    lead = q.shape[:-3]
    tq = q.shape[-3]
    qg = q.reshape(*lead, tq, N_KV_HEADS, GROUP, HEAD_DIM)
    s = jnp.einsum('...qkgd,...skd->...kgqs', qg, k).astype(jnp.float32) * (HEAD_DIM ** -0.5)
    s = jnp.where(mask, s, -jnp.inf)
    sink = sinks.astype(jnp.float32).reshape(N_KV_HEADS, GROUP)[:, :, None, None]
    m = jnp.maximum(jnp.max(s, axis=-1, keepdims=True), sink)
    p = jnp.exp(s - m)
    p = p / (jnp.sum(p, axis=-1, keepdims=True) + jnp.exp(sink - m))
    o = jnp.einsum('...kgqs,...skd->...qkgd', p.astype(v.dtype), v)
    return o.reshape(*lead, tq, ATTN_WIDTH)


def attn_prompt(q, k, v, sinks):
    b, s = q.shape[:2]
    nb = s // WINDOW
    qb = q.reshape(b, nb, WINDOW, N_HEADS, HEAD_DIM)
    kb = k.reshape(b, nb, WINDOW, N_KV_HEADS, HEAD_DIM)
    vb = v.reshape(b, nb, WINDOW, N_KV_HEADS, HEAD_DIM)
    pad = ((0, 0), (1, 0), (0, 0), (0, 0), (0, 0))
    kcat = jnp.concatenate([jnp.pad(kb, pad)[:, :-1], kb], axis=2)
    vcat = jnp.concatenate([jnp.pad(vb, pad)[:, :-1], vb], axis=2)
    qi = jnp.arange(WINDOW)
    kj = jnp.arange(2 * WINDOW)
    diff = qi[:, None] + WINDOW - kj[None, :]
    band = (diff >= 0) & (diff < WINDOW)
    has_prev = (jnp.arange(nb)[:, None, None] > 0) | (kj[None, None, :] >= WINDOW)
    mask = (band[None] & has_prev)[:, None, None]
    out = sink_attention(qb, kcat, vcat, mask, sinks)
    wb = min(WINDOW, s)
    return out.reshape(b, s, ATTN_WIDTH), k[:, s - wb:], v[:, s - wb:]


def attn_sample(q, k, v, sinks, k_buf, v_buf):
    tb = k_buf.shape[1]
    t = q.shape[1]
    kcat = jnp.concatenate([k_buf, k], axis=1)
    vcat = jnp.concatenate([v_buf, v], axis=1)
    q_pos = PAST_LEN + jnp.arange(t)
    k_pos = PAST_LEN - tb + jnp.arange(tb + t)
    diff = q_pos[:, None] - k_pos[None, :]
    mask = (diff >= 0) & (diff < WINDOW)
    out = sink_attention(q, kcat, vcat, mask, sinks)
    return out, kcat[:, t:], vcat[:, t:]


def causal_conv(xp, w, bias):
    t = xp.shape[1] - (CONV_W - 1)
    out = bias + xp[:, 0:t] * w[0]
    for j in range(1, CONV_W):
        out = out + xp[:, j:j + t] * w[j]
    return out


def rglru(xc, h0, wa, ba, wx, bx, lam):
    b, t, _ = xc.shape
    xb = xc.reshape(b, t, LRU_HEADS, LRU_BLOCK)
    r = jax.nn.sigmoid(jnp.einsum('bthi,hij->bthj', xb, wa).reshape(b, t, LRU_WIDTH) + ba)
    ig = jax.nn.sigmoid(jnp.einsum('bthi,hij->bthj', xb, wx).reshape(b, t, LRU_WIDTH) + bx)
    log_a = -LRU_C * r.astype(jnp.float32) * jax.nn.softplus(-lam.astype(jnp.float32))
    a = jnp.exp(log_a)
    u = jnp.sqrt(-jnp.expm1(2.0 * log_a)) * (ig * xc).astype(jnp.float32)

    def step(h, au):
        a_t, u_t = au
        h = a_t * h + u_t
        return h, h

    h_last, hs = lax.scan(step, h0.astype(jnp.float32), (jnp.swapaxes(a, 0, 1), jnp.swapaxes(u, 0, 1)))
    return jnp.swapaxes(hs, 0, 1).astype(xc.dtype), h_last.astype(h0.dtype)


def swiglu(x, wg, wu, wd):
    return (jax.nn.silu(x @ wg) * (x @ wu)) @ wd


def half_ffn(x, g_pre, wg, wu, wd, g_post):
    return x + 0.5 * rmsnorm(swiglu(rmsnorm(x, g_pre), wg, wu, wd), g_post)


def token_mixer(x, p, l, pos, attend, conv_prefix, h0):
    b, t = x.shape[:2]
    h = rmsnorm(x, p['norm_pre_mix'][l])
    z = h @ p['w_in'][l]
    o1 = ATTN_WIDTH
    o2 = o1 + KV_WIDTH
    o3 = o2 + KV_WIDTH
    o4 = o3 + LRU_WIDTH
    q = rope(z[..., :o1].reshape(b, t, N_HEADS, HEAD_DIM), pos)
    k = rope(z[..., o1:o2].reshape(b, t, N_KV_HEADS, HEAD_DIM), pos)
    v = z[..., o2:o3].reshape(b, t, N_KV_HEADS, HEAD_DIM)
    xr = z[..., o3:o4]
    yg = z[..., o4:]
    attn, new_k, new_v = attend(q, k, v, p['sinks'][l])
    xp = jnp.concatenate([conv_prefix, xr], axis=1)
    xc = causal_conv(xp, p['conv_w'][l], p['conv_b'][l])
    new_conv = xp[:, xp.shape[1] - (CONV_W - 1):]
    lru, h_last = rglru(xc, h0, p['gate_a_w'][l], p['gate_a_b'][l], p['gate_x_w'][l], p['gate_x_b'][l], p['lru_lambda'][l])
    lru = lru * jax.nn.gelu(yg)
    mixed = jnp.concatenate([rmsnorm(attn, p['attn_out_norm'][l]), rmsnorm(lru, p['lru_out_norm'][l])], axis=-1)
    x = x + rmsnorm(mixed @ p['w_o'][l], p['norm_post_mix'][l])
    return x, new_k, new_v, new_conv, h_last


def layer(x, p, l, pos, attend, conv_prefix, h0):
    x = half_ffn(x, p['norm_pre_ffn1'][l], p['ffn1_w_gate'][l], p['ffn1_w_up'][l], p['ffn1_w_down'][l], p['norm_post_ffn1'][l])
    x, nk, nv, nc, nh = token_mixer(x, p, l, pos, attend, conv_prefix, h0)
    x = half_ffn(x, p['norm_pre_ffn2'][l], p['ffn2_w_gate'][l], p['ffn2_w_up'][l], p['ffn2_w_down'][l], p['norm_post_ffn2'][l])
    return x, nk, nv, nc, nh


def setup_inputs(seed: int = 0) -> dict:
    key = jax.random.key(seed)
    ks = iter(jax.random.split(key, 40))
    f32 = jnp.float32
    wb = min(WINDOW, PAST_LEN)

    def nrm(shape, scale):
        return jax.random.normal(next(ks), shape, f32) * scale

    def gain(shape):
        return 1.0 + 0.02 * jax.random.normal(next(ks), shape, f32)

    u = jax.random.uniform(next(ks), (DEPTH, LRU_WIDTH), f32, 0.9, 0.999)
    lam = jnp.log(u) - jnp.log1p(-u)
    return {
        'x_prompt': nrm((BATCH, SEQ, D_MODEL), 1.0),
        'x_sample': nrm((DEC_BATCH, DEC_SEQ, D_MODEL), 1.0),
        'cache_k': nrm((DEPTH, DEC_BATCH, wb, N_KV_HEADS, HEAD_DIM), 1.0),
        'cache_v': nrm((DEPTH, DEC_BATCH, wb, N_KV_HEADS, HEAD_DIM), 1.0),
        'state_conv': nrm((DEPTH, DEC_BATCH, CONV_W - 1, LRU_WIDTH), 1.0),
        'state_h': nrm((DEPTH, DEC_BATCH, LRU_WIDTH), 0.5),
        'norm_pre_ffn1': gain((DEPTH, D_MODEL)),
        'ffn1_w_gate': nrm((DEPTH, D_MODEL, D_FF), D_MODEL ** -0.5),
        'ffn1_w_up': nrm((DEPTH, D_MODEL, D_FF), D_MODEL ** -0.5),
        'ffn1_w_down': nrm((DEPTH, D_FF, D_MODEL), D_FF ** -0.5),
        'norm_post_ffn1': gain((DEPTH, D_MODEL)),
        'norm_pre_mix': gain((DEPTH, D_MODEL)),
        'w_in': nrm((DEPTH, D_MODEL, IN_WIDTH), D_MODEL ** -0.5),
        'sinks': nrm((DEPTH, N_HEADS), 0.5),
        'conv_w': nrm((DEPTH, CONV_W, LRU_WIDTH), CONV_W ** -0.5),
        'conv_b': nrm((DEPTH, LRU_WIDTH), 0.01),
        'gate_a_w': nrm((DEPTH, LRU_HEADS, LRU_BLOCK, LRU_BLOCK), LRU_BLOCK ** -0.5),
        'gate_a_b': nrm((DEPTH, LRU_WIDTH), 0.01),
        'gate_x_w': nrm((DEPTH, LRU_HEADS, LRU_BLOCK, LRU_BLOCK), LRU_BLOCK ** -0.5),
        'gate_x_b': nrm((DEPTH, LRU_WIDTH), 0.01),
        'lru_lambda': lam,
        'attn_out_norm': gain((DEPTH, ATTN_WIDTH)),
        'lru_out_norm': gain((DEPTH, LRU_WIDTH)),
        'w_o': nrm((DEPTH, MIX_WIDTH, D_MODEL), MIX_WIDTH ** -0.5),
        'norm_post_mix': gain((DEPTH, D_MODEL)),
        'norm_pre_ffn2': gain((DEPTH, D_MODEL)),
        'ffn2_w_gate': nrm((DEPTH, D_MODEL, D_FF), D_MODEL ** -0.5),
        'ffn2_w_up': nrm((DEPTH, D_MODEL, D_FF), D_MODEL ** -0.5),
        'ffn2_w_down': nrm((DEPTH, D_FF, D_MODEL), D_FF ** -0.5),
        'norm_post_ffn2': gain((DEPTH, D_MODEL)),
    }


def reference(x_prompt, x_sample, cache_k, cache_v, state_conv, state_h,
              norm_pre_ffn1, ffn1_w_gate, ffn1_w_up, ffn1_w_down, norm_post_ffn1,
              norm_pre_mix, w_in, sinks, conv_w, conv_b, gate_a_w, gate_a_b, gate_x_w, gate_x_b,
              lru_lambda, attn_out_norm, lru_out_norm, w_o, norm_post_mix,
              norm_pre_ffn2, ffn2_w_gate, ffn2_w_up, ffn2_w_down, norm_post_ffn2):
    p = dict(norm_pre_ffn1=norm_pre_ffn1, ffn1_w_gate=ffn1_w_gate, ffn1_w_up=ffn1_w_up,
             ffn1_w_down=ffn1_w_down, norm_post_ffn1=norm_post_ffn1, norm_pre_mix=norm_pre_mix,
             w_in=w_in, sinks=sinks, conv_w=conv_w, conv_b=conv_b, gate_a_w=gate_a_w,
             gate_a_b=gate_a_b, gate_x_w=gate_x_w, gate_x_b=gate_x_b, lru_lambda=lru_lambda,
             attn_out_norm=attn_out_norm, lru_out_norm=lru_out_norm, w_o=w_o,
             norm_post_mix=norm_post_mix, norm_pre_ffn2=norm_pre_ffn2, ffn2_w_gate=ffn2_w_gate,
             ffn2_w_up=ffn2_w_up, ffn2_w_down=ffn2_w_down, norm_post_ffn2=norm_post_ffn2)
    b, s = x_prompt.shape[:2]
    db, t = x_sample.shape[:2]
    pos_p = jnp.arange(s)
    pos_s = PAST_LEN + jnp.arange(t)
    xp, xs = x_prompt, x_sample
    kp, vp, cp, hp = [], [], [], []
    ksm, vsm, csm, hsm = [], [], [], []
    for l in range(DEPTH):
        conv0 = jnp.zeros((b, CONV_W - 1, LRU_WIDTH), x_prompt.dtype)
        h0 = jnp.zeros((b, LRU_WIDTH), state_h.dtype)
        xp, nk, nv, nc, nh = layer(xp, p, l, pos_p, attn_prompt, conv0, h0)
        kp.append(nk); vp.append(nv); cp.append(nc); hp.append(nh)
        kb, vb = cache_k[l], cache_v[l]
        attend_s = lambda q, k, v, snk, kb=kb, vb=vb: attn_sample(q, k, v, snk, kb, vb)
        xs, nk, nv, nc, nh = layer(xs, p, l, pos_s, attend_s, state_conv[l], state_h[l])
        ksm.append(nk); vsm.append(nv); csm.append(nc); hsm.append(nh)
    return (xp, xs,
            jnp.stack(kp), jnp.stack(vp), jnp.stack(cp), jnp.stack(hp),
            jnp.stack(ksm), jnp.stack(vsm), jnp.stack(csm), jnp.stack(hsm))
```

```python
import functools

import jax
import jax.numpy as jnp
import numpy as np
from jax import lax
from jax.experimental import pallas as pl
from jax.experimental.pallas import tpu as pltpu

D_MODEL = 1024
PAST_LEN = 16384
N_HEADS = 8
HEAD_DIM = 64
N_KV_HEADS = 2
GROUP = N_HEADS // N_KV_HEADS
ATTN_WIDTH = N_HEADS * HEAD_DIM
KV_WIDTH = N_KV_HEADS * HEAD_DIM
WINDOW = 128
ROPE_THETA = 500000.0
ROT_DIM = HEAD_DIM // 4
LRU_WIDTH = D_MODEL // 2
LRU_HEADS = 8
LRU_BLOCK = LRU_WIDTH // LRU_HEADS
CONV_W = 4
LRU_C = 8.0
D_FF = 2816
IN_WIDTH = ATTN_WIDTH + 2 * KV_WIDTH + 2 * LRU_WIDTH
EPS = 1e-6

LANES = 128
SUBLANES = 8
FF_CHUNK = 256
N_FF_CHUNKS = D_FF // FF_CHUNK
TOKEN_TILE = 256
SAMPLE_GROUP = 8
VMEM_LIMIT_BYTES = 52 * 1024 * 1024

O_K = ATTN_WIDTH
O_V = O_K + KV_WIDTH
O_XR = O_V + KV_WIDTH
O_YG = O_XR + LRU_WIDTH

F32 = jnp.float32
BF16 = jnp.bfloat16


def _rmsnorm(x, g):
    return (x * lax.rsqrt(jnp.mean(x * x, axis=-1, keepdims=True) + EPS)) * g


def _half_ffn(x, g_pre, wgu_ref, wd_ref, g_post, act_scr):
    h = _rmsnorm(x, g_pre).astype(BF16)
    for c in range(N_FF_CHUNKS):
        gu = jnp.dot(h, wgu_ref[:, 2 * FF_CHUNK * c:2 * FF_CHUNK * (c + 1)],
                     preferred_element_type=F32)
        g = gu[:, :FF_CHUNK]
        u = gu[:, FF_CHUNK:]
        act_scr[:, FF_CHUNK * c:FF_CHUNK * (c + 1)] = ((g * jax.nn.sigmoid(g)) * u).astype(BF16)
    y = jnp.dot(act_scr[...], wd_ref[...], preferred_element_type=F32)
    return x + 0.5 * _rmsnorm(y, g_post)


def _gelu_tanh(x):
    c = np.float32(np.sqrt(2.0 / np.pi))
    return x * (0.5 * (1.0 + jnp.tanh(c * (x + 0.044715 * (x * x * x)))))


def _softplus(x):
    return jnp.maximum(x, 0.0) + jnp.log1p(jnp.exp(-jnp.abs(x)))


def _lru_coeffs(xc, wgate_ref, ba, bx, lam):
    gates = jnp.dot(xc.astype(BF16), wgate_ref[...], preferred_element_type=F32)
    r = jax.nn.sigmoid(gates[:, :LRU_WIDTH] + ba)
    ig = jax.nn.sigmoid(gates[:, LRU_WIDTH:] + bx)
    log_a = (-LRU_C * r) * _softplus(-lam)
    a = jnp.exp(log_a)
    u = jnp.sqrt(1.0 - a * a) * (ig * xc)
    return a, u


def _scan_rows(a, u):
    n = a.shape[0]
    row = lax.broadcasted_iota(jnp.int32, a.shape, 0)
    d = 1
    while d < n:
        keep = row >= d
        u = jnp.where(keep, a * pltpu.roll(u, d, 0) + u, u)
        if 2 * d < n:
            a = jnp.where(keep, a * pltpu.roll(a, d, 0), a)
        d *= 2
    return u


def _expand_q(q, rows):
    lane = lax.broadcasted_iota(jnp.int32, (rows, LANES), 1)
    pieces = []
    for h in range(N_HEADS):
        kv = h // GROUP
        src = q[:, LANES * (h // 2):LANES * (h // 2 + 1)]
        if (h % 2) != kv:
            src = pltpu.roll(src, HEAD_DIM, 1)
        in_half = (lane >= HEAD_DIM) if kv == 1 else (lane < HEAD_DIM)
        pieces.append(jnp.where(in_half, src, 0.0))
    return jnp.concatenate(pieces, axis=0).astype(BF16)


def _collect_heads(o, rows):
    lane = lax.broadcasted_iota(jnp.int32, (rows, LANES), 1)
    groups = []
    for j in range(N_HEADS // 2):
        kv = (2 * j) // GROUP
        even = o[rows * (2 * j):rows * (2 * j + 1)]
        odd = o[rows * (2 * j + 1):rows * (2 * j + 2)]
        if kv == 1:
            even = pltpu.roll(even, HEAD_DIM, 1)
        else:
            odd = pltpu.roll(odd, HEAD_DIM, 1)
        groups.append(jnp.where(lane < HEAD_DIM, even, odd))
    return jnp.concatenate(groups, axis=1)


def _sink_column(sinks_ref, rows):
    return jnp.concatenate([jnp.full((rows, 1), sinks_ref[h], F32) for h in range(N_HEADS)], axis=0)


def _mix_tail(x1, attn, hs, yg, g_attn, g_lru, wo_ref, g_postmix,
              g_pre2, wgu_ref, wd_ref, g_post2, act_scr):
    lru = hs * _gelu_tanh(yg)
    mixed = jnp.concatenate([_rmsnorm(attn, g_attn), _rmsnorm(lru, g_lru)], axis=-1).astype(BF16)
    y = jnp.dot(mixed, wo_ref[...], preferred_element_type=F32)
    x2 = x1 + _rmsnorm(y, g_postmix)
    return _half_ffn(x2, g_pre2, wgu_ref, wd_ref, g_post2, act_scr)


def _ffn_in_kernel(x_ref, cos_ref, sin_up_ref, sin_dn_ref, g_pre_ref, wgu_ref, wd_ref, g_post_ref,
                   g_mix_ref, win_ref, x1_ref, z_ref, act_scr):
    x1 = _half_ffn(x_ref[...], g_pre_ref[...], wgu_ref, wd_ref, g_post_ref[...], act_scr)
    x1_ref[...] = x1
    h = _rmsnorm(x1, g_mix_ref[...]).astype(BF16)
    z = jnp.dot(h, win_ref[...], preferred_element_type=F32)
    cos, sin_up, sin_dn = cos_ref[...], sin_up_ref[...], sin_dn_ref[...]
    half = ROT_DIM // 2
    for j in range(O_V // LANES):
        zj = z[:, LANES * j:LANES * (j + 1)]
        rot = zj * cos + pltpu.roll(zj, LANES - half, 1) * sin_up + pltpu.roll(zj, half, 1) * sin_dn
        if j < ATTN_WIDTH // LANES:
            rot = rot * (HEAD_DIM ** -0.5)
        z_ref[:, LANES * j:LANES * (j + 1)] = rot
    z_ref[:, O_V:] = z[:, O_V:]


def _ffn_in(x, tabs, tab_tiles, w, tile):
    n = x.shape[0]
    full = lambda shape: pl.BlockSpec(shape, lambda i: (0,) * len(shape))
    tab_spec = pl.BlockSpec((tile, LANES), lambda i: (i % tab_tiles, 0))
    return pl.pallas_call(
        _ffn_in_kernel,
        grid=(n // tile,),
        in_specs=[
            pl.BlockSpec((tile, D_MODEL), lambda i: (i, 0)),
            tab_spec, tab_spec, tab_spec,
            full((1, D_MODEL)), full((D_MODEL, 2 * D_FF)), full((D_FF, D_MODEL)), full((1, D_MODEL)),
            full((1, D_MODEL)), full((D_MODEL, IN_WIDTH)),
        ],
        out_specs=[
            pl.BlockSpec((tile, D_MODEL), lambda i: (i, 0)),
            pl.BlockSpec((tile, IN_WIDTH), lambda i: (i, 0)),
        ],
        out_shape=[
            jax.ShapeDtypeStruct((n, D_MODEL), F32),
            jax.ShapeDtypeStruct((n, IN_WIDTH), F32),
        ],
        scratch_shapes=[pltpu.VMEM((tile, D_FF), BF16)],
        compiler_params=pltpu.CompilerParams(
            dimension_semantics=("arbitrary",), vmem_limit_bytes=VMEM_LIMIT_BYTES),
        name="ffn_in",
    )(x, *tabs, w["g_pre1"], w["wgu1"], w["wd1"], w["g_post1"], w["g_mix"], w["w_in"])


def _mix_prompt_kernel(sinks_ref, x1_ref, z_ref, conv_w_ref, conv_b_ref, wgate_ref, ba_ref, bx_ref,
                       lam_ref, g_attn_ref, g_lru_ref, wo_ref, g_postmix_ref, g_pre2_ref, wgu_ref,
                       wd_ref, g_post2_ref,
                       x3_ref, nk_ref, nv_ref, nconv_ref, nh_ref,
                       k_scr, v_scr, conv_scr, h_scr, act_scr):
    tile = x1_ref.shape[0]
    s_idx = pl.program_id(1)

    @pl.when(s_idx == 0)
    def _():
        k_scr[0:WINDOW, :] = jnp.zeros((WINDOW, KV_WIDTH), BF16)
        v_scr[0:WINDOW, :] = jnp.zeros((WINDOW, KV_WIDTH), BF16)
        conv_scr[0:SUBLANES, :] = jnp.zeros((SUBLANES, LRU_WIDTH), F32)
        h_scr[...] = jnp.zeros((1, LRU_WIDTH), F32)

    k_tile = z_ref[:, O_K:O_V]
    v_tile = z_ref[:, O_V:O_XR]
    k_scr[WINDOW:, :] = k_tile.astype(BF16)
    v_scr[WINDOW:, :] = v_tile.astype(BF16)
    nk_ref[0] = k_tile[tile - WINDOW:, :]
    nv_ref[0] = v_tile[tile - WINDOW:, :]

    sink = _sink_column(sinks_ref, WINDOW)
    qi = lax.broadcasted_iota(jnp.int32, (N_HEADS * WINDOW, 2 * WINDOW), 0) & (WINDOW - 1)
    kj = lax.broadcasted_iota(jnp.int32, (N_HEADS * WINDOW, 2 * WINDOW), 1)
    band = (kj > qi) & (kj <= qi + WINDOW)
    attn_blocks = []
    for i in range(tile // WINDOW):
        q_exp = _expand_q(z_ref[WINDOW * i:WINDOW * (i + 1), 0:ATTN_WIDTH], WINDOW)
        keys = k_scr[WINDOW * i:WINDOW * (i + 2), :]
        vals = v_scr[WINDOW * i:WINDOW * (i + 2), :]
        s = lax.dot_general(q_exp, keys, (((1,), (1,)), ((), ())), preferred_element_type=F32)
        if i == 0:
            first_key = jnp.where(s_idx > 0, 0, WINDOW)
            mask = band & (kj >= first_key)
        else:
            mask = band
        s = jnp.where(mask, s, -jnp.inf)
        m = jnp.maximum(jnp.max(s, axis=-1, keepdims=True), sink)
        p = jnp.exp(s - m)
        denom = jnp.sum(p, axis=-1, keepdims=True) + jnp.exp(sink - m)
        o = jnp.dot(p.astype(BF16), vals, preferred_element_type=F32) / denom
        attn_blocks.append(_collect_heads(o, WINDOW))
    attn = jnp.concatenate(attn_blocks, axis=0)
    k_scr[0:WINDOW, :] = k_scr[tile:tile + WINDOW, :]
    v_scr[0:WINDOW, :] = v_scr[tile:tile + WINDOW, :]

    xr = z_ref[:, O_XR:O_YG]
    conv_scr[SUBLANES:, :] = xr
    xc = conv_b_ref[...] + conv_scr[SUBLANES - 3:SUBLANES - 3 + tile, :] * conv_w_ref[0:1, :]
    for j in range(1, CONV_W):
        xc = xc + conv_scr[SUBLANES - 3 + j:SUBLANES - 3 + j + tile, :] * conv_w_ref[j:j + 1, :]
    conv_scr[0:SUBLANES, :] = conv_scr[tile:tile + SUBLANES, :]
    nconv_ref[0] = xr[tile - (CONV_W - 1):, :]

    a, u = _lru_coeffs(xc, wgate_ref, ba_ref[...], bx_ref[...], lam_ref[...])
    row = lax.broadcasted_iota(jnp.int32, u.shape, 0)
    u = jnp.where(row == 0, a * h_scr[...] + u, u)
    hs = _scan_rows(a, u)
    h_last = hs[tile - 1:tile, :]
    h_scr[...] = h_last
    nh_ref[0] = h_last

    x3_ref[...] = _mix_tail(x1_ref[...], attn, hs, z_ref[:, O_YG:], g_attn_ref[...], g_lru_ref[...],
                            wo_ref, g_postmix_ref[...], g_pre2_ref[...], wgu_ref, wd_ref,
                            g_post2_ref[...], act_scr)


def _mix_prompt(x1, z, w, batch, seq, tile):
    n_s = seq // tile
    full = lambda shape: pl.BlockSpec(shape, lambda b, s: (0,) * len(shape))
    tok = lambda width: pl.BlockSpec((tile, width), lambda b, s: (b * n_s + s, 0))
    per_batch = lambda rows, width: pl.BlockSpec((1, rows, width), lambda b, s: (b, 0, 0))
    return pl.pallas_call(
        _mix_prompt_kernel,
        grid=(batch, n_s),
        in_specs=[
            pl.BlockSpec(memory_space=pltpu.SMEM),
            tok(D_MODEL), tok(IN_WIDTH),
            full((CONV_W, LRU_WIDTH)), full((1, LRU_WIDTH)), full((LRU_WIDTH, 2 * LRU_WIDTH)),
            full((1, LRU_WIDTH)), full((1, LRU_WIDTH)), full((1, LRU_WIDTH)),
            full((1, ATTN_WIDTH)), full((1, LRU_WIDTH)), full((D_MODEL, D_MODEL)), full((1, D_MODEL)),
            full((1, D_MODEL)), full((D_MODEL, 2 * D_FF)), full((D_FF, D_MODEL)), full((1, D_MODEL)),
        ],
        out_specs=[
            tok(D_MODEL),
            per_batch(WINDOW, KV_WIDTH), per_batch(WINDOW, KV_WIDTH),
            per_batch(CONV_W - 1, LRU_WIDTH), per_batch(1, LRU_WIDTH),
        ],
        out_shape=[
            jax.ShapeDtypeStruct((batch * seq, D_MODEL), F32),
            jax.ShapeDtypeStruct((batch, WINDOW, KV_WIDTH), F32),
            jax.ShapeDtypeStruct((batch, WINDOW, KV_WIDTH), F32),
            jax.ShapeDtypeStruct((batch, CONV_W - 1, LRU_WIDTH), F32),
            jax.ShapeDtypeStruct((batch, 1, LRU_WIDTH), F32),
        ],
        scratch_shapes=[
            pltpu.VMEM((WINDOW + tile, KV_WIDTH), BF16),
            pltpu.VMEM((WINDOW + tile, KV_WIDTH), BF16),
            pltpu.VMEM((SUBLANES + tile, LRU_WIDTH), F32),
            pltpu.VMEM((1, LRU_WIDTH), F32),
            pltpu.VMEM((tile, D_FF), BF16),
        ],
        compiler_params=pltpu.CompilerParams(
            dimension_semantics=("arbitrary", "arbitrary"), vmem_limit_bytes=VMEM_LIMIT_BYTES),
        name="mix_prompt",
    )(w["sinks"], x1, z, w["conv_w"], w["conv_b"], w["w_gate"], w["gate_a_b"], w["gate_x_b"],
      w["lam"], w["g_attn"], w["g_lru"], w["w_o"], w["g_postmix"], w["g_pre2"], w["wgu2"],
      w["wd2"], w["g_post2"])


def _attn_sample_kernel(sinks_ref, zq_ref, kvnew_ref, ck_ref, cv_ref, attn_ref, nk_ref, nv_ref):
    dec_seq, g = zq_ref.shape[0], zq_ref.shape[1]
    rows = dec_seq * g
    cache_len = ck_ref.shape[1]
    zq = zq_ref[...].reshape(rows, zq_ref.shape[2])
    q_exp = _expand_q(zq[:, 0:ATTN_WIDTH], rows)
    k_new = zq[:, O_K:O_V].astype(BF16)
    v_new = zq[:, O_V:O_XR].astype(BF16)
    k_cache = ck_ref[...].reshape(g * cache_len, KV_WIDTH).astype(BF16)
    v_cache = cv_ref[...].reshape(g * cache_len, KV_WIDTH).astype(BF16)
    contract_last = (((1,), (1,)), ((), ()))
    s_c = lax.dot_general(q_exp, k_cache, contract_last, preferred_element_type=F32)
    s_n = lax.dot_general(q_exp, k_new, contract_last, preferred_element_type=F32)

    def row_ids(shape):
        r = lax.broadcasted_iota(jnp.int32, shape, 0)
        return (r % rows) // g, r % g

    t_q, b_q = row_ids(s_c.shape)
    c = lax.broadcasted_iota(jnp.int32, s_c.shape, 1)
    mask_c = ((c // cache_len) == b_q) & ((c % cache_len) > t_q)
    t_q, b_q = row_ids(s_n.shape)
    c = lax.broadcasted_iota(jnp.int32, s_n.shape, 1)
    mask_n = ((c % g) == b_q) & ((c // g) <= t_q)
    s_c = jnp.where(mask_c, s_c, -jnp.inf)
    s_n = jnp.where(mask_n, s_n, -jnp.inf)

    sink = _sink_column(sinks_ref, rows)
    m = jnp.maximum(jnp.maximum(jnp.max(s_c, axis=-1, keepdims=True),
                                jnp.max(s_n, axis=-1, keepdims=True)), sink)
    p_c = jnp.exp(s_c - m)
    p_n = jnp.exp(s_n - m)
    denom = (jnp.sum(p_c, axis=-1, keepdims=True) + jnp.sum(p_n, axis=-1, keepdims=True)
             + jnp.exp(sink - m))
    o = (jnp.dot(p_c.astype(BF16), v_cache, preferred_element_type=F32)
         + jnp.dot(p_n.astype(BF16), v_new, preferred_element_type=F32)) / denom
    attn_ref[...] = _collect_heads(o, rows).reshape(dec_seq, g, ATTN_WIDTH)

    nk_ref[:, 0:cache_len - dec_seq, :] = ck_ref[:, dec_seq:, :]
    nv_ref[:, 0:cache_len - dec_seq, :] = cv_ref[:, dec_seq:, :]
    nk_ref[:, cache_len - dec_seq:, :] = kvnew_ref[:, :, 0:KV_WIDTH]
    nv_ref[:, cache_len - dec_seq:, :] = kvnew_ref[:, :, KV_WIDTH:]


def _attn_sample(z3, kvnew, cache_k, cache_v, sinks):
    dec_seq, dec_batch, _ = z3.shape
    cache_len = cache_k.shape[1]
    g = SAMPLE_GROUP
    cache_spec = pl.BlockSpec((g, cache_len, KV_WIDTH), lambda i: (i, 0, 0))
    return pl.pallas_call(
        _attn_sample_kernel,
        grid=(dec_batch // g,),
        in_specs=[
            pl.BlockSpec(memory_space=pltpu.SMEM),
            pl.BlockSpec((dec_seq, g, O_XR), lambda i: (0, i, 0)),
            pl.BlockSpec((g, dec_seq, 2 * KV_WIDTH), lambda i: (i, 0, 0)),
            cache_spec, cache_spec,
        ],
        out_specs=[
            pl.BlockSpec((dec_seq, g, ATTN_WIDTH), lambda i: (0, i, 0)),
            cache_spec, cache_spec,
        ],
        out_shape=[
            jax.ShapeDtypeStruct((dec_seq, dec_batch, ATTN_WIDTH), F32),
            jax.ShapeDtypeStruct(cache_k.shape, F32),
            jax.ShapeDtypeStruct(cache_v.shape, F32),
        ],
        compiler_params=pltpu.CompilerParams(
            dimension_semantics=("arbitrary",), vmem_limit_bytes=VMEM_LIMIT_BYTES),
        name="attn_sample",
    )(sinks, z3, kvnew, cache_k, cache_v)


def _mix_sample_kernel(x1_ref, z_ref, attn_ref, sconv_ref, sh_ref, conv_w_ref, conv_b_ref, wgate_ref,
                       ba_ref, bx_ref, lam_ref, g_attn_ref, g_lru_ref, wo_ref, g_postmix_ref,
                       g_pre2_ref, wgu_ref, wd_ref, g_post2_ref,
                       x3_ref, nconv_ref, nh_ref, act_scr):
    nb = sh_ref.shape[0]
    dec_seq = x1_ref.shape[0] // nb
    xr = z_ref[:, O_XR:O_YG]
    xp = [sconv_ref[j] for j in range(CONV_W - 1)] + [xr[nb * t:nb * (t + 1)] for t in range(dec_seq)]
    xc_steps = []
    for t in range(dec_seq):
        acc = conv_b_ref[...] + xp[t] * conv_w_ref[0:1, :]
        for j in range(1, CONV_W):
            acc = acc + xp[t + j] * conv_w_ref[j:j + 1, :]
        xc_steps.append(acc)
    for j in range(CONV_W - 1):
        nconv_ref[j] = xp[dec_seq + j]
    a, u = _lru_coeffs(jnp.concatenate(xc_steps, axis=0), wgate_ref, ba_ref[...], bx_ref[...],
                       lam_ref[...])
    h = sh_ref[...]
    hs_steps = []
    for t in range(dec_seq):
        h = a[nb * t:nb * (t + 1)] * h + u[nb * t:nb * (t + 1)]
        hs_steps.append(h)
    nh_ref[...] = h
    hs = jnp.concatenate(hs_steps, axis=0)
    x3_ref[...] = _mix_tail(x1_ref[...], attn_ref[...], hs, z_ref[:, O_YG:], g_attn_ref[...],
                            g_lru_ref[...], wo_ref, g_postmix_ref[...], g_pre2_ref[...], wgu_ref,
                            wd_ref, g_post2_ref[...], act_scr)


def _mix_sample(x1, z, attn, sconv, sh, w):
    n = x1.shape[0]
    nb = sh.shape[0]
    full = lambda shape: pl.BlockSpec(shape, lambda i: (0,) * len(shape))
    return pl.pallas_call(
        _mix_sample_kernel,
        grid=(1,),
        in_specs=[
            full((n, D_MODEL)), full((n, IN_WIDTH)), full((n, ATTN_WIDTH)),
            full((CONV_W - 1, nb, LRU_WIDTH)), full((nb, LRU_WIDTH)),
            full((CONV_W, LRU_WIDTH)), full((1, LRU_WIDTH)), full((LRU_WIDTH, 2 * LRU_WIDTH)),
            full((1, LRU_WIDTH)), full((1, LRU_WIDTH)), full((1, LRU_WIDTH)),
            full((1, ATTN_WIDTH)), full((1, LRU_WIDTH)), full((D_MODEL, D_MODEL)), full((1, D_MODEL)),
            full((1, D_MODEL)), full((D_MODEL, 2 * D_FF)), full((D_FF, D_MODEL)), full((1, D_MODEL)),
        ],
        out_specs=[full((n, D_MODEL)), full((CONV_W - 1, nb, LRU_WIDTH)), full((nb, LRU_WIDTH))],
        out_shape=[
            jax.ShapeDtypeStruct((n, D_MODEL), F32),
            jax.ShapeDtypeStruct((CONV_W - 1, nb, LRU_WIDTH), F32),
            jax.ShapeDtypeStruct((nb, LRU_WIDTH), F32),
        ],
        scratch_shapes=[pltpu.VMEM((n, D_FF), BF16)],
        compiler_params=pltpu.CompilerParams(
            dimension_semantics=("arbitrary",), vmem_limit_bytes=VMEM_LIMIT_BYTES),
        name="mix_sample",
    )(x1, z, attn, sconv, sh, w["conv_w"], w["conv_b"], w["w_gate"], w["gate_a_b"], w["gate_x_b"],
      w["lam"], w["g_attn"], w["g_lru"], w["w_o"], w["g_postmix"], w["g_pre2"], w["wgu2"],
      w["wd2"], w["g_post2"])


def _rope_tables(pos):
    half = ROT_DIM // 2
    inv = ROPE_THETA ** (-(jnp.arange(half, dtype=F32) * 2.0) / ROT_DIM)
    ang = pos.astype(F32)[:, None] * inv[None, :]
    cos, sin = jnp.cos(ang), jnp.sin(ang)
    n = pos.shape[0]
    rest = jnp.zeros((n, HEAD_DIM - ROT_DIM), F32)
    zero = jnp.zeros((n, half), F32)
    per_head = lambda parts: jnp.tile(jnp.concatenate(parts, axis=-1), (1, LANES // HEAD_DIM))
    return (per_head([cos, cos, rest + 1.0]), per_head([-sin, zero, rest]), per_head([zero, sin, rest]))


def _interleave_gate_up(wg, wu):
    d = wg.shape[0]
    both = jnp.stack([wg.reshape(d, N_FF_CHUNKS, FF_CHUNK), wu.reshape(d, N_FF_CHUNKS, FF_CHUNK)], axis=2)
    return both.reshape(d, 2 * D_FF).astype(BF16)


def _block_diag(w):
    eye = jnp.eye(LRU_HEADS, dtype=w.dtype)
    return jnp.einsum("hij,hg->higj", w, eye).reshape(LRU_WIDTH, LRU_WIDTH)


def kernel(x_prompt, x_sample, cache_k, cache_v, state_conv, state_h, norm_pre_ffn1, ffn1_w_gate, ffn1_w_up, ffn1_w_down, norm_post_ffn1, norm_pre_mix, w_in, sinks, conv_w, conv_b, gate_a_w, gate_a_b, gate_x_w, gate_x_b, lru_lambda, attn_out_norm, lru_out_norm, w_o, norm_post_mix, norm_pre_ffn2, ffn2_w_gate, ffn2_w_up, ffn2_w_down, norm_post_ffn2):
    batch, seq, _ = x_prompt.shape
    dec_batch, dec_seq, _ = x_sample.shape
    depth = w_in.shape[0]
    cache_len = cache_k.shape[2]
    tile = TOKEN_TILE
    assert seq % tile == 0 and tile % WINDOW == 0 and dec_batch % SAMPLE_GROUP == 0

    tabs_p = _rope_tables(jnp.arange(seq))
    tabs_s = _rope_tables(jnp.repeat(PAST_LEN + jnp.arange(dec_seq), dec_batch))
    xp = x_prompt.reshape(batch * seq, D_MODEL)
    xs = jnp.swapaxes(x_sample, 0, 1).reshape(dec_seq * dec_batch, D_MODEL)
    n_s = dec_seq * dec_batch
    row = lambda v: v.reshape(1, -1)

    outs = [[] for _ in range(8)]
    for l in range(depth):
        w = dict(
            g_pre1=row(norm_pre_ffn1[l]), wgu1=_interleave_gate_up(ffn1_w_gate[l], ffn1_w_up[l]),
            wd1=ffn1_w_down[l].astype(BF16), g_post1=row(norm_post_ffn1[l]),
            g_mix=row(norm_pre_mix[l]), w_in=w_in[l].astype(BF16), sinks=sinks[l],
            conv_w=conv_w[l], conv_b=row(conv_b[l]),
            w_gate=jnp.concatenate([_block_diag(gate_a_w[l]), _block_diag(gate_x_w[l])], axis=1).astype(BF16),
            gate_a_b=row(gate_a_b[l]), gate_x_b=row(gate_x_b[l]), lam=row(lru_lambda[l]),
            g_attn=row(attn_out_norm[l]), g_lru=row(lru_out_norm[l]), w_o=w_o[l].astype(BF16),
            g_postmix=row(norm_post_mix[l]), g_pre2=row(norm_pre_ffn2[l]),
            wgu2=_interleave_gate_up(ffn2_w_gate[l], ffn2_w_up[l]), wd2=ffn2_w_down[l].astype(BF16),
            g_post2=row(norm_post_ffn2[l]),
        )
        x1p, zp = _ffn_in(xp, tabs_p, seq // tile, w, tile)
        xp, nk, nv, nc, nh = _mix_prompt(x1p, zp, w, batch, seq, tile)

        x1s, zs = _ffn_in(xs, tabs_s, n_s // tile, w, tile)
        z3 = zs.reshape(dec_seq, dec_batch, IN_WIDTH)
        kvnew = jnp.swapaxes(z3[:, :, O_K:O_XR], 0, 1)
        attn_s, nks, nvs = _attn_sample(z3, kvnew, cache_k[l].reshape(dec_batch, cache_len, KV_WIDTH),
                                        cache_v[l].reshape(dec_batch, cache_len, KV_WIDTH), w["sinks"])
        xs, ncs, nhs = _mix_sample(x1s, zs, attn_s.reshape(n_s, ATTN_WIDTH),
                                   jnp.swapaxes(state_conv[l], 0, 1), state_h[l], w)
        kv_shape = (-1, WINDOW, N_KV_HEADS, HEAD_DIM)
        for acc, val in zip(outs, (nk.reshape(kv_shape), nv.reshape(kv_shape), nc, nh.reshape(batch, LRU_WIDTH),
                                   nks.reshape(dec_batch, cache_len, N_KV_HEADS, HEAD_DIM),
                                   nvs.reshape(dec_batch, cache_len, N_KV_HEADS, HEAD_DIM),
                                   jnp.swapaxes(ncs, 0, 1), nhs)):
            acc.append(val)

    y_prompt = xp.reshape(batch, seq, D_MODEL)
    y_sample = jnp.swapaxes(xs.reshape(dec_seq, dec_batch, D_MODEL), 0, 1)
    return (y_prompt, y_sample) + tuple(jnp.stack(o) for o in outs)
```

```python
import functools

import jax
import jax.numpy as jnp
import numpy as np
from jax import lax
from jax.experimental import pallas as pl
from jax.experimental.pallas import tpu as pltpu

D_MODEL = 1024
PAST_LEN = 16384
N_HEADS = 8
HEAD_DIM = 64
N_KV_HEADS = 2
GROUP = N_HEADS // N_KV_HEADS
ATTN_WIDTH = N_HEADS * HEAD_DIM
KV_WIDTH = N_KV_HEADS * HEAD_DIM
WINDOW = 128
ROPE_THETA = 500000.0
ROT_DIM = HEAD_DIM // 4
LRU_WIDTH = D_MODEL // 2
LRU_HEADS = 8
LRU_BLOCK = LRU_WIDTH // LRU_HEADS
CONV_W = 4
LRU_C = 8.0
D_FF = 2816
IN_WIDTH = ATTN_WIDTH + 2 * KV_WIDTH + 2 * LRU_WIDTH
EPS = 1e-6

LANES = 128
SUBLANES = 8
FF_CHUNK = 256
N_FF_CHUNKS = D_FF // FF_CHUNK
FFN_TILE = 512
MIX_TILE = 256
SAMPLE_GROUP = 8
CAST_ROWS = 512
VMEM_LIMIT_BYTES = 56 * 1024 * 1024

O_K = ATTN_WIDTH
O_V = O_K + KV_WIDTH
O_XR = O_V + KV_WIDTH
O_YG = O_XR + LRU_WIDTH

F32 = jnp.float32
BF16 = jnp.bfloat16


def _rmsnorm(x, g):
    return (x * lax.rsqrt(jnp.mean(x * x, axis=-1, keepdims=True) + EPS)) * g


def _half_ffn(x, g_pre, wg_ref, wu_ref, wd_ref, g_post, act_scr):
    h = _rmsnorm(x, g_pre).astype(BF16)
    for c in range(N_FF_CHUNKS):
        cols = slice(FF_CHUNK * c, FF_CHUNK * (c + 1))
        g = jnp.dot(h, wg_ref[:, cols], preferred_element_type=F32)
        u = jnp.dot(h, wu_ref[:, cols], preferred_element_type=F32)
        act_scr[:, cols] = ((g * jax.nn.sigmoid(g)) * u).astype(BF16)
    y = jnp.dot(act_scr[...], wd_ref[...], preferred_element_type=F32)
    return x + 0.5 * _rmsnorm(y, g_post)


def _gelu_tanh(x):
    c = np.float32(np.sqrt(2.0 / np.pi))
    return x * (0.5 * (1.0 + jnp.tanh(c * (x + 0.044715 * (x * x * x)))))


def _softplus(x):
    return jnp.maximum(x, 0.0) + jnp.log1p(jnp.exp(-jnp.abs(x)))


def _lru_coeffs(xc, wgate_ref, ba, bx, lam):
    gates = jnp.dot(xc.astype(BF16), wgate_ref[...], preferred_element_type=F32)
    r = jax.nn.sigmoid(gates[:, :LRU_WIDTH] + ba)
    ig = jax.nn.sigmoid(gates[:, LRU_WIDTH:] + bx)
    log_a = (-LRU_C * r) * _softplus(-lam)
    a = jnp.exp(log_a)
    u = jnp.sqrt(1.0 - a * a) * (ig * xc)
    return a, u


def _scan_rows(a, u):
    n = a.shape[0]
    row = lax.broadcasted_iota(jnp.int32, a.shape, 0)
    d = 1
    while d < n:
        keep = row >= d
        u = jnp.where(keep, a * pltpu.roll(u, d, 0) + u, u)
        if 2 * d < n:
            a = jnp.where(keep, a * pltpu.roll(a, d, 0), a)
        d *= 2
    return u


def _expand_q(q, rows):
    lane = lax.broadcasted_iota(jnp.int32, (rows, LANES), 1)
    pieces = []
    for h in range(N_HEADS):
        kv = h // GROUP
        src = q[:, LANES * (h // 2):LANES * (h // 2 + 1)]
        if (h % 2) != kv:
            src = pltpu.roll(src, HEAD_DIM, 1)
        in_half = (lane >= HEAD_DIM) if kv == 1 else (lane < HEAD_DIM)
        pieces.append(jnp.where(in_half, src, 0.0))
    return jnp.concatenate(pieces, axis=0).astype(BF16)


def _collect_heads(o, rows):
    lane = lax.broadcasted_iota(jnp.int32, (rows, LANES), 1)
    groups = []
    for j in range(N_HEADS // 2):
        kv = (2 * j) // GROUP
        even = o[rows * (2 * j):rows * (2 * j + 1)]
        odd = o[rows * (2 * j + 1):rows * (2 * j + 2)]
        if kv == 1:
            even = pltpu.roll(even, HEAD_DIM, 1)
        else:
            odd = pltpu.roll(odd, HEAD_DIM, 1)
        groups.append(jnp.where(lane < HEAD_DIM, even, odd))
    return jnp.concatenate(groups, axis=1)


def _sink_column(sinks_ref, layer, rows):
    return jnp.concatenate([jnp.full((rows, 1), sinks_ref[layer, h], F32) for h in range(N_HEADS)],
                           axis=0)


def _mix_tail(x1, attn, hs, yg, g_attn, g_lru, wo_ref, g_postmix,
              g_pre2, wg_ref, wu_ref, wd_ref, g_post2, act_scr):
    lru = hs * _gelu_tanh(yg)
    mixed = jnp.concatenate([_rmsnorm(attn, g_attn), _rmsnorm(lru, g_lru)], axis=-1).astype(BF16)
    y = jnp.dot(mixed, wo_ref[...], preferred_element_type=F32)
    x2 = x1 + _rmsnorm(y, g_postmix)
    return _half_ffn(x2, g_pre2, wg_ref, wu_ref, wd_ref, g_post2, act_scr)


def _layer_spec(shape, layer, grid_rank):
    zeros = (0,) * len(shape)
    return pl.BlockSpec((None,) + tuple(shape), lambda *_: (layer,) + zeros,
                        pipeline_mode=pl.Buffered(1))


def _cast_kernel(w_ref, o_ref):
    o_ref[...] = w_ref[...].astype(BF16)


def _to_bf16(w):
    depth, rows, cols = w.shape
    flat = w.reshape(depth * rows, cols)
    out = pl.pallas_call(
        _cast_kernel,
        grid=(depth * rows // CAST_ROWS,),
        in_specs=[pl.BlockSpec((CAST_ROWS, cols), lambda i: (i, 0))],
        out_specs=pl.BlockSpec((CAST_ROWS, cols), lambda i: (i, 0)),
        out_shape=jax.ShapeDtypeStruct(flat.shape, BF16),
        compiler_params=pltpu.CompilerParams(dimension_semantics=("arbitrary",)),
        name="cast_bf16",
    )(flat)
    return out.reshape(depth, rows, cols)


def _ffn_in_kernel(x_ref, cos_ref, sin_up_ref, sin_dn_ref, g_pre_ref, wg_ref, wu_ref, wd_ref,
                   g_post_ref, g_mix_ref, win_ref, x1_ref, z_ref, act_scr):
    x1 = _half_ffn(x_ref[...], g_pre_ref[...], wg_ref, wu_ref, wd_ref, g_post_ref[...], act_scr)
    x1_ref[...] = x1
    h = _rmsnorm(x1, g_mix_ref[...]).astype(BF16)
    z = jnp.dot(h, win_ref[...], preferred_element_type=F32)
    cos, sin_up, sin_dn = cos_ref[...], sin_up_ref[...], sin_dn_ref[...]
    half = ROT_DIM // 2
    for j in range(O_V // LANES):
        zj = z[:, LANES * j:LANES * (j + 1)]
        rot = zj * cos + pltpu.roll(zj, LANES - half, 1) * sin_up + pltpu.roll(zj, half, 1) * sin_dn
        if j < ATTN_WIDTH // LANES:
            rot = rot * (HEAD_DIM ** -0.5)
        z_ref[:, LANES * j:LANES * (j + 1)] = rot
    z_ref[:, O_V:] = z[:, O_V:]


def _ffn_in(x, tabs, tab_tiles, w, layer, tile):
    n = x.shape[0]
    lspec = lambda *shape: _layer_spec(shape, layer, 1)
    tab_spec = pl.BlockSpec((tile, LANES), lambda i: (i % tab_tiles, 0))
    return pl.pallas_call(
        _ffn_in_kernel,
        grid=(n // tile,),
        in_specs=[
            pl.BlockSpec((tile, D_MODEL), lambda i: (i, 0)),
            tab_spec, tab_spec, tab_spec,
            lspec(1, D_MODEL), lspec(D_MODEL, D_FF), lspec(D_MODEL, D_FF), lspec(D_FF, D_MODEL),
            lspec(1, D_MODEL), lspec(1, D_MODEL), lspec(D_MODEL, IN_WIDTH),
        ],
        out_specs=[
            pl.BlockSpec((tile, D_MODEL), lambda i: (i, 0)),
            pl.BlockSpec((tile, IN_WIDTH), lambda i: (i, 0)),
        ],
        out_shape=[
            jax.ShapeDtypeStruct((n, D_MODEL), F32),
            jax.ShapeDtypeStruct((n, IN_WIDTH), F32),
        ],
        scratch_shapes=[pltpu.VMEM((tile, D_FF), BF16)],
        compiler_params=pltpu.CompilerParams(
            dimension_semantics=("arbitrary",), vmem_limit_bytes=VMEM_LIMIT_BYTES),
        name="ffn_in",
    )(x, *tabs, w["g_pre1"], w["wg1"], w["wu1"], w["wd1"], w["g_post1"], w["g_mix"], w["w_in"])


def _mix_prompt_kernel(layer, sinks_ref, x1_ref, z_ref, conv_w_ref, conv_b_ref, wgate_ref, ba_ref,
                       bx_ref, lam_ref, g_attn_ref, g_lru_ref, wo_ref, g_postmix_ref, g_pre2_ref,
                       wg_ref, wu_ref, wd_ref, g_post2_ref,
                       x3_ref, nk_ref, nv_ref, nconv_ref, nh_ref,
                       k_scr, v_scr, conv_scr, h_scr, act_scr):
    tile = x1_ref.shape[0]
    s_idx = pl.program_id(1)

    @pl.when(s_idx == 0)
    def _():
        k_scr[0:WINDOW, :] = jnp.zeros((WINDOW, KV_WIDTH), BF16)
        v_scr[0:WINDOW, :] = jnp.zeros((WINDOW, KV_WIDTH), BF16)
        conv_scr[0:SUBLANES, :] = jnp.zeros((SUBLANES, LRU_WIDTH), F32)
        h_scr[...] = jnp.zeros((1, LRU_WIDTH), F32)

    k_tile = z_ref[:, O_K:O_V]
    v_tile = z_ref[:, O_V:O_XR]
    k_scr[WINDOW:, :] = k_tile.astype(BF16)
    v_scr[WINDOW:, :] = v_tile.astype(BF16)
    nk_ref[0] = k_tile[tile - WINDOW:, :]
    nv_ref[0] = v_tile[tile - WINDOW:, :]

    sink = _sink_column(sinks_ref, layer, WINDOW)
    qi = lax.broadcasted_iota(jnp.int32, (N_HEADS * WINDOW, 2 * WINDOW), 0) & (WINDOW - 1)
    kj = lax.broadcasted_iota(jnp.int32, (N_HEADS * WINDOW, 2 * WINDOW), 1)
    band = (kj > qi) & (kj <= qi + WINDOW)
    attn_blocks = []
    for i in range(tile // WINDOW):
        q_exp = _expand_q(z_ref[WINDOW * i:WINDOW * (i + 1), 0:ATTN_WIDTH], WINDOW)
        keys = k_scr[WINDOW * i:WINDOW * (i + 2), :]
        vals = v_scr[WINDOW * i:WINDOW * (i + 2), :]
        s = lax.dot_general(q_exp, keys, (((1,), (1,)), ((), ())), preferred_element_type=F32)
        if i == 0:
            first_key = jnp.where(s_idx > 0, 0, WINDOW)
            mask = band & (kj >= first_key)
        else:
            mask = band
        s = jnp.where(mask, s, -jnp.inf)
        m = jnp.maximum(jnp.max(s, axis=-1, keepdims=True), sink)
        p = jnp.exp(s - m)
        denom = jnp.sum(p, axis=-1, keepdims=True) + jnp.exp(sink - m)
        o = jnp.dot(p.astype(BF16), vals, preferred_element_type=F32) / denom
        attn_blocks.append(_collect_heads(o, WINDOW))
    attn = jnp.concatenate(attn_blocks, axis=0)
    k_scr[0:WINDOW, :] = k_scr[tile:tile + WINDOW, :]
    v_scr[0:WINDOW, :] = v_scr[tile:tile + WINDOW, :]

    xr = z_ref[:, O_XR:O_YG]
    conv_scr[SUBLANES:, :] = xr
    xc = conv_b_ref[...] + conv_scr[SUBLANES - 3:SUBLANES - 3 + tile, :] * conv_w_ref[0:1, :]
    for j in range(1, CONV_W):
        xc = xc + conv_scr[SUBLANES - 3 + j:SUBLANES - 3 + j + tile, :] * conv_w_ref[j:j + 1, :]
    conv_scr[0:SUBLANES, :] = conv_scr[tile:tile + SUBLANES, :]
    nconv_ref[0] = xr[tile - (CONV_W - 1):, :]

    a, u = _lru_coeffs(xc, wgate_ref, ba_ref[...], bx_ref[...], lam_ref[...])
    row = lax.broadcasted_iota(jnp.int32, u.shape, 0)
    u = jnp.where(row == 0, a * h_scr[...] + u, u)
    hs = _scan_rows(a, u)
    h_last = hs[tile - 1:tile, :]
    h_scr[...] = h_last
    nh_ref[0] = h_last

    x3_ref[...] = _mix_tail(x1_ref[...], attn, hs, z_ref[:, O_YG:], g_attn_ref[...], g_lru_ref[...],
                            wo_ref, g_postmix_ref[...], g_pre2_ref[...], wg_ref, wu_ref, wd_ref,
                            g_post2_ref[...], act_scr)


def _mix_weight_specs(layer, grid_rank):
    lspec = lambda *shape: _layer_spec(shape, layer, grid_rank)
    return [
        lspec(CONV_W, LRU_WIDTH), lspec(1, LRU_WIDTH), lspec(LRU_WIDTH, 2 * LRU_WIDTH),
        lspec(1, LRU_WIDTH), lspec(1, LRU_WIDTH), lspec(1, LRU_WIDTH),
        lspec(1, ATTN_WIDTH), lspec(1, LRU_WIDTH), lspec(D_MODEL, D_MODEL), lspec(1, D_MODEL),
        lspec(1, D_MODEL), lspec(D_MODEL, D_FF), lspec(D_MODEL, D_FF), lspec(D_FF, D_MODEL),
        lspec(1, D_MODEL),
    ]


def _mix_weights(w):
    return (w["conv_w"], w["conv_b"], w["w_gate"], w["gate_a_b"], w["gate_x_b"], w["lam"],
            w["g_attn"], w["g_lru"], w["w_o"], w["g_postmix"], w["g_pre2"], w["wg2"], w["wu2"],
            w["wd2"], w["g_post2"])


def _mix_prompt(x1, z, w, layer, batch, seq, tile):
    n_s = seq // tile
    tok = lambda width: pl.BlockSpec((tile, width), lambda b, s: (b * n_s + s, 0))
    per_batch = lambda rows, width: pl.BlockSpec((1, rows, width), lambda b, s: (b, 0, 0))
    return pl.pallas_call(
        functools.partial(_mix_prompt_kernel, layer),
        grid=(batch, n_s),
        in_specs=[pl.BlockSpec(memory_space=pltpu.SMEM), tok(D_MODEL), tok(IN_WIDTH)]
                 + _mix_weight_specs(layer, 2),
        out_specs=[
            tok(D_MODEL),
            per_batch(WINDOW, KV_WIDTH), per_batch(WINDOW, KV_WIDTH),
            per_batch(CONV_W - 1, LRU_WIDTH), per_batch(1, LRU_WIDTH),
        ],
        out_shape=[
            jax.ShapeDtypeStruct((batch * seq, D_MODEL), F32),
            jax.ShapeDtypeStruct((batch, WINDOW, KV_WIDTH), F32),
            jax.ShapeDtypeStruct((batch, WINDOW, KV_WIDTH), F32),
            jax.ShapeDtypeStruct((batch, CONV_W - 1, LRU_WIDTH), F32),
            jax.ShapeDtypeStruct((batch, 1, LRU_WIDTH), F32),
        ],
        scratch_shapes=[
            pltpu.VMEM((WINDOW + tile, KV_WIDTH), BF16),
            pltpu.VMEM((WINDOW + tile, KV_WIDTH), BF16),
            pltpu.VMEM((SUBLANES + tile, LRU_WIDTH), F32),
            pltpu.VMEM((1, LRU_WIDTH), F32),
            pltpu.VMEM((tile, D_FF), BF16),
        ],
        compiler_params=pltpu.CompilerParams(
            dimension_semantics=("arbitrary", "arbitrary"), vmem_limit_bytes=VMEM_LIMIT_BYTES),
        name="mix_prompt",
    )(w["sinks"], x1, z, *_mix_weights(w))


def _attn_sample_kernel(layer, sinks_ref, zq_ref, kvnew_ref, ck_ref, cv_ref, attn_ref, nk_ref, nv_ref):
    dec_seq, g = zq_ref.shape[0], zq_ref.shape[1]
    rows = dec_seq * g
    cache_len = ck_ref.shape[1]
    zq = zq_ref[...].reshape(rows, zq_ref.shape[2])
    q_exp = _expand_q(zq[:, 0:ATTN_WIDTH], rows)
    k_new = zq[:, O_K:O_V].astype(BF16)
    v_new = zq[:, O_V:O_XR].astype(BF16)
    k_cache = ck_ref[...].reshape(g * cache_len, KV_WIDTH).astype(BF16)
    v_cache = cv_ref[...].reshape(g * cache_len, KV_WIDTH).astype(BF16)
    contract_last = (((1,), (1,)), ((), ()))
    s_c = lax.dot_general(q_exp, k_cache, contract_last, preferred_element_type=F32)
    s_n = lax.dot_general(q_exp, k_new, contract_last, preferred_element_type=F32)

    def row_ids(shape):
        r = lax.broadcasted_iota(jnp.int32, shape, 0)
        return (r % rows) // g, r % g

    t_q, b_q = row_ids(s_c.shape)
    c = lax.broadcasted_iota(jnp.int32, s_c.shape, 1)
    mask_c = ((c // cache_len) == b_q) & ((c % cache_len) > t_q)
    t_q, b_q = row_ids(s_n.shape)
    c = lax.broadcasted_iota(jnp.int32, s_n.shape, 1)
    mask_n = ((c % g) == b_q) & ((c // g) <= t_q)
    s_c = jnp.where(mask_c, s_c, -jnp.inf)
    s_n = jnp.where(mask_n, s_n, -jnp.inf)

    sink = _sink_column(sinks_ref, layer, rows)
    m = jnp.maximum(jnp.maximum(jnp.max(s_c, axis=-1, keepdims=True),
                                jnp.max(s_n, axis=-1, keepdims=True)), sink)
    p_c = jnp.exp(s_c - m)
    p_n = jnp.exp(s_n - m)
    denom = (jnp.sum(p_c, axis=-1, keepdims=True) + jnp.sum(p_n, axis=-1, keepdims=True)
             + jnp.exp(sink - m))
    o = (jnp.dot(p_c.astype(BF16), v_cache, preferred_element_type=F32)
         + jnp.dot(p_n.astype(BF16), v_new, preferred_element_type=F32)) / denom
    attn_ref[...] = _collect_heads(o, rows).reshape(dec_seq, g, ATTN_WIDTH)

    nk_ref[:, 0:cache_len - dec_seq, :] = ck_ref[:, dec_seq:, :]
    nv_ref[:, 0:cache_len - dec_seq, :] = cv_ref[:, dec_seq:, :]
    nk_ref[:, cache_len - dec_seq:, :] = kvnew_ref[:, :, 0:KV_WIDTH]
    nv_ref[:, cache_len - dec_seq:, :] = kvnew_ref[:, :, KV_WIDTH:]


def _attn_sample(z3, kvnew, cache_k, cache_v, sinks, layer):
    dec_seq, dec_batch, _ = z3.shape
    cache_len = cache_k.shape[2]
    g = SAMPLE_GROUP
    cache_spec = pl.BlockSpec((None, g, cache_len, KV_WIDTH), lambda i: (layer, i, 0, 0))
    new_cache_spec = pl.BlockSpec((g, cache_len, KV_WIDTH), lambda i: (i, 0, 0))
    return pl.pallas_call(
        functools.partial(_attn_sample_kernel, layer),
        grid=(dec_batch // g,),
        in_specs=[
            pl.BlockSpec(memory_space=pltpu.SMEM),
            pl.BlockSpec((dec_seq, g, O_XR), lambda i: (0, i, 0)),
            pl.BlockSpec((g, dec_seq, 2 * KV_WIDTH), lambda i: (i, 0, 0)),
            cache_spec, cache_spec,
        ],
        out_specs=[
            pl.BlockSpec((dec_seq, g, ATTN_WIDTH), lambda i: (0, i, 0)),
            new_cache_spec, new_cache_spec,
        ],
        out_shape=[
            jax.ShapeDtypeStruct((dec_seq, dec_batch, ATTN_WIDTH), F32),
            jax.ShapeDtypeStruct(cache_k.shape[1:], F32),
            jax.ShapeDtypeStruct(cache_v.shape[1:], F32),
        ],
        compiler_params=pltpu.CompilerParams(
            dimension_semantics=("arbitrary",), vmem_limit_bytes=VMEM_LIMIT_BYTES),
        name="attn_sample",
    )(sinks, z3, kvnew, cache_k, cache_v)


def _mix_sample_kernel(x1_ref, z_ref, attn_ref, sconv_ref, sh_ref, conv_w_ref, conv_b_ref, wgate_ref,
                       ba_ref, bx_ref, lam_ref, g_attn_ref, g_lru_ref, wo_ref, g_postmix_ref,
                       g_pre2_ref, wg_ref, wu_ref, wd_ref, g_post2_ref,
                       x3_ref, nconv_ref, nh_ref, act_scr):
    nb = sh_ref.shape[0]
    dec_seq = x1_ref.shape[0] // nb
    xr = z_ref[:, O_XR:O_YG]
    xp = [sconv_ref[j] for j in range(CONV_W - 1)] + [xr[nb * t:nb * (t + 1)] for t in range(dec_seq)]
    xc_steps = []
    for t in range(dec_seq):
        acc = conv_b_ref[...] + xp[t] * conv_w_ref[0:1, :]
        for j in range(1, CONV_W):
            acc = acc + xp[t + j] * conv_w_ref[j:j + 1, :]
        xc_steps.append(acc)
    for j in range(CONV_W - 1):
        nconv_ref[j] = xp[dec_seq + j]
    a, u = _lru_coeffs(jnp.concatenate(xc_steps, axis=0), wgate_ref, ba_ref[...], bx_ref[...],
                       lam_ref[...])
    h = sh_ref[...]
    hs_steps = []
    for t in range(dec_seq):
        h = a[nb * t:nb * (t + 1)] * h + u[nb * t:nb * (t + 1)]
        hs_steps.append(h)
    nh_ref[...] = h
    hs = jnp.concatenate(hs_steps, axis=0)
    x3_ref[...] = _mix_tail(x1_ref[...], attn_ref[...], hs, z_ref[:, O_YG:], g_attn_ref[...],
                            g_lru_ref[...], wo_ref, g_postmix_ref[...], g_pre2_ref[...], wg_ref,
                            wu_ref, wd_ref, g_post2_ref[...], act_scr)


def _mix_sample(x1, z, attn, sconv, state_h, w, layer):
    n = x1.shape[0]
    nb = state_h.shape[1]
    full = lambda *shape: pl.BlockSpec(shape, lambda i: (0,) * len(shape))
    return pl.pallas_call(
        _mix_sample_kernel,
        grid=(1,),
        in_specs=[
            full(n, D_MODEL), full(n, IN_WIDTH), full(n, ATTN_WIDTH),
            full(CONV_W - 1, nb, LRU_WIDTH), _layer_spec((nb, LRU_WIDTH), layer, 1),
        ] + _mix_weight_specs(layer, 1),
        out_specs=[full(n, D_MODEL), full(CONV_W - 1, nb, LRU_WIDTH), full(nb, LRU_WIDTH)],
        out_shape=[
            jax.ShapeDtypeStruct((n, D_MODEL), F32),
            jax.ShapeDtypeStruct((CONV_W - 1, nb, LRU_WIDTH), F32),
            jax.ShapeDtypeStruct((nb, LRU_WIDTH), F32),
        ],
        scratch_shapes=[pltpu.VMEM((n, D_FF), BF16)],
        compiler_params=pltpu.CompilerParams(
            dimension_semantics=("arbitrary",), vmem_limit_bytes=VMEM_LIMIT_BYTES),
        name="mix_sample",
    )(x1, z, attn, sconv, state_h, *_mix_weights(w))


def _rope_tables(pos):
    half = ROT_DIM // 2
    inv = ROPE_THETA ** (-(jnp.arange(half, dtype=F32) * 2.0) / ROT_DIM)
    ang = pos.astype(F32)[:, None] * inv[None, :]
    cos, sin = jnp.cos(ang), jnp.sin(ang)
    n = pos.shape[0]
    rest = jnp.zeros((n, HEAD_DIM - ROT_DIM), F32)
    zero = jnp.zeros((n, half), F32)
    per_head = lambda parts: jnp.tile(jnp.concatenate(parts, axis=-1), (1, LANES // HEAD_DIM))
    return (per_head([cos, cos, rest + 1.0]), per_head([-sin, zero, rest]), per_head([zero, sin, rest]))


def _block_diag(w):
    eye = jnp.eye(LRU_HEADS, dtype=w.dtype)
    return jnp.einsum("lhij,hg->lhigj", w, eye).reshape(w.shape[0], LRU_WIDTH, LRU_WIDTH)


def kernel(x_prompt, x_sample, cache_k, cache_v, state_conv, state_h, norm_pre_ffn1, ffn1_w_gate, ffn1_w_up, ffn1_w_down, norm_post_ffn1, norm_pre_mix, w_in, sinks, conv_w, conv_b, gate_a_w, gate_a_b, gate_x_w, gate_x_b, lru_lambda, attn_out_norm, lru_out_norm, w_o, norm_post_mix, norm_pre_ffn2, ffn2_w_gate, ffn2_w_up, ffn2_w_down, norm_post_ffn2):
    batch, seq, _ = x_prompt.shape
    dec_batch, dec_seq, _ = x_sample.shape
    depth = w_in.shape[0]
    cache_len = cache_k.shape[2]
    n_s = dec_seq * dec_batch
    assert seq % FFN_TILE == 0 and n_s % FFN_TILE == 0
    assert seq % MIX_TILE == 0 and MIX_TILE % WINDOW == 0 and dec_batch % SAMPLE_GROUP == 0

    tabs_p = _rope_tables(jnp.arange(seq))
    tabs_s = _rope_tables(jnp.repeat(PAST_LEN + jnp.arange(dec_seq), dec_batch))
    xp = x_prompt.reshape(batch * seq, D_MODEL)
    xs = jnp.swapaxes(x_sample, 0, 1).reshape(n_s, D_MODEL)
    rows = lambda v: v.reshape(depth, 1, -1)

    w = dict(
        g_pre1=rows(norm_pre_ffn1), wg1=_to_bf16(ffn1_w_gate), wu1=_to_bf16(ffn1_w_up),
        wd1=_to_bf16(ffn1_w_down), g_post1=rows(norm_post_ffn1), g_mix=rows(norm_pre_mix),
        w_in=_to_bf16(w_in), sinks=sinks, conv_w=conv_w, conv_b=rows(conv_b),
        w_gate=jnp.concatenate([_block_diag(gate_a_w), _block_diag(gate_x_w)], axis=2).astype(BF16),
        gate_a_b=rows(gate_a_b), gate_x_b=rows(gate_x_b), lam=rows(lru_lambda),
        g_attn=rows(attn_out_norm), g_lru=rows(lru_out_norm), w_o=_to_bf16(w_o),
        g_postmix=rows(norm_post_mix), g_pre2=rows(norm_pre_ffn2), wg2=_to_bf16(ffn2_w_gate),
        wu2=_to_bf16(ffn2_w_up), wd2=_to_bf16(ffn2_w_down), g_post2=rows(norm_post_ffn2),
    )
    ck = cache_k.reshape(depth, dec_batch, cache_len, KV_WIDTH)
    cv = cache_v.reshape(depth, dec_batch, cache_len, KV_WIDTH)

    outs = [[] for _ in range(8)]
    for l in range(depth):
        x1p, zp = _ffn_in(xp, tabs_p, seq // FFN_TILE, w, l, FFN_TILE)
        xp, nk, nv, nc, nh = _mix_prompt(x1p, zp, w, l, batch, seq, MIX_TILE)

        x1s, zs = _ffn_in(xs, tabs_s, n_s // FFN_TILE, w, l, FFN_TILE)
        z3 = zs.reshape(dec_seq, dec_batch, IN_WIDTH)
        kvnew = jnp.swapaxes(z3[:, :, O_K:O_XR], 0, 1)
        attn_s, nks, nvs = _attn_sample(z3, kvnew, ck, cv, sinks, l)
        xs, ncs, nhs = _mix_sample(x1s, zs, attn_s.reshape(n_s, ATTN_WIDTH),
                                   jnp.swapaxes(state_conv[l], 0, 1), state_h, w, l)
        kv_p = (batch, WINDOW, N_KV_HEADS, HEAD_DIM)
        kv_s = (dec_batch, cache_len, N_KV_HEADS, HEAD_DIM)
        for acc, val in zip(outs, (nk.reshape(kv_p), nv.reshape(kv_p), nc,
                                   nh.reshape(batch, LRU_WIDTH), nks.reshape(kv_s),
                                   nvs.reshape(kv_s), jnp.swapaxes(ncs, 0, 1), nhs)):
            acc.append(val)

    y_prompt = xp.reshape(batch, seq, D_MODEL)
    y_sample = jnp.swapaxes(xs.reshape(dec_seq, dec_batch, D_MODEL), 0, 1)
    return (y_prompt, y_sample) + tuple(jnp.stack(o) for o in outs)
```

```python
import functools

import jax
import jax.numpy as jnp
import numpy as np
from jax import lax
from jax.experimental import pallas as pl
from jax.experimental.pallas import tpu as pltpu

D_MODEL = 1024
PAST_LEN = 16384
N_HEADS = 8
HEAD_DIM = 64
N_KV_HEADS = 2
GROUP = N_HEADS // N_KV_HEADS
ATTN_WIDTH = N_HEADS * HEAD_DIM
KV_WIDTH = N_KV_HEADS * HEAD_DIM
WINDOW = 128
ROPE_THETA = 500000.0
ROT_DIM = HEAD_DIM // 4
LRU_WIDTH = D_MODEL // 2
LRU_HEADS = 8
LRU_BLOCK = LRU_WIDTH // LRU_HEADS
CONV_W = 4
LRU_C = 8.0
D_FF = 2816
IN_WIDTH = ATTN_WIDTH + 2 * KV_WIDTH + 2 * LRU_WIDTH
EPS = 1e-6

LANES = 128
SUBLANES = 8
FF_CHUNK = 256
N_FF_CHUNKS = D_FF // FF_CHUNK
FFN_TILE = 512
MIX_TILE = 256
SAMPLE_GROUP = 8
CAST_ROWS = 512
VMEM_LIMIT_BYTES = 56 * 1024 * 1024

O_K = ATTN_WIDTH
O_V = O_K + KV_WIDTH
O_XR = O_V + KV_WIDTH
O_YG = O_XR + LRU_WIDTH

F32 = jnp.float32
BF16 = jnp.bfloat16


def _rmsnorm(x, g):
    return (x * lax.rsqrt(jnp.mean(x * x, axis=-1, keepdims=True) + EPS)) * g


def _half_ffn_steps(x, g_pre, wg_ref, wu_ref, wd_ref, g_post, act_scr):
    h = _rmsnorm(x, g_pre).astype(BF16)
    for c in range(N_FF_CHUNKS):
        cols = slice(FF_CHUNK * c, FF_CHUNK * (c + 1))
        g = jnp.dot(h, wg_ref[:, cols], preferred_element_type=F32)
        u = jnp.dot(h, wu_ref[:, cols], preferred_element_type=F32)
        act_scr[:, cols] = ((g * jax.nn.sigmoid(g)) * u).astype(BF16)
        yield
    ys = []
    for c in range(D_MODEL // FF_CHUNK):
        ys.append(jnp.dot(act_scr[...], wd_ref[:, FF_CHUNK * c:FF_CHUNK * (c + 1)],
                          preferred_element_type=F32))
        yield
    return x + 0.5 * _rmsnorm(jnp.concatenate(ys, axis=1), g_post)


def _advance(gen, done):
    if gen in done:
        return
    try:
        next(gen)
    except StopIteration as stop:
        done[gen] = stop.value


def _interleave(first, second):
    done = {}
    while len(done) < 2:
        _advance(first, done)
        _advance(second, done)
    return done[first], done[second]


def _half_ffn(*args):
    done = {}
    gen = _half_ffn_steps(*args)
    while gen not in done:
        _advance(gen, done)
    return done[gen]


def _gelu_tanh(x):
    c = np.float32(np.sqrt(2.0 / np.pi))
    return x * (0.5 * (1.0 + jnp.tanh(c * (x + 0.044715 * (x * x * x)))))


def _softplus(x):
    return jnp.maximum(x, 0.0) + jnp.log1p(jnp.exp(-jnp.abs(x)))


def _lru_gates(xc, wgate_ref):
    return jnp.dot(xc.astype(BF16), wgate_ref[...], preferred_element_type=F32)


def _lru_coeffs(xc, gate_a, gate_x, ba, bx, lam):
    r = jax.nn.sigmoid(gate_a + ba)
    ig = jax.nn.sigmoid(gate_x + bx)
    log_a = (-LRU_C * r) * _softplus(-lam)
    a = jnp.exp(log_a)
    v = 1.0 - a * a
    u = jnp.where(v > 0.0, v * lax.rsqrt(v), 0.0) * (ig * xc)
    return a, u


def _scan_rows(a, u):
    n = a.shape[0]
    row = lax.broadcasted_iota(jnp.int32, a.shape, 0)
    d = 1
    while d < n:
        keep = row >= d
        u = jnp.where(keep, a * pltpu.roll(u, d, 0) + u, u)
        if 2 * d < n:
            a = jnp.where(keep, a * pltpu.roll(a, d, 0), a)
        d *= 2
    return u


def _scan_rows_carry(a, u, h0):
    n, c = a.shape
    groups = n // SUBLANES
    a3 = a.reshape(groups, SUBLANES, c)
    u3 = u.reshape(groups, SUBLANES, c)
    sub = lax.broadcasted_iota(jnp.int32, a3.shape, 1)
    d = 1
    while d < SUBLANES:
        keep = sub >= d
        u3 = jnp.where(keep, a3 * pltpu.roll(u3, d, 1) + u3, u3)
        a3 = jnp.where(keep, a3 * pltpu.roll(a3, d, 1), a3)
        d *= 2
    a_grp = a3[:, SUBLANES - 1, :]
    h_grp = u3[:, SUBLANES - 1, :]
    first = lax.broadcasted_iota(jnp.int32, a_grp.shape, 0) == 0
    h_end = _scan_rows(a_grp, jnp.where(first, a_grp * h0 + h_grp, h_grp))
    h_in = jnp.where(first, h0, pltpu.roll(h_end, 1, 0))
    hs = u3 + a3 * h_in[:, None, :]
    return hs.reshape(n, c)


def _expand_q(q, rows):
    lane = lax.broadcasted_iota(jnp.int32, (rows, LANES), 1)
    pieces = []
    for h in range(N_HEADS):
        kv = h // GROUP
        src = q[:, LANES * (h // 2):LANES * (h // 2 + 1)]
        if (h % 2) != kv:
            src = pltpu.roll(src, HEAD_DIM, 1)
        in_half = (lane >= HEAD_DIM) if kv == 1 else (lane < HEAD_DIM)
        pieces.append(jnp.where(in_half, src, 0.0))
    return jnp.concatenate(pieces, axis=0).astype(BF16)


def _collect_heads(o, rows):
    lane = lax.broadcasted_iota(jnp.int32, (rows, LANES), 1)
    groups = []
    for j in range(N_HEADS // 2):
        kv = (2 * j) // GROUP
        even = o[rows * (2 * j):rows * (2 * j + 1)]
        odd = o[rows * (2 * j + 1):rows * (2 * j + 2)]
        if kv == 1:
            even = pltpu.roll(even, HEAD_DIM, 1)
        else:
            odd = pltpu.roll(odd, HEAD_DIM, 1)
        groups.append(jnp.where(lane < HEAD_DIM, even, odd))
    return jnp.concatenate(groups, axis=1)


def _collect_heads_t(o_t, rows):
    return jnp.concatenate(
        [o_t[HEAD_DIM * (h // GROUP):HEAD_DIM * (h // GROUP + 1), rows * h:rows * (h + 1)]
         for h in range(N_HEADS)], axis=0)


def _sink_row(sinks_ref, layer, rows):
    head = lax.broadcasted_iota(jnp.int32, (1, N_HEADS * rows), 1) // rows
    out = jnp.full((1, N_HEADS * rows), sinks_ref[layer, 0], F32)
    for h in range(1, N_HEADS):
        out = jnp.where(head == h, sinks_ref[layer, h], out)
    return out


def _sink_column(sinks_ref, layer, rows):
    return jnp.concatenate([jnp.full((rows, 1), sinks_ref[layer, h], F32) for h in range(N_HEADS)],
                           axis=0)


def _mix_normed(attn, hs, yg, g_attn, g_lru):
    lru = hs * _gelu_tanh(yg)
    return jnp.concatenate([_rmsnorm(attn, g_attn), _rmsnorm(lru, g_lru)], axis=-1).astype(BF16)


def _mix_project(x1, mixed, wo_ref, g_postmix):
    y = jnp.dot(mixed, wo_ref[...], preferred_element_type=F32)
    return x1 + _rmsnorm(y, g_postmix)


def _mix_out(x1, attn, hs, yg, g_attn, g_lru, wo_ref, g_postmix):
    return _mix_project(x1, _mix_normed(attn, hs, yg, g_attn, g_lru), wo_ref, g_postmix)


def _layer_spec(shape, layer, grid_rank):
    zeros = (0,) * len(shape)
    return pl.BlockSpec((None,) + tuple(shape), lambda *_: (layer,) + zeros,
                        pipeline_mode=pl.Buffered(1))


def _cast_kernel(w_ref, o_ref):
    o_ref[...] = w_ref[...].astype(BF16)


def _to_bf16(w):
    depth, rows, cols = w.shape
    flat = w.reshape(depth * rows, cols)
    out = pl.pallas_call(
        _cast_kernel,
        grid=(depth * rows // CAST_ROWS,),
        in_specs=[pl.BlockSpec((CAST_ROWS, cols), lambda i: (i, 0))],
        out_specs=pl.BlockSpec((CAST_ROWS, cols), lambda i: (i, 0)),
        out_shape=jax.ShapeDtypeStruct(flat.shape, BF16),
        compiler_params=pltpu.CompilerParams(dimension_semantics=("arbitrary",)),
        name="cast_bf16",
    )(flat)
    return out.reshape(depth, rows, cols)


def _ffn_in_kernel(x_ref, cos_ref, sin_up_ref, sin_dn_ref, g_pre_ref, wg_ref, wu_ref, wd_ref,
                   g_post_ref, g_mix_ref, win_ref, x1_ref, z_ref, act_scr):
    x1 = _half_ffn(x_ref[...], g_pre_ref[...], wg_ref, wu_ref, wd_ref, g_post_ref[...], act_scr)
    x1_ref[...] = x1
    h = _rmsnorm(x1, g_mix_ref[...]).astype(BF16)
    z = jnp.dot(h, win_ref[...], preferred_element_type=F32)
    cos, sin_up, sin_dn = cos_ref[...], sin_up_ref[...], sin_dn_ref[...]
    half = ROT_DIM // 2
    for j in range(O_V // LANES):
        zj = z[:, LANES * j:LANES * (j + 1)]
        rot = zj * cos + pltpu.roll(zj, LANES - half, 1) * sin_up + pltpu.roll(zj, half, 1) * sin_dn
        if j < ATTN_WIDTH // LANES:
            rot = rot * (HEAD_DIM ** -0.5)
        z_ref[:, LANES * j:LANES * (j + 1)] = rot
    z_ref[:, O_V:] = z[:, O_V:]


def _ffn_in(x, tabs, tab_tiles, w, layer, tile):
    n = x.shape[0]
    lspec = lambda *shape: _layer_spec(shape, layer, 1)
    tab_spec = pl.BlockSpec((tile, LANES), lambda i: (i % tab_tiles, 0))
    return pl.pallas_call(
        _ffn_in_kernel,
        grid=(n // tile,),
        in_specs=[
            pl.BlockSpec((tile, D_MODEL), lambda i: (i, 0)),
            tab_spec, tab_spec, tab_spec,
            lspec(1, D_MODEL), lspec(D_MODEL, D_FF), lspec(D_MODEL, D_FF), lspec(D_FF, D_MODEL),
            lspec(1, D_MODEL), lspec(1, D_MODEL), lspec(D_MODEL, IN_WIDTH),
        ],
        out_specs=[
            pl.BlockSpec((tile, D_MODEL), lambda i: (i, 0)),
            pl.BlockSpec((tile, IN_WIDTH), lambda i: (i, 0)),
        ],
        out_shape=[
            jax.ShapeDtypeStruct((n, D_MODEL), F32),
            jax.ShapeDtypeStruct((n, IN_WIDTH), F32),
        ],
        scratch_shapes=[pltpu.VMEM((tile, D_FF), BF16)],
        compiler_params=pltpu.CompilerParams(
            dimension_semantics=("arbitrary",), vmem_limit_bytes=VMEM_LIMIT_BYTES),
        name="ffn_in",
    )(x, *tabs, w["g_pre1"], w["wg1"], w["wu1"], w["wd1"], w["g_post1"], w["g_mix"], w["w_in"])


def _mix_prompt_kernel(layer, n_seq_tiles, sinks_ref, x1_ref, z_ref, conv_w_ref, conv_b_ref, wgate_ref, ba_ref,
                       bx_ref, lam_ref, g_attn_ref, g_lru_ref, wo_ref, g_postmix_ref, g_pre2_ref,
                       wg_ref, wu_ref, wd_ref, g_post2_ref,
                       x3_ref, nk_ref, nv_ref, nconv_ref, nh_ref,
                       k_scr, vt_scr, conv_scr, h_scr, x2_scr, act_scr):
    tile = x1_ref.shape[0]
    g_idx = pl.program_id(0)
    n_tiles = pl.num_programs(0) - 1
    s_idx = g_idx % n_seq_tiles

    @pl.when(g_idx == 0)
    def _():
        x2_scr[...] = jnp.zeros(x2_scr.shape, F32)

    @pl.when(s_idx == 0)
    def _():
        k_scr[0:WINDOW, :] = jnp.zeros((WINDOW, KV_WIDTH), BF16)
        vt_scr[:, 0:WINDOW] = jnp.zeros((KV_WIDTH, WINDOW), BF16)
        conv_scr[0:SUBLANES, :] = jnp.zeros((SUBLANES, LRU_WIDTH), F32)
        h_scr[...] = jnp.zeros((1, LRU_WIDTH), F32)

    k_tile = z_ref[:, O_K:O_V]
    v_tile = z_ref[:, O_V:O_XR]
    xr = z_ref[:, O_XR:O_YG]

    def mixer_steps():
        k_scr[WINDOW:, :] = k_tile.astype(BF16)
        vt_scr[:, WINDOW:] = v_tile.T.astype(BF16)
        sink = _sink_row(sinks_ref, layer, WINDOW)
        kj = lax.broadcasted_iota(jnp.int32, (2 * WINDOW, N_HEADS * WINDOW), 0)
        qi = lax.broadcasted_iota(jnp.int32, (2 * WINDOW, N_HEADS * WINDOW), 1) & (WINDOW - 1)
        band = (kj > qi) & (kj <= qi + WINDOW)
        attn_blocks = []
        for i in range(tile // WINDOW):
            q_exp = _expand_q(z_ref[WINDOW * i:WINDOW * (i + 1), 0:ATTN_WIDTH], WINDOW)
            keys = k_scr[WINDOW * i:WINDOW * (i + 2), :]
            vals_t = vt_scr[:, WINDOW * i:WINDOW * (i + 2)]
            s = lax.dot_general(keys, q_exp, (((1,), (1,)), ((), ())), preferred_element_type=F32)
            if i == 0:
                first_key = jnp.where(s_idx > 0, 0, WINDOW)
                mask = band & (kj >= first_key)
            else:
                mask = band
            s = jnp.where(mask, s, -jnp.inf)
            m = jnp.maximum(jnp.max(s, axis=0, keepdims=True), sink)
            p = jnp.exp(s - m)
            denom = jnp.sum(p, axis=0, keepdims=True) + jnp.exp(sink - m)
            yield
            o_t = jnp.dot(vals_t, p.astype(BF16), preferred_element_type=F32) * (1.0 / denom)
            attn_blocks.append(_collect_heads_t(o_t, WINDOW).T)
            yield
        attn = jnp.concatenate(attn_blocks, axis=0)
        k_scr[0:WINDOW, :] = k_scr[tile:tile + WINDOW, :]
        vt_scr[:, 0:WINDOW] = vt_scr[:, tile:tile + WINDOW]

        conv_scr[SUBLANES:, :] = xr
        xc = conv_b_ref[...] + conv_scr[SUBLANES - 3:SUBLANES - 3 + tile, :] * conv_w_ref[0:1, :]
        for j in range(1, CONV_W):
            xc = xc + conv_scr[SUBLANES - 3 + j:SUBLANES - 3 + j + tile, :] * conv_w_ref[j:j + 1, :]
        conv_scr[0:SUBLANES, :] = conv_scr[tile:tile + SUBLANES, :]
        gates = _lru_gates(xc, wgate_ref)
        yield

        hs_groups = []
        for j in range(LRU_WIDTH // LANES):
            ch = slice(LANES * j, LANES * (j + 1))
            a, u = _lru_coeffs(xc[:, ch], gates[:, ch], gates[:, LRU_WIDTH + LANES * j:
                                                              LRU_WIDTH + LANES * (j + 1)],
                               ba_ref[:, ch], bx_ref[:, ch], lam_ref[:, ch])
            yield
            hs_groups.append(_scan_rows_carry(a, u, h_scr[:, ch]))
            yield
        hs = jnp.concatenate(hs_groups, axis=1)
        h_last = hs[tile - 1:tile, :]
        h_scr[...] = h_last
        mixed = _mix_normed(attn, hs, z_ref[:, O_YG:], g_attn_ref[...], g_lru_ref[...])
        yield
        x2 = _mix_project(x1_ref[...], mixed, wo_ref, g_postmix_ref[...])
        return x2, h_last

    ffn = _half_ffn_steps(x2_scr[...], g_pre2_ref[...], wg_ref, wu_ref, wd_ref, g_post2_ref[...],
                          act_scr)
    x3, (x2, h_last) = _interleave(ffn, mixer_steps())
    x3_ref[...] = x3
    x2_scr[...] = x2

    @pl.when(g_idx < n_tiles)
    def _():
        nk_ref[0] = k_tile[tile - WINDOW:, :]
        nv_ref[0] = v_tile[tile - WINDOW:, :]
        nconv_ref[0] = xr[tile - (CONV_W - 1):, :]
        nh_ref[0] = h_last


def _mix_weight_specs(layer, grid_rank):
    lspec = lambda *shape: _layer_spec(shape, layer, grid_rank)
    return [
        lspec(CONV_W, LRU_WIDTH), lspec(1, LRU_WIDTH), lspec(LRU_WIDTH, 2 * LRU_WIDTH),
        lspec(1, LRU_WIDTH), lspec(1, LRU_WIDTH), lspec(1, LRU_WIDTH),
        lspec(1, ATTN_WIDTH), lspec(1, LRU_WIDTH), lspec(D_MODEL, D_MODEL), lspec(1, D_MODEL),
        lspec(1, D_MODEL), lspec(D_MODEL, D_FF), lspec(D_MODEL, D_FF), lspec(D_FF, D_MODEL),
        lspec(1, D_MODEL),
    ]


def _mix_weights(w):
    return (w["conv_w"], w["conv_b"], w["w_gate"], w["gate_a_b"], w["gate_x_b"], w["lam"],
            w["g_attn"], w["g_lru"], w["w_o"], w["g_postmix"], w["g_pre2"], w["wg2"], w["wu2"],
            w["wd2"], w["g_post2"])


def _mix_prompt(x1, z, w, layer, batch, seq, tile):
    n_s = seq // tile
    n_tiles = batch * n_s
    mix_tile = lambda g: jnp.minimum(g, n_tiles - 1)
    tok = lambda width: pl.BlockSpec((tile, width), lambda g: (mix_tile(g), 0))
    per_batch = lambda rows, width: pl.BlockSpec((1, rows, width), lambda g: (mix_tile(g) // n_s, 0, 0))
    return pl.pallas_call(
        functools.partial(_mix_prompt_kernel, layer, n_s),
        grid=(n_tiles + 1,),
        in_specs=[pl.BlockSpec(memory_space=pltpu.SMEM), tok(D_MODEL), tok(IN_WIDTH)]
                 + _mix_weight_specs(layer, 1),
        out_specs=[
            pl.BlockSpec((tile, D_MODEL), lambda g: (jnp.maximum(g - 1, 0), 0)),
            per_batch(WINDOW, KV_WIDTH), per_batch(WINDOW, KV_WIDTH),
            per_batch(CONV_W - 1, LRU_WIDTH), per_batch(1, LRU_WIDTH),
        ],
        out_shape=[
            jax.ShapeDtypeStruct((batch * seq, D_MODEL), F32),
            jax.ShapeDtypeStruct((batch, WINDOW, KV_WIDTH), F32),
            jax.ShapeDtypeStruct((batch, WINDOW, KV_WIDTH), F32),
            jax.ShapeDtypeStruct((batch, CONV_W - 1, LRU_WIDTH), F32),
            jax.ShapeDtypeStruct((batch, 1, LRU_WIDTH), F32),
        ],
        scratch_shapes=[
            pltpu.VMEM((WINDOW + tile, KV_WIDTH), BF16),
            pltpu.VMEM((KV_WIDTH, WINDOW + tile), BF16),
            pltpu.VMEM((SUBLANES + tile, LRU_WIDTH), F32),
            pltpu.VMEM((1, LRU_WIDTH), F32),
            pltpu.VMEM((tile, D_MODEL), F32),
            pltpu.VMEM((tile, D_FF), BF16),
        ],
        compiler_params=pltpu.CompilerParams(
            dimension_semantics=("arbitrary",), vmem_limit_bytes=VMEM_LIMIT_BYTES),
        name="mix_prompt",
    )(w["sinks"], x1, z, *_mix_weights(w))


def _attn_sample_kernel(layer, sinks_ref, zq_ref, kvnew_ref, ck_ref, cv_ref, attn_ref, nk_ref, nv_ref):
    dec_seq, g = zq_ref.shape[0], zq_ref.shape[1]
    rows = dec_seq * g
    cache_len = ck_ref.shape[1]
    zq = zq_ref[...].reshape(rows, zq_ref.shape[2])
    q_exp = _expand_q(zq[:, 0:ATTN_WIDTH], rows)
    k_new = zq[:, O_K:O_V].astype(BF16)
    v_new = zq[:, O_V:O_XR].astype(BF16)
    k_cache = ck_ref[...].reshape(g * cache_len, KV_WIDTH).astype(BF16)
    v_cache = cv_ref[...].reshape(g * cache_len, KV_WIDTH).astype(BF16)
    contract_last = (((1,), (1,)), ((), ()))
    s_c = lax.dot_general(q_exp, k_cache, contract_last, preferred_element_type=F32)
    s_n = lax.dot_general(q_exp, k_new, contract_last, preferred_element_type=F32)

    def row_ids(shape):
        r = lax.broadcasted_iota(jnp.int32, shape, 0)
        return (r % rows) // g, r % g

    t_q, b_q = row_ids(s_c.shape)
    c = lax.broadcasted_iota(jnp.int32, s_c.shape, 1)
    mask_c = ((c // cache_len) == b_q) & ((c % cache_len) > t_q)
    t_q, b_q = row_ids(s_n.shape)
    c = lax.broadcasted_iota(jnp.int32, s_n.shape, 1)
    mask_n = ((c % g) == b_q) & ((c // g) <= t_q)
    s_c = jnp.where(mask_c, s_c, -jnp.inf)
    s_n = jnp.where(mask_n, s_n, -jnp.inf)

    sink = _sink_column(sinks_ref, layer, rows)
    m = jnp.maximum(jnp.maximum(jnp.max(s_c, axis=-1, keepdims=True),
                                jnp.max(s_n, axis=-1, keepdims=True)), sink)
    p_c = jnp.exp(s_c - m)
    p_n = jnp.exp(s_n - m)
    denom = (jnp.sum(p_c, axis=-1, keepdims=True) + jnp.sum(p_n, axis=-1, keepdims=True)
             + jnp.exp(sink - m))
    o = (jnp.dot(p_c.astype(BF16), v_cache, preferred_element_type=F32)
         + jnp.dot(p_n.astype(BF16), v_new, preferred_element_type=F32)) / denom
    attn_ref[...] = _collect_heads(o, rows).reshape(dec_seq, g, ATTN_WIDTH)

    nk_ref[:, 0:cache_len - dec_seq, :] = ck_ref[:, dec_seq:, :]
    nv_ref[:, 0:cache_len - dec_seq, :] = cv_ref[:, dec_seq:, :]
    nk_ref[:, cache_len - dec_seq:, :] = kvnew_ref[:, :, 0:KV_WIDTH]
    nv_ref[:, cache_len - dec_seq:, :] = kvnew_ref[:, :, KV_WIDTH:]


def _attn_sample(z3, kvnew, cache_k, cache_v, sinks, layer):
    dec_seq, dec_batch, _ = z3.shape
    cache_len = cache_k.shape[2]
    g = SAMPLE_GROUP
    cache_spec = pl.BlockSpec((None, g, cache_len, KV_WIDTH), lambda i: (layer, i, 0, 0))
    new_cache_spec = pl.BlockSpec((g, cache_len, KV_WIDTH), lambda i: (i, 0, 0))
    return pl.pallas_call(
        functools.partial(_attn_sample_kernel, layer),
        grid=(dec_batch // g,),
        in_specs=[
            pl.BlockSpec(memory_space=pltpu.SMEM),
            pl.BlockSpec((dec_seq, g, O_XR), lambda i: (0, i, 0)),
            pl.BlockSpec((g, dec_seq, 2 * KV_WIDTH), lambda i: (i, 0, 0)),
            cache_spec, cache_spec,
        ],
        out_specs=[
            pl.BlockSpec((dec_seq, g, ATTN_WIDTH), lambda i: (0, i, 0)),
            new_cache_spec, new_cache_spec,
        ],
        out_shape=[
            jax.ShapeDtypeStruct((dec_seq, dec_batch, ATTN_WIDTH), F32),
            jax.ShapeDtypeStruct(cache_k.shape[1:], F32),
            jax.ShapeDtypeStruct(cache_v.shape[1:], F32),
        ],
        compiler_params=pltpu.CompilerParams(
            dimension_semantics=("arbitrary",), vmem_limit_bytes=VMEM_LIMIT_BYTES),
        name="attn_sample",
    )(sinks, z3, kvnew, cache_k, cache_v)


def _mix_sample_kernel(x1_ref, z_ref, attn_ref, sconv_ref, sh_ref, conv_w_ref, conv_b_ref, wgate_ref,
                       ba_ref, bx_ref, lam_ref, g_attn_ref, g_lru_ref, wo_ref, g_postmix_ref,
                       g_pre2_ref, wg_ref, wu_ref, wd_ref, g_post2_ref,
                       x3_ref, nconv_ref, nh_ref, act_scr):
    nb = sh_ref.shape[0]
    dec_seq = x1_ref.shape[0] // nb
    xr = z_ref[:, O_XR:O_YG]
    xp = [sconv_ref[j] for j in range(CONV_W - 1)] + [xr[nb * t:nb * (t + 1)] for t in range(dec_seq)]
    xc_steps = []
    for t in range(dec_seq):
        acc = conv_b_ref[...] + xp[t] * conv_w_ref[0:1, :]
        for j in range(1, CONV_W):
            acc = acc + xp[t + j] * conv_w_ref[j:j + 1, :]
        xc_steps.append(acc)
    for j in range(CONV_W - 1):
        nconv_ref[j] = xp[dec_seq + j]
    xc = jnp.concatenate(xc_steps, axis=0)
    gates = _lru_gates(xc, wgate_ref)
    a, u = _lru_coeffs(xc, gates[:, :LRU_WIDTH], gates[:, LRU_WIDTH:], ba_ref[...], bx_ref[...],
                       lam_ref[...])
    h = sh_ref[...]
    hs_steps = []
    for t in range(dec_seq):
        h = a[nb * t:nb * (t + 1)] * h + u[nb * t:nb * (t + 1)]
        hs_steps.append(h)
    nh_ref[...] = h
    hs = jnp.concatenate(hs_steps, axis=0)
    x2 = _mix_out(x1_ref[...], attn_ref[...], hs, z_ref[:, O_YG:], g_attn_ref[...], g_lru_ref[...],
                  wo_ref, g_postmix_ref[...])
    x3_ref[...] = _half_ffn(x2, g_pre2_ref[...], wg_ref, wu_ref, wd_ref, g_post2_ref[...], act_scr)


def _mix_sample(x1, z, attn, sconv, state_h, w, layer):
    n = x1.shape[0]
    nb = state_h.shape[1]
    full = lambda *shape: pl.BlockSpec(shape, lambda i: (0,) * len(shape))
    return pl.pallas_call(
        _mix_sample_kernel,
        grid=(1,),
        in_specs=[
            full(n, D_MODEL), full(n, IN_WIDTH), full(n, ATTN_WIDTH),
            full(CONV_W - 1, nb, LRU_WIDTH), _layer_spec((nb, LRU_WIDTH), layer, 1),
        ] + _mix_weight_specs(layer, 1),
        out_specs=[full(n, D_MODEL), full(CONV_W - 1, nb, LRU_WIDTH), full(nb, LRU_WIDTH)],
        out_shape=[
            jax.ShapeDtypeStruct((n, D_MODEL), F32),
            jax.ShapeDtypeStruct((CONV_W - 1, nb, LRU_WIDTH), F32),
            jax.ShapeDtypeStruct((nb, LRU_WIDTH), F32),
        ],
        scratch_shapes=[pltpu.VMEM((n, D_FF), BF16)],
        compiler_params=pltpu.CompilerParams(
            dimension_semantics=("arbitrary",), vmem_limit_bytes=VMEM_LIMIT_BYTES),
        name="mix_sample",
    )(x1, z, attn, sconv, state_h, *_mix_weights(w))


def _rope_tables(pos):
    half = ROT_DIM // 2
    inv = ROPE_THETA ** (-(jnp.arange(half, dtype=F32) * 2.0) / ROT_DIM)
    ang = pos.astype(F32)[:, None] * inv[None, :]
    cos, sin = jnp.cos(ang), jnp.sin(ang)
    n = pos.shape[0]
    rest = jnp.zeros((n, HEAD_DIM - ROT_DIM), F32)
    zero = jnp.zeros((n, half), F32)
    per_head = lambda parts: jnp.tile(jnp.concatenate(parts, axis=-1), (1, LANES // HEAD_DIM))
    return (per_head([cos, cos, rest + 1.0]), per_head([-sin, zero, rest]), per_head([zero, sin, rest]))


def _block_diag(w):
    eye = jnp.eye(LRU_HEADS, dtype=w.dtype)
    return jnp.einsum("lhij,hg->lhigj", w, eye).reshape(w.shape[0], LRU_WIDTH, LRU_WIDTH)


def kernel(x_prompt, x_sample, cache_k, cache_v, state_conv, state_h, norm_pre_ffn1, ffn1_w_gate, ffn1_w_up, ffn1_w_down, norm_post_ffn1, norm_pre_mix, w_in, sinks, conv_w, conv_b, gate_a_w, gate_a_b, gate_x_w, gate_x_b, lru_lambda, attn_out_norm, lru_out_norm, w_o, norm_post_mix, norm_pre_ffn2, ffn2_w_gate, ffn2_w_up, ffn2_w_down, norm_post_ffn2):
    batch, seq, _ = x_prompt.shape
    dec_batch, dec_seq, _ = x_sample.shape
    depth = w_in.shape[0]
    cache_len = cache_k.shape[2]
    n_s = dec_seq * dec_batch
    assert seq % FFN_TILE == 0 and n_s % FFN_TILE == 0
    assert seq % MIX_TILE == 0 and MIX_TILE % WINDOW == 0 and dec_batch % SAMPLE_GROUP == 0

    tabs_p = _rope_tables(jnp.arange(seq))
    tabs_s = _rope_tables(jnp.repeat(PAST_LEN + jnp.arange(dec_seq), dec_batch))
    xp = x_prompt.reshape(batch * seq, D_MODEL)
    xs = jnp.swapaxes(x_sample, 0, 1).reshape(n_s, D_MODEL)
    rows = lambda v: v.reshape(depth, 1, -1)

    w = dict(
        g_pre1=rows(norm_pre_ffn1), wg1=_to_bf16(ffn1_w_gate), wu1=_to_bf16(ffn1_w_up),
        wd1=_to_bf16(ffn1_w_down), g_post1=rows(norm_post_ffn1), g_mix=rows(norm_pre_mix),
        w_in=_to_bf16(w_in), sinks=sinks, conv_w=conv_w, conv_b=rows(conv_b),
        w_gate=jnp.concatenate([_block_diag(gate_a_w), _block_diag(gate_x_w)], axis=2).astype(BF16),
        gate_a_b=rows(gate_a_b), gate_x_b=rows(gate_x_b), lam=rows(lru_lambda),
        g_attn=rows(attn_out_norm), g_lru=rows(lru_out_norm), w_o=_to_bf16(w_o),
        g_postmix=rows(norm_post_mix), g_pre2=rows(norm_pre_ffn2), wg2=_to_bf16(ffn2_w_gate),
        wu2=_to_bf16(ffn2_w_up), wd2=_to_bf16(ffn2_w_down), g_post2=rows(norm_post_ffn2),
    )
    ck = cache_k.reshape(depth, dec_batch, cache_len, KV_WIDTH)
    cv = cache_v.reshape(depth, dec_batch, cache_len, KV_WIDTH)

    outs = [[] for _ in range(8)]
    for l in range(depth):
        x1p, zp = _ffn_in(xp, tabs_p, seq // FFN_TILE, w, l, FFN_TILE)
        xp, nk, nv, nc, nh = _mix_prompt(x1p, zp, w, l, batch, seq, MIX_TILE)

        x1s, zs = _ffn_in(xs, tabs_s, n_s // FFN_TILE, w, l, FFN_TILE)
        z3 = zs.reshape(dec_seq, dec_batch, IN_WIDTH)
        kvnew = jnp.swapaxes(z3[:, :, O_K:O_XR], 0, 1)
        attn_s, nks, nvs = _attn_sample(z3, kvnew, ck, cv, sinks, l)
        xs, ncs, nhs = _mix_sample(x1s, zs, attn_s.reshape(n_s, ATTN_WIDTH),
                                   jnp.swapaxes(state_conv[l], 0, 1), state_h, w, l)
        kv_p = (batch, WINDOW, N_KV_HEADS, HEAD_DIM)
        kv_s = (dec_batch, cache_len, N_KV_HEADS, HEAD_DIM)
        for acc, val in zip(outs, (nk.reshape(kv_p), nv.reshape(kv_p), nc,
                                   nh.reshape(batch, LRU_WIDTH), nks.reshape(kv_s),
                                   nvs.reshape(kv_s), jnp.swapaxes(ncs, 0, 1), nhs)):
            acc.append(val)

    y_prompt = xp.reshape(batch, seq, D_MODEL)
    y_sample = jnp.swapaxes(xs.reshape(dec_seq, dec_batch, D_MODEL), 0, 1)
    return (y_prompt, y_sample) + tuple(jnp.stack(o) for o in outs)
```

```python
import functools

import jax
import jax.numpy as jnp
import numpy as np
from jax import lax
from jax.experimental import pallas as pl
from jax.experimental.pallas import tpu as pltpu

D_MODEL = 1024
PAST_LEN = 16384
N_HEADS = 8
HEAD_DIM = 64
N_KV_HEADS = 2
GROUP = N_HEADS // N_KV_HEADS
ATTN_WIDTH = N_HEADS * HEAD_DIM
KV_WIDTH = N_KV_HEADS * HEAD_DIM
WINDOW = 128
ROPE_THETA = 500000.0
ROT_DIM = HEAD_DIM // 4
LRU_WIDTH = D_MODEL // 2
LRU_HEADS = 8
LRU_BLOCK = LRU_WIDTH // LRU_HEADS
CONV_W = 4
LRU_C = 8.0
D_FF = 2816
IN_WIDTH = ATTN_WIDTH + 2 * KV_WIDTH + 2 * LRU_WIDTH
EPS = 1e-6

LANES = 128
SUBLANES = 8
FF_CHUNK = 256
N_FF_CHUNKS = D_FF // FF_CHUNK
FFN_TILE = 512
MIX_TILE = 512
SAMPLE_GROUP = 8
CAST_ROWS = 512
VMEM_LIMIT_BYTES = 56 * 1024 * 1024

O_K = ATTN_WIDTH
O_V = O_K + KV_WIDTH
O_XR = O_V + KV_WIDTH
O_YG = O_XR + LRU_WIDTH

F32 = jnp.float32
BF16 = jnp.bfloat16


def _rmsnorm(x, g):
    return (x * lax.rsqrt(jnp.mean(x * x, axis=-1, keepdims=True) + EPS)) * g


def _half_ffn_steps(x, g_pre, wg_ref, wu_ref, wd_ref, g_post, act_scr):
    h = _rmsnorm(x, g_pre).astype(BF16)
    for c in range(N_FF_CHUNKS):
        cols = slice(FF_CHUNK * c, FF_CHUNK * (c + 1))
        g = jnp.dot(h, wg_ref[:, cols], preferred_element_type=F32)
        u = jnp.dot(h, wu_ref[:, cols], preferred_element_type=F32)
        act_scr[:, cols] = ((g * jax.nn.sigmoid(g)) * u).astype(BF16)
        yield
    ys = []
    for c in range(D_MODEL // FF_CHUNK):
        ys.append(jnp.dot(act_scr[...], wd_ref[:, FF_CHUNK * c:FF_CHUNK * (c + 1)],
                          preferred_element_type=F32))
        yield
    return x + _rmsnorm(jnp.concatenate(ys, axis=1), 0.5 * g_post)


def _advance(gen, done):
    if gen in done:
        return
    try:
        next(gen)
    except StopIteration as stop:
        done[gen] = stop.value


def _interleave(first, second):
    done = {}
    while len(done) < 2:
        _advance(first, done)
        _advance(second, done)
    return done[first], done[second]


def _half_ffn(*args):
    done = {}
    gen = _half_ffn_steps(*args)
    while gen not in done:
        _advance(gen, done)
    return done[gen]


def _gelu_tanh(x):
    c = np.float32(np.sqrt(2.0 / np.pi))
    return x * (0.5 * (1.0 + jnp.tanh(c * (x + 0.044715 * (x * x * x)))))


def _softplus(x):
    return jnp.maximum(x, 0.0) + jnp.log1p(jnp.exp(-jnp.abs(x)))


def _lru_gates(xc, wgate_ref):
    return jnp.dot(xc.astype(BF16), wgate_ref[...], preferred_element_type=F32)


def _lru_coeffs(xc, gate_a, gate_x, ba, bx, lam):
    r = jax.nn.sigmoid(gate_a + ba)
    ig = jax.nn.sigmoid(gate_x + bx)
    log_a = (-LRU_C * r) * _softplus(-lam)
    a = jnp.exp(log_a)
    v = 1.0 - a * a
    u = jnp.where(v > 0.0, v * lax.rsqrt(v), 0.0) * (ig * xc)
    return a, u


def _scan_rows(a, u):
    n = a.shape[0]
    row = lax.broadcasted_iota(jnp.int32, a.shape, 0)
    d = 1
    while d < n:
        keep = row >= d
        u = jnp.where(keep, a * pltpu.roll(u, d, 0) + u, u)
        if 2 * d < n:
            a = jnp.where(keep, a * pltpu.roll(a, d, 0), a)
        d *= 2
    return u


def _scan_rows_carry(a, u, h0):
    n, c = a.shape
    groups = n // SUBLANES
    a3 = a.reshape(groups, SUBLANES, c)
    u3 = u.reshape(groups, SUBLANES, c)
    sub = lax.broadcasted_iota(jnp.int32, a3.shape, 1)
    d = 1
    while d < SUBLANES:
        keep = sub >= d
        u3 = jnp.where(keep, a3 * pltpu.roll(u3, d, 1) + u3, u3)
        a3 = jnp.where(keep, a3 * pltpu.roll(a3, d, 1), a3)
        d *= 2
    a_grp = a3[:, SUBLANES - 1, :]
    h_grp = u3[:, SUBLANES - 1, :]
    first = lax.broadcasted_iota(jnp.int32, a_grp.shape, 0) == 0
    h_end = _scan_rows(a_grp, jnp.where(first, a_grp * h0 + h_grp, h_grp))
    h_in = jnp.where(first, h0, pltpu.roll(h_end, 1, 0))
    hs = u3 + a3 * h_in[:, None, :]
    return hs.reshape(n, c)


def _expand_q(q, rows):
    lane = lax.broadcasted_iota(jnp.int32, (rows, LANES), 1)
    pieces = []
    for h in range(N_HEADS):
        kv = h // GROUP
        src = q[:, LANES * (h // 2):LANES * (h // 2 + 1)]
        if (h % 2) != kv:
            src = pltpu.roll(src, HEAD_DIM, 1)
        in_half = (lane >= HEAD_DIM) if kv == 1 else (lane < HEAD_DIM)
        pieces.append(jnp.where(in_half, src, 0.0))
    return jnp.concatenate(pieces, axis=0).astype(BF16)


def _collect_heads(o, rows):
    lane = lax.broadcasted_iota(jnp.int32, (rows, LANES), 1)
    groups = []
    for j in range(N_HEADS // 2):
        kv = (2 * j) // GROUP
        even = o[rows * (2 * j):rows * (2 * j + 1)]
        odd = o[rows * (2 * j + 1):rows * (2 * j + 2)]
        if kv == 1:
            even = pltpu.roll(even, HEAD_DIM, 1)
        else:
            odd = pltpu.roll(odd, HEAD_DIM, 1)
        groups.append(jnp.where(lane < HEAD_DIM, even, odd))
    return jnp.concatenate(groups, axis=1)


def _collect_heads_t(o_t, rows):
    return jnp.concatenate(
        [o_t[HEAD_DIM * (h // GROUP):HEAD_DIM * (h // GROUP + 1), rows * h:rows * (h + 1)]
         for h in range(N_HEADS)], axis=0)


def _sink_row(sinks_ref, layer, rows):
    head = lax.broadcasted_iota(jnp.int32, (1, N_HEADS * rows), 1) // rows
    out = jnp.full((1, N_HEADS * rows), sinks_ref[layer, 0], F32)
    for h in range(1, N_HEADS):
        out = jnp.where(head == h, sinks_ref[layer, h], out)
    return out


def _sink_column(sinks_ref, layer, rows):
    return jnp.concatenate([jnp.full((rows, 1), sinks_ref[layer, h], F32) for h in range(N_HEADS)],
                           axis=0)


def _mix_normed(attn, hs, yg, g_attn, g_lru):
    lru = hs * _gelu_tanh(yg)
    return jnp.concatenate([_rmsnorm(attn, g_attn), _rmsnorm(lru, g_lru)], axis=-1).astype(BF16)


def _mix_project(x1, mixed, wo_ref, g_postmix):
    y = jnp.dot(mixed, wo_ref[...], preferred_element_type=F32)
    return x1 + _rmsnorm(y, g_postmix)


def _mix_out(x1, attn, hs, yg, g_attn, g_lru, wo_ref, g_postmix):
    return _mix_project(x1, _mix_normed(attn, hs, yg, g_attn, g_lru), wo_ref, g_postmix)


def _layer_spec(shape, layer, grid_rank):
    zeros = (0,) * len(shape)
    return pl.BlockSpec((None,) + tuple(shape), lambda *_: (layer,) + zeros,
                        pipeline_mode=pl.Buffered(1))


def _cast_kernel(w_ref, o_ref):
    o_ref[...] = w_ref[...].astype(BF16)


def _to_bf16(w):
    depth, rows, cols = w.shape
    flat = w.reshape(depth * rows, cols)
    out = pl.pallas_call(
        _cast_kernel,
        grid=(depth * rows // CAST_ROWS,),
        in_specs=[pl.BlockSpec((CAST_ROWS, cols), lambda i: (i, 0))],
        out_specs=pl.BlockSpec((CAST_ROWS, cols), lambda i: (i, 0)),
        out_shape=jax.ShapeDtypeStruct(flat.shape, BF16),
        compiler_params=pltpu.CompilerParams(dimension_semantics=("arbitrary",)),
        name="cast_bf16",
    )(flat)
    return out.reshape(depth, rows, cols)


def _ffn_in_kernel(x_ref, cos_ref, sin_up_ref, sin_dn_ref, g_pre_ref, wg_ref, wu_ref, wd_ref,
                   g_post_ref, g_mix_ref, win_ref, x1_ref, z_ref, act_scr):
    x1 = _half_ffn(x_ref[...], g_pre_ref[...], wg_ref, wu_ref, wd_ref, g_post_ref[...], act_scr)
    x1_ref[...] = x1
    h = _rmsnorm(x1, g_mix_ref[...]).astype(BF16)
    z = jnp.dot(h, win_ref[...], preferred_element_type=F32)
    cos, sin_up, sin_dn = cos_ref[...], sin_up_ref[...], sin_dn_ref[...]
    half = ROT_DIM // 2
    for j in range(O_V // LANES):
        zj = z[:, LANES * j:LANES * (j + 1)]
        rot = zj * cos + pltpu.roll(zj, LANES - half, 1) * sin_up + pltpu.roll(zj, half, 1) * sin_dn
        if j < ATTN_WIDTH // LANES:
            rot = rot * (HEAD_DIM ** -0.5)
        z_ref[:, LANES * j:LANES * (j + 1)] = rot
    z_ref[:, O_V:] = z[:, O_V:]


def _ffn_in(x, tabs, tab_tiles, w, layer, tile):
    n = x.shape[0]
    lspec = lambda *shape: _layer_spec(shape, layer, 1)
    tab_spec = pl.BlockSpec((tile, LANES), lambda i: (i % tab_tiles, 0))
    return pl.pallas_call(
        _ffn_in_kernel,
        grid=(n // tile,),
        in_specs=[
            pl.BlockSpec((tile, D_MODEL), lambda i: (i, 0)),
            tab_spec, tab_spec, tab_spec,
            lspec(1, D_MODEL), lspec(D_MODEL, D_FF), lspec(D_MODEL, D_FF), lspec(D_FF, D_MODEL),
            lspec(1, D_MODEL), lspec(1, D_MODEL), lspec(D_MODEL, IN_WIDTH),
        ],
        out_specs=[
            pl.BlockSpec((tile, D_MODEL), lambda i: (i, 0)),
            pl.BlockSpec((tile, IN_WIDTH), lambda i: (i, 0)),
        ],
        out_shape=[
            jax.ShapeDtypeStruct((n, D_MODEL), F32),
            jax.ShapeDtypeStruct((n, IN_WIDTH), F32),
        ],
        scratch_shapes=[pltpu.VMEM((tile, D_FF), BF16)],
        compiler_params=pltpu.CompilerParams(
            dimension_semantics=("arbitrary",), vmem_limit_bytes=VMEM_LIMIT_BYTES),
        name="ffn_in",
    )(x, *tabs, w["g_pre1"], w["wg1"], w["wu1"], w["wd1"], w["g_post1"], w["g_mix"], w["w_in"])


def _mix_prompt_kernel(layer, n_seq_tiles, sinks_ref, x1_ref, z_ref, conv_w_ref, conv_b_ref, wgate_ref, ba_ref,
                       bx_ref, lam_ref, g_attn_ref, g_lru_ref, wo_ref, g_postmix_ref, g_pre2_ref,
                       wg_ref, wu_ref, wd_ref, g_post2_ref,
                       x3_ref, nk_ref, nv_ref, nconv_ref, nh_ref,
                       k_scr, vt_scr, conv_scr, h_scr, x2_scr, bias_scr, attn_scr, xc_scr, gates_scr,
                       hs_scr, act_scr):
    tile = x1_ref.shape[0]
    g_idx = pl.program_id(0)
    n_tiles = pl.num_programs(0) - 1
    s_idx = g_idx % n_seq_tiles

    @pl.when(g_idx == 0)
    def _():
        x2_scr[...] = jnp.zeros(x2_scr.shape, F32)
        kj = lax.broadcasted_iota(jnp.int32, (2 * WINDOW, N_HEADS * WINDOW), 0)
        qi = lax.broadcasted_iota(jnp.int32, (2 * WINDOW, N_HEADS * WINDOW), 1) & (WINDOW - 1)
        band = (kj > qi) & (kj <= qi + WINDOW)
        bias_scr[0] = jnp.where(band, 0.0, -jnp.inf)
        bias_scr[1] = jnp.where(band & (kj >= WINDOW), 0.0, -jnp.inf)

    @pl.when(s_idx == 0)
    def _():
        k_scr[0:WINDOW, :] = jnp.zeros((WINDOW, KV_WIDTH), BF16)
        vt_scr[:, 0:WINDOW] = jnp.zeros((KV_WIDTH, WINDOW), BF16)
        conv_scr[0:SUBLANES, :] = jnp.zeros((SUBLANES, LRU_WIDTH), F32)
        h_scr[...] = jnp.zeros((1, LRU_WIDTH), F32)

    k_tile = z_ref[:, O_K:O_V]
    v_tile = z_ref[:, O_V:O_XR]
    xr = z_ref[:, O_XR:O_YG]

    def mixer_steps():
        k_scr[WINDOW:, :] = k_tile.astype(BF16)
        vt_scr[:, WINDOW:] = v_tile.T.astype(BF16)
        sink = _sink_row(sinks_ref, layer, WINDOW)
        for i in range(tile // WINDOW):
            q_exp = _expand_q(z_ref[WINDOW * i:WINDOW * (i + 1), 0:ATTN_WIDTH], WINDOW)
            keys = k_scr[WINDOW * i:WINDOW * (i + 2), :]
            vals_t = vt_scr[:, WINDOW * i:WINDOW * (i + 2)]
            s = lax.dot_general(keys, q_exp, (((1,), (1,)), ((), ())), preferred_element_type=F32)
            s = s + bias_scr[jnp.where(s_idx == 0, 1, 0) if i == 0 else 0]
            m = jnp.maximum(jnp.max(s, axis=0, keepdims=True), sink)
            p = jnp.exp(s - m)
            denom = jnp.sum(p, axis=0, keepdims=True) + jnp.exp(sink - m)
            yield
            o_t = jnp.dot(vals_t, p.astype(BF16), preferred_element_type=F32) * (1.0 / denom)
            attn_scr[WINDOW * i:WINDOW * (i + 1), :] = _collect_heads_t(o_t, WINDOW).T
            yield
        k_scr[0:WINDOW, :] = k_scr[tile:tile + WINDOW, :]
        vt_scr[:, 0:WINDOW] = vt_scr[:, tile:tile + WINDOW]

        conv_scr[SUBLANES:, :] = xr
        xc = conv_b_ref[...] + conv_scr[SUBLANES - 3:SUBLANES - 3 + tile, :] * conv_w_ref[0:1, :]
        for j in range(1, CONV_W):
            xc = xc + conv_scr[SUBLANES - 3 + j:SUBLANES - 3 + j + tile, :] * conv_w_ref[j:j + 1, :]
        conv_scr[0:SUBLANES, :] = conv_scr[tile:tile + SUBLANES, :]
        xc_scr[...] = xc
        gates_scr[...] = _lru_gates(xc, wgate_ref)
        yield

        for j in range(LRU_WIDTH // LANES):
            ch = slice(LANES * j, LANES * (j + 1))
            gate_x_ch = slice(LRU_WIDTH + LANES * j, LRU_WIDTH + LANES * (j + 1))
            a, u = _lru_coeffs(xc_scr[:, ch], gates_scr[:, ch], gates_scr[:, gate_x_ch],
                               ba_ref[:, ch], bx_ref[:, ch], lam_ref[:, ch])
            yield
            hs_scr[:, ch] = _scan_rows_carry(a, u, h_scr[:, ch])
            yield
        h_last = hs_scr[tile - 1:tile, :]
        h_scr[...] = h_last
        mixed = _mix_normed(attn_scr[...], hs_scr[...], z_ref[:, O_YG:], g_attn_ref[...],
                            g_lru_ref[...])
        yield
        x2 = _mix_project(x1_ref[...], mixed, wo_ref, g_postmix_ref[...])
        return x2, h_last

    ffn = _half_ffn_steps(x2_scr[...], g_pre2_ref[...], wg_ref, wu_ref, wd_ref, g_post2_ref[...],
                          act_scr)
    x3, (x2, h_last) = _interleave(ffn, mixer_steps())
    x3_ref[...] = x3
    x2_scr[...] = x2

    @pl.when(g_idx < n_tiles)
    def _():
        nk_ref[0] = k_tile[tile - WINDOW:, :]
        nv_ref[0] = v_tile[tile - WINDOW:, :]
        nconv_ref[0] = xr[tile - (CONV_W - 1):, :]
        nh_ref[0] = h_last


def _mix_weight_specs(layer, grid_rank):
    lspec = lambda *shape: _layer_spec(shape, layer, grid_rank)
    return [
        lspec(CONV_W, LRU_WIDTH), lspec(1, LRU_WIDTH), lspec(LRU_WIDTH, 2 * LRU_WIDTH),
        lspec(1, LRU_WIDTH), lspec(1, LRU_WIDTH), lspec(1, LRU_WIDTH),
        lspec(1, ATTN_WIDTH), lspec(1, LRU_WIDTH), lspec(D_MODEL, D_MODEL), lspec(1, D_MODEL),
        lspec(1, D_MODEL), lspec(D_MODEL, D_FF), lspec(D_MODEL, D_FF), lspec(D_FF, D_MODEL),
        lspec(1, D_MODEL),
    ]


def _mix_weights(w):
    return (w["conv_w"], w["conv_b"], w["w_gate"], w["gate_a_b"], w["gate_x_b"], w["lam"],
            w["g_attn"], w["g_lru"], w["w_o"], w["g_postmix"], w["g_pre2"], w["wg2"], w["wu2"],
            w["wd2"], w["g_post2"])


def _mix_prompt(x1, z, w, layer, batch, seq, tile):
    n_s = seq // tile
    n_tiles = batch * n_s
    mix_tile = lambda g: jnp.minimum(g, n_tiles - 1)
    tok = lambda width: pl.BlockSpec((tile, width), lambda g: (mix_tile(g), 0))
    per_batch = lambda rows, width: pl.BlockSpec((1, rows, width), lambda g: (mix_tile(g) // n_s, 0, 0))
    return pl.pallas_call(
        functools.partial(_mix_prompt_kernel, layer, n_s),
        grid=(n_tiles + 1,),
        in_specs=[pl.BlockSpec(memory_space=pltpu.SMEM), tok(D_MODEL), tok(IN_WIDTH)]
                 + _mix_weight_specs(layer, 1),
        out_specs=[
            pl.BlockSpec((tile, D_MODEL), lambda g: (jnp.maximum(g - 1, 0), 0)),
            per_batch(WINDOW, KV_WIDTH), per_batch(WINDOW, KV_WIDTH),
            per_batch(CONV_W - 1, LRU_WIDTH), per_batch(1, LRU_WIDTH),
        ],
        out_shape=[
            jax.ShapeDtypeStruct((batch * seq, D_MODEL), F32),
            jax.ShapeDtypeStruct((batch, WINDOW, KV_WIDTH), F32),
            jax.ShapeDtypeStruct((batch, WINDOW, KV_WIDTH), F32),
            jax.ShapeDtypeStruct((batch, CONV_W - 1, LRU_WIDTH), F32),
            jax.ShapeDtypeStruct((batch, 1, LRU_WIDTH), F32),
        ],
        scratch_shapes=[
            pltpu.VMEM((WINDOW + tile, KV_WIDTH), BF16),
            pltpu.VMEM((KV_WIDTH, WINDOW + tile), BF16),
            pltpu.VMEM((SUBLANES + tile, LRU_WIDTH), F32),
            pltpu.VMEM((1, LRU_WIDTH), F32),
            pltpu.VMEM((tile, D_MODEL), F32),
            pltpu.VMEM((2, 2 * WINDOW, N_HEADS * WINDOW), F32),
            pltpu.VMEM((tile, ATTN_WIDTH), F32),
            pltpu.VMEM((tile, LRU_WIDTH), F32),
            pltpu.VMEM((tile, 2 * LRU_WIDTH), F32),
            pltpu.VMEM((tile, LRU_WIDTH), F32),
            pltpu.VMEM((tile, D_FF), BF16),
        ],
        compiler_params=pltpu.CompilerParams(
            dimension_semantics=("arbitrary",), vmem_limit_bytes=VMEM_LIMIT_BYTES),
        name="mix_prompt",
    )(w["sinks"], x1, z, *_mix_weights(w))


def _attn_sample_kernel(layer, sinks_ref, zq_ref, kvnew_ref, ck_ref, cv_ref, attn_ref, nk_ref, nv_ref):
    dec_seq, g = zq_ref.shape[0], zq_ref.shape[1]
    rows = dec_seq * g
    cache_len = ck_ref.shape[1]
    zq = zq_ref[...].reshape(rows, zq_ref.shape[2])
    q_exp = _expand_q(zq[:, 0:ATTN_WIDTH], rows)
    k_new = zq[:, O_K:O_V].astype(BF16)
    v_new = zq[:, O_V:O_XR].astype(BF16)
    k_cache = ck_ref[...].reshape(g * cache_len, KV_WIDTH).astype(BF16)
    v_cache = cv_ref[...].reshape(g * cache_len, KV_WIDTH).astype(BF16)
    contract_last = (((1,), (1,)), ((), ()))
    s_c = lax.dot_general(q_exp, k_cache, contract_last, preferred_element_type=F32)
    s_n = lax.dot_general(q_exp, k_new, contract_last, preferred_element_type=F32)

    def row_ids(shape):
        r = lax.broadcasted_iota(jnp.int32, shape, 0)
        return (r % rows) // g, r % g

    t_q, b_q = row_ids(s_c.shape)
    c = lax.broadcasted_iota(jnp.int32, s_c.shape, 1)
    mask_c = ((c // cache_len) == b_q) & ((c % cache_len) > t_q)
    t_q, b_q = row_ids(s_n.shape)
    c = lax.broadcasted_iota(jnp.int32, s_n.shape, 1)
    mask_n = ((c % g) == b_q) & ((c // g) <= t_q)
    s_c = jnp.where(mask_c, s_c, -jnp.inf)
    s_n = jnp.where(mask_n, s_n, -jnp.inf)

    sink = _sink_column(sinks_ref, layer, rows)
    m = jnp.maximum(jnp.maximum(jnp.max(s_c, axis=-1, keepdims=True),
                                jnp.max(s_n, axis=-1, keepdims=True)), sink)
    p_c = jnp.exp(s_c - m)
    p_n = jnp.exp(s_n - m)
    denom = (jnp.sum(p_c, axis=-1, keepdims=True) + jnp.sum(p_n, axis=-1, keepdims=True)
             + jnp.exp(sink - m))
    o = (jnp.dot(p_c.astype(BF16), v_cache, preferred_element_type=F32)
         + jnp.dot(p_n.astype(BF16), v_new, preferred_element_type=F32)) / denom
    attn_ref[...] = _collect_heads(o, rows).reshape(dec_seq, g, ATTN_WIDTH)

    nk_ref[:, 0:cache_len - dec_seq, :] = ck_ref[:, dec_seq:, :]
    nv_ref[:, 0:cache_len - dec_seq, :] = cv_ref[:, dec_seq:, :]
    nk_ref[:, cache_len - dec_seq:, :] = kvnew_ref[:, :, 0:KV_WIDTH]
    nv_ref[:, cache_len - dec_seq:, :] = kvnew_ref[:, :, KV_WIDTH:]


def _attn_sample(z3, kvnew, cache_k, cache_v, sinks, layer):
    dec_seq, dec_batch, _ = z3.shape
    cache_len = cache_k.shape[2]
    g = SAMPLE_GROUP
    cache_spec = pl.BlockSpec((None, g, cache_len, KV_WIDTH), lambda i: (layer, i, 0, 0))
    new_cache_spec = pl.BlockSpec((g, cache_len, KV_WIDTH), lambda i: (i, 0, 0))
    return pl.pallas_call(
        functools.partial(_attn_sample_kernel, layer),
        grid=(dec_batch // g,),
        in_specs=[
            pl.BlockSpec(memory_space=pltpu.SMEM),
            pl.BlockSpec((dec_seq, g, O_XR), lambda i: (0, i, 0)),
            pl.BlockSpec((g, dec_seq, 2 * KV_WIDTH), lambda i: (i, 0, 0)),
            cache_spec, cache_spec,
        ],
        out_specs=[
            pl.BlockSpec((dec_seq, g, ATTN_WIDTH), lambda i: (0, i, 0)),
            new_cache_spec, new_cache_spec,
        ],
        out_shape=[
            jax.ShapeDtypeStruct((dec_seq, dec_batch, ATTN_WIDTH), F32),
            jax.ShapeDtypeStruct(cache_k.shape[1:], F32),
            jax.ShapeDtypeStruct(cache_v.shape[1:], F32),
        ],
        compiler_params=pltpu.CompilerParams(
            dimension_semantics=("arbitrary",), vmem_limit_bytes=VMEM_LIMIT_BYTES),
        name="attn_sample",
    )(sinks, z3, kvnew, cache_k, cache_v)


def _mix_sample_kernel(x1_ref, z_ref, attn_ref, sconv_ref, sh_ref, conv_w_ref, conv_b_ref, wgate_ref,
                       ba_ref, bx_ref, lam_ref, g_attn_ref, g_lru_ref, wo_ref, g_postmix_ref,
                       g_pre2_ref, wg_ref, wu_ref, wd_ref, g_post2_ref,
                       x3_ref, nconv_ref, nh_ref, act_scr):
    nb = sh_ref.shape[0]
    dec_seq = x1_ref.shape[0] // nb
    xr = z_ref[:, O_XR:O_YG]
    xp = [sconv_ref[j] for j in range(CONV_W - 1)] + [xr[nb * t:nb * (t + 1)] for t in range(dec_seq)]
    xc_steps = []
    for t in range(dec_seq):
        acc = conv_b_ref[...] + xp[t] * conv_w_ref[0:1, :]
        for j in range(1, CONV_W):
            acc = acc + xp[t + j] * conv_w_ref[j:j + 1, :]
        xc_steps.append(acc)
    for j in range(CONV_W - 1):
        nconv_ref[j] = xp[dec_seq + j]
    xc = jnp.concatenate(xc_steps, axis=0)
    gates = _lru_gates(xc, wgate_ref)
    a, u = _lru_coeffs(xc, gates[:, :LRU_WIDTH], gates[:, LRU_WIDTH:], ba_ref[...], bx_ref[...],
                       lam_ref[...])
    h = sh_ref[...]
    hs_steps = []
    for t in range(dec_seq):
        h = a[nb * t:nb * (t + 1)] * h + u[nb * t:nb * (t + 1)]
        hs_steps.append(h)
    nh_ref[...] = h
    hs = jnp.concatenate(hs_steps, axis=0)
    x2 = _mix_out(x1_ref[...], attn_ref[...], hs, z_ref[:, O_YG:], g_attn_ref[...], g_lru_ref[...],
                  wo_ref, g_postmix_ref[...])
    x3_ref[...] = _half_ffn(x2, g_pre2_ref[...], wg_ref, wu_ref, wd_ref, g_post2_ref[...], act_scr)


def _mix_sample(x1, z, attn, sconv, state_h, w, layer):
    n = x1.shape[0]
    nb = state_h.shape[1]
    full = lambda *shape: pl.BlockSpec(shape, lambda i: (0,) * len(shape))
    return pl.pallas_call(
        _mix_sample_kernel,
        grid=(1,),
        in_specs=[
            full(n, D_MODEL), full(n, IN_WIDTH), full(n, ATTN_WIDTH),
            full(CONV_W - 1, nb, LRU_WIDTH), _layer_spec((nb, LRU_WIDTH), layer, 1),
        ] + _mix_weight_specs(layer, 1),
        out_specs=[full(n, D_MODEL), full(CONV_W - 1, nb, LRU_WIDTH), full(nb, LRU_WIDTH)],
        out_shape=[
            jax.ShapeDtypeStruct((n, D_MODEL), F32),
            jax.ShapeDtypeStruct((CONV_W - 1, nb, LRU_WIDTH), F32),
            jax.ShapeDtypeStruct((nb, LRU_WIDTH), F32),
        ],
        scratch_shapes=[pltpu.VMEM((n, D_FF), BF16)],
        compiler_params=pltpu.CompilerParams(
            dimension_semantics=("arbitrary",), vmem_limit_bytes=VMEM_LIMIT_BYTES),
        name="mix_sample",
    )(x1, z, attn, sconv, state_h, *_mix_weights(w))


def _rope_tables(pos):
    half = ROT_DIM // 2
    inv = ROPE_THETA ** (-(jnp.arange(half, dtype=F32) * 2.0) / ROT_DIM)
    ang = pos.astype(F32)[:, None] * inv[None, :]
    cos, sin = jnp.cos(ang), jnp.sin(ang)
    n = pos.shape[0]
    rest = jnp.zeros((n, HEAD_DIM - ROT_DIM), F32)
    zero = jnp.zeros((n, half), F32)
    per_head = lambda parts: jnp.tile(jnp.concatenate(parts, axis=-1), (1, LANES // HEAD_DIM))
    return (per_head([cos, cos, rest + 1.0]), per_head([-sin, zero, rest]), per_head([zero, sin, rest]))


def _block_diag(w):
    eye = jnp.eye(LRU_HEADS, dtype=w.dtype)
    return jnp.einsum("lhij,hg->lhigj", w, eye).reshape(w.shape[0], LRU_WIDTH, LRU_WIDTH)


def kernel(x_prompt, x_sample, cache_k, cache_v, state_conv, state_h, norm_pre_ffn1, ffn1_w_gate, ffn1_w_up, ffn1_w_down, norm_post_ffn1, norm_pre_mix, w_in, sinks, conv_w, conv_b, gate_a_w, gate_a_b, gate_x_w, gate_x_b, lru_lambda, attn_out_norm, lru_out_norm, w_o, norm_post_mix, norm_pre_ffn2, ffn2_w_gate, ffn2_w_up, ffn2_w_down, norm_post_ffn2):
    batch, seq, _ = x_prompt.shape
    dec_batch, dec_seq, _ = x_sample.shape
    depth = w_in.shape[0]
    cache_len = cache_k.shape[2]
    n_s = dec_seq * dec_batch
    assert seq % FFN_TILE == 0 and n_s % FFN_TILE == 0
    assert seq % MIX_TILE == 0 and MIX_TILE % WINDOW == 0 and dec_batch % SAMPLE_GROUP == 0

    tabs_p = _rope_tables(jnp.arange(seq))
    tabs_s = _rope_tables(jnp.repeat(PAST_LEN + jnp.arange(dec_seq), dec_batch))
    xp = x_prompt.reshape(batch * seq, D_MODEL)
    xs = jnp.swapaxes(x_sample, 0, 1).reshape(n_s, D_MODEL)
    rows = lambda v: v.reshape(depth, 1, -1)

    w = dict(
        g_pre1=rows(norm_pre_ffn1), wg1=_to_bf16(ffn1_w_gate), wu1=_to_bf16(ffn1_w_up),
        wd1=_to_bf16(ffn1_w_down), g_post1=rows(norm_post_ffn1), g_mix=rows(norm_pre_mix),
        w_in=_to_bf16(w_in), sinks=sinks, conv_w=conv_w, conv_b=rows(conv_b),
        w_gate=jnp.concatenate([_block_diag(gate_a_w), _block_diag(gate_x_w)], axis=2).astype(BF16),
        gate_a_b=rows(gate_a_b), gate_x_b=rows(gate_x_b), lam=rows(lru_lambda),
        g_attn=rows(attn_out_norm), g_lru=rows(lru_out_norm), w_o=_to_bf16(w_o),
        g_postmix=rows(norm_post_mix), g_pre2=rows(norm_pre_ffn2), wg2=_to_bf16(ffn2_w_gate),
        wu2=_to_bf16(ffn2_w_up), wd2=_to_bf16(ffn2_w_down), g_post2=rows(norm_post_ffn2),
    )
    ck = cache_k.reshape(depth, dec_batch, cache_len, KV_WIDTH)
    cv = cache_v.reshape(depth, dec_batch, cache_len, KV_WIDTH)

    outs = [[] for _ in range(8)]
    for l in range(depth):
        x1p, zp = _ffn_in(xp, tabs_p, seq // FFN_TILE, w, l, FFN_TILE)
        xp, nk, nv, nc, nh = _mix_prompt(x1p, zp, w, l, batch, seq, MIX_TILE)

        x1s, zs = _ffn_in(xs, tabs_s, n_s // FFN_TILE, w, l, FFN_TILE)
        z3 = zs.reshape(dec_seq, dec_batch, IN_WIDTH)
        kvnew = jnp.swapaxes(z3[:, :, O_K:O_XR], 0, 1)
        attn_s, nks, nvs = _attn_sample(z3, kvnew, ck, cv, sinks, l)
        xs, ncs, nhs = _mix_sample(x1s, zs, attn_s.reshape(n_s, ATTN_WIDTH),
                                   jnp.swapaxes(state_conv[l], 0, 1), state_h, w, l)
        kv_p = (batch, WINDOW, N_KV_HEADS, HEAD_DIM)
        kv_s = (dec_batch, cache_len, N_KV_HEADS, HEAD_DIM)
        for acc, val in zip(outs, (nk.reshape(kv_p), nv.reshape(kv_p), nc,
                                   nh.reshape(batch, LRU_WIDTH), nks.reshape(kv_s),
                                   nvs.reshape(kv_s), jnp.swapaxes(ncs, 0, 1), nhs)):
            acc.append(val)

    y_prompt = xp.reshape(batch, seq, D_MODEL)
    y_sample = jnp.swapaxes(xs.reshape(dec_seq, dec_batch, D_MODEL), 0, 1)
    return (y_prompt, y_sample) + tuple(jnp.stack(o) for o in outs)
```

```python
import functools

import jax
import jax.numpy as jnp
import numpy as np
from jax import lax
from jax.experimental import pallas as pl
from jax.experimental.pallas import tpu as pltpu

D_MODEL = 1024
PAST_LEN = 16384
N_HEADS = 8
HEAD_DIM = 64
N_KV_HEADS = 2
GROUP = N_HEADS // N_KV_HEADS
ATTN_WIDTH = N_HEADS * HEAD_DIM
KV_WIDTH = N_KV_HEADS * HEAD_DIM
WINDOW = 128
ROPE_THETA = 500000.0
ROT_DIM = HEAD_DIM // 4
LRU_WIDTH = D_MODEL // 2
LRU_HEADS = 8
LRU_BLOCK = LRU_WIDTH // LRU_HEADS
CONV_W = 4
LRU_C = 8.0
D_FF = 2816
IN_WIDTH = ATTN_WIDTH + 2 * KV_WIDTH + 2 * LRU_WIDTH
EPS = 1e-6

LANES = 128
SUBLANES = 8
FF_CHUNK = 256
N_FF_CHUNKS = D_FF // FF_CHUNK
FFN_TILE = 512
MIX_TILE = 512
SAMPLE_GROUP = 8
SHIFT_GROUP = 16
CAST_ROWS = 512
VMEM_LIMIT_BYTES = 56 * 1024 * 1024

O_K = ATTN_WIDTH
O_V = O_K + KV_WIDTH
O_XR = O_V + KV_WIDTH
O_YG = O_XR + LRU_WIDTH

F32 = jnp.float32
BF16 = jnp.bfloat16


def _rmsnorm(x, g):
    return (x * lax.rsqrt(jnp.mean(x * x, axis=-1, keepdims=True) + EPS)) * g


def _half_ffn_steps(x, g_pre, wg_ref, wu_ref, wd_ref, g_post, act_scr):
    h = _rmsnorm(x, g_pre).astype(BF16)
    for c in range(N_FF_CHUNKS):
        cols = slice(FF_CHUNK * c, FF_CHUNK * (c + 1))
        g = jnp.dot(h, wg_ref[:, cols], preferred_element_type=F32)
        u = jnp.dot(h, wu_ref[:, cols], preferred_element_type=F32)
        act_scr[:, cols] = ((g * jax.nn.sigmoid(g)) * u).astype(BF16)
        yield
    ys = []
    for c in range(D_MODEL // FF_CHUNK):
        ys.append(jnp.dot(act_scr[...], wd_ref[:, FF_CHUNK * c:FF_CHUNK * (c + 1)],
                          preferred_element_type=F32))
        yield
    return x + _rmsnorm(jnp.concatenate(ys, axis=1), 0.5 * g_post)


def _advance(gen, done):
    if gen in done:
        return
    try:
        next(gen)
    except StopIteration as stop:
        done[gen] = stop.value


def _interleave(first, second):
    done = {}
    while len(done) < 2:
        _advance(first, done)
        _advance(second, done)
    return done[first], done[second]


def _run(gen):
    done = {}
    while gen not in done:
        _advance(gen, done)
    return done[gen]


def _half_ffn(*args):
    return _run(_half_ffn_steps(*args))


def _gelu_tanh(x):
    c = np.float32(np.sqrt(2.0 / np.pi))
    return x * (0.5 * (1.0 + jnp.tanh(c * (x + 0.044715 * (x * x * x)))))


def _softplus(x):
    return jnp.maximum(x, 0.0) + jnp.log1p(jnp.exp(-jnp.abs(x)))


def _lru_gates(xc, wgate_ref):
    return jnp.dot(xc.astype(BF16), wgate_ref[...], preferred_element_type=F32)


def _lru_coeffs(xc, gate_a, gate_x, ba, bx, lam):
    r = jax.nn.sigmoid(gate_a + ba)
    ig = jax.nn.sigmoid(gate_x + bx)
    log_a = (-LRU_C * r) * _softplus(-lam)
    a = jnp.exp(log_a)
    v = 1.0 - a * a
    u = jnp.where(v > 0.0, v * lax.rsqrt(v), 0.0) * (ig * xc)
    return a, u


def _scan_rows(a, u):
    n = a.shape[0]
    row = lax.broadcasted_iota(jnp.int32, a.shape, 0)
    d = 1
    while d < n:
        keep = row >= d
        u = jnp.where(keep, a * pltpu.roll(u, d, 0) + u, u)
        if 2 * d < n:
            a = jnp.where(keep, a * pltpu.roll(a, d, 0), a)
        d *= 2
    return u


def _scan_rows_carry(a, u, h0):
    n, c = a.shape
    groups = n // SUBLANES
    a3 = a.reshape(groups, SUBLANES, c)
    u3 = u.reshape(groups, SUBLANES, c)
    sub = lax.broadcasted_iota(jnp.int32, a3.shape, 1)
    d = 1
    while d < SUBLANES:
        keep = sub >= d
        u3 = jnp.where(keep, a3 * pltpu.roll(u3, d, 1) + u3, u3)
        a3 = jnp.where(keep, a3 * pltpu.roll(a3, d, 1), a3)
        d *= 2
    a_grp = a3[:, SUBLANES - 1, :]
    h_grp = u3[:, SUBLANES - 1, :]
    first = lax.broadcasted_iota(jnp.int32, a_grp.shape, 0) == 0
    h_end = _scan_rows(a_grp, jnp.where(first, a_grp * h0 + h_grp, h_grp))
    h_in = jnp.where(first, h0, pltpu.roll(h_end, 1, 0))
    hs = u3 + a3 * h_in[:, None, :]
    return hs.reshape(n, c)


def _expand_q(q, rows):
    lane = lax.broadcasted_iota(jnp.int32, (rows, LANES), 1)
    pieces = []
    for h in range(N_HEADS):
        kv = h // GROUP
        src = q[:, LANES * (h // 2):LANES * (h // 2 + 1)]
        if (h % 2) != kv:
            src = pltpu.roll(src, HEAD_DIM, 1)
        in_half = (lane >= HEAD_DIM) if kv == 1 else (lane < HEAD_DIM)
        pieces.append(jnp.where(in_half, src, 0.0))
    return jnp.concatenate(pieces, axis=0).astype(BF16)


def _collect_heads(o, rows):
    lane = lax.broadcasted_iota(jnp.int32, (rows, LANES), 1)
    groups = []
    for j in range(N_HEADS // 2):
        kv = (2 * j) // GROUP
        even = o[rows * (2 * j):rows * (2 * j + 1)]
        odd = o[rows * (2 * j + 1):rows * (2 * j + 2)]
        if kv == 1:
            even = pltpu.roll(even, HEAD_DIM, 1)
        else:
            odd = pltpu.roll(odd, HEAD_DIM, 1)
        groups.append(jnp.where(lane < HEAD_DIM, even, odd))
    return jnp.concatenate(groups, axis=1)


def _collect_heads_t(o_t, rows):
    return jnp.concatenate(
        [o_t[HEAD_DIM * (h // GROUP):HEAD_DIM * (h // GROUP + 1), rows * h:rows * (h + 1)]
         for h in range(N_HEADS)], axis=0)


def _sink_row(sinks_ref, layer, rows):
    head = lax.broadcasted_iota(jnp.int32, (1, N_HEADS * rows), 1) // rows
    out = jnp.full((1, N_HEADS * rows), sinks_ref[layer, 0], F32)
    for h in range(1, N_HEADS):
        out = jnp.where(head == h, sinks_ref[layer, h], out)
    return out


def _sink_column(sinks_ref, layer, rows):
    return jnp.concatenate([jnp.full((rows, 1), sinks_ref[layer, h], F32) for h in range(N_HEADS)],
                           axis=0)


def _mix_normed(attn, hs, yg, g_attn, g_lru):
    lru = hs * _gelu_tanh(yg)
    return jnp.concatenate([_rmsnorm(attn, g_attn), _rmsnorm(lru, g_lru)], axis=-1).astype(BF16)


def _mix_project(x1, mixed, wo_ref, g_postmix):
    y = jnp.dot(mixed, wo_ref[...], preferred_element_type=F32)
    return x1 + _rmsnorm(y, g_postmix)


def _mix_out(x1, attn, hs, yg, g_attn, g_lru, wo_ref, g_postmix):
    return _mix_project(x1, _mix_normed(attn, hs, yg, g_attn, g_lru), wo_ref, g_postmix)


def _layer_spec(shape, layer, grid_rank):
    zeros = (0,) * len(shape)
    return pl.BlockSpec((None,) + tuple(shape), lambda *_: (layer,) + zeros,
                        pipeline_mode=pl.Buffered(1))


def _cast_kernel(w_ref, o_ref):
    o_ref[...] = w_ref[...].astype(BF16)


def _to_bf16(w):
    depth, rows, cols = w.shape
    flat = w.reshape(depth * rows, cols)
    out = pl.pallas_call(
        _cast_kernel,
        grid=(depth * rows // CAST_ROWS,),
        in_specs=[pl.BlockSpec((CAST_ROWS, cols), lambda i: (i, 0))],
        out_specs=pl.BlockSpec((CAST_ROWS, cols), lambda i: (i, 0)),
        out_shape=jax.ShapeDtypeStruct(flat.shape, BF16),
        compiler_params=pltpu.CompilerParams(dimension_semantics=("arbitrary",)),
        name="cast_bf16",
    )(flat)
    return out.reshape(depth, rows, cols)


def _in_proj_steps(x1, g_mix, win_ref, cos, sin_up, sin_dn, z_ref):
    h = _rmsnorm(x1, g_mix).astype(BF16)
    half = ROT_DIM // 2
    for c in range(IN_WIDTH // FF_CHUNK):
        zc = jnp.dot(h, win_ref[:, FF_CHUNK * c:FF_CHUNK * (c + 1)], preferred_element_type=F32)
        for k in range(FF_CHUNK // LANES):
            j = (FF_CHUNK // LANES) * c + k
            zj = zc[:, LANES * k:LANES * (k + 1)]
            if j < O_V // LANES:
                zj = (zj * cos + pltpu.roll(zj, LANES - half, 1) * sin_up
                      + pltpu.roll(zj, half, 1) * sin_dn)
            if j < ATTN_WIDTH // LANES:
                zj = zj * (HEAD_DIM ** -0.5)
            z_ref[:, LANES * j:LANES * (j + 1)] = zj
        yield


def _ffn_in_kernel(skewed, x_ref, cos_ref, sin_up_ref, sin_dn_ref, g_pre_ref, wg_ref, wu_ref, wd_ref,
                   g_post_ref, g_mix_ref, win_ref, x1_ref, z_ref, act_scr, *x1_scr):
    def ffn_steps():
        return _half_ffn_steps(x_ref[...], g_pre_ref[...], wg_ref, wu_ref, wd_ref, g_post_ref[...],
                               act_scr)

    def in_proj_steps(x1):
        return _in_proj_steps(x1, g_mix_ref[...], win_ref, cos_ref[...], sin_up_ref[...],
                              sin_dn_ref[...], z_ref)

    if not skewed:
        x1 = _run(ffn_steps())
        x1_ref[...] = x1
        _run(in_proj_steps(x1))
        return
    (x1_scr,) = x1_scr
    g_idx = pl.program_id(0)
    n_tiles = pl.num_programs(0) - 1

    @pl.when(g_idx == 0)
    def _():
        x1_scr[...] = jnp.zeros(x1_scr.shape, F32)

    @pl.when(g_idx < n_tiles)
    def _():
        x1, _ = _interleave(ffn_steps(), in_proj_steps(x1_scr[...]))
        x1_ref[...] = x1
        x1_scr[...] = x1

    @pl.when(g_idx == n_tiles)
    def _():
        _run(in_proj_steps(x1_scr[...]))


def _ffn_in(x, tabs, tab_tiles, w, layer, tile):
    n = x.shape[0] // tile
    skewed = n > 1
    ffn_tile = (lambda g: jnp.minimum(g, n - 1)) if skewed else (lambda g: g)
    proj_tile = (lambda g: jnp.maximum(g - 1, 0)) if skewed else (lambda g: g)
    lspec = lambda *shape: _layer_spec(shape, layer, 1)
    tab_spec = pl.BlockSpec((tile, LANES), lambda g: (proj_tile(g) % tab_tiles, 0))
    return pl.pallas_call(
        functools.partial(_ffn_in_kernel, skewed),
        grid=(n + 1 if skewed else n,),
        in_specs=[
            pl.BlockSpec((tile, D_MODEL), lambda g: (ffn_tile(g), 0)),
            tab_spec, tab_spec, tab_spec,
            lspec(1, D_MODEL), lspec(D_MODEL, D_FF), lspec(D_MODEL, D_FF), lspec(D_FF, D_MODEL),
            lspec(1, D_MODEL), lspec(1, D_MODEL), lspec(D_MODEL, IN_WIDTH),
        ],
        out_specs=[
            pl.BlockSpec((tile, D_MODEL), lambda g: (ffn_tile(g), 0)),
            pl.BlockSpec((tile, IN_WIDTH), lambda g: (proj_tile(g), 0)),
        ],
        out_shape=[
            jax.ShapeDtypeStruct((n * tile, D_MODEL), F32),
            jax.ShapeDtypeStruct((n * tile, IN_WIDTH), F32),
        ],
        scratch_shapes=[pltpu.VMEM((tile, D_FF), BF16)]
                       + ([pltpu.VMEM((tile, D_MODEL), F32)] if skewed else []),
        compiler_params=pltpu.CompilerParams(
            dimension_semantics=("arbitrary",), vmem_limit_bytes=VMEM_LIMIT_BYTES),
        name="ffn_in",
    )(x, *tabs, w["g_pre1"], w["wg1"], w["wu1"], w["wd1"], w["g_post1"], w["g_mix"], w["w_in"])


def _mix_prompt_kernel(layer, n_seq_tiles, sinks_ref, x1_ref, z_ref, conv_w_ref, conv_b_ref, wgate_ref, ba_ref,
                       bx_ref, lam_ref, g_attn_ref, g_lru_ref, wo_ref, g_postmix_ref, g_pre2_ref,
                       wg_ref, wu_ref, wd_ref, g_post2_ref,
                       x3_ref, nk_ref, nv_ref, nconv_ref, nh_ref,
                       k_scr, vt_scr, conv_scr, h_scr, x2_scr, bias_scr, attn_scr, xc_scr, gates_scr,
                       hs_scr, act_scr):
    tile = x1_ref.shape[0]
    g_idx = pl.program_id(0)
    n_tiles = pl.num_programs(0) - 1
    s_idx = g_idx % n_seq_tiles

    @pl.when(g_idx == 0)
    def _():
        x2_scr[...] = jnp.zeros(x2_scr.shape, F32)
        kj = lax.broadcasted_iota(jnp.int32, (2 * WINDOW, N_HEADS * WINDOW), 0)
        qi = lax.broadcasted_iota(jnp.int32, (2 * WINDOW, N_HEADS * WINDOW), 1) & (WINDOW - 1)
        band = (kj > qi) & (kj <= qi + WINDOW)
        bias_scr[0] = jnp.where(band, 0.0, -jnp.inf)
        bias_scr[1] = jnp.where(band & (kj >= WINDOW), 0.0, -jnp.inf)

    @pl.when(s_idx == 0)
    def _():
        k_scr[0:WINDOW, :] = jnp.zeros((WINDOW, KV_WIDTH), BF16)
        vt_scr[:, 0:WINDOW] = jnp.zeros((KV_WIDTH, WINDOW), BF16)
        conv_scr[0:SUBLANES, :] = jnp.zeros((SUBLANES, LRU_WIDTH), F32)
        h_scr[...] = jnp.zeros((1, LRU_WIDTH), F32)

    def mixer_steps():
        k_scr[WINDOW:, :] = z_ref[:, O_K:O_V].astype(BF16)
        vt_scr[:, WINDOW:] = z_ref[:, O_V:O_XR].T.astype(BF16)
        sink = _sink_row(sinks_ref, layer, WINDOW)
        for i in range(tile // WINDOW):
            q_exp = _expand_q(z_ref[WINDOW * i:WINDOW * (i + 1), 0:ATTN_WIDTH], WINDOW)
            keys = k_scr[WINDOW * i:WINDOW * (i + 2), :]
            vals_t = vt_scr[:, WINDOW * i:WINDOW * (i + 2)]
            s = lax.dot_general(keys, q_exp, (((1,), (1,)), ((), ())), preferred_element_type=F32)
            s = s + bias_scr[jnp.where(s_idx == 0, 1, 0) if i == 0 else 0]
            m = jnp.maximum(jnp.max(s, axis=0, keepdims=True), sink)
            p = jnp.exp(s - m)
            denom = jnp.sum(p, axis=0, keepdims=True) + jnp.exp(sink - m)
            yield
            o_t = jnp.dot(vals_t, p.astype(BF16), preferred_element_type=F32) * (1.0 / denom)
            attn_scr[WINDOW * i:WINDOW * (i + 1), :] = _collect_heads_t(o_t, WINDOW).T
            yield
        k_scr[0:WINDOW, :] = k_scr[tile:tile + WINDOW, :]
        vt_scr[:, 0:WINDOW] = vt_scr[:, tile:tile + WINDOW]

        conv_scr[SUBLANES:, :] = z_ref[:, O_XR:O_YG]
        xc = conv_b_ref[...] + conv_scr[SUBLANES - 3:SUBLANES - 3 + tile, :] * conv_w_ref[0:1, :]
        for j in range(1, CONV_W):
            xc = xc + conv_scr[SUBLANES - 3 + j:SUBLANES - 3 + j + tile, :] * conv_w_ref[j:j + 1, :]
        conv_scr[0:SUBLANES, :] = conv_scr[tile:tile + SUBLANES, :]
        xc_scr[...] = xc
        gates_scr[...] = _lru_gates(xc, wgate_ref)
        yield

        for j in range(LRU_WIDTH // LANES):
            ch = slice(LANES * j, LANES * (j + 1))
            gate_x_ch = slice(LRU_WIDTH + LANES * j, LRU_WIDTH + LANES * (j + 1))
            a, u = _lru_coeffs(xc_scr[:, ch], gates_scr[:, ch], gates_scr[:, gate_x_ch],
                               ba_ref[:, ch], bx_ref[:, ch], lam_ref[:, ch])
            yield
            hs_scr[:, ch] = _scan_rows_carry(a, u, h_scr[:, ch])
            yield
        h_last = hs_scr[tile - 1:tile, :]
        h_scr[...] = h_last
        mixed = _mix_normed(attn_scr[...], hs_scr[...], z_ref[:, O_YG:], g_attn_ref[...],
                            g_lru_ref[...])
        yield
        x2 = _mix_project(x1_ref[...], mixed, wo_ref, g_postmix_ref[...])
        return x2, h_last

    def ffn_steps():
        return _half_ffn_steps(x2_scr[...], g_pre2_ref[...], wg_ref, wu_ref, wd_ref,
                               g_post2_ref[...], act_scr)

    @pl.when(g_idx < n_tiles)
    def _():
        x3, (x2, h_last) = _interleave(ffn_steps(), mixer_steps())
        x3_ref[...] = x3
        x2_scr[...] = x2
        nk_ref[0] = z_ref[tile - WINDOW:, O_K:O_V]
        nv_ref[0] = z_ref[tile - WINDOW:, O_V:O_XR]
        nconv_ref[0] = z_ref[tile - (CONV_W - 1):, O_XR:O_YG]
        nh_ref[0] = h_last

    @pl.when(g_idx == n_tiles)
    def _():
        x3_ref[...] = _run(ffn_steps())


def _mix_weight_specs(layer, grid_rank):
    lspec = lambda *shape: _layer_spec(shape, layer, grid_rank)
    return [
        lspec(CONV_W, LRU_WIDTH), lspec(1, LRU_WIDTH), lspec(LRU_WIDTH, 2 * LRU_WIDTH),
        lspec(1, LRU_WIDTH), lspec(1, LRU_WIDTH), lspec(1, LRU_WIDTH),
        lspec(1, ATTN_WIDTH), lspec(1, LRU_WIDTH), lspec(D_MODEL, D_MODEL), lspec(1, D_MODEL),
        lspec(1, D_MODEL), lspec(D_MODEL, D_FF), lspec(D_MODEL, D_FF), lspec(D_FF, D_MODEL),
        lspec(1, D_MODEL),
    ]


def _mix_weights(w):
    return (w["conv_w"], w["conv_b"], w["w_gate"], w["gate_a_b"], w["gate_x_b"], w["lam"],
            w["g_attn"], w["g_lru"], w["w_o"], w["g_postmix"], w["g_pre2"], w["wg2"], w["wu2"],
            w["wd2"], w["g_post2"])


def _mix_prompt(x1, z, w, layer, batch, seq, tile):
    n_s = seq // tile
    n_tiles = batch * n_s
    mix_tile = lambda g: jnp.minimum(g, n_tiles - 1)
    tok = lambda width: pl.BlockSpec((tile, width), lambda g: (mix_tile(g), 0))
    per_batch = lambda rows, width: pl.BlockSpec((1, rows, width), lambda g: (mix_tile(g) // n_s, 0, 0))
    return pl.pallas_call(
        functools.partial(_mix_prompt_kernel, layer, n_s),
        grid=(n_tiles + 1,),
        in_specs=[pl.BlockSpec(memory_space=pltpu.SMEM), tok(D_MODEL), tok(IN_WIDTH)]
                 + _mix_weight_specs(layer, 1),
        out_specs=[
            pl.BlockSpec((tile, D_MODEL), lambda g: (jnp.maximum(g - 1, 0), 0)),
            per_batch(WINDOW, KV_WIDTH), per_batch(WINDOW, KV_WIDTH),
            per_batch(CONV_W - 1, LRU_WIDTH), per_batch(1, LRU_WIDTH),
        ],
        out_shape=[
            jax.ShapeDtypeStruct((batch * seq, D_MODEL), F32),
            jax.ShapeDtypeStruct((batch, WINDOW, KV_WIDTH), F32),
            jax.ShapeDtypeStruct((batch, WINDOW, KV_WIDTH), F32),
            jax.ShapeDtypeStruct((batch, CONV_W - 1, LRU_WIDTH), F32),
            jax.ShapeDtypeStruct((batch, 1, LRU_WIDTH), F32),
        ],
        scratch_shapes=[
            pltpu.VMEM((WINDOW + tile, KV_WIDTH), BF16),
            pltpu.VMEM((KV_WIDTH, WINDOW + tile), BF16),
            pltpu.VMEM((SUBLANES + tile, LRU_WIDTH), F32),
            pltpu.VMEM((1, LRU_WIDTH), F32),
            pltpu.VMEM((tile, D_MODEL), F32),
            pltpu.VMEM((2, 2 * WINDOW, N_HEADS * WINDOW), F32),
            pltpu.VMEM((tile, ATTN_WIDTH), F32),
            pltpu.VMEM((tile, LRU_WIDTH), F32),
            pltpu.VMEM((tile, 2 * LRU_WIDTH), F32),
            pltpu.VMEM((tile, LRU_WIDTH), F32),
            pltpu.VMEM((tile, D_FF), BF16),
        ],
        compiler_params=pltpu.CompilerParams(
            dimension_semantics=("arbitrary",), vmem_limit_bytes=VMEM_LIMIT_BYTES),
        name="mix_prompt",
    )(w["sinks"], x1, z, *_mix_weights(w))


def _attn_sample_kernel(layer, sinks_ref, zq_ref, ck_ref, cv_ref, attn_ref):
    dec_seq, g = zq_ref.shape[0], zq_ref.shape[1]
    rows = dec_seq * g
    cache_len = ck_ref.shape[1]
    zq = zq_ref[...].reshape(rows, zq_ref.shape[2])
    q_exp = _expand_q(zq[:, 0:ATTN_WIDTH], rows)
    k_new = zq[:, O_K:O_V].astype(BF16)
    v_new = zq[:, O_V:O_XR].astype(BF16)
    k_cache = ck_ref[...].reshape(g * cache_len, KV_WIDTH).astype(BF16)
    v_cache = cv_ref[...].reshape(g * cache_len, KV_WIDTH).astype(BF16)
    contract_last = (((1,), (1,)), ((), ()))
    s_c = lax.dot_general(q_exp, k_cache, contract_last, preferred_element_type=F32)
    s_n = lax.dot_general(q_exp, k_new, contract_last, preferred_element_type=F32)

    def row_ids(shape):
        r = lax.broadcasted_iota(jnp.int32, shape, 0)
        return (r % rows) // g, r % g

    t_q, b_q = row_ids(s_c.shape)
    c = lax.broadcasted_iota(jnp.int32, s_c.shape, 1)
    mask_c = ((c // cache_len) == b_q) & ((c % cache_len) > t_q)
    t_q, b_q = row_ids(s_n.shape)
    c = lax.broadcasted_iota(jnp.int32, s_n.shape, 1)
    mask_n = ((c % g) == b_q) & ((c // g) <= t_q)
    s_c = jnp.where(mask_c, s_c, -jnp.inf)
    s_n = jnp.where(mask_n, s_n, -jnp.inf)

    sink = _sink_column(sinks_ref, layer, rows)
    m = jnp.maximum(jnp.maximum(jnp.max(s_c, axis=-1, keepdims=True),
                                jnp.max(s_n, axis=-1, keepdims=True)), sink)
    p_c = jnp.exp(s_c - m)
    p_n = jnp.exp(s_n - m)
    denom = (jnp.sum(p_c, axis=-1, keepdims=True) + jnp.sum(p_n, axis=-1, keepdims=True)
             + jnp.exp(sink - m))
    o = (jnp.dot(p_c.astype(BF16), v_cache, preferred_element_type=F32)
         + jnp.dot(p_n.astype(BF16), v_new, preferred_element_type=F32)) / denom
    attn_ref[...] = _collect_heads(o, rows).reshape(dec_seq, g, ATTN_WIDTH)


def _attn_sample(z3, cache_k, cache_v, sinks, layer):
    dec_seq, dec_batch, _ = z3.shape
    cache_len = cache_k.shape[2]
    g = SAMPLE_GROUP
    cache_spec = pl.BlockSpec((None, g, cache_len, KV_WIDTH), lambda i: (layer, i, 0, 0))
    return pl.pallas_call(
        functools.partial(_attn_sample_kernel, layer),
        grid=(dec_batch // g,),
        in_specs=[
            pl.BlockSpec(memory_space=pltpu.SMEM),
            pl.BlockSpec((dec_seq, g, O_XR), lambda i: (0, i, 0)),
            cache_spec, cache_spec,
        ],
        out_specs=pl.BlockSpec((dec_seq, g, ATTN_WIDTH), lambda i: (0, i, 0)),
        out_shape=jax.ShapeDtypeStruct((dec_seq, dec_batch, ATTN_WIDTH), F32),
        compiler_params=pltpu.CompilerParams(
            dimension_semantics=("arbitrary",), vmem_limit_bytes=VMEM_LIMIT_BYTES),
        name="attn_sample",
    )(sinks, z3, cache_k, cache_v)


def _shift_caches_kernel(kvnew_ref, ck_ref, cv_ref, nk_ref, nv_ref):
    dec_seq = kvnew_ref.shape[1]
    cache_len = ck_ref.shape[1]
    nk_ref[:, 0:cache_len - dec_seq, :] = ck_ref[:, dec_seq:, :]
    nv_ref[:, 0:cache_len - dec_seq, :] = cv_ref[:, dec_seq:, :]
    nk_ref[:, cache_len - dec_seq:, :] = kvnew_ref[:, :, 0:KV_WIDTH]
    nv_ref[:, cache_len - dec_seq:, :] = kvnew_ref[:, :, KV_WIDTH:]


def _shift_caches(kvnew, cache_k, cache_v):
    depth, dec_batch, cache_len, _ = cache_k.shape
    dec_seq = kvnew.shape[2]
    g = SHIFT_GROUP
    spec = lambda rows, width: pl.BlockSpec((None, g, rows, width), lambda l, i: (l, i, 0, 0))
    return pl.pallas_call(
        _shift_caches_kernel,
        grid=(depth, dec_batch // g),
        in_specs=[spec(dec_seq, 2 * KV_WIDTH), spec(cache_len, KV_WIDTH), spec(cache_len, KV_WIDTH)],
        out_specs=[spec(cache_len, KV_WIDTH), spec(cache_len, KV_WIDTH)],
        out_shape=[jax.ShapeDtypeStruct(cache_k.shape, F32), jax.ShapeDtypeStruct(cache_v.shape, F32)],
        compiler_params=pltpu.CompilerParams(dimension_semantics=("arbitrary", "arbitrary")),
        name="shift_caches",
    )(kvnew, cache_k, cache_v)


def _mix_sample_kernel(x1_ref, z_ref, attn_ref, sconv_ref, sh_ref, conv_w_ref, conv_b_ref, wgate_ref,
                       ba_ref, bx_ref, lam_ref, g_attn_ref, g_lru_ref, wo_ref, g_postmix_ref,
                       g_pre2_ref, wg_ref, wu_ref, wd_ref, g_post2_ref,
                       x3_ref, nconv_ref, nh_ref, act_scr):
    nb = sh_ref.shape[0]
    dec_seq = x1_ref.shape[0] // nb
    xr = z_ref[:, O_XR:O_YG]
    xp = [sconv_ref[j] for j in range(CONV_W - 1)] + [xr[nb * t:nb * (t + 1)] for t in range(dec_seq)]
    xc_steps = []
    for t in range(dec_seq):
        acc = conv_b_ref[...] + xp[t] * conv_w_ref[0:1, :]
        for j in range(1, CONV_W):
            acc = acc + xp[t + j] * conv_w_ref[j:j + 1, :]
        xc_steps.append(acc)
    for j in range(CONV_W - 1):
        nconv_ref[j] = xp[dec_seq + j]
    xc = jnp.concatenate(xc_steps, axis=0)
    gates = _lru_gates(xc, wgate_ref)
    a, u = _lru_coeffs(xc, gates[:, :LRU_WIDTH], gates[:, LRU_WIDTH:], ba_ref[...], bx_ref[...],
                       lam_ref[...])
    h = sh_ref[...]
    hs_steps = []
    for t in range(dec_seq):
        h = a[nb * t:nb * (t + 1)] * h + u[nb * t:nb * (t + 1)]
        hs_steps.append(h)
    nh_ref[...] = h
    hs = jnp.concatenate(hs_steps, axis=0)
    x2 = _mix_out(x1_ref[...], attn_ref[...], hs, z_ref[:, O_YG:], g_attn_ref[...], g_lru_ref[...],
                  wo_ref, g_postmix_ref[...])
    x3_ref[...] = _half_ffn(x2, g_pre2_ref[...], wg_ref, wu_ref, wd_ref, g_post2_ref[...], act_scr)


def _mix_sample(x1, z, attn, sconv, state_h, w, layer):
    n = x1.shape[0]
    nb = state_h.shape[1]
    full = lambda *shape: pl.BlockSpec(shape, lambda i: (0,) * len(shape))
    return pl.pallas_call(
        _mix_sample_kernel,
        grid=(1,),
        in_specs=[
            full(n, D_MODEL), full(n, IN_WIDTH), full(n, ATTN_WIDTH),
            full(CONV_W - 1, nb, LRU_WIDTH), _layer_spec((nb, LRU_WIDTH), layer, 1),
        ] + _mix_weight_specs(layer, 1),
        out_specs=[full(n, D_MODEL), full(CONV_W - 1, nb, LRU_WIDTH), full(nb, LRU_WIDTH)],
        out_shape=[
            jax.ShapeDtypeStruct((n, D_MODEL), F32),
            jax.ShapeDtypeStruct((CONV_W - 1, nb, LRU_WIDTH), F32),
            jax.ShapeDtypeStruct((nb, LRU_WIDTH), F32),
        ],
        scratch_shapes=[pltpu.VMEM((n, D_FF), BF16)],
        compiler_params=pltpu.CompilerParams(
            dimension_semantics=("arbitrary",), vmem_limit_bytes=VMEM_LIMIT_BYTES),
        name="mix_sample",
    )(x1, z, attn, sconv, state_h, *_mix_weights(w))


def _rope_tables(pos):
    half = ROT_DIM // 2
    inv = ROPE_THETA ** (-(jnp.arange(half, dtype=F32) * 2.0) / ROT_DIM)
    ang = pos.astype(F32)[:, None] * inv[None, :]
    cos, sin = jnp.cos(ang), jnp.sin(ang)
    n = pos.shape[0]
    rest = jnp.zeros((n, HEAD_DIM - ROT_DIM), F32)
    zero = jnp.zeros((n, half), F32)
    per_head = lambda parts: jnp.tile(jnp.concatenate(parts, axis=-1), (1, LANES // HEAD_DIM))
    return (per_head([cos, cos, rest + 1.0]), per_head([-sin, zero, rest]), per_head([zero, sin, rest]))


def _block_diag(w):
    eye = jnp.eye(LRU_HEADS, dtype=w.dtype)
    return jnp.einsum("lhij,hg->lhigj", w, eye).reshape(w.shape[0], LRU_WIDTH, LRU_WIDTH)


def kernel(x_prompt, x_sample, cache_k, cache_v, state_conv, state_h, norm_pre_ffn1, ffn1_w_gate, ffn1_w_up, ffn1_w_down, norm_post_ffn1, norm_pre_mix, w_in, sinks, conv_w, conv_b, gate_a_w, gate_a_b, gate_x_w, gate_x_b, lru_lambda, attn_out_norm, lru_out_norm, w_o, norm_post_mix, norm_pre_ffn2, ffn2_w_gate, ffn2_w_up, ffn2_w_down, norm_post_ffn2):
    batch, seq, _ = x_prompt.shape
    dec_batch, dec_seq, _ = x_sample.shape
    depth = w_in.shape[0]
    cache_len = cache_k.shape[2]
    n_s = dec_seq * dec_batch
    assert seq % FFN_TILE == 0 and n_s % FFN_TILE == 0
    assert seq % MIX_TILE == 0 and MIX_TILE % WINDOW == 0
    assert dec_batch % SAMPLE_GROUP == 0 and dec_batch % SHIFT_GROUP == 0

    tabs_p = _rope_tables(jnp.arange(seq))
    tabs_s = _rope_tables(jnp.repeat(PAST_LEN + jnp.arange(dec_seq), dec_batch))
    xp = x_prompt.reshape(batch * seq, D_MODEL)
    xs = jnp.swapaxes(x_sample, 0, 1).reshape(n_s, D_MODEL)
    rows = lambda v: v.reshape(depth, 1, -1)

    w = dict(
        g_pre1=rows(norm_pre_ffn1), wg1=_to_bf16(ffn1_w_gate), wu1=_to_bf16(ffn1_w_up),
        wd1=_to_bf16(ffn1_w_down), g_post1=rows(norm_post_ffn1), g_mix=rows(norm_pre_mix),
        w_in=_to_bf16(w_in), sinks=sinks, conv_w=conv_w, conv_b=rows(conv_b),
        w_gate=jnp.concatenate([_block_diag(gate_a_w), _block_diag(gate_x_w)], axis=2).astype(BF16),
        gate_a_b=rows(gate_a_b), gate_x_b=rows(gate_x_b), lam=rows(lru_lambda),
        g_attn=rows(attn_out_norm), g_lru=rows(lru_out_norm), w_o=_to_bf16(w_o),
        g_postmix=rows(norm_post_mix), g_pre2=rows(norm_pre_ffn2), wg2=_to_bf16(ffn2_w_gate),
        wu2=_to_bf16(ffn2_w_up), wd2=_to_bf16(ffn2_w_down), g_post2=rows(norm_post_ffn2),
    )
    ck = cache_k.reshape(depth, dec_batch, cache_len, KV_WIDTH)
    cv = cache_v.reshape(depth, dec_batch, cache_len, KV_WIDTH)

    outs = [[] for _ in range(6)]
    kvnew = []
    for l in range(depth):
        x1p, zp = _ffn_in(xp, tabs_p, seq // FFN_TILE, w, l, FFN_TILE)
        xp, nk, nv, nc, nh = _mix_prompt(x1p, zp, w, l, batch, seq, MIX_TILE)

        x1s, zs = _ffn_in(xs, tabs_s, n_s // FFN_TILE, w, l, FFN_TILE)
        z3 = zs.reshape(dec_seq, dec_batch, IN_WIDTH)
        kvnew.append(jnp.swapaxes(z3[:, :, O_K:O_XR], 0, 1))
        attn_s = _attn_sample(z3, ck, cv, sinks, l)
        xs, ncs, nhs = _mix_sample(x1s, zs, attn_s.reshape(n_s, ATTN_WIDTH),
                                   jnp.swapaxes(state_conv[l], 0, 1), state_h, w, l)
        kv_p = (batch, WINDOW, N_KV_HEADS, HEAD_DIM)
        for acc, val in zip(outs, (nk.reshape(kv_p), nv.reshape(kv_p), nc,
                                   nh.reshape(batch, LRU_WIDTH), jnp.swapaxes(ncs, 0, 1), nhs)):
            acc.append(val)

    nks, nvs = _shift_caches(jnp.stack(kvnew), ck, cv)
    y_prompt = xp.reshape(batch, seq, D_MODEL)
    y_sample = jnp.swapaxes(xs.reshape(dec_seq, dec_batch, D_MODEL), 0, 1)
    kp, vp, cp, hp, cs, hs = (jnp.stack(o) for o in outs)
    return (y_prompt, y_sample, kp, vp, cp, hp, nks.reshape(cache_k.shape), nvs.reshape(cache_v.shape),
            cs, hs)
```

```python
import functools

import jax
import jax.numpy as jnp
import numpy as np
from jax import lax
from jax.experimental import pallas as pl
from jax.experimental.pallas import tpu as pltpu

D_MODEL = 1024
PAST_LEN = 16384
N_HEADS = 8
HEAD_DIM = 64
N_KV_HEADS = 2
GROUP = N_HEADS // N_KV_HEADS
ATTN_WIDTH = N_HEADS * HEAD_DIM
KV_WIDTH = N_KV_HEADS * HEAD_DIM
WINDOW = 128
ROPE_THETA = 500000.0
ROT_DIM = HEAD_DIM // 4
LRU_WIDTH = D_MODEL // 2
LRU_HEADS = 8
LRU_BLOCK = LRU_WIDTH // LRU_HEADS
CONV_W = 4
LRU_C = 8.0
D_FF = 2816
IN_WIDTH = ATTN_WIDTH + 2 * KV_WIDTH + 2 * LRU_WIDTH
EPS = 1e-6

LANES = 128
SUBLANES = 8
FF_CHUNK = 256
N_FF_CHUNKS = D_FF // FF_CHUNK
FFN_TILE = 512
MIX_TILE = 512
SAMPLE_GROUP = 8
SHIFT_GROUP = 16
CAST_ROWS = 512
VMEM_LIMIT_BYTES = 56 * 1024 * 1024

O_K = ATTN_WIDTH
O_V = O_K + KV_WIDTH
O_XR = O_V + KV_WIDTH
O_YG = O_XR + LRU_WIDTH

F32 = jnp.float32
BF16 = jnp.bfloat16


def _rmsnorm(x, g):
    return (x * lax.rsqrt(jnp.mean(x * x, axis=-1, keepdims=True) + EPS)) * g


def _half_ffn_steps(x, g_pre, wg_ref, wu_ref, wd_ref, g_post, act_scr):
    h = _rmsnorm(x, g_pre).astype(BF16)
    for c in range(N_FF_CHUNKS):
        cols = slice(FF_CHUNK * c, FF_CHUNK * (c + 1))
        g = jnp.dot(h, wg_ref[:, cols], preferred_element_type=F32)
        u = jnp.dot(h, wu_ref[:, cols], preferred_element_type=F32)
        act_scr[:, cols] = ((g * jax.nn.sigmoid(g)) * u).astype(BF16)
        yield
    ys = []
    for c in range(D_MODEL // FF_CHUNK):
        ys.append(jnp.dot(act_scr[...], wd_ref[:, FF_CHUNK * c:FF_CHUNK * (c + 1)],
                          preferred_element_type=F32))
        yield
    return x + _rmsnorm(jnp.concatenate(ys, axis=1), 0.5 * g_post)


def _advance(gen, done):
    if gen in done:
        return
    try:
        next(gen)
    except StopIteration as stop:
        done[gen] = stop.value


def _interleave(first, second):
    done = {}
    while len(done) < 2:
        _advance(first, done)
        _advance(second, done)
    return done[first], done[second]


def _run(gen):
    done = {}
    while gen not in done:
        _advance(gen, done)
    return done[gen]


def _half_ffn(*args):
    return _run(_half_ffn_steps(*args))


def _gelu_tanh(x):
    c = np.float32(np.sqrt(2.0 / np.pi))
    return x * (0.5 * (1.0 + jnp.tanh(c * (x + 0.044715 * (x * x * x)))))


def _softplus(x):
    return jnp.maximum(x, 0.0) + jnp.log1p(jnp.exp(-jnp.abs(x)))


def _lru_gates(xc, wgate_ref):
    return jnp.dot(xc.astype(BF16), wgate_ref[...], preferred_element_type=F32)


def _lru_coeffs(xc, gate_a, gate_x, ba, bx, lam):
    r = jax.nn.sigmoid(gate_a + ba)
    ig = jax.nn.sigmoid(gate_x + bx)
    a = jnp.exp2(r * ((-LRU_C * np.float32(np.log2(np.e))) * _softplus(-lam)))
    v = 1.0 - a * a
    u = jnp.where(v > 0.0, v * lax.rsqrt(v), 0.0) * (ig * xc)
    return a, u


def _scan_rows(a, u):
    n = a.shape[0]
    row = lax.broadcasted_iota(jnp.int32, a.shape, 0)
    d = 1
    while d < n:
        keep = row >= d
        u = jnp.where(keep, a * pltpu.roll(u, d, 0) + u, u)
        if 2 * d < n:
            a = jnp.where(keep, a * pltpu.roll(a, d, 0), a)
        d *= 2
    return u


def _phase_blocks(slab_ref, j):
    groups = slab_ref.shape[1] // SUBLANES
    return [slab_ref[j, pl.ds(k, groups, stride=SUBLANES), :] for k in range(SUBLANES)]


def _prev_group(block, carry_row):
    first = lax.broadcasted_iota(jnp.int32, block.shape, 0) == 0
    return jnp.where(first, carry_row, pltpu.roll(block, 1, 0))


def _conv_phases(x_phases, carry_rows, w, b):
    wrapped = {SUBLANES - m: _prev_group(x_phases[SUBLANES - m], carry_rows[CONV_W - 1 - m])
               for m in range(1, CONV_W)}
    out = []
    for k in range(SUBLANES):
        acc = b
        for tap in range(CONV_W):
            src = k - (CONV_W - 1 - tap)
            x = x_phases[src] if src >= 0 else wrapped[src + SUBLANES]
            acc = acc + x * w[tap:tap + 1, :]
        out.append(acc)
    return out


def _scan_phases(a, u, h0):
    groups = a.shape[0] // SUBLANES
    blk = lambda v, k: v[groups * k:groups * (k + 1)]
    h_zero = [blk(u, 0)]
    a_prod = [blk(a, 0)]
    for k in range(1, SUBLANES):
        h_zero.append(blk(a, k) * h_zero[-1] + blk(u, k))
        a_prod.append(blk(a, k) * a_prod[-1])
    first = lax.broadcasted_iota(jnp.int32, a_prod[-1].shape, 0) == 0
    h_end = _scan_rows(a_prod[-1], jnp.where(first, a_prod[-1] * h0 + h_zero[-1], h_zero[-1]))
    h_in = _prev_group(h_end, h0)
    return [h_zero[k] + a_prod[k] * h_in for k in range(SUBLANES)]


def _expand_q(q, rows):
    lane = lax.broadcasted_iota(jnp.int32, (rows, LANES), 1)
    pieces = []
    for h in range(N_HEADS):
        kv = h // GROUP
        src = q[:, LANES * (h // 2):LANES * (h // 2 + 1)]
        if (h % 2) != kv:
            src = pltpu.roll(src, HEAD_DIM, 1)
        in_half = (lane >= HEAD_DIM) if kv == 1 else (lane < HEAD_DIM)
        pieces.append(jnp.where(in_half, src, 0.0))
    return jnp.concatenate(pieces, axis=0).astype(BF16)


def _collect_heads(o, rows):
    lane = lax.broadcasted_iota(jnp.int32, (rows, LANES), 1)
    groups = []
    for j in range(N_HEADS // 2):
        kv = (2 * j) // GROUP
        even = o[rows * (2 * j):rows * (2 * j + 1)]
        odd = o[rows * (2 * j + 1):rows * (2 * j + 2)]
        if kv == 1:
            even = pltpu.roll(even, HEAD_DIM, 1)
        else:
            odd = pltpu.roll(odd, HEAD_DIM, 1)
        groups.append(jnp.where(lane < HEAD_DIM, even, odd))
    return jnp.concatenate(groups, axis=1)


def _collect_heads_t(o_t, rows):
    return jnp.concatenate(
        [o_t[HEAD_DIM * (h // GROUP):HEAD_DIM * (h // GROUP + 1), rows * h:rows * (h + 1)]
         for h in range(N_HEADS)], axis=0)


def _sink_row(sinks_ref, layer, rows):
    head = lax.broadcasted_iota(jnp.int32, (1, N_HEADS * rows), 1) // rows
    out = jnp.full((1, N_HEADS * rows), sinks_ref[layer, 0], F32)
    for h in range(1, N_HEADS):
        out = jnp.where(head == h, sinks_ref[layer, h], out)
    return out


def _sink_column(sinks_ref, layer, rows):
    return jnp.concatenate([jnp.full((rows, 1), sinks_ref[layer, h], F32) for h in range(N_HEADS)],
                           axis=0)


def _mix_normed(attn, hs, yg, g_attn, g_lru):
    lru = hs * _gelu_tanh(yg)
    return jnp.concatenate([_rmsnorm(attn, g_attn), _rmsnorm(lru, g_lru)], axis=-1).astype(BF16)


def _mix_project(x1, mixed, wo_ref, g_postmix):
    y = jnp.dot(mixed, wo_ref[...], preferred_element_type=F32)
    return x1 + _rmsnorm(y, g_postmix)


def _mix_out(x1, attn, hs, yg, g_attn, g_lru, wo_ref, g_postmix):
    return _mix_project(x1, _mix_normed(attn, hs, yg, g_attn, g_lru), wo_ref, g_postmix)


def _layer_spec(shape, layer, grid_rank):
    zeros = (0,) * len(shape)
    return pl.BlockSpec((None,) + tuple(shape), lambda *_: (layer,) + zeros,
                        pipeline_mode=pl.Buffered(1))


def _cast_kernel(w_ref, o_ref):
    o_ref[...] = w_ref[...].astype(BF16)


def _to_bf16(w):
    depth, rows, cols = w.shape
    flat = w.reshape(depth * rows, cols)
    out = pl.pallas_call(
        _cast_kernel,
        grid=(depth * rows // CAST_ROWS,),
        in_specs=[pl.BlockSpec((CAST_ROWS, cols), lambda i: (i, 0))],
        out_specs=pl.BlockSpec((CAST_ROWS, cols), lambda i: (i, 0)),
        out_shape=jax.ShapeDtypeStruct(flat.shape, BF16),
        compiler_params=pltpu.CompilerParams(dimension_semantics=("arbitrary",)),
        name="cast_bf16",
    )(flat)
    return out.reshape(depth, rows, cols)


def _in_proj_steps(x1, g_mix, win_ref, cos, sin_up, sin_dn, z_ref):
    h = _rmsnorm(x1, g_mix).astype(BF16)
    half = ROT_DIM // 2
    for c in range(IN_WIDTH // FF_CHUNK):
        zc = jnp.dot(h, win_ref[:, FF_CHUNK * c:FF_CHUNK * (c + 1)], preferred_element_type=F32)
        for k in range(FF_CHUNK // LANES):
            j = (FF_CHUNK // LANES) * c + k
            zj = zc[:, LANES * k:LANES * (k + 1)]
            if j < O_V // LANES:
                zj = (zj * cos + pltpu.roll(zj, LANES - half, 1) * sin_up
                      + pltpu.roll(zj, half, 1) * sin_dn)
            if j < ATTN_WIDTH // LANES:
                zj = zj * (HEAD_DIM ** -0.5)
            z_ref[:, LANES * j:LANES * (j + 1)] = zj
        yield


def _ffn_in_kernel(skewed, x_ref, cos_ref, sin_up_ref, sin_dn_ref, g_pre_ref, wg_ref, wu_ref, wd_ref,
                   g_post_ref, g_mix_ref, win_ref, x1_ref, z_ref, act_scr, *x1_scr):
    def ffn_steps():
        return _half_ffn_steps(x_ref[...], g_pre_ref[...], wg_ref, wu_ref, wd_ref, g_post_ref[...],
                               act_scr)

    def in_proj_steps(x1):
        return _in_proj_steps(x1, g_mix_ref[...], win_ref, cos_ref[...], sin_up_ref[...],
                              sin_dn_ref[...], z_ref)

    if not skewed:
        x1 = _run(ffn_steps())
        x1_ref[...] = x1
        _run(in_proj_steps(x1))
        return
    (x1_scr,) = x1_scr
    g_idx = pl.program_id(0)
    n_tiles = pl.num_programs(0) - 1

    @pl.when(g_idx == 0)
    def _():
        x1_scr[...] = jnp.zeros(x1_scr.shape, F32)

    @pl.when(g_idx < n_tiles)
    def _():
        x1, _ = _interleave(ffn_steps(), in_proj_steps(x1_scr[...]))
        x1_ref[...] = x1
        x1_scr[...] = x1

    @pl.when(g_idx == n_tiles)
    def _():
        _run(in_proj_steps(x1_scr[...]))


def _ffn_in(x, tabs, tab_tiles, w, layer, tile):
    n = x.shape[0] // tile
    skewed = n > 1
    ffn_tile = (lambda g: jnp.minimum(g, n - 1)) if skewed else (lambda g: g)
    proj_tile = (lambda g: jnp.maximum(g - 1, 0)) if skewed else (lambda g: g)
    lspec = lambda *shape: _layer_spec(shape, layer, 1)
    tab_spec = pl.BlockSpec((tile, LANES), lambda g: (proj_tile(g) % tab_tiles, 0))
    return pl.pallas_call(
        functools.partial(_ffn_in_kernel, skewed),
        grid=(n + 1 if skewed else n,),
        in_specs=[
            pl.BlockSpec((tile, D_MODEL), lambda g: (ffn_tile(g), 0)),
            tab_spec, tab_spec, tab_spec,
            lspec(1, D_MODEL), lspec(D_MODEL, D_FF), lspec(D_MODEL, D_FF), lspec(D_FF, D_MODEL),
            lspec(1, D_MODEL), lspec(1, D_MODEL), lspec(D_MODEL, IN_WIDTH),
        ],
        out_specs=[
            pl.BlockSpec((tile, D_MODEL), lambda g: (ffn_tile(g), 0)),
            pl.BlockSpec((tile, IN_WIDTH), lambda g: (proj_tile(g), 0)),
        ],
        out_shape=[
            jax.ShapeDtypeStruct((n * tile, D_MODEL), F32),
            jax.ShapeDtypeStruct((n * tile, IN_WIDTH), F32),
        ],
        scratch_shapes=[pltpu.VMEM((tile, D_FF), BF16)]
                       + ([pltpu.VMEM((tile, D_MODEL), F32)] if skewed else []),
        compiler_params=pltpu.CompilerParams(
            dimension_semantics=("arbitrary",), vmem_limit_bytes=VMEM_LIMIT_BYTES),
        name="ffn_in",
    )(x, *tabs, w["g_pre1"], w["wg1"], w["wu1"], w["wd1"], w["g_post1"], w["g_mix"], w["w_in"])


def _mix_prompt_kernel(layer, n_seq_tiles, sinks_ref, x1_ref, z_ref, conv_w_ref, conv_b_ref, wgate_ref, ba_ref,
                       bx_ref, lam_ref, g_attn_ref, g_lru_ref, wo_ref, g_postmix_ref, g_pre2_ref,
                       wg_ref, wu_ref, wd_ref, g_post2_ref,
                       x3_ref, nk_ref, nv_ref, nconv_ref, nh_ref,
                       k_scr, vt_scr, conv_scr, h_scr, x2_scr, bias_scr, attn_scr, xr_scr, xc_scr,
                       gates_scr, hs_scr, act_scr):
    tile = x1_ref.shape[0]
    g_idx = pl.program_id(0)
    n_tiles = pl.num_programs(0) - 1
    s_idx = g_idx % n_seq_tiles

    @pl.when(g_idx == 0)
    def _():
        x2_scr[...] = jnp.zeros(x2_scr.shape, F32)
        kj = lax.broadcasted_iota(jnp.int32, (2 * WINDOW, N_HEADS * WINDOW), 0)
        qi = lax.broadcasted_iota(jnp.int32, (2 * WINDOW, N_HEADS * WINDOW), 1) & (WINDOW - 1)
        band = (kj > qi) & (kj <= qi + WINDOW)
        bias_scr[0] = jnp.where(band, 0.0, -jnp.inf)
        bias_scr[1] = jnp.where(band & (kj >= WINDOW), 0.0, -jnp.inf)

    @pl.when(s_idx == 0)
    def _():
        k_scr[0:WINDOW, :] = jnp.zeros((WINDOW, KV_WIDTH), BF16)
        vt_scr[:, 0:WINDOW] = jnp.zeros((KV_WIDTH, WINDOW), BF16)
        conv_scr[...] = jnp.zeros((CONV_W - 1, LRU_WIDTH), F32)
        h_scr[...] = jnp.zeros((1, LRU_WIDTH), F32)

    def mixer_steps():
        k_scr[WINDOW:, :] = z_ref[:, O_K:O_V].astype(BF16)
        vt_scr[:, WINDOW:] = z_ref[:, O_V:O_XR].T.astype(BF16)
        sink = _sink_row(sinks_ref, layer, WINDOW)
        for i in range(tile // WINDOW):
            q_exp = _expand_q(z_ref[WINDOW * i:WINDOW * (i + 1), 0:ATTN_WIDTH], WINDOW)
            keys = k_scr[WINDOW * i:WINDOW * (i + 2), :]
            vals_t = vt_scr[:, WINDOW * i:WINDOW * (i + 2)]
            s = lax.dot_general(keys, q_exp, (((1,), (1,)), ((), ())), preferred_element_type=F32)
            s = s + bias_scr[jnp.where(s_idx == 0, 1, 0) if i == 0 else 0]
            m = jnp.maximum(jnp.max(s, axis=0, keepdims=True), sink)
            p = jnp.exp(s - m)
            denom = jnp.sum(p, axis=0, keepdims=True) + jnp.exp(sink - m)
            yield
            o_t = jnp.dot(vals_t, p.astype(BF16), preferred_element_type=F32) * (1.0 / denom)
            attn_scr[WINDOW * i:WINDOW * (i + 1), :] = _collect_heads_t(o_t, WINDOW).T
            yield
        k_scr[0:WINDOW, :] = k_scr[tile:tile + WINDOW, :]
        vt_scr[:, 0:WINDOW] = vt_scr[:, tile:tile + WINDOW]

        for j in range(LRU_WIDTH // LANES):
            ch = slice(LANES * j, LANES * (j + 1))
            xr_scr[j] = z_ref[:, O_XR + LANES * j:O_XR + LANES * (j + 1)]
            carry_rows = [conv_scr[m:m + 1, ch] for m in range(CONV_W - 1)]
            xc_phases = _conv_phases(_phase_blocks(xr_scr, j), carry_rows, conv_w_ref[:, ch],
                                     conv_b_ref[:, ch])
            xc_scr[:, ch] = jnp.concatenate(xc_phases, axis=0)
        conv_scr[...] = z_ref[tile - (CONV_W - 1):, O_XR:O_YG]
        gates_scr[...] = _lru_gates(xc_scr[...], wgate_ref)
        yield

        groups = tile // SUBLANES
        h_last = []
        for j in range(LRU_WIDTH // LANES):
            ch = slice(LANES * j, LANES * (j + 1))
            gate_x_ch = slice(LRU_WIDTH + LANES * j, LRU_WIDTH + LANES * (j + 1))
            a, u = _lru_coeffs(xc_scr[:, ch], gates_scr[:, ch], gates_scr[:, gate_x_ch],
                               ba_ref[:, ch], bx_ref[:, ch], lam_ref[:, ch])
            yield
            h_phases = _scan_phases(a, u, h_scr[:, ch])
            for k in range(SUBLANES):
                hs_scr[j, pl.ds(k, groups, stride=SUBLANES), :] = h_phases[k]
            h_last.append(h_phases[SUBLANES - 1][groups - 1:groups, :])
            yield
        h_last = jnp.concatenate(h_last, axis=1)
        h_scr[...] = h_last
        hs = jnp.concatenate([hs_scr[j] for j in range(LRU_WIDTH // LANES)], axis=1)
        mixed = _mix_normed(attn_scr[...], hs, z_ref[:, O_YG:], g_attn_ref[...], g_lru_ref[...])
        yield
        x2 = _mix_project(x1_ref[...], mixed, wo_ref, g_postmix_ref[...])
        return x2, h_last

    def ffn_steps():
        return _half_ffn_steps(x2_scr[...], g_pre2_ref[...], wg_ref, wu_ref, wd_ref,
                               g_post2_ref[...], act_scr)

    @pl.when(g_idx < n_tiles)
    def _():
        x3, (x2, h_last) = _interleave(ffn_steps(), mixer_steps())
        x3_ref[...] = x3
        x2_scr[...] = x2
        nk_ref[0] = z_ref[tile - WINDOW:, O_K:O_V]
        nv_ref[0] = z_ref[tile - WINDOW:, O_V:O_XR]
        nconv_ref[0] = z_ref[tile - (CONV_W - 1):, O_XR:O_YG]
        nh_ref[0] = h_last

    @pl.when(g_idx == n_tiles)
    def _():
        x3_ref[...] = _run(ffn_steps())


def _mix_weight_specs(layer, grid_rank):
    lspec = lambda *shape: _layer_spec(shape, layer, grid_rank)
    return [
        lspec(CONV_W, LRU_WIDTH), lspec(1, LRU_WIDTH), lspec(LRU_WIDTH, 2 * LRU_WIDTH),
        lspec(1, LRU_WIDTH), lspec(1, LRU_WIDTH), lspec(1, LRU_WIDTH),
        lspec(1, ATTN_WIDTH), lspec(1, LRU_WIDTH), lspec(D_MODEL, D_MODEL), lspec(1, D_MODEL),
        lspec(1, D_MODEL), lspec(D_MODEL, D_FF), lspec(D_MODEL, D_FF), lspec(D_FF, D_MODEL),
        lspec(1, D_MODEL),
    ]


def _mix_weights(w):
    return (w["conv_w"], w["conv_b"], w["w_gate"], w["gate_a_b"], w["gate_x_b"], w["lam"],
            w["g_attn"], w["g_lru"], w["w_o"], w["g_postmix"], w["g_pre2"], w["wg2"], w["wu2"],
            w["wd2"], w["g_post2"])


def _mix_prompt(x1, z, w, layer, batch, seq, tile):
    n_s = seq // tile
    n_tiles = batch * n_s
    mix_tile = lambda g: jnp.minimum(g, n_tiles - 1)
    tok = lambda width: pl.BlockSpec((tile, width), lambda g: (mix_tile(g), 0))
    per_batch = lambda rows, width: pl.BlockSpec((1, rows, width), lambda g: (mix_tile(g) // n_s, 0, 0))
    return pl.pallas_call(
        functools.partial(_mix_prompt_kernel, layer, n_s),
        grid=(n_tiles + 1,),
        in_specs=[pl.BlockSpec(memory_space=pltpu.SMEM), tok(D_MODEL), tok(IN_WIDTH)]
                 + _mix_weight_specs(layer, 1),
        out_specs=[
            pl.BlockSpec((tile, D_MODEL), lambda g: (jnp.maximum(g - 1, 0), 0)),
            per_batch(WINDOW, KV_WIDTH), per_batch(WINDOW, KV_WIDTH),
            per_batch(CONV_W - 1, LRU_WIDTH), per_batch(1, LRU_WIDTH),
        ],
        out_shape=[
            jax.ShapeDtypeStruct((batch * seq, D_MODEL), F32),
            jax.ShapeDtypeStruct((batch, WINDOW, KV_WIDTH), F32),
            jax.ShapeDtypeStruct((batch, WINDOW, KV_WIDTH), F32),
            jax.ShapeDtypeStruct((batch, CONV_W - 1, LRU_WIDTH), F32),
            jax.ShapeDtypeStruct((batch, 1, LRU_WIDTH), F32),
        ],
        scratch_shapes=[
            pltpu.VMEM((WINDOW + tile, KV_WIDTH), BF16),
            pltpu.VMEM((KV_WIDTH, WINDOW + tile), BF16),
            pltpu.VMEM((CONV_W - 1, LRU_WIDTH), F32),
            pltpu.VMEM((1, LRU_WIDTH), F32),
            pltpu.VMEM((tile, D_MODEL), F32),
            pltpu.VMEM((2, 2 * WINDOW, N_HEADS * WINDOW), F32),
            pltpu.VMEM((tile, ATTN_WIDTH), F32),
            pltpu.VMEM((LRU_WIDTH // LANES, tile, LANES), F32),
            pltpu.VMEM((tile, LRU_WIDTH), F32),
            pltpu.VMEM((tile, 2 * LRU_WIDTH), F32),
            pltpu.VMEM((LRU_WIDTH // LANES, tile, LANES), F32),
            pltpu.VMEM((tile, D_FF), BF16),
        ],
        compiler_params=pltpu.CompilerParams(
            dimension_semantics=("arbitrary",), vmem_limit_bytes=VMEM_LIMIT_BYTES),
        name="mix_prompt",
    )(w["sinks"], x1, z, *_mix_weights(w))


def _attn_sample_kernel(layer, sinks_ref, zq_ref, ck_ref, cv_ref, attn_ref):
    dec_seq, g = zq_ref.shape[0], zq_ref.shape[1]
    rows = dec_seq * g
    cache_len = ck_ref.shape[1]
    zq = zq_ref[...].reshape(rows, zq_ref.shape[2])
    q_exp = _expand_q(zq[:, 0:ATTN_WIDTH], rows)
    k_new = zq[:, O_K:O_V].astype(BF16)
    v_new = zq[:, O_V:O_XR].astype(BF16)
    k_cache = ck_ref[...].reshape(g * cache_len, KV_WIDTH).astype(BF16)
    v_cache = cv_ref[...].reshape(g * cache_len, KV_WIDTH).astype(BF16)
    contract_last = (((1,), (1,)), ((), ()))
    s_c = lax.dot_general(q_exp, k_cache, contract_last, preferred_element_type=F32)
    s_n = lax.dot_general(q_exp, k_new, contract_last, preferred_element_type=F32)

    def row_ids(shape):
        r = lax.broadcasted_iota(jnp.int32, shape, 0)
        return (r % rows) // g, r % g

    t_q, b_q = row_ids(s_c.shape)
    c = lax.broadcasted_iota(jnp.int32, s_c.shape, 1)
    mask_c = ((c // cache_len) == b_q) & ((c % cache_len) > t_q)
    t_q, b_q = row_ids(s_n.shape)
    c = lax.broadcasted_iota(jnp.int32, s_n.shape, 1)
    mask_n = ((c % g) == b_q) & ((c // g) <= t_q)
    s_c = jnp.where(mask_c, s_c, -jnp.inf)
    s_n = jnp.where(mask_n, s_n, -jnp.inf)

    sink = _sink_column(sinks_ref, layer, rows)
    m = jnp.maximum(jnp.maximum(jnp.max(s_c, axis=-1, keepdims=True),
                                jnp.max(s_n, axis=-1, keepdims=True)), sink)
    p_c = jnp.exp(s_c - m)
    p_n = jnp.exp(s_n - m)
    denom = (jnp.sum(p_c, axis=-1, keepdims=True) + jnp.sum(p_n, axis=-1, keepdims=True)
             + jnp.exp(sink - m))
    o = (jnp.dot(p_c.astype(BF16), v_cache, preferred_element_type=F32)
         + jnp.dot(p_n.astype(BF16), v_new, preferred_element_type=F32)) / denom
    attn_ref[...] = _collect_heads(o, rows).reshape(dec_seq, g, ATTN_WIDTH)


def _attn_sample(z3, cache_k, cache_v, sinks, layer):
    dec_seq, dec_batch, _ = z3.shape
    cache_len = cache_k.shape[2]
    g = SAMPLE_GROUP
    cache_spec = pl.BlockSpec((None, g, cache_len, KV_WIDTH), lambda i: (layer, i, 0, 0))
    return pl.pallas_call(
        functools.partial(_attn_sample_kernel, layer),
        grid=(dec_batch // g,),
        in_specs=[
            pl.BlockSpec(memory_space=pltpu.SMEM),
            pl.BlockSpec((dec_seq, g, O_XR), lambda i: (0, i, 0)),
            cache_spec, cache_spec,
        ],
        out_specs=pl.BlockSpec((dec_seq, g, ATTN_WIDTH), lambda i: (0, i, 0)),
        out_shape=jax.ShapeDtypeStruct((dec_seq, dec_batch, ATTN_WIDTH), F32),
        compiler_params=pltpu.CompilerParams(
            dimension_semantics=("arbitrary",), vmem_limit_bytes=VMEM_LIMIT_BYTES),
        name="attn_sample",
    )(sinks, z3, cache_k, cache_v)


def _shift_caches_kernel(kvnew_ref, ck_ref, cv_ref, nk_ref, nv_ref):
    dec_seq = kvnew_ref.shape[1]
    cache_len = ck_ref.shape[1]
    nk_ref[:, 0:cache_len - dec_seq, :] = ck_ref[:, dec_seq:, :]
    nv_ref[:, 0:cache_len - dec_seq, :] = cv_ref[:, dec_seq:, :]
    nk_ref[:, cache_len - dec_seq:, :] = kvnew_ref[:, :, 0:KV_WIDTH]
    nv_ref[:, cache_len - dec_seq:, :] = kvnew_ref[:, :, KV_WIDTH:]


def _shift_caches(kvnew, cache_k, cache_v):
    depth, dec_batch, cache_len, _ = cache_k.shape
    dec_seq = kvnew.shape[2]
    g = SHIFT_GROUP
    spec = lambda rows, width: pl.BlockSpec((None, g, rows, width), lambda l, i: (l, i, 0, 0))
    return pl.pallas_call(
        _shift_caches_kernel,
        grid=(depth, dec_batch // g),
        in_specs=[spec(dec_seq, 2 * KV_WIDTH), spec(cache_len, KV_WIDTH), spec(cache_len, KV_WIDTH)],
        out_specs=[spec(cache_len, KV_WIDTH), spec(cache_len, KV_WIDTH)],
        out_shape=[jax.ShapeDtypeStruct(cache_k.shape, F32), jax.ShapeDtypeStruct(cache_v.shape, F32)],
        compiler_params=pltpu.CompilerParams(dimension_semantics=("arbitrary", "arbitrary")),
        name="shift_caches",
    )(kvnew, cache_k, cache_v)


def _mix_sample_kernel(x1_ref, z_ref, attn_ref, sconv_ref, sh_ref, conv_w_ref, conv_b_ref, wgate_ref,
                       ba_ref, bx_ref, lam_ref, g_attn_ref, g_lru_ref, wo_ref, g_postmix_ref,
                       g_pre2_ref, wg_ref, wu_ref, wd_ref, g_post2_ref,
                       x3_ref, nconv_ref, nh_ref, act_scr):
    nb = sh_ref.shape[0]
    dec_seq = x1_ref.shape[0] // nb
    xr = z_ref[:, O_XR:O_YG]
    xp = [sconv_ref[j] for j in range(CONV_W - 1)] + [xr[nb * t:nb * (t + 1)] for t in range(dec_seq)]
    xc_steps = []
    for t in range(dec_seq):
        acc = conv_b_ref[...] + xp[t] * conv_w_ref[0:1, :]
        for j in range(1, CONV_W):
            acc = acc + xp[t + j] * conv_w_ref[j:j + 1, :]
        xc_steps.append(acc)
    for j in range(CONV_W - 1):
        nconv_ref[j] = xp[dec_seq + j]
    xc = jnp.concatenate(xc_steps, axis=0)
    gates = _lru_gates(xc, wgate_ref)
    a, u = _lru_coeffs(xc, gates[:, :LRU_WIDTH], gates[:, LRU_WIDTH:], ba_ref[...], bx_ref[...],
                       lam_ref[...])
    h = sh_ref[...]
    hs_steps = []
    for t in range(dec_seq):
        h = a[nb * t:nb * (t + 1)] * h + u[nb * t:nb * (t + 1)]
        hs_steps.append(h)
    nh_ref[...] = h
    hs = jnp.concatenate(hs_steps, axis=0)
    x2 = _mix_out(x1_ref[...], attn_ref[...], hs, z_ref[:, O_YG:], g_attn_ref[...], g_lru_ref[...],
                  wo_ref, g_postmix_ref[...])
    x3_ref[...] = _half_ffn(x2, g_pre2_ref[...], wg_ref, wu_ref, wd_ref, g_post2_ref[...], act_scr)


def _mix_sample(x1, z, attn, sconv, state_h, w, layer):
    n = x1.shape[0]
    nb = state_h.shape[1]
    full = lambda *shape: pl.BlockSpec(shape, lambda i: (0,) * len(shape))
    return pl.pallas_call(
        _mix_sample_kernel,
        grid=(1,),
        in_specs=[
            full(n, D_MODEL), full(n, IN_WIDTH), full(n, ATTN_WIDTH),
            full(CONV_W - 1, nb, LRU_WIDTH), _layer_spec((nb, LRU_WIDTH), layer, 1),
        ] + _mix_weight_specs(layer, 1),
        out_specs=[full(n, D_MODEL), full(CONV_W - 1, nb, LRU_WIDTH), full(nb, LRU_WIDTH)],
        out_shape=[
            jax.ShapeDtypeStruct((n, D_MODEL), F32),
            jax.ShapeDtypeStruct((CONV_W - 1, nb, LRU_WIDTH), F32),
            jax.ShapeDtypeStruct((nb, LRU_WIDTH), F32),
        ],
        scratch_shapes=[pltpu.VMEM((n, D_FF), BF16)],
        compiler_params=pltpu.CompilerParams(
            dimension_semantics=("arbitrary",), vmem_limit_bytes=VMEM_LIMIT_BYTES),
        name="mix_sample",
    )(x1, z, attn, sconv, state_h, *_mix_weights(w))


def _rope_tables(pos):
    half = ROT_DIM // 2
    inv = ROPE_THETA ** (-(jnp.arange(half, dtype=F32) * 2.0) / ROT_DIM)
    ang = pos.astype(F32)[:, None] * inv[None, :]
    cos, sin = jnp.cos(ang), jnp.sin(ang)
    n = pos.shape[0]
    rest = jnp.zeros((n, HEAD_DIM - ROT_DIM), F32)
    zero = jnp.zeros((n, half), F32)
    per_head = lambda parts: jnp.tile(jnp.concatenate(parts, axis=-1), (1, LANES // HEAD_DIM))
    return (per_head([cos, cos, rest + 1.0]), per_head([-sin, zero, rest]), per_head([zero, sin, rest]))


def _block_diag(w):
    eye = jnp.eye(LRU_HEADS, dtype=w.dtype)
    return jnp.einsum("lhij,hg->lhigj", w, eye).reshape(w.shape[0], LRU_WIDTH, LRU_WIDTH)


def kernel(x_prompt, x_sample, cache_k, cache_v, state_conv, state_h, norm_pre_ffn1, ffn1_w_gate, ffn1_w_up, ffn1_w_down, norm_post_ffn1, norm_pre_mix, w_in, sinks, conv_w, conv_b, gate_a_w, gate_a_b, gate_x_w, gate_x_b, lru_lambda, attn_out_norm, lru_out_norm, w_o, norm_post_mix, norm_pre_ffn2, ffn2_w_gate, ffn2_w_up, ffn2_w_down, norm_post_ffn2):
    batch, seq, _ = x_prompt.shape
    dec_batch, dec_seq, _ = x_sample.shape
    depth = w_in.shape[0]
    cache_len = cache_k.shape[2]
    n_s = dec_seq * dec_batch
    assert seq % FFN_TILE == 0 and n_s % FFN_TILE == 0
    assert seq % MIX_TILE == 0 and MIX_TILE % WINDOW == 0
    assert dec_batch % SAMPLE_GROUP == 0 and dec_batch % SHIFT_GROUP == 0

    tabs_p = _rope_tables(jnp.arange(seq))
    tabs_s = _rope_tables(jnp.repeat(PAST_LEN + jnp.arange(dec_seq), dec_batch))
    xp = x_prompt.reshape(batch * seq, D_MODEL)
    xs = jnp.swapaxes(x_sample, 0, 1).reshape(n_s, D_MODEL)
    rows = lambda v: v.reshape(depth, 1, -1)

    w = dict(
        g_pre1=rows(norm_pre_ffn1), wg1=_to_bf16(ffn1_w_gate), wu1=_to_bf16(ffn1_w_up),
        wd1=_to_bf16(ffn1_w_down), g_post1=rows(norm_post_ffn1), g_mix=rows(norm_pre_mix),
        w_in=_to_bf16(w_in), sinks=sinks, conv_w=conv_w, conv_b=rows(conv_b),
        w_gate=jnp.concatenate([_block_diag(gate_a_w), _block_diag(gate_x_w)], axis=2).astype(BF16),
        gate_a_b=rows(gate_a_b), gate_x_b=rows(gate_x_b), lam=rows(lru_lambda),
        g_attn=rows(attn_out_norm), g_lru=rows(lru_out_norm), w_o=_to_bf16(w_o),
        g_postmix=rows(norm_post_mix), g_pre2=rows(norm_pre_ffn2), wg2=_to_bf16(ffn2_w_gate),
        wu2=_to_bf16(ffn2_w_up), wd2=_to_bf16(ffn2_w_down), g_post2=rows(norm_post_ffn2),
    )
    ck = cache_k.reshape(depth, dec_batch, cache_len, KV_WIDTH)
    cv = cache_v.reshape(depth, dec_batch, cache_len, KV_WIDTH)

    outs = [[] for _ in range(6)]
    kvnew = []
    for l in range(depth):
        x1p, zp = _ffn_in(xp, tabs_p, seq // FFN_TILE, w, l, FFN_TILE)
        xp, nk, nv, nc, nh = _mix_prompt(x1p, zp, w, l, batch, seq, MIX_TILE)

        x1s, zs = _ffn_in(xs, tabs_s, n_s // FFN_TILE, w, l, FFN_TILE)
        z3 = zs.reshape(dec_seq, dec_batch, IN_WIDTH)
        kvnew.append(jnp.swapaxes(z3[:, :, O_K:O_XR], 0, 1))
        attn_s = _attn_sample(z3, ck, cv, sinks, l)
        xs, ncs, nhs = _mix_sample(x1s, zs, attn_s.reshape(n_s, ATTN_WIDTH),
                                   jnp.swapaxes(state_conv[l], 0, 1), state_h, w, l)
        kv_p = (batch, WINDOW, N_KV_HEADS, HEAD_DIM)
        for acc, val in zip(outs, (nk.reshape(kv_p), nv.reshape(kv_p), nc,
                                   nh.reshape(batch, LRU_WIDTH), jnp.swapaxes(ncs, 0, 1), nhs)):
            acc.append(val)

    nks, nvs = _shift_caches(jnp.stack(kvnew), ck, cv)
    y_prompt = xp.reshape(batch, seq, D_MODEL)
    y_sample = jnp.swapaxes(xs.reshape(dec_seq, dec_batch, D_MODEL), 0, 1)
    kp, vp, cp, hp, cs, hs = (jnp.stack(o) for o in outs)
    return (y_prompt, y_sample, kp, vp, cp, hp, nks.reshape(cache_k.shape), nvs.reshape(cache_v.shape),
            cs, hs)
```

```python
import functools

import jax
import jax.numpy as jnp
import numpy as np
from jax import lax
from jax.experimental import pallas as pl
from jax.experimental.pallas import tpu as pltpu

D_MODEL = 1024
PAST_LEN = 16384
N_HEADS = 8
HEAD_DIM = 64
N_KV_HEADS = 2
GROUP = N_HEADS // N_KV_HEADS
ATTN_WIDTH = N_HEADS * HEAD_DIM
KV_WIDTH = N_KV_HEADS * HEAD_DIM
WINDOW = 128
ROPE_THETA = 500000.0
ROT_DIM = HEAD_DIM // 4
LRU_WIDTH = D_MODEL // 2
LRU_HEADS = 8
LRU_BLOCK = LRU_WIDTH // LRU_HEADS
CONV_W = 4
LRU_C = 8.0
D_FF = 2816
IN_WIDTH = ATTN_WIDTH + 2 * KV_WIDTH + 2 * LRU_WIDTH
EPS = 1e-6

LANES = 128
SUBLANES = 8
FF_CHUNK = 256
N_FF_CHUNKS = D_FF // FF_CHUNK
FFN_TILE = 512
MIX_TILE = 512
SAMPLE_GROUP = 8
SHIFT_GROUP = 16
CAST_ROWS = 512
SIDE_ROWS = 1024
BF16_SUBLANES = 16
VMEM_LIMIT_BYTES = 56 * 1024 * 1024

O_K = ATTN_WIDTH
O_V = O_K + KV_WIDTH
O_XR = O_V + KV_WIDTH
O_YG = O_XR + LRU_WIDTH

F32 = jnp.float32
BF16 = jnp.bfloat16
LOG2_E = np.float32(np.log2(np.e))


def _rmsnorm(x, g):
    return (x * lax.rsqrt(jnp.mean(x * x, axis=-1, keepdims=True) + EPS)) * g


def _half_ffn_steps(x, g_pre, wg_ref, wu_ref, wd_ref, g_post, act_scr):
    h = _rmsnorm(x, g_pre).astype(BF16)
    for c in range(N_FF_CHUNKS):
        cols = slice(FF_CHUNK * c, FF_CHUNK * (c + 1))
        g = jnp.dot(h, wg_ref[:, cols], preferred_element_type=F32)
        u = jnp.dot(h, wu_ref[:, cols], preferred_element_type=F32)
        act_scr[:, cols] = ((g * jax.nn.sigmoid(g)) * u).astype(BF16)
        yield
    ys = []
    for c in range(D_MODEL // FF_CHUNK):
        ys.append(jnp.dot(act_scr[...], wd_ref[:, FF_CHUNK * c:FF_CHUNK * (c + 1)],
                          preferred_element_type=F32))
        yield
    return x + _rmsnorm(jnp.concatenate(ys, axis=1), 0.5 * g_post)


def _advance(gen, done):
    if gen in done:
        return
    try:
        next(gen)
    except StopIteration as stop:
        done[gen] = stop.value


def _interleave(first, second):
    done = {}
    while len(done) < 2:
        _advance(first, done)
        _advance(second, done)
    return done[first], done[second]


def _run(gen):
    done = {}
    while gen not in done:
        _advance(gen, done)
    return done[gen]


def _half_ffn(*args):
    return _run(_half_ffn_steps(*args))


def _gelu_tanh_times(x, y):
    c = np.float32(np.sqrt(2.0 / np.pi))
    t = jnp.tanh(x * ((x * x) * (np.float32(0.044715) * c) + c))
    w = (0.5 * x) * y
    return t * w + w


def _softplus(x):
    return jnp.maximum(x, 0.0) + jnp.log1p(jnp.exp(-jnp.abs(x)))


def _lru_gates(xc, wgate_ref):
    return jnp.dot(xc.astype(BF16), wgate_ref[...], preferred_element_type=F32)


def _lru_coeffs(xc, gate_a, gate_x, ba, bx, lam):
    r = jax.nn.sigmoid(gate_a + ba)
    ig = jax.nn.sigmoid(gate_x + bx)
    a = jnp.exp2(r * ((-LRU_C * LOG2_E) * _softplus(-lam)))
    v = 1.0 - a * a
    u = jnp.where(v > 0.0, v * lax.rsqrt(v), 0.0) * (ig * xc)
    return a, u


def _scan_rows(a, u):
    n = a.shape[0]
    row = lax.broadcasted_iota(jnp.int32, a.shape, 0)
    d = 1
    while d < n:
        keep = row >= d
        u = jnp.where(keep, a * pltpu.roll(u, d, 0) + u, u)
        if 2 * d < n:
            a = jnp.where(keep, a * pltpu.roll(a, d, 0), a)
        d *= 2
    return u


def _phase_blocks(slab_ref, j):
    groups = slab_ref.shape[1] // SUBLANES
    return [slab_ref[j, pl.ds(k, groups, stride=SUBLANES), :] for k in range(SUBLANES)]


def _prev_group(block, carry_row):
    first = lax.broadcasted_iota(jnp.int32, block.shape, 0) == 0
    return jnp.where(first, carry_row, pltpu.roll(block, 1, 0))


def _conv_phases(x_phases, carry_rows, w, b):
    wrapped = {SUBLANES - m: _prev_group(x_phases[SUBLANES - m], carry_rows[CONV_W - 1 - m])
               for m in range(1, CONV_W)}
    out = []
    for k in range(SUBLANES):
        acc = b
        for tap in range(CONV_W):
            src = k - (CONV_W - 1 - tap)
            x = x_phases[src] if src >= 0 else wrapped[src + SUBLANES]
            acc = acc + x * w[tap:tap + 1, :]
        out.append(acc)
    return out


def _scan_phases(a, u, h0):
    groups = a.shape[0] // SUBLANES
    blk = lambda v, k: v[groups * k:groups * (k + 1)]
    h_zero = [blk(u, 0)]
    a_prod = [blk(a, 0)]
    for k in range(1, SUBLANES):
        h_zero.append(blk(a, k) * h_zero[-1] + blk(u, k))
        a_prod.append(blk(a, k) * a_prod[-1])
    first = lax.broadcasted_iota(jnp.int32, a_prod[-1].shape, 0) == 0
    h_end = _scan_rows(a_prod[-1], jnp.where(first, a_prod[-1] * h0 + h_zero[-1], h_zero[-1]))
    h_in = _prev_group(h_end, h0)
    return [h_zero[k] + a_prod[k] * h_in for k in range(SUBLANES)]


def _expand_q(q, rows):
    lane = lax.broadcasted_iota(jnp.int32, (rows, LANES), 1)
    pieces = []
    for h in range(N_HEADS):
        kv = h // GROUP
        src = q[:, LANES * (h // 2):LANES * (h // 2 + 1)]
        if (h % 2) != kv:
            src = pltpu.roll(src, HEAD_DIM, 1)
        in_half = (lane >= HEAD_DIM) if kv == 1 else (lane < HEAD_DIM)
        pieces.append(jnp.where(in_half, src, 0.0))
    return jnp.concatenate(pieces, axis=0).astype(BF16)


def _collect_heads(o, rows):
    lane = lax.broadcasted_iota(jnp.int32, (rows, LANES), 1)
    groups = []
    for j in range(N_HEADS // 2):
        kv = (2 * j) // GROUP
        even = o[rows * (2 * j):rows * (2 * j + 1)]
        odd = o[rows * (2 * j + 1):rows * (2 * j + 2)]
        if kv == 1:
            even = pltpu.roll(even, HEAD_DIM, 1)
        else:
            odd = pltpu.roll(odd, HEAD_DIM, 1)
        groups.append(jnp.where(lane < HEAD_DIM, even, odd))
    return jnp.concatenate(groups, axis=1)


def _collect_heads_t(o_t, rows):
    return jnp.concatenate(
        [o_t[HEAD_DIM * (h // GROUP):HEAD_DIM * (h // GROUP + 1), rows * h:rows * (h + 1)]
         for h in range(N_HEADS)], axis=0)


def _sink_row(sinks_ref, layer, rows):
    head = lax.broadcasted_iota(jnp.int32, (1, N_HEADS * rows), 1) // rows
    out = jnp.full((1, N_HEADS * rows), sinks_ref[layer, 0], F32)
    for h in range(1, N_HEADS):
        out = jnp.where(head == h, sinks_ref[layer, h], out)
    return out * LOG2_E


def _sink_column(sinks_ref, layer, rows):
    return jnp.concatenate([jnp.full((rows, 1), sinks_ref[layer, h], F32) for h in range(N_HEADS)],
                           axis=0) * LOG2_E


def _mix_normed(attn, hs, yg, g_attn, g_lru):
    lru = _gelu_tanh_times(yg, hs)
    return jnp.concatenate([_rmsnorm(attn, g_attn), _rmsnorm(lru, g_lru)], axis=-1).astype(BF16)


def _mix_project(x1, mixed, wo_ref, g_postmix):
    y = jnp.dot(mixed, wo_ref[...], preferred_element_type=F32)
    return x1 + _rmsnorm(y, g_postmix)


def _mix_out(x1, attn, hs, yg, g_attn, g_lru, wo_ref, g_postmix):
    return _mix_project(x1, _mix_normed(attn, hs, yg, g_attn, g_lru), wo_ref, g_postmix)


def _layer_spec(shape, layer, grid_rank):
    zeros = (0,) * len(shape)
    return pl.BlockSpec((None,) + tuple(shape), lambda *_: (layer,) + zeros,
                        pipeline_mode=pl.Buffered(1))


def _cast_kernel(w_ref, o_ref):
    o_ref[...] = w_ref[...].astype(BF16)


def _to_bf16(w, layer):
    _, rows, cols = w.shape
    return pl.pallas_call(
        _cast_kernel,
        grid=(rows // CAST_ROWS,),
        in_specs=[pl.BlockSpec((None, CAST_ROWS, cols), lambda i: (layer, i, 0))],
        out_specs=pl.BlockSpec((CAST_ROWS, cols), lambda i: (i, 0)),
        out_shape=jax.ShapeDtypeStruct((rows, cols), BF16),
        compiler_params=pltpu.CompilerParams(dimension_semantics=("arbitrary",)),
        name="cast_bf16",
    )(w)


def _side_view(w):
    return w.reshape(w.shape[0], SIDE_ROWS, -1)


def _side_cast_specs(sources, layer, n_tiles):
    rows = SIDE_ROWS // n_tiles
    assert SIDE_ROWS % n_tiles == 0 and rows % BF16_SUBLANES == 0
    block = lambda g: jnp.minimum(g, n_tiles - 1)
    in_specs = [pl.BlockSpec((None, rows, s.shape[2]), lambda g: (layer, block(g), 0))
                for s in sources]
    out_specs = [pl.BlockSpec((rows, s.shape[2]), lambda g: (block(g), 0)) for s in sources]
    out_shapes = [jax.ShapeDtypeStruct((SIDE_ROWS, s.shape[2]), BF16) for s in sources]
    return in_specs, out_specs, out_shapes


def _side_cast(src_refs, dst_refs):
    for src, dst in zip(src_refs, dst_refs):
        dst[...] = src[...].astype(BF16)


def _resident(shape):
    return pl.BlockSpec(shape, lambda *_: (0,) * len(shape))


def _in_proj_steps(x1, g_mix, win_ref, cos, sin_up, sin_dn, z_ref):
    h = _rmsnorm(x1, g_mix).astype(BF16)
    half = ROT_DIM // 2
    for c in range(IN_WIDTH // FF_CHUNK):
        zc = jnp.dot(h, win_ref[:, FF_CHUNK * c:FF_CHUNK * (c + 1)], preferred_element_type=F32)
        for k in range(FF_CHUNK // LANES):
            j = (FF_CHUNK // LANES) * c + k
            zj = zc[:, LANES * k:LANES * (k + 1)]
            if j < O_V // LANES:
                zj = (zj * cos + pltpu.roll(zj, LANES - half, 1) * sin_up
                      + pltpu.roll(zj, half, 1) * sin_dn)
            if j < ATTN_WIDTH // LANES:
                zj = zj * (LOG2_E * HEAD_DIM ** -0.5)
            z_ref[:, LANES * j:LANES * (j + 1)] = zj
        yield


def _ffn_in_kernel(skewed, n_side, *refs):
    (x_ref, cos_ref, sin_up_ref, sin_dn_ref, g_pre_ref, wg_ref, wu_ref, wd_ref, g_post_ref,
     g_mix_ref, win_ref) = refs[:11]
    side_src = refs[11:11 + n_side]
    x1_ref, z_ref = refs[11 + n_side:13 + n_side]
    side_dst = refs[13 + n_side:13 + 2 * n_side]
    act_scr, *x1_scr = refs[13 + 2 * n_side:]

    def ffn_steps():
        return _half_ffn_steps(x_ref[...], g_pre_ref[...], wg_ref, wu_ref, wd_ref, g_post_ref[...],
                               act_scr)

    def in_proj_steps(x1):
        return _in_proj_steps(x1, g_mix_ref[...], win_ref, cos_ref[...], sin_up_ref[...],
                              sin_dn_ref[...], z_ref)

    if not skewed:
        x1 = _run(ffn_steps())
        x1_ref[...] = x1
        _run(in_proj_steps(x1))
        return
    (x1_scr,) = x1_scr
    g_idx = pl.program_id(0)
    n_tiles = pl.num_programs(0) - 1

    @pl.when(g_idx == 0)
    def _():
        x1_scr[...] = jnp.zeros(x1_scr.shape, F32)

    @pl.when(g_idx < n_tiles)
    def _():
        x1, _ = _interleave(ffn_steps(), in_proj_steps(x1_scr[...]))
        x1_ref[...] = x1
        x1_scr[...] = x1
        _side_cast(side_src, side_dst)

    @pl.when(g_idx == n_tiles)
    def _():
        _run(in_proj_steps(x1_scr[...]))


def _ffn_in(x, tabs, tab_tiles, w, big, layer, tile, side=()):
    n = x.shape[0] // tile
    skewed = n > 1
    assert skewed or not side
    ffn_tile = (lambda g: jnp.minimum(g, n - 1)) if skewed else (lambda g: g)
    proj_tile = (lambda g: jnp.maximum(g - 1, 0)) if skewed else (lambda g: g)
    lspec = lambda *shape: _layer_spec(shape, layer, 1)
    tab_spec = pl.BlockSpec((tile, LANES), lambda g: (proj_tile(g) % tab_tiles, 0))
    side_in, side_out, side_shapes = _side_cast_specs(side, layer, n)
    wg, wu, wd, w_in = big
    return pl.pallas_call(
        functools.partial(_ffn_in_kernel, skewed, len(side)),
        grid=(n + 1 if skewed else n,),
        in_specs=[
            pl.BlockSpec((tile, D_MODEL), lambda g: (ffn_tile(g), 0)),
            tab_spec, tab_spec, tab_spec,
            lspec(1, D_MODEL), _resident(wg.shape), _resident(wu.shape), _resident(wd.shape),
            lspec(1, D_MODEL), lspec(1, D_MODEL), _resident(w_in.shape),
        ] + side_in,
        out_specs=[
            pl.BlockSpec((tile, D_MODEL), lambda g: (ffn_tile(g), 0)),
            pl.BlockSpec((tile, IN_WIDTH), lambda g: (proj_tile(g), 0)),
        ] + side_out,
        out_shape=[
            jax.ShapeDtypeStruct((n * tile, D_MODEL), F32),
            jax.ShapeDtypeStruct((n * tile, IN_WIDTH), F32),
        ] + side_shapes,
        scratch_shapes=[pltpu.VMEM((tile, D_FF), BF16)]
                       + ([pltpu.VMEM((tile, D_MODEL), F32)] if skewed else []),
        compiler_params=pltpu.CompilerParams(
            dimension_semantics=("arbitrary",), vmem_limit_bytes=VMEM_LIMIT_BYTES),
        name="ffn_in",
    )(x, *tabs, w["g_pre1"], wg, wu, wd, w["g_post1"], w["g_mix"], w_in, *side)


def _mix_prompt_kernel(layer, n_seq_tiles, n_side, *refs):
    sinks_ref, x1_ref, z_ref = refs[:3]
    (conv_w_ref, conv_b_ref, wgate_ref, ba_ref, bx_ref, lam_ref, g_attn_ref, g_lru_ref, wo_ref,
     g_postmix_ref, g_pre2_ref, wg_ref, wu_ref, wd_ref, g_post2_ref) = refs[3:3 + N_MIX_WEIGHTS]
    n_in = 3 + N_MIX_WEIGHTS + n_side
    side_src = refs[3 + N_MIX_WEIGHTS:n_in]
    x3_ref, nk_ref, nv_ref, nconv_ref, nh_ref = refs[n_in:n_in + 5]
    side_dst = refs[n_in + 5:n_in + 5 + n_side]
    (k_scr, vt_scr, conv_scr, h_scr, x2_scr, bias_scr, attn_scr, xr_scr, xc_scr, gates_scr, hs_scr,
     act_scr) = refs[n_in + 5 + n_side:]
    tile = x1_ref.shape[0]
    g_idx = pl.program_id(0)
    n_tiles = pl.num_programs(0) - 1
    s_idx = g_idx % n_seq_tiles

    @pl.when(g_idx == 0)
    def _():
        x2_scr[...] = jnp.zeros(x2_scr.shape, F32)
        kj = lax.broadcasted_iota(jnp.int32, (2 * WINDOW, N_HEADS * WINDOW), 0)
        qi = lax.broadcasted_iota(jnp.int32, (2 * WINDOW, N_HEADS * WINDOW), 1) & (WINDOW - 1)
        band = (kj > qi) & (kj <= qi + WINDOW)
        bias_scr[0] = jnp.where(band, 0.0, -jnp.inf)
        bias_scr[1] = jnp.where(band & (kj >= WINDOW), 0.0, -jnp.inf)

    @pl.when(s_idx == 0)
    def _():
        k_scr[0:WINDOW, :] = jnp.zeros((WINDOW, KV_WIDTH), BF16)
        vt_scr[:, 0:WINDOW] = jnp.zeros((KV_WIDTH, WINDOW), BF16)
        conv_scr[...] = jnp.zeros((CONV_W - 1, LRU_WIDTH), F32)
        h_scr[...] = jnp.zeros((1, LRU_WIDTH), F32)

    def mixer_steps():
        k_scr[WINDOW:, :] = z_ref[:, O_K:O_V].astype(BF16)
        vt_scr[:, WINDOW:] = z_ref[:, O_V:O_XR].T.astype(BF16)
        sink = _sink_row(sinks_ref, layer, WINDOW)
        for i in range(tile // WINDOW):
            q_exp = _expand_q(z_ref[WINDOW * i:WINDOW * (i + 1), 0:ATTN_WIDTH], WINDOW)
            keys = k_scr[WINDOW * i:WINDOW * (i + 2), :]
            vals_t = vt_scr[:, WINDOW * i:WINDOW * (i + 2)]
            s = lax.dot_general(keys, q_exp, (((1,), (1,)), ((), ())), preferred_element_type=F32)
            s = s + bias_scr[jnp.where(s_idx == 0, 1, 0) if i == 0 else 0]
            m = jnp.maximum(jnp.max(s, axis=0, keepdims=True), sink)
            p = jnp.exp2(s - m)
            denom = jnp.sum(p, axis=0, keepdims=True) + jnp.exp2(sink - m)
            yield
            o_t = jnp.dot(vals_t, p.astype(BF16), preferred_element_type=F32) * (1.0 / denom)
            attn_scr[WINDOW * i:WINDOW * (i + 1), :] = _collect_heads_t(o_t, WINDOW).T
            yield
        k_scr[0:WINDOW, :] = k_scr[tile:tile + WINDOW, :]
        vt_scr[:, 0:WINDOW] = vt_scr[:, tile:tile + WINDOW]

        for j in range(LRU_WIDTH // LANES):
            ch = slice(LANES * j, LANES * (j + 1))
            xr_scr[j] = z_ref[:, O_XR + LANES * j:O_XR + LANES * (j + 1)]
            carry_rows = [conv_scr[m:m + 1, ch] for m in range(CONV_W - 1)]
            xc_phases = _conv_phases(_phase_blocks(xr_scr, j), carry_rows, conv_w_ref[:, ch],
                                     conv_b_ref[:, ch])
            xc_scr[:, ch] = jnp.concatenate(xc_phases, axis=0)
        conv_scr[...] = z_ref[tile - (CONV_W - 1):, O_XR:O_YG]
        gates_scr[...] = _lru_gates(xc_scr[...], wgate_ref)
        yield

        groups = tile // SUBLANES
        h_last = []
        for j in range(LRU_WIDTH // LANES):
            ch = slice(LANES * j, LANES * (j + 1))
            gate_x_ch = slice(LRU_WIDTH + LANES * j, LRU_WIDTH + LANES * (j + 1))
            a, u = _lru_coeffs(xc_scr[:, ch], gates_scr[:, ch], gates_scr[:, gate_x_ch],
                               ba_ref[:, ch], bx_ref[:, ch], lam_ref[:, ch])
            yield
            h_phases = _scan_phases(a, u, h_scr[:, ch])
            for k in range(SUBLANES):
                hs_scr[j, pl.ds(k, groups, stride=SUBLANES), :] = h_phases[k]
            h_last.append(h_phases[SUBLANES - 1][groups - 1:groups, :])
            yield
        h_last = jnp.concatenate(h_last, axis=1)
        h_scr[...] = h_last
        hs = jnp.concatenate([hs_scr[j] for j in range(LRU_WIDTH // LANES)], axis=1)
        mixed = _mix_normed(attn_scr[...], hs, z_ref[:, O_YG:], g_attn_ref[...], g_lru_ref[...])
        yield
        x2 = _mix_project(x1_ref[...], mixed, wo_ref, g_postmix_ref[...])
        return x2, h_last

    def ffn_steps():
        return _half_ffn_steps(x2_scr[...], g_pre2_ref[...], wg_ref, wu_ref, wd_ref,
                               g_post2_ref[...], act_scr)

    @pl.when(g_idx < n_tiles)
    def _():
        x3, (x2, h_last) = _interleave(ffn_steps(), mixer_steps())
        x3_ref[...] = x3
        x2_scr[...] = x2
        nk_ref[0] = z_ref[tile - WINDOW:, O_K:O_V]
        nv_ref[0] = z_ref[tile - WINDOW:, O_V:O_XR]
        nconv_ref[0] = z_ref[tile - (CONV_W - 1):, O_XR:O_YG]
        nh_ref[0] = h_last
        _side_cast(side_src, side_dst)

    @pl.when(g_idx == n_tiles)
    def _():
        x3_ref[...] = _run(ffn_steps())


N_MIX_WEIGHTS = 15


def _mix_weight_specs(layer, big):
    lspec = lambda *shape: _layer_spec(shape, layer, 1)
    w_o, wg, wu, wd = big
    return [
        lspec(CONV_W, LRU_WIDTH), lspec(1, LRU_WIDTH), lspec(LRU_WIDTH, 2 * LRU_WIDTH),
        lspec(1, LRU_WIDTH), lspec(1, LRU_WIDTH), lspec(1, LRU_WIDTH),
        lspec(1, ATTN_WIDTH), lspec(1, LRU_WIDTH), _resident(w_o.shape), lspec(1, D_MODEL),
        lspec(1, D_MODEL), _resident(wg.shape), _resident(wu.shape), _resident(wd.shape),
        lspec(1, D_MODEL),
    ]


def _mix_weights(w, big):
    w_o, wg, wu, wd = big
    return (w["conv_w"], w["conv_b"], w["w_gate"], w["gate_a_b"], w["gate_x_b"], w["lam"],
            w["g_attn"], w["g_lru"], w_o, w["g_postmix"], w["g_pre2"], wg, wu, wd, w["g_post2"])


def _mix_prompt(x1, z, w, big, layer, batch, seq, tile, side=(), side_layer=0):
    n_s = seq // tile
    n_tiles = batch * n_s
    mix_tile = lambda g: jnp.minimum(g, n_tiles - 1)
    tok = lambda width: pl.BlockSpec((tile, width), lambda g: (mix_tile(g), 0))
    per_batch = lambda rows, width: pl.BlockSpec((1, rows, width), lambda g: (mix_tile(g) // n_s, 0, 0))
    side_in, side_out, side_shapes = _side_cast_specs(side, side_layer, n_tiles)
    return pl.pallas_call(
        functools.partial(_mix_prompt_kernel, layer, n_s, len(side)),
        grid=(n_tiles + 1,),
        in_specs=[pl.BlockSpec(memory_space=pltpu.SMEM), tok(D_MODEL), tok(IN_WIDTH)]
                 + _mix_weight_specs(layer, big) + side_in,
        out_specs=[
            pl.BlockSpec((tile, D_MODEL), lambda g: (jnp.maximum(g - 1, 0), 0)),
            per_batch(WINDOW, KV_WIDTH), per_batch(WINDOW, KV_WIDTH),
            per_batch(CONV_W - 1, LRU_WIDTH), per_batch(1, LRU_WIDTH),
        ] + side_out,
        out_shape=[
            jax.ShapeDtypeStruct((batch * seq, D_MODEL), F32),
            jax.ShapeDtypeStruct((batch, WINDOW, KV_WIDTH), F32),
            jax.ShapeDtypeStruct((batch, WINDOW, KV_WIDTH), F32),
            jax.ShapeDtypeStruct((batch, CONV_W - 1, LRU_WIDTH), F32),
            jax.ShapeDtypeStruct((batch, 1, LRU_WIDTH), F32),
        ] + side_shapes,
        scratch_shapes=[
            pltpu.VMEM((WINDOW + tile, KV_WIDTH), BF16),
            pltpu.VMEM((KV_WIDTH, WINDOW + tile), BF16),
            pltpu.VMEM((CONV_W - 1, LRU_WIDTH), F32),
            pltpu.VMEM((1, LRU_WIDTH), F32),
            pltpu.VMEM((tile, D_MODEL), F32),
            pltpu.VMEM((2, 2 * WINDOW, N_HEADS * WINDOW), F32),
            pltpu.VMEM((tile, ATTN_WIDTH), F32),
            pltpu.VMEM((LRU_WIDTH // LANES, tile, LANES), F32),
            pltpu.VMEM((tile, LRU_WIDTH), F32),
            pltpu.VMEM((tile, 2 * LRU_WIDTH), F32),
            pltpu.VMEM((LRU_WIDTH // LANES, tile, LANES), F32),
            pltpu.VMEM((tile, D_FF), BF16),
        ],
        compiler_params=pltpu.CompilerParams(
            dimension_semantics=("arbitrary",), vmem_limit_bytes=VMEM_LIMIT_BYTES),
        name="mix_prompt",
    )(w["sinks"], x1, z, *_mix_weights(w, big), *side)


def _attn_sample_kernel(layer, sinks_ref, zq_ref, ck_ref, cv_ref, attn_ref):
    dec_seq, g = zq_ref.shape[0], zq_ref.shape[1]
    rows = dec_seq * g
    cache_len = ck_ref.shape[1]
    zq = zq_ref[...].reshape(rows, zq_ref.shape[2])
    q_exp = _expand_q(zq[:, 0:ATTN_WIDTH], rows)
    k_new = zq[:, O_K:O_V].astype(BF16)
    v_new = zq[:, O_V:O_XR].astype(BF16)
    k_cache = ck_ref[...].reshape(g * cache_len, KV_WIDTH).astype(BF16)
    v_cache = cv_ref[...].reshape(g * cache_len, KV_WIDTH).astype(BF16)
    contract_last = (((1,), (1,)), ((), ()))
    s_c = lax.dot_general(q_exp, k_cache, contract_last, preferred_element_type=F32)
    s_n = lax.dot_general(q_exp, k_new, contract_last, preferred_element_type=F32)

    def row_ids(shape):
        r = lax.broadcasted_iota(jnp.int32, shape, 0)
        return (r % rows) // g, r % g

    t_q, b_q = row_ids(s_c.shape)
    c = lax.broadcasted_iota(jnp.int32, s_c.shape, 1)
    mask_c = ((c // cache_len) == b_q) & ((c % cache_len) > t_q)
    t_q, b_q = row_ids(s_n.shape)
    c = lax.broadcasted_iota(jnp.int32, s_n.shape, 1)
    mask_n = ((c % g) == b_q) & ((c // g) <= t_q)
    s_c = jnp.where(mask_c, s_c, -jnp.inf)
    s_n = jnp.where(mask_n, s_n, -jnp.inf)

    sink = _sink_column(sinks_ref, layer, rows)
    m = jnp.maximum(jnp.maximum(jnp.max(s_c, axis=-1, keepdims=True),
                                jnp.max(s_n, axis=-1, keepdims=True)), sink)
    p_c = jnp.exp2(s_c - m)
    p_n = jnp.exp2(s_n - m)
    denom = (jnp.sum(p_c, axis=-1, keepdims=True) + jnp.sum(p_n, axis=-1, keepdims=True)
             + jnp.exp2(sink - m))
    o = (jnp.dot(p_c.astype(BF16), v_cache, preferred_element_type=F32)
         + jnp.dot(p_n.astype(BF16), v_new, preferred_element_type=F32)) / denom
    attn_ref[...] = _collect_heads(o, rows).reshape(dec_seq, g, ATTN_WIDTH)


def _attn_sample(z3, cache_k, cache_v, sinks, layer):
    dec_seq, dec_batch, _ = z3.shape
    cache_len = cache_k.shape[2]
    g = SAMPLE_GROUP
    cache_spec = pl.BlockSpec((None, g, cache_len, KV_WIDTH), lambda i: (layer, i, 0, 0))
    return pl.pallas_call(
        functools.partial(_attn_sample_kernel, layer),
        grid=(dec_batch // g,),
        in_specs=[
            pl.BlockSpec(memory_space=pltpu.SMEM),
            pl.BlockSpec((dec_seq, g, O_XR), lambda i: (0, i, 0)),
            cache_spec, cache_spec,
        ],
        out_specs=pl.BlockSpec((dec_seq, g, ATTN_WIDTH), lambda i: (0, i, 0)),
        out_shape=jax.ShapeDtypeStruct((dec_seq, dec_batch, ATTN_WIDTH), F32),
        compiler_params=pltpu.CompilerParams(
            dimension_semantics=("arbitrary",), vmem_limit_bytes=VMEM_LIMIT_BYTES),
        name="attn_sample",
    )(sinks, z3, cache_k, cache_v)


def _shift_caches_kernel(kvnew_ref, ck_ref, cv_ref, nk_ref, nv_ref):
    dec_seq = kvnew_ref.shape[1]
    cache_len = ck_ref.shape[1]
    nk_ref[:, 0:cache_len - dec_seq, :] = ck_ref[:, dec_seq:, :]
    nv_ref[:, 0:cache_len - dec_seq, :] = cv_ref[:, dec_seq:, :]
    nk_ref[:, cache_len - dec_seq:, :] = kvnew_ref[:, :, 0:KV_WIDTH]
    nv_ref[:, cache_len - dec_seq:, :] = kvnew_ref[:, :, KV_WIDTH:]


def _shift_caches(kvnew, cache_k, cache_v):
    depth, dec_batch, cache_len, _ = cache_k.shape
    dec_seq = kvnew.shape[2]
    g = SHIFT_GROUP
    spec = lambda rows, width: pl.BlockSpec((None, g, rows, width), lambda l, i: (l, i, 0, 0))
    return pl.pallas_call(
        _shift_caches_kernel,
        grid=(depth, dec_batch // g),
        in_specs=[spec(dec_seq, 2 * KV_WIDTH), spec(cache_len, KV_WIDTH), spec(cache_len, KV_WIDTH)],
        out_specs=[spec(cache_len, KV_WIDTH), spec(cache_len, KV_WIDTH)],
        out_shape=[jax.ShapeDtypeStruct(cache_k.shape, F32), jax.ShapeDtypeStruct(cache_v.shape, F32)],
        compiler_params=pltpu.CompilerParams(dimension_semantics=("arbitrary", "arbitrary")),
        name="shift_caches",
    )(kvnew, cache_k, cache_v)


def _mix_sample_kernel(x1_ref, z_ref, attn_ref, sconv_ref, sh_ref, conv_w_ref, conv_b_ref, wgate_ref,
                       ba_ref, bx_ref, lam_ref, g_attn_ref, g_lru_ref, wo_ref, g_postmix_ref,
                       g_pre2_ref, wg_ref, wu_ref, wd_ref, g_post2_ref,
                       x3_ref, nconv_ref, nh_ref, act_scr):
    nb = sh_ref.shape[0]
    dec_seq = x1_ref.shape[0] // nb
    xr = z_ref[:, O_XR:O_YG]
    xp = [sconv_ref[j] for j in range(CONV_W - 1)] + [xr[nb * t:nb * (t + 1)] for t in range(dec_seq)]
    xc_steps = []
    for t in range(dec_seq):
        acc = conv_b_ref[...] + xp[t] * conv_w_ref[0:1, :]
        for j in range(1, CONV_W):
            acc = acc + xp[t + j] * conv_w_ref[j:j + 1, :]
        xc_steps.append(acc)
    for j in range(CONV_W - 1):
        nconv_ref[j] = xp[dec_seq + j]
    xc = jnp.concatenate(xc_steps, axis=0)
    gates = _lru_gates(xc, wgate_ref)
    a, u = _lru_coeffs(xc, gates[:, :LRU_WIDTH], gates[:, LRU_WIDTH:], ba_ref[...], bx_ref[...],
                       lam_ref[...])
    h = sh_ref[...]
    hs_steps = []
    for t in range(dec_seq):
        h = a[nb * t:nb * (t + 1)] * h + u[nb * t:nb * (t + 1)]
        hs_steps.append(h)
    nh_ref[...] = h
    hs = jnp.concatenate(hs_steps, axis=0)
    x2 = _mix_out(x1_ref[...], attn_ref[...], hs, z_ref[:, O_YG:], g_attn_ref[...], g_lru_ref[...],
                  wo_ref, g_postmix_ref[...])
    x3_ref[...] = _half_ffn(x2, g_pre2_ref[...], wg_ref, wu_ref, wd_ref, g_post2_ref[...], act_scr)


def _mix_sample(x1, z, attn, sconv, state_h, w, big, layer):
    n = x1.shape[0]
    nb = state_h.shape[1]
    full = lambda *shape: pl.BlockSpec(shape, lambda i: (0,) * len(shape))
    return pl.pallas_call(
        _mix_sample_kernel,
        grid=(1,),
        in_specs=[
            full(n, D_MODEL), full(n, IN_WIDTH), full(n, ATTN_WIDTH),
            full(CONV_W - 1, nb, LRU_WIDTH), _layer_spec((nb, LRU_WIDTH), layer, 1),
        ] + _mix_weight_specs(layer, big),
        out_specs=[full(n, D_MODEL), full(CONV_W - 1, nb, LRU_WIDTH), full(nb, LRU_WIDTH)],
        out_shape=[
            jax.ShapeDtypeStruct((n, D_MODEL), F32),
            jax.ShapeDtypeStruct((CONV_W - 1, nb, LRU_WIDTH), F32),
            jax.ShapeDtypeStruct((nb, LRU_WIDTH), F32),
        ],
        scratch_shapes=[pltpu.VMEM((n, D_FF), BF16)],
        compiler_params=pltpu.CompilerParams(
            dimension_semantics=("arbitrary",), vmem_limit_bytes=VMEM_LIMIT_BYTES),
        name="mix_sample",
    )(x1, z, attn, sconv, state_h, *_mix_weights(w, big))


def _rope_tables(pos):
    half = ROT_DIM // 2
    inv = ROPE_THETA ** (-(jnp.arange(half, dtype=F32) * 2.0) / ROT_DIM)
    ang = pos.astype(F32)[:, None] * inv[None, :]
    cos, sin = jnp.cos(ang), jnp.sin(ang)
    n = pos.shape[0]
    rest = jnp.zeros((n, HEAD_DIM - ROT_DIM), F32)
    zero = jnp.zeros((n, half), F32)
    per_head = lambda parts: jnp.tile(jnp.concatenate(parts, axis=-1), (1, LANES // HEAD_DIM))
    return (per_head([cos, cos, rest + 1.0]), per_head([-sin, zero, rest]), per_head([zero, sin, rest]))


def _block_diag(w):
    eye = jnp.eye(LRU_HEADS, dtype=w.dtype)
    return jnp.einsum("lhij,hg->lhigj", w, eye).reshape(w.shape[0], LRU_WIDTH, LRU_WIDTH)


def kernel(x_prompt, x_sample, cache_k, cache_v, state_conv, state_h, norm_pre_ffn1, ffn1_w_gate, ffn1_w_up, ffn1_w_down, norm_post_ffn1, norm_pre_mix, w_in, sinks, conv_w, conv_b, gate_a_w, gate_a_b, gate_x_w, gate_x_b, lru_lambda, attn_out_norm, lru_out_norm, w_o, norm_post_mix, norm_pre_ffn2, ffn2_w_gate, ffn2_w_up, ffn2_w_down, norm_post_ffn2):
    batch, seq, _ = x_prompt.shape
    dec_batch, dec_seq, _ = x_sample.shape
    depth = w_in.shape[0]
    cache_len = cache_k.shape[2]
    n_s = dec_seq * dec_batch
    assert seq % FFN_TILE == 0 and n_s % FFN_TILE == 0
    assert seq % MIX_TILE == 0 and MIX_TILE % WINDOW == 0
    assert dec_batch % SAMPLE_GROUP == 0 and dec_batch % SHIFT_GROUP == 0

    tabs_p = _rope_tables(jnp.arange(seq))
    tabs_s = _rope_tables(jnp.repeat(PAST_LEN + jnp.arange(dec_seq), dec_batch))
    xp = x_prompt.reshape(batch * seq, D_MODEL)
    xs = jnp.swapaxes(x_sample, 0, 1).reshape(n_s, D_MODEL)
    rows = lambda v: v.reshape(depth, 1, -1)

    w = dict(
        g_pre1=rows(norm_pre_ffn1), g_post1=rows(norm_post_ffn1), g_mix=rows(norm_pre_mix),
        sinks=sinks, conv_w=conv_w, conv_b=rows(conv_b),
        w_gate=jnp.concatenate([_block_diag(gate_a_w), _block_diag(gate_x_w)], axis=2).astype(BF16),
        gate_a_b=rows(gate_a_b), gate_x_b=rows(gate_x_b), lam=rows(lru_lambda),
        g_attn=rows(attn_out_norm), g_lru=rows(lru_out_norm),
        g_postmix=rows(norm_post_mix), g_pre2=rows(norm_pre_ffn2), g_post2=rows(norm_post_ffn2),
    )
    ffn_in_src = [_side_view(v) for v in (ffn1_w_gate, ffn1_w_up, ffn1_w_down, w_in)]
    ffn_in_shapes = [v.shape[1:] for v in (ffn1_w_gate, ffn1_w_up, ffn1_w_down, w_in)]
    mix_src = [_side_view(v) for v in (w_o, ffn2_w_gate, ffn2_w_up, ffn2_w_down)]
    mix_shapes = [v.shape[1:] for v in (w_o, ffn2_w_gate, ffn2_w_up, ffn2_w_down)]
    unview = lambda arrays, shapes: [a.reshape(s) for a, s in zip(arrays, shapes)]
    ck = cache_k.reshape(depth, dec_batch, cache_len, KV_WIDTH)
    cv = cache_v.reshape(depth, dec_batch, cache_len, KV_WIDTH)

    outs = [[] for _ in range(6)]
    kvnew = []
    ffn_in_big = unview([_to_bf16(s, 0) for s in ffn_in_src], ffn_in_shapes)
    for l in range(depth):
        x1p, zp, *cast = _ffn_in(xp, tabs_p, seq // FFN_TILE, w, ffn_in_big, l, FFN_TILE, side=mix_src)
        mix_big = unview(cast, mix_shapes)
        nxt = ffn_in_src if l + 1 < depth else ()
        xp, nk, nv, nc, nh, *cast = _mix_prompt(x1p, zp, w, mix_big, l, batch, seq, MIX_TILE,
                                                side=nxt, side_layer=l + 1)

        x1s, zs = _ffn_in(xs, tabs_s, n_s // FFN_TILE, w, ffn_in_big, l, FFN_TILE)
        z3 = zs.reshape(dec_seq, dec_batch, IN_WIDTH)
        kvnew.append(jnp.swapaxes(z3[:, :, O_K:O_XR], 0, 1))
        attn_s = _attn_sample(z3, ck, cv, sinks, l)
        xs, ncs, nhs = _mix_sample(x1s, zs, attn_s.reshape(n_s, ATTN_WIDTH),
                                   jnp.swapaxes(state_conv[l], 0, 1), state_h, w, mix_big, l)
        ffn_in_big = unview(cast, ffn_in_shapes)
        kv_p = (batch, WINDOW, N_KV_HEADS, HEAD_DIM)
        for acc, val in zip(outs, (nk.reshape(kv_p), nv.reshape(kv_p), nc,
                                   nh.reshape(batch, LRU_WIDTH), jnp.swapaxes(ncs, 0, 1), nhs)):
            acc.append(val)

    nks, nvs = _shift_caches(jnp.stack(kvnew), ck, cv)
    y_prompt = xp.reshape(batch, seq, D_MODEL)
    y_sample = jnp.swapaxes(xs.reshape(dec_seq, dec_batch, D_MODEL), 0, 1)
    kp, vp, cp, hp, cs, hs = (jnp.stack(o) for o in outs)
    return (y_prompt, y_sample, kp, vp, cp, hp, nks.reshape(cache_k.shape), nvs.reshape(cache_v.shape),
            cs, hs)
```

```python
import functools

import jax
import jax.numpy as jnp
import numpy as np
from jax import lax
from jax.experimental import pallas as pl
from jax.experimental.pallas import tpu as pltpu

D_MODEL = 1024
PAST_LEN = 16384
N_HEADS = 8
HEAD_DIM = 64
N_KV_HEADS = 2
GROUP = N_HEADS // N_KV_HEADS
ATTN_WIDTH = N_HEADS * HEAD_DIM
KV_WIDTH = N_KV_HEADS * HEAD_DIM
WINDOW = 128
ROPE_THETA = 500000.0
ROT_DIM = HEAD_DIM // 4
LRU_WIDTH = D_MODEL // 2
LRU_HEADS = 8
LRU_BLOCK = LRU_WIDTH // LRU_HEADS
CONV_W = 4
LRU_C = 8.0
D_FF = 2816
IN_WIDTH = ATTN_WIDTH + 2 * KV_WIDTH + 2 * LRU_WIDTH
EPS = 1e-6

LANES = 128
SUBLANES = 8
FF_CHUNK = 256
N_FF_CHUNKS = D_FF // FF_CHUNK
FFN_TILE = 512
MIX_TILE = 512
SAMPLE_GROUP = 8
SHIFT_GROUP = 16
CAST_ROWS = 256
BF16_SUBLANES = 16
VMEM_LIMIT_BYTES = 56 * 1024 * 1024

O_K = ATTN_WIDTH
O_V = O_K + KV_WIDTH
O_XR = O_V + KV_WIDTH
O_YG = O_XR + LRU_WIDTH

F32 = jnp.float32
BF16 = jnp.bfloat16
LOG2_E = np.float32(np.log2(np.e))


def _rmsnorm(x, g):
    return (x * lax.rsqrt(jnp.mean(x * x, axis=-1, keepdims=True) + EPS)) * g


def _half_ffn_steps(x, g_pre, wg_ref, wu_ref, wd_ref, g_post, act_scr):
    h = _rmsnorm(x, g_pre).astype(BF16)
    for c in range(N_FF_CHUNKS):
        cols = slice(FF_CHUNK * c, FF_CHUNK * (c + 1))
        g = jnp.dot(h, wg_ref[:, cols], preferred_element_type=F32)
        u = jnp.dot(h, wu_ref[:, cols], preferred_element_type=F32)
        act_scr[:, cols] = ((g * jax.nn.sigmoid(g)) * u).astype(BF16)
        yield
    ys = []
    for c in range(D_MODEL // FF_CHUNK):
        ys.append(jnp.dot(act_scr[...], wd_ref[:, FF_CHUNK * c:FF_CHUNK * (c + 1)],
                          preferred_element_type=F32))
        yield
    return x + _rmsnorm(jnp.concatenate(ys, axis=1), 0.5 * g_post)


def _advance(gen, done):
    if gen in done:
        return
    try:
        next(gen)
    except StopIteration as stop:
        done[gen] = stop.value


def _interleave(first, second):
    done = {}
    while len(done) < 2:
        _advance(first, done)
        _advance(second, done)
    return done[first], done[second]


def _run(gen):
    done = {}
    while gen not in done:
        _advance(gen, done)
    return done[gen]


def _half_ffn(*args):
    return _run(_half_ffn_steps(*args))


def _gelu_tanh_times(x, y):
    c = np.float32(np.sqrt(2.0 / np.pi))
    t = jnp.tanh(x * ((x * x) * (np.float32(0.044715) * c) + c))
    w = (0.5 * x) * y
    return t * w + w


def _softplus(x):
    return jnp.maximum(x, 0.0) + jnp.log1p(jnp.exp(-jnp.abs(x)))


def _lru_gates(xc, wgate_ref):
    return jnp.dot(xc.astype(BF16), wgate_ref[...], preferred_element_type=F32)


def _lru_coeffs(xc, gate_a, gate_x, ba, bx, lam):
    r = jax.nn.sigmoid(gate_a + ba)
    ig = jax.nn.sigmoid(gate_x + bx)
    a = jnp.exp2(r * ((-LRU_C * LOG2_E) * _softplus(-lam)))
    v = 1.0 - a * a
    u = jnp.where(v > 0.0, v * lax.rsqrt(v), 0.0) * (ig * xc)
    return a, u


def _scan_rows(a, u):
    n = a.shape[0]
    row = lax.broadcasted_iota(jnp.int32, a.shape, 0)
    d = 1
    while d < n:
        keep = row >= d
        u = jnp.where(keep, a * pltpu.roll(u, d, 0) + u, u)
        if 2 * d < n:
            a = jnp.where(keep, a * pltpu.roll(a, d, 0), a)
        d *= 2
    return u


def _phase_blocks(slab_ref, j):
    groups = slab_ref.shape[1] // SUBLANES
    return [slab_ref[j, pl.ds(k, groups, stride=SUBLANES), :] for k in range(SUBLANES)]


def _prev_group(block, carry_row):
    first = lax.broadcasted_iota(jnp.int32, block.shape, 0) == 0
    return jnp.where(first, carry_row, pltpu.roll(block, 1, 0))


def _conv_phases(x_phases, carry_rows, w, b):
    wrapped = {SUBLANES - m: _prev_group(x_phases[SUBLANES - m], carry_rows[CONV_W - 1 - m])
               for m in range(1, CONV_W)}
    out = []
    for k in range(SUBLANES):
        acc = b
        for tap in range(CONV_W):
            src = k - (CONV_W - 1 - tap)
            x = x_phases[src] if src >= 0 else wrapped[src + SUBLANES]
            acc = acc + x * w[tap:tap + 1, :]
        out.append(acc)
    return out


def _scan_phases(a, u, h0):
    groups = a.shape[0] // SUBLANES
    blk = lambda v, k: v[groups * k:groups * (k + 1)]
    h_zero = [blk(u, 0)]
    a_prod = [blk(a, 0)]
    for k in range(1, SUBLANES):
        h_zero.append(blk(a, k) * h_zero[-1] + blk(u, k))
        a_prod.append(blk(a, k) * a_prod[-1])
    first = lax.broadcasted_iota(jnp.int32, a_prod[-1].shape, 0) == 0
    h_end = _scan_rows(a_prod[-1], jnp.where(first, a_prod[-1] * h0 + h_zero[-1], h_zero[-1]))
    h_in = _prev_group(h_end, h0)
    return [h_zero[k] + a_prod[k] * h_in for k in range(SUBLANES)]


def _expand_q(q, rows):
    lane = lax.broadcasted_iota(jnp.int32, (rows, LANES), 1)
    pieces = []
    for h in range(N_HEADS):
        kv = h // GROUP
        src = q[:, LANES * (h // 2):LANES * (h // 2 + 1)]
        if (h % 2) != kv:
            src = pltpu.roll(src, HEAD_DIM, 1)
        in_half = (lane >= HEAD_DIM) if kv == 1 else (lane < HEAD_DIM)
        pieces.append(jnp.where(in_half, src, 0.0))
    return jnp.concatenate(pieces, axis=0).astype(BF16)


def _collect_heads(o, rows):
    lane = lax.broadcasted_iota(jnp.int32, (rows, LANES), 1)
    groups = []
    for j in range(N_HEADS // 2):
        kv = (2 * j) // GROUP
        even = o[rows * (2 * j):rows * (2 * j + 1)]
        odd = o[rows * (2 * j + 1):rows * (2 * j + 2)]
        if kv == 1:
            even = pltpu.roll(even, HEAD_DIM, 1)
        else:
            odd = pltpu.roll(odd, HEAD_DIM, 1)
        groups.append(jnp.where(lane < HEAD_DIM, even, odd))
    return jnp.concatenate(groups, axis=1)


def _collect_heads_t(o_t, rows):
    return jnp.concatenate(
        [o_t[HEAD_DIM * (h // GROUP):HEAD_DIM * (h // GROUP + 1), rows * h:rows * (h + 1)]
         for h in range(N_HEADS)], axis=0)


def _sink_row(sinks_ref, layer, rows):
    head = lax.broadcasted_iota(jnp.int32, (1, N_HEADS * rows), 1) // rows
    out = jnp.full((1, N_HEADS * rows), sinks_ref[layer, 0], F32)
    for h in range(1, N_HEADS):
        out = jnp.where(head == h, sinks_ref[layer, h], out)
    return out * LOG2_E


def _sink_column(sinks_ref, layer, rows):
    return jnp.concatenate([jnp.full((rows, 1), sinks_ref[layer, h], F32) for h in range(N_HEADS)],
                           axis=0) * LOG2_E


def _mix_normed(attn, hs, yg, g_attn, g_lru):
    lru = _gelu_tanh_times(yg, hs)
    return jnp.concatenate([_rmsnorm(attn, g_attn), _rmsnorm(lru, g_lru)], axis=-1).astype(BF16)


def _mix_project(x1, mixed, wo_ref, g_postmix):
    y = jnp.dot(mixed, wo_ref[...], preferred_element_type=F32)
    return x1 + _rmsnorm(y, g_postmix)


def _mix_out(x1, attn, hs, yg, g_attn, g_lru, wo_ref, g_postmix):
    return _mix_project(x1, _mix_normed(attn, hs, yg, g_attn, g_lru), wo_ref, g_postmix)


def _layer_spec(shape, layer, grid_rank):
    zeros = (0,) * len(shape)
    return pl.BlockSpec((None,) + tuple(shape), lambda *_: (layer,) + zeros,
                        pipeline_mode=pl.Buffered(1))


def _cast_kernel(w_ref, o_ref):
    o_ref[...] = w_ref[...].astype(BF16)


def _to_bf16(w, layer):
    _, rows, cols = w.shape
    assert rows % CAST_ROWS == 0
    return pl.pallas_call(
        _cast_kernel,
        grid=(rows // CAST_ROWS,),
        in_specs=[pl.BlockSpec((None, CAST_ROWS, cols), lambda i: (layer, i, 0))],
        out_specs=pl.BlockSpec((CAST_ROWS, cols), lambda i: (i, 0)),
        out_shape=jax.ShapeDtypeStruct((rows, cols), BF16),
        compiler_params=pltpu.CompilerParams(dimension_semantics=("arbitrary",)),
        name="cast_bf16",
    )(w)


def _side_block_rows(rows, n_tiles):
    for k in range(BF16_SUBLANES, rows + 1, BF16_SUBLANES):
        if rows % k == 0 and rows // k <= n_tiles:
            return k
    raise ValueError(f"cannot split {rows} rows over {n_tiles} steps")


def _side_cast_specs(sources, n_tiles):
    in_specs, out_specs, out_shapes = [], [], []
    for w, layer in sources:
        _, rows, cols = w.shape
        br = _side_block_rows(rows, n_tiles)
        block = functools.partial(jnp.minimum, rows // br - 1)
        in_specs.append(pl.BlockSpec((None, br, cols),
                                     lambda g, layer=layer, block=block: (layer, block(g), 0)))
        out_specs.append(pl.BlockSpec((br, cols), lambda g, block=block: (block(g), 0)))
        out_shapes.append(jax.ShapeDtypeStruct((rows, cols), BF16))
    return in_specs, out_specs, out_shapes


def _side_cast(src_refs, dst_refs):
    for src, dst in zip(src_refs, dst_refs):
        dst[...] = src[...].astype(BF16)


def _resident(shape):
    return pl.BlockSpec(shape, lambda *_: (0,) * len(shape))


def _in_proj_steps(x1, g_mix, win_ref, cos, sin_up, sin_dn, z_ref):
    h = _rmsnorm(x1, g_mix).astype(BF16)
    half = ROT_DIM // 2
    for c in range(IN_WIDTH // FF_CHUNK):
        zc = jnp.dot(h, win_ref[:, FF_CHUNK * c:FF_CHUNK * (c + 1)], preferred_element_type=F32)
        for k in range(FF_CHUNK // LANES):
            j = (FF_CHUNK // LANES) * c + k
            zj = zc[:, LANES * k:LANES * (k + 1)]
            if j < O_V // LANES:
                zj = (zj * cos + pltpu.roll(zj, LANES - half, 1) * sin_up
                      + pltpu.roll(zj, half, 1) * sin_dn)
            if j < ATTN_WIDTH // LANES:
                zj = zj * (LOG2_E * HEAD_DIM ** -0.5)
            z_ref[:, LANES * j:LANES * (j + 1)] = zj
        yield


def _ffn_in_kernel(skewed, n_side, *refs):
    (x_ref, cos_ref, sin_up_ref, sin_dn_ref, g_pre_ref, wg_ref, wu_ref, wd_ref, g_post_ref,
     g_mix_ref, win_ref) = refs[:11]
    side_src = refs[11:11 + n_side]
    x1_ref, z_ref = refs[11 + n_side:13 + n_side]
    side_dst = refs[13 + n_side:13 + 2 * n_side]
    act_scr, *x1_scr = refs[13 + 2 * n_side:]

    def ffn_steps():
        return _half_ffn_steps(x_ref[...], g_pre_ref[...], wg_ref, wu_ref, wd_ref, g_post_ref[...],
                               act_scr)

    def in_proj_steps(x1):
        return _in_proj_steps(x1, g_mix_ref[...], win_ref, cos_ref[...], sin_up_ref[...],
                              sin_dn_ref[...], z_ref)

    if not skewed:
        x1 = _run(ffn_steps())
        x1_ref[...] = x1
        _run(in_proj_steps(x1))
        return
    (x1_scr,) = x1_scr
    g_idx = pl.program_id(0)
    n_tiles = pl.num_programs(0) - 1

    @pl.when(g_idx == 0)
    def _():
        x1_scr[...] = jnp.zeros(x1_scr.shape, F32)

    @pl.when(g_idx < n_tiles)
    def _():
        x1, _ = _interleave(ffn_steps(), in_proj_steps(x1_scr[...]))
        x1_ref[...] = x1
        x1_scr[...] = x1
        _side_cast(side_src, side_dst)

    @pl.when(g_idx == n_tiles)
    def _():
        _run(in_proj_steps(x1_scr[...]))


def _ffn_in(x, tabs, tab_tiles, w, big, layer, tile, side=()):
    n = x.shape[0] // tile
    skewed = n > 1
    assert skewed or not side
    ffn_tile = (lambda g: jnp.minimum(g, n - 1)) if skewed else (lambda g: g)
    proj_tile = (lambda g: jnp.maximum(g - 1, 0)) if skewed else (lambda g: g)
    lspec = lambda *shape: _layer_spec(shape, layer, 1)
    tab_spec = pl.BlockSpec((tile, LANES), lambda g: (proj_tile(g) % tab_tiles, 0))
    side_in, side_out, side_shapes = _side_cast_specs(side, n)
    wg, wu, wd, w_in = big
    return pl.pallas_call(
        functools.partial(_ffn_in_kernel, skewed, len(side)),
        grid=(n + 1 if skewed else n,),
        in_specs=[
            pl.BlockSpec((tile, D_MODEL), lambda g: (ffn_tile(g), 0)),
            tab_spec, tab_spec, tab_spec,
            lspec(1, D_MODEL), _resident(wg.shape), _resident(wu.shape), _resident(wd.shape),
            lspec(1, D_MODEL), lspec(1, D_MODEL), _resident(w_in.shape),
        ] + side_in,
        out_specs=[
            pl.BlockSpec((tile, D_MODEL), lambda g: (ffn_tile(g), 0)),
            pl.BlockSpec((tile, IN_WIDTH), lambda g: (proj_tile(g), 0)),
        ] + side_out,
        out_shape=[
            jax.ShapeDtypeStruct((n * tile, D_MODEL), F32),
            jax.ShapeDtypeStruct((n * tile, IN_WIDTH), F32),
        ] + side_shapes,
        scratch_shapes=[pltpu.VMEM((tile, D_FF), BF16)]
                       + ([pltpu.VMEM((tile, D_MODEL), F32)] if skewed else []),
        compiler_params=pltpu.CompilerParams(
            dimension_semantics=("arbitrary",), vmem_limit_bytes=VMEM_LIMIT_BYTES),
        name="ffn_in",
    )(x, *tabs, w["g_pre1"], wg, wu, wd, w["g_post1"], w["g_mix"], w_in, *[src for src, _ in side])


def _mix_prompt_kernel(layer, n_seq_tiles, sinks_ref, x1_ref, z_ref, conv_w_ref, conv_b_ref, wgate_ref, ba_ref,
                       bx_ref, lam_ref, g_attn_ref, g_lru_ref, wo_ref, g_postmix_ref, g_pre2_ref,
                       wg_ref, wu_ref, wd_ref, g_post2_ref,
                       x3_ref, nk_ref, nv_ref, nconv_ref, nh_ref,
                       k_scr, vt_scr, conv_scr, h_scr, x2_scr, bias_scr, attn_scr, xr_scr, xc_scr,
                       gates_scr, hs_scr, act_scr):
    tile = x1_ref.shape[0]
    g_idx = pl.program_id(0)
    n_tiles = pl.num_programs(0) - 1
    s_idx = g_idx % n_seq_tiles

    @pl.when(g_idx == 0)
    def _():
        x2_scr[...] = jnp.zeros(x2_scr.shape, F32)
        kj = lax.broadcasted_iota(jnp.int32, (2 * WINDOW, N_HEADS * WINDOW), 0)
        qi = lax.broadcasted_iota(jnp.int32, (2 * WINDOW, N_HEADS * WINDOW), 1) & (WINDOW - 1)
        band = (kj > qi) & (kj <= qi + WINDOW)
        bias_scr[0] = jnp.where(band, 0.0, -jnp.inf)
        bias_scr[1] = jnp.where(band & (kj >= WINDOW), 0.0, -jnp.inf)

    @pl.when(s_idx == 0)
    def _():
        k_scr[0:WINDOW, :] = jnp.zeros((WINDOW, KV_WIDTH), BF16)
        vt_scr[:, 0:WINDOW] = jnp.zeros((KV_WIDTH, WINDOW), BF16)
        conv_scr[...] = jnp.zeros((CONV_W - 1, LRU_WIDTH), F32)
        h_scr[...] = jnp.zeros((1, LRU_WIDTH), F32)

    def mixer_steps():
        k_scr[WINDOW:, :] = z_ref[:, O_K:O_V].astype(BF16)
        vt_scr[:, WINDOW:] = z_ref[:, O_V:O_XR].T.astype(BF16)
        sink = _sink_row(sinks_ref, layer, WINDOW)
        for i in range(tile // WINDOW):
            q_exp = _expand_q(z_ref[WINDOW * i:WINDOW * (i + 1), 0:ATTN_WIDTH], WINDOW)
            keys = k_scr[WINDOW * i:WINDOW * (i + 2), :]
            vals_t = vt_scr[:, WINDOW * i:WINDOW * (i + 2)]
            s = lax.dot_general(keys, q_exp, (((1,), (1,)), ((), ())), preferred_element_type=F32)
            s = s + bias_scr[jnp.where(s_idx == 0, 1, 0) if i == 0 else 0]
            m = jnp.maximum(jnp.max(s, axis=0, keepdims=True), sink)
            p = jnp.exp2(s - m)
            denom = jnp.sum(p, axis=0, keepdims=True) + jnp.exp2(sink - m)
            yield
            o_t = jnp.dot(vals_t, p.astype(BF16), preferred_element_type=F32) * (1.0 / denom)
            attn_scr[WINDOW * i:WINDOW * (i + 1), :] = _collect_heads_t(o_t, WINDOW).T
            yield
        k_scr[0:WINDOW, :] = k_scr[tile:tile + WINDOW, :]
        vt_scr[:, 0:WINDOW] = vt_scr[:, tile:tile + WINDOW]

        for j in range(LRU_WIDTH // LANES):
            ch = slice(LANES * j, LANES * (j + 1))
            xr_scr[j] = z_ref[:, O_XR + LANES * j:O_XR + LANES * (j + 1)]
            carry_rows = [conv_scr[m:m + 1, ch] for m in range(CONV_W - 1)]
            xc_phases = _conv_phases(_phase_blocks(xr_scr, j), carry_rows, conv_w_ref[:, ch],
                                     conv_b_ref[:, ch])
            xc_scr[:, ch] = jnp.concatenate(xc_phases, axis=0)
        conv_scr[...] = z_ref[tile - (CONV_W - 1):, O_XR:O_YG]
        gates_scr[...] = _lru_gates(xc_scr[...], wgate_ref)
        yield

        groups = tile // SUBLANES
        h_last = []
        for j in range(LRU_WIDTH // LANES):
            ch = slice(LANES * j, LANES * (j + 1))
            gate_x_ch = slice(LRU_WIDTH + LANES * j, LRU_WIDTH + LANES * (j + 1))
            a, u = _lru_coeffs(xc_scr[:, ch], gates_scr[:, ch], gates_scr[:, gate_x_ch],
                               ba_ref[:, ch], bx_ref[:, ch], lam_ref[:, ch])
            yield
            h_phases = _scan_phases(a, u, h_scr[:, ch])
            for k in range(SUBLANES):
                hs_scr[j, pl.ds(k, groups, stride=SUBLANES), :] = h_phases[k]
            h_last.append(h_phases[SUBLANES - 1][groups - 1:groups, :])
            yield
        h_last = jnp.concatenate(h_last, axis=1)
        h_scr[...] = h_last
        hs = jnp.concatenate([hs_scr[j] for j in range(LRU_WIDTH // LANES)], axis=1)
        mixed = _mix_normed(attn_scr[...], hs, z_ref[:, O_YG:], g_attn_ref[...], g_lru_ref[...])
        yield
        x2 = _mix_project(x1_ref[...], mixed, wo_ref, g_postmix_ref[...])
        return x2, h_last

    def ffn_steps():
        return _half_ffn_steps(x2_scr[...], g_pre2_ref[...], wg_ref, wu_ref, wd_ref,
                               g_post2_ref[...], act_scr)

    @pl.when(g_idx < n_tiles)
    def _():
        x3, (x2, h_last) = _interleave(ffn_steps(), mixer_steps())
        x3_ref[...] = x3
        x2_scr[...] = x2
        nk_ref[0] = z_ref[tile - WINDOW:, O_K:O_V]
        nv_ref[0] = z_ref[tile - WINDOW:, O_V:O_XR]
        nconv_ref[0] = z_ref[tile - (CONV_W - 1):, O_XR:O_YG]
        nh_ref[0] = h_last

    @pl.when(g_idx == n_tiles)
    def _():
        x3_ref[...] = _run(ffn_steps())


def _mix_weight_specs(layer, big):
    lspec = lambda *shape: _layer_spec(shape, layer, 1)
    w_o, wg, wu, wd = big
    return [
        lspec(CONV_W, LRU_WIDTH), lspec(1, LRU_WIDTH), lspec(LRU_WIDTH, 2 * LRU_WIDTH),
        lspec(1, LRU_WIDTH), lspec(1, LRU_WIDTH), lspec(1, LRU_WIDTH),
        lspec(1, ATTN_WIDTH), lspec(1, LRU_WIDTH), _resident(w_o.shape), lspec(1, D_MODEL),
        lspec(1, D_MODEL), _resident(wg.shape), _resident(wu.shape), _resident(wd.shape),
        lspec(1, D_MODEL),
    ]


def _mix_weights(w, big):
    w_o, wg, wu, wd = big
    return (w["conv_w"], w["conv_b"], w["w_gate"], w["gate_a_b"], w["gate_x_b"], w["lam"],
            w["g_attn"], w["g_lru"], w_o, w["g_postmix"], w["g_pre2"], wg, wu, wd, w["g_post2"])


def _mix_prompt(x1, z, w, big, layer, batch, seq, tile):
    n_s = seq // tile
    n_tiles = batch * n_s
    mix_tile = lambda g: jnp.minimum(g, n_tiles - 1)
    tok = lambda width: pl.BlockSpec((tile, width), lambda g: (mix_tile(g), 0))
    per_batch = lambda rows, width: pl.BlockSpec((1, rows, width), lambda g: (mix_tile(g) // n_s, 0, 0))
    return pl.pallas_call(
        functools.partial(_mix_prompt_kernel, layer, n_s),
        grid=(n_tiles + 1,),
        in_specs=[pl.BlockSpec(memory_space=pltpu.SMEM), tok(D_MODEL), tok(IN_WIDTH)]
                 + _mix_weight_specs(layer, big),
        out_specs=[
            pl.BlockSpec((tile, D_MODEL), lambda g: (jnp.maximum(g - 1, 0), 0)),
            per_batch(WINDOW, KV_WIDTH), per_batch(WINDOW, KV_WIDTH),
            per_batch(CONV_W - 1, LRU_WIDTH), per_batch(1, LRU_WIDTH),
        ],
        out_shape=[
            jax.ShapeDtypeStruct((batch * seq, D_MODEL), F32),
            jax.ShapeDtypeStruct((batch, WINDOW, KV_WIDTH), F32),
            jax.ShapeDtypeStruct((batch, WINDOW, KV_WIDTH), F32),
            jax.ShapeDtypeStruct((batch, CONV_W - 1, LRU_WIDTH), F32),
            jax.ShapeDtypeStruct((batch, 1, LRU_WIDTH), F32),
        ],
        scratch_shapes=[
            pltpu.VMEM((WINDOW + tile, KV_WIDTH), BF16),
            pltpu.VMEM((KV_WIDTH, WINDOW + tile), BF16),
            pltpu.VMEM((CONV_W - 1, LRU_WIDTH), F32),
            pltpu.VMEM((1, LRU_WIDTH), F32),
            pltpu.VMEM((tile, D_MODEL), F32),
            pltpu.VMEM((2, 2 * WINDOW, N_HEADS * WINDOW), F32),
            pltpu.VMEM((tile, ATTN_WIDTH), F32),
            pltpu.VMEM((LRU_WIDTH // LANES, tile, LANES), F32),
            pltpu.VMEM((tile, LRU_WIDTH), F32),
            pltpu.VMEM((tile, 2 * LRU_WIDTH), F32),
            pltpu.VMEM((LRU_WIDTH // LANES, tile, LANES), F32),
            pltpu.VMEM((tile, D_FF), BF16),
        ],
        compiler_params=pltpu.CompilerParams(
            dimension_semantics=("arbitrary",), vmem_limit_bytes=VMEM_LIMIT_BYTES),
        name="mix_prompt",
    )(w["sinks"], x1, z, *_mix_weights(w, big))


def _attn_sample_kernel(layer, sinks_ref, zq_ref, ck_ref, cv_ref, attn_ref):
    dec_seq, g = zq_ref.shape[0], zq_ref.shape[1]
    rows = dec_seq * g
    cache_len = ck_ref.shape[1]
    zq = zq_ref[...].reshape(rows, zq_ref.shape[2])
    q_exp = _expand_q(zq[:, 0:ATTN_WIDTH], rows)
    k_new = zq[:, O_K:O_V].astype(BF16)
    v_new = zq[:, O_V:O_XR].astype(BF16)
    k_cache = ck_ref[...].reshape(g * cache_len, KV_WIDTH).astype(BF16)
    v_cache = cv_ref[...].reshape(g * cache_len, KV_WIDTH).astype(BF16)
    contract_last = (((1,), (1,)), ((), ()))
    s_c = lax.dot_general(q_exp, k_cache, contract_last, preferred_element_type=F32)
    s_n = lax.dot_general(q_exp, k_new, contract_last, preferred_element_type=F32)

    def row_ids(shape):
        r = lax.broadcasted_iota(jnp.int32, shape, 0)
        return (r % rows) // g, r % g

    t_q, b_q = row_ids(s_c.shape)
    c = lax.broadcasted_iota(jnp.int32, s_c.shape, 1)
    mask_c = ((c // cache_len) == b_q) & ((c % cache_len) > t_q)
    t_q, b_q = row_ids(s_n.shape)
    c = lax.broadcasted_iota(jnp.int32, s_n.shape, 1)
    mask_n = ((c % g) == b_q) & ((c // g) <= t_q)
    s_c = jnp.where(mask_c, s_c, -jnp.inf)
    s_n = jnp.where(mask_n, s_n, -jnp.inf)

    sink = _sink_column(sinks_ref, layer, rows)
    m = jnp.maximum(jnp.maximum(jnp.max(s_c, axis=-1, keepdims=True),
                                jnp.max(s_n, axis=-1, keepdims=True)), sink)
    p_c = jnp.exp2(s_c - m)
    p_n = jnp.exp2(s_n - m)
    denom = (jnp.sum(p_c, axis=-1, keepdims=True) + jnp.sum(p_n, axis=-1, keepdims=True)
             + jnp.exp2(sink - m))
    o = (jnp.dot(p_c.astype(BF16), v_cache, preferred_element_type=F32)
         + jnp.dot(p_n.astype(BF16), v_new, preferred_element_type=F32)) / denom
    attn_ref[...] = _collect_heads(o, rows).reshape(dec_seq, g, ATTN_WIDTH)


def _attn_sample(z3, cache_k, cache_v, sinks, layer):
    dec_seq, dec_batch, _ = z3.shape
    cache_len = cache_k.shape[2]
    g = SAMPLE_GROUP
    cache_spec = pl.BlockSpec((None, g, cache_len, KV_WIDTH), lambda i: (layer, i, 0, 0))
    return pl.pallas_call(
        functools.partial(_attn_sample_kernel, layer),
        grid=(dec_batch // g,),
        in_specs=[
            pl.BlockSpec(memory_space=pltpu.SMEM),
            pl.BlockSpec((dec_seq, g, O_XR), lambda i: (0, i, 0)),
            cache_spec, cache_spec,
        ],
        out_specs=pl.BlockSpec((dec_seq, g, ATTN_WIDTH), lambda i: (0, i, 0)),
        out_shape=jax.ShapeDtypeStruct((dec_seq, dec_batch, ATTN_WIDTH), F32),
        compiler_params=pltpu.CompilerParams(
            dimension_semantics=("arbitrary",), vmem_limit_bytes=VMEM_LIMIT_BYTES),
        name="attn_sample",
    )(sinks, z3, cache_k, cache_v)


def _shift_caches_kernel(kvnew_ref, ck_ref, cv_ref, nk_ref, nv_ref):
    dec_seq = kvnew_ref.shape[1]
    cache_len = ck_ref.shape[1]
    nk_ref[:, 0:cache_len - dec_seq, :] = ck_ref[:, dec_seq:, :]
    nv_ref[:, 0:cache_len - dec_seq, :] = cv_ref[:, dec_seq:, :]
    nk_ref[:, cache_len - dec_seq:, :] = kvnew_ref[:, :, 0:KV_WIDTH]
    nv_ref[:, cache_len - dec_seq:, :] = kvnew_ref[:, :, KV_WIDTH:]


def _shift_caches(kvnew, cache_k, cache_v):
    depth, dec_batch, cache_len, _ = cache_k.shape
    dec_seq = kvnew.shape[2]
    g = SHIFT_GROUP
    spec = lambda rows, width: pl.BlockSpec((None, g, rows, width), lambda l, i: (l, i, 0, 0))
    return pl.pallas_call(
        _shift_caches_kernel,
        grid=(depth, dec_batch // g),
        in_specs=[spec(dec_seq, 2 * KV_WIDTH), spec(cache_len, KV_WIDTH), spec(cache_len, KV_WIDTH)],
        out_specs=[spec(cache_len, KV_WIDTH), spec(cache_len, KV_WIDTH)],
        out_shape=[jax.ShapeDtypeStruct(cache_k.shape, F32), jax.ShapeDtypeStruct(cache_v.shape, F32)],
        compiler_params=pltpu.CompilerParams(dimension_semantics=("arbitrary", "arbitrary")),
        name="shift_caches",
    )(kvnew, cache_k, cache_v)


def _mix_sample_kernel(x1_ref, z_ref, attn_ref, sconv_ref, sh_ref, conv_w_ref, conv_b_ref, wgate_ref,
                       ba_ref, bx_ref, lam_ref, g_attn_ref, g_lru_ref, wo_ref, g_postmix_ref,
                       g_pre2_ref, wg_ref, wu_ref, wd_ref, g_post2_ref,
                       x3_ref, nconv_ref, nh_ref, act_scr):
    nb = sh_ref.shape[0]
    dec_seq = x1_ref.shape[0] // nb
    xr = z_ref[:, O_XR:O_YG]
    xp = [sconv_ref[j] for j in range(CONV_W - 1)] + [xr[nb * t:nb * (t + 1)] for t in range(dec_seq)]
    xc_steps = []
    for t in range(dec_seq):
        acc = conv_b_ref[...] + xp[t] * conv_w_ref[0:1, :]
        for j in range(1, CONV_W):
            acc = acc + xp[t + j] * conv_w_ref[j:j + 1, :]
        xc_steps.append(acc)
    for j in range(CONV_W - 1):
        nconv_ref[j] = xp[dec_seq + j]
    xc = jnp.concatenate(xc_steps, axis=0)
    gates = _lru_gates(xc, wgate_ref)
    a, u = _lru_coeffs(xc, gates[:, :LRU_WIDTH], gates[:, LRU_WIDTH:], ba_ref[...], bx_ref[...],
                       lam_ref[...])
    h = sh_ref[...]
    hs_steps = []
    for t in range(dec_seq):
        h = a[nb * t:nb * (t + 1)] * h + u[nb * t:nb * (t + 1)]
        hs_steps.append(h)
    nh_ref[...] = h
    hs = jnp.concatenate(hs_steps, axis=0)
    x2 = _mix_out(x1_ref[...], attn_ref[...], hs, z_ref[:, O_YG:], g_attn_ref[...], g_lru_ref[...],
                  wo_ref, g_postmix_ref[...])
    x3_ref[...] = _half_ffn(x2, g_pre2_ref[...], wg_ref, wu_ref, wd_ref, g_post2_ref[...], act_scr)


def _mix_sample(x1, z, attn, sconv, state_h, w, big, layer):
    n = x1.shape[0]
    nb = state_h.shape[1]
    full = lambda *shape: pl.BlockSpec(shape, lambda i: (0,) * len(shape))
    return pl.pallas_call(
        _mix_sample_kernel,
        grid=(1,),
        in_specs=[
            full(n, D_MODEL), full(n, IN_WIDTH), full(n, ATTN_WIDTH),
            full(CONV_W - 1, nb, LRU_WIDTH), _layer_spec((nb, LRU_WIDTH), layer, 1),
        ] + _mix_weight_specs(layer, big),
        out_specs=[full(n, D_MODEL), full(CONV_W - 1, nb, LRU_WIDTH), full(nb, LRU_WIDTH)],
        out_shape=[
            jax.ShapeDtypeStruct((n, D_MODEL), F32),
            jax.ShapeDtypeStruct((CONV_W - 1, nb, LRU_WIDTH), F32),
            jax.ShapeDtypeStruct((nb, LRU_WIDTH), F32),
        ],
        scratch_shapes=[pltpu.VMEM((n, D_FF), BF16)],
        compiler_params=pltpu.CompilerParams(
            dimension_semantics=("arbitrary",), vmem_limit_bytes=VMEM_LIMIT_BYTES),
        name="mix_sample",
    )(x1, z, attn, sconv, state_h, *_mix_weights(w, big))


def _rope_tables(pos):
    half = ROT_DIM // 2
    inv = ROPE_THETA ** (-(jnp.arange(half, dtype=F32) * 2.0) / ROT_DIM)
    ang = pos.astype(F32)[:, None] * inv[None, :]
    cos, sin = jnp.cos(ang), jnp.sin(ang)
    n = pos.shape[0]
    rest = jnp.zeros((n, HEAD_DIM - ROT_DIM), F32)
    zero = jnp.zeros((n, half), F32)
    per_head = lambda parts: jnp.tile(jnp.concatenate(parts, axis=-1), (1, LANES // HEAD_DIM))
    return (per_head([cos, cos, rest + 1.0]), per_head([-sin, zero, rest]), per_head([zero, sin, rest]))


def _block_diag(w):
    eye = jnp.eye(LRU_HEADS, dtype=w.dtype)
    return jnp.einsum("lhij,hg->lhigj", w, eye).reshape(w.shape[0], LRU_WIDTH, LRU_WIDTH)


def kernel(x_prompt, x_sample, cache_k, cache_v, state_conv, state_h, norm_pre_ffn1, ffn1_w_gate, ffn1_w_up, ffn1_w_down, norm_post_ffn1, norm_pre_mix, w_in, sinks, conv_w, conv_b, gate_a_w, gate_a_b, gate_x_w, gate_x_b, lru_lambda, attn_out_norm, lru_out_norm, w_o, norm_post_mix, norm_pre_ffn2, ffn2_w_gate, ffn2_w_up, ffn2_w_down, norm_post_ffn2):
    batch, seq, _ = x_prompt.shape
    dec_batch, dec_seq, _ = x_sample.shape
    depth = w_in.shape[0]
    cache_len = cache_k.shape[2]
    n_s = dec_seq * dec_batch
    assert seq % FFN_TILE == 0 and n_s % FFN_TILE == 0
    assert seq % MIX_TILE == 0 and MIX_TILE % WINDOW == 0
    assert dec_batch % SAMPLE_GROUP == 0 and dec_batch % SHIFT_GROUP == 0

    tabs_p = _rope_tables(jnp.arange(seq))
    tabs_s = _rope_tables(jnp.repeat(PAST_LEN + jnp.arange(dec_seq), dec_batch))
    xp = x_prompt.reshape(batch * seq, D_MODEL)
    xs = jnp.swapaxes(x_sample, 0, 1).reshape(n_s, D_MODEL)
    rows = lambda v: v.reshape(depth, 1, -1)

    w = dict(
        g_pre1=rows(norm_pre_ffn1), g_post1=rows(norm_post_ffn1), g_mix=rows(norm_pre_mix),
        sinks=sinks, conv_w=conv_w, conv_b=rows(conv_b),
        w_gate=jnp.concatenate([_block_diag(gate_a_w), _block_diag(gate_x_w)], axis=2).astype(BF16),
        gate_a_b=rows(gate_a_b), gate_x_b=rows(gate_x_b), lam=rows(lru_lambda),
        g_attn=rows(attn_out_norm), g_lru=rows(lru_out_norm),
        g_postmix=rows(norm_post_mix), g_pre2=rows(norm_pre_ffn2), g_post2=rows(norm_post_ffn2),
    )
    ffn_in_src = (ffn1_w_gate, ffn1_w_up, ffn1_w_down, w_in)
    mix_src = (w_o, ffn2_w_gate, ffn2_w_up, ffn2_w_down)
    ck = cache_k.reshape(depth, dec_batch, cache_len, KV_WIDTH)
    cv = cache_v.reshape(depth, dec_batch, cache_len, KV_WIDTH)

    outs = [[] for _ in range(6)]
    kvnew = []
    ffn_in_big = [_to_bf16(s, 0) for s in ffn_in_src]
    for l in range(depth):
        side = [(s, l) for s in mix_src] + [(s, l + 1) for s in ffn_in_src if l + 1 < depth]
        x1p, zp, *cast = _ffn_in(xp, tabs_p, seq // FFN_TILE, w, ffn_in_big, l, FFN_TILE, side=side)
        mix_big, next_ffn_in_big = cast[:len(mix_src)], cast[len(mix_src):]
        xp, nk, nv, nc, nh = _mix_prompt(x1p, zp, w, mix_big, l, batch, seq, MIX_TILE)

        x1s, zs = _ffn_in(xs, tabs_s, n_s // FFN_TILE, w, ffn_in_big, l, FFN_TILE)
        z3 = zs.reshape(dec_seq, dec_batch, IN_WIDTH)
        kvnew.append(jnp.swapaxes(z3[:, :, O_K:O_XR], 0, 1))
        attn_s = _attn_sample(z3, ck, cv, sinks, l)
        xs, ncs, nhs = _mix_sample(x1s, zs, attn_s.reshape(n_s, ATTN_WIDTH),
                                   jnp.swapaxes(state_conv[l], 0, 1), state_h, w, mix_big, l)
        ffn_in_big = next_ffn_in_big
        kv_p = (batch, WINDOW, N_KV_HEADS, HEAD_DIM)
        for acc, val in zip(outs, (nk.reshape(kv_p), nv.reshape(kv_p), nc,
                                   nh.reshape(batch, LRU_WIDTH), jnp.swapaxes(ncs, 0, 1), nhs)):
            acc.append(val)

    nks, nvs = _shift_caches(jnp.stack(kvnew), ck, cv)
    y_prompt = xp.reshape(batch, seq, D_MODEL)
    y_sample = jnp.swapaxes(xs.reshape(dec_seq, dec_batch, D_MODEL), 0, 1)
    kp, vp, cp, hp, cs, hs = (jnp.stack(o) for o in outs)
    return (y_prompt, y_sample, kp, vp, cp, hp, nks.reshape(cache_k.shape), nvs.reshape(cache_v.shape),
            cs, hs)
```

```python
import functools

import jax
import jax.numpy as jnp
import numpy as np
from jax import lax
from jax.experimental import pallas as pl
from jax.experimental.pallas import tpu as pltpu

D_MODEL = 1024
PAST_LEN = 16384
N_HEADS = 8
HEAD_DIM = 64
N_KV_HEADS = 2
GROUP = N_HEADS // N_KV_HEADS
ATTN_WIDTH = N_HEADS * HEAD_DIM
KV_WIDTH = N_KV_HEADS * HEAD_DIM
WINDOW = 128
ROPE_THETA = 500000.0
ROT_DIM = HEAD_DIM // 4
LRU_WIDTH = D_MODEL // 2
LRU_HEADS = 8
LRU_BLOCK = LRU_WIDTH // LRU_HEADS
CONV_W = 4
LRU_C = 8.0
D_FF = 2816
IN_WIDTH = ATTN_WIDTH + 2 * KV_WIDTH + 2 * LRU_WIDTH
EPS = 1e-6

LANES = 128
SUBLANES = 8
FF_CHUNK = 256
GATE_TILE = 256
N_FF_CHUNKS = D_FF // FF_CHUNK
FFN_TILE = 512
MIX_TILE = 512
SAMPLE_GROUP = 8
SHIFT_GROUP = 16
CAST_ROWS = 256
BF16_SUBLANES = 16
V7X_VMEM_BYTES = 64 * 1024 * 1024
VMEM_LIMIT_BYTES = V7X_VMEM_BYTES * 7 // 8

O_K = ATTN_WIDTH
O_V = O_K + KV_WIDTH
O_XR = O_V + KV_WIDTH
O_YG = O_XR + LRU_WIDTH

F32 = jnp.float32
BF16 = jnp.bfloat16
LOG2_E = np.float32(np.log2(np.e))


def _rmsnorm(x, g):
    return (x * lax.rsqrt(jnp.mean(x * x, axis=-1, keepdims=True) + EPS)) * g


def _half_ffn_steps(x, g_pre, wg_ref, wu_ref, wd_ref, g_post, act_scr):
    h = _rmsnorm(x, g_pre).astype(BF16)
    for c in range(N_FF_CHUNKS):
        cols = slice(FF_CHUNK * c, FF_CHUNK * (c + 1))
        g = jnp.dot(h, wg_ref[:, cols], preferred_element_type=F32)
        u = jnp.dot(h, wu_ref[:, cols], preferred_element_type=F32)
        act_scr[:, cols] = ((g * jax.nn.sigmoid(g)) * u).astype(BF16)
        yield
    ys = []
    for c in range(D_MODEL // FF_CHUNK):
        ys.append(jnp.dot(act_scr[...], wd_ref[:, FF_CHUNK * c:FF_CHUNK * (c + 1)],
                          preferred_element_type=F32))
        yield
    return x + _rmsnorm(jnp.concatenate(ys, axis=1), 0.5 * g_post)


def _advance(gen, done):
    if gen in done:
        return
    try:
        next(gen)
    except StopIteration as stop:
        done[gen] = stop.value


def _interleave(first, second):
    done = {}
    while len(done) < 2:
        _advance(first, done)
        _advance(second, done)
    return done[first], done[second]


def _run(gen):
    done = {}
    while gen not in done:
        _advance(gen, done)
    return done[gen]


def _half_ffn(*args):
    return _run(_half_ffn_steps(*args))


def _gelu_tanh_times(x, y):
    c = np.float32(np.sqrt(2.0 / np.pi))
    t = jnp.tanh(x * ((x * x) * (np.float32(0.044715) * c) + c))
    w = (0.5 * x) * y
    return t * w + w


def _softplus(x):
    return jnp.maximum(x, 0.0) + jnp.log1p(jnp.exp(-jnp.abs(x)))


def _lru_gates(xc, wgate_ref):
    x = xc.astype(BF16)
    parts = []
    for gate in range(2):
        for t in range(LRU_WIDTH // GATE_TILE):
            rows = slice(GATE_TILE * t, GATE_TILE * (t + 1))
            cols = slice(LRU_WIDTH * gate + GATE_TILE * t, LRU_WIDTH * gate + GATE_TILE * (t + 1))
            parts.append(jnp.dot(x[:, rows], wgate_ref[rows, cols], preferred_element_type=F32))
    return jnp.concatenate(parts, axis=1)


def _lru_coeffs(xc, gate_a, gate_x, ba, bx, lam):
    r = jax.nn.sigmoid(gate_a + ba)
    ig = jax.nn.sigmoid(gate_x + bx)
    a = jnp.exp2(r * ((-LRU_C * LOG2_E) * _softplus(-lam)))
    v = 1.0 - a * a
    u = jnp.where(v > 0.0, v * lax.rsqrt(v), 0.0) * (ig * xc)
    return a, u


def _scan_rows(a, u):
    n = a.shape[0]
    row = lax.broadcasted_iota(jnp.int32, a.shape, 0)
    d = 1
    while d < n:
        keep = row >= d
        u = jnp.where(keep, a * pltpu.roll(u, d, 0) + u, u)
        if 2 * d < n:
            a = jnp.where(keep, a * pltpu.roll(a, d, 0), a)
        d *= 2
    return u


def _phase_blocks(slab_ref, j):
    groups = slab_ref.shape[1] // SUBLANES
    return [slab_ref[j, pl.ds(k, groups, stride=SUBLANES), :] for k in range(SUBLANES)]


def _prev_group(block, carry_row):
    first = lax.broadcasted_iota(jnp.int32, block.shape, 0) == 0
    return jnp.where(first, carry_row, pltpu.roll(block, 1, 0))


def _conv_phases(x_phases, carry_rows, w, b):
    wrapped = {SUBLANES - m: _prev_group(x_phases[SUBLANES - m], carry_rows[CONV_W - 1 - m])
               for m in range(1, CONV_W)}
    out = []
    for k in range(SUBLANES):
        acc = b
        for tap in range(CONV_W):
            src = k - (CONV_W - 1 - tap)
            x = x_phases[src] if src >= 0 else wrapped[src + SUBLANES]
            acc = acc + x * w[tap:tap + 1, :]
        out.append(acc)
    return out


def _scan_phases(a, u, h0):
    groups = a.shape[0] // SUBLANES
    blk = lambda v, k: v[groups * k:groups * (k + 1)]
    h_zero = [blk(u, 0)]
    a_prod = [blk(a, 0)]
    for k in range(1, SUBLANES):
        h_zero.append(blk(a, k) * h_zero[-1] + blk(u, k))
        a_prod.append(blk(a, k) * a_prod[-1])
    first = lax.broadcasted_iota(jnp.int32, a_prod[-1].shape, 0) == 0
    h_end = _scan_rows(a_prod[-1], jnp.where(first, a_prod[-1] * h0 + h_zero[-1], h_zero[-1]))
    h_in = _prev_group(h_end, h0)
    return [h_zero[k] + a_prod[k] * h_in for k in range(SUBLANES)]


def _expand_q(q, rows):
    lane = lax.broadcasted_iota(jnp.int32, (rows, LANES), 1)
    pieces = []
    for h in range(N_HEADS):
        kv = h // GROUP
        src = q[:, LANES * (h // 2):LANES * (h // 2 + 1)]
        if (h % 2) != kv:
            src = pltpu.roll(src, HEAD_DIM, 1)
        in_half = (lane >= HEAD_DIM) if kv == 1 else (lane < HEAD_DIM)
        pieces.append(jnp.where(in_half, src, 0.0))
    return jnp.concatenate(pieces, axis=0).astype(BF16)


def _collect_heads(o, rows):
    lane = lax.broadcasted_iota(jnp.int32, (rows, LANES), 1)
    groups = []
    for j in range(N_HEADS // 2):
        kv = (2 * j) // GROUP
        even = o[rows * (2 * j):rows * (2 * j + 1)]
        odd = o[rows * (2 * j + 1):rows * (2 * j + 2)]
        if kv == 1:
            even = pltpu.roll(even, HEAD_DIM, 1)
        else:
            odd = pltpu.roll(odd, HEAD_DIM, 1)
        groups.append(jnp.where(lane < HEAD_DIM, even, odd))
    return jnp.concatenate(groups, axis=1)


def _attend_values_t(vals_t, p, inv_denom, rows):
    width = GROUP * rows
    outs = []
    for kv in range(N_KV_HEADS):
        cols = slice(width * kv, width * (kv + 1))
        o_kv = jnp.dot(vals_t[HEAD_DIM * kv:HEAD_DIM * (kv + 1), :], p[:, cols],
                       preferred_element_type=F32) * inv_denom[:, cols]
        outs += [o_kv[:, rows * g:rows * (g + 1)] for g in range(GROUP)]
    return jnp.concatenate(outs, axis=0)


def _sink_row(sinks_ref, layer, rows):
    head = lax.broadcasted_iota(jnp.int32, (1, N_HEADS * rows), 1) // rows
    out = jnp.full((1, N_HEADS * rows), sinks_ref[layer, 0], F32)
    for h in range(1, N_HEADS):
        out = jnp.where(head == h, sinks_ref[layer, h], out)
    return out * LOG2_E


def _sink_column(sinks_ref, layer, rows):
    return jnp.concatenate([jnp.full((rows, 1), sinks_ref[layer, h], F32) for h in range(N_HEADS)],
                           axis=0) * LOG2_E


def _mix_normed(attn, hs, yg, g_attn, g_lru):
    lru = _gelu_tanh_times(yg, hs)
    return jnp.concatenate([_rmsnorm(attn, g_attn), _rmsnorm(lru, g_lru)], axis=-1).astype(BF16)


def _mix_project(x1, mixed, wo_ref, g_postmix):
    y = jnp.dot(mixed, wo_ref[...], preferred_element_type=F32)
    return x1 + _rmsnorm(y, g_postmix)


def _mix_out(x1, attn, hs, yg, g_attn, g_lru, wo_ref, g_postmix):
    return _mix_project(x1, _mix_normed(attn, hs, yg, g_attn, g_lru), wo_ref, g_postmix)


def _layer_spec(shape, layer, grid_rank):
    zeros = (0,) * len(shape)
    return pl.BlockSpec((None,) + tuple(shape), lambda *_: (layer,) + zeros,
                        pipeline_mode=pl.Buffered(1))


def _cast_kernel(w_ref, o_ref):
    o_ref[...] = w_ref[...].astype(BF16)


def _to_bf16(w, layer):
    _, rows, cols = w.shape
    assert rows % CAST_ROWS == 0
    return pl.pallas_call(
        _cast_kernel,
        grid=(rows // CAST_ROWS,),
        in_specs=[pl.BlockSpec((None, CAST_ROWS, cols), lambda i: (layer, i, 0))],
        out_specs=pl.BlockSpec((CAST_ROWS, cols), lambda i: (i, 0)),
        out_shape=jax.ShapeDtypeStruct((rows, cols), BF16),
        compiler_params=pltpu.CompilerParams(dimension_semantics=("arbitrary",)),
        name="cast_bf16",
    )(w)


def _side_block_rows(rows, n_tiles):
    for k in range(BF16_SUBLANES, rows + 1, BF16_SUBLANES):
        if rows % k == 0 and rows // k <= n_tiles:
            return k
    raise ValueError(f"cannot split {rows} rows over {n_tiles} steps")


def _side_cast_specs(sources, n_tiles):
    in_specs, out_specs, out_shapes = [], [], []
    for w, layer in sources:
        _, rows, cols = w.shape
        br = _side_block_rows(rows, n_tiles)
        block = functools.partial(jnp.minimum, rows // br - 1)
        in_specs.append(pl.BlockSpec((None, br, cols),
                                     lambda g, layer=layer, block=block: (layer, block(g), 0)))
        out_specs.append(pl.BlockSpec((br, cols), lambda g, block=block: (block(g), 0)))
        out_shapes.append(jax.ShapeDtypeStruct((rows, cols), BF16))
    return in_specs, out_specs, out_shapes


def _side_cast(src_refs, dst_refs):
    for src, dst in zip(src_refs, dst_refs):
        dst[...] = src[...].astype(BF16)


def _resident(shape):
    return pl.BlockSpec(shape, lambda *_: (0,) * len(shape))


def _in_proj_steps(x1, g_mix, win_ref, cos, sin_up, sin_dn, z_ref):
    h = _rmsnorm(x1, g_mix).astype(BF16)
    half = ROT_DIM // 2
    for c in range(IN_WIDTH // FF_CHUNK):
        zc = jnp.dot(h, win_ref[:, FF_CHUNK * c:FF_CHUNK * (c + 1)], preferred_element_type=F32)
        for k in range(FF_CHUNK // LANES):
            j = (FF_CHUNK // LANES) * c + k
            zj = zc[:, LANES * k:LANES * (k + 1)]
            if j < O_V // LANES:
                zj = (zj * cos + pltpu.roll(zj, LANES - half, 1) * sin_up
                      + pltpu.roll(zj, half, 1) * sin_dn)
            if j < ATTN_WIDTH // LANES:
                zj = zj * (LOG2_E * HEAD_DIM ** -0.5)
            z_ref[:, LANES * j:LANES * (j + 1)] = zj
        yield


def _ffn_in_kernel(skewed, n_side, *refs):
    (x_ref, cos_ref, sin_up_ref, sin_dn_ref, g_pre_ref, wg_ref, wu_ref, wd_ref, g_post_ref,
     g_mix_ref, win_ref) = refs[:11]
    side_src = refs[11:11 + n_side]
    x1_ref, z_ref = refs[11 + n_side:13 + n_side]
    side_dst = refs[13 + n_side:13 + 2 * n_side]
    act_scr, *x1_scr = refs[13 + 2 * n_side:]

    def ffn_steps():
        return _half_ffn_steps(x_ref[...], g_pre_ref[...], wg_ref, wu_ref, wd_ref, g_post_ref[...],
                               act_scr)

    def in_proj_steps(x1):
        return _in_proj_steps(x1, g_mix_ref[...], win_ref, cos_ref[...], sin_up_ref[...],
                              sin_dn_ref[...], z_ref)

    if not skewed:
        x1 = _run(ffn_steps())
        x1_ref[...] = x1
        _run(in_proj_steps(x1))
        return
    (x1_scr,) = x1_scr
    g_idx = pl.program_id(0)
    n_tiles = pl.num_programs(0) - 1

    @pl.when(g_idx == 0)
    def _():
        x1_scr[...] = jnp.zeros(x1_scr.shape, F32)

    @pl.when(g_idx < n_tiles)
    def _():
        x1, _ = _interleave(ffn_steps(), in_proj_steps(x1_scr[...]))
        x1_ref[...] = x1
        x1_scr[...] = x1
        _side_cast(side_src, side_dst)

    @pl.when(g_idx == n_tiles)
    def _():
        _run(in_proj_steps(x1_scr[...]))


def _ffn_in(x, tabs, tab_tiles, w, big, layer, tile, side=()):
    n = x.shape[0] // tile
    skewed = n > 1
    assert skewed or not side
    ffn_tile = (lambda g: jnp.minimum(g, n - 1)) if skewed else (lambda g: g)
    proj_tile = (lambda g: jnp.maximum(g - 1, 0)) if skewed else (lambda g: g)
    lspec = lambda *shape: _layer_spec(shape, layer, 1)
    tab_spec = pl.BlockSpec((tile, LANES), lambda g: (proj_tile(g) % tab_tiles, 0))
    side_in, side_out, side_shapes = _side_cast_specs(side, n)
    wg, wu, wd, w_in = big
    return pl.pallas_call(
        functools.partial(_ffn_in_kernel, skewed, len(side)),
        grid=(n + 1 if skewed else n,),
        in_specs=[
            pl.BlockSpec((tile, D_MODEL), lambda g: (ffn_tile(g), 0)),
            tab_spec, tab_spec, tab_spec,
            lspec(1, D_MODEL), _resident(wg.shape), _resident(wu.shape), _resident(wd.shape),
            lspec(1, D_MODEL), lspec(1, D_MODEL), _resident(w_in.shape),
        ] + side_in,
        out_specs=[
            pl.BlockSpec((tile, D_MODEL), lambda g: (ffn_tile(g), 0)),
            pl.BlockSpec((tile, IN_WIDTH), lambda g: (proj_tile(g), 0)),
        ] + side_out,
        out_shape=[
            jax.ShapeDtypeStruct((n * tile, D_MODEL), F32),
            jax.ShapeDtypeStruct((n * tile, IN_WIDTH), F32),
        ] + side_shapes,
        scratch_shapes=[pltpu.VMEM((tile, D_FF), BF16)]
                       + ([pltpu.VMEM((tile, D_MODEL), F32)] if skewed else []),
        compiler_params=pltpu.CompilerParams(
            dimension_semantics=("arbitrary",), vmem_limit_bytes=VMEM_LIMIT_BYTES),
        name="ffn_in",
    )(x, *tabs, w["g_pre1"], wg, wu, wd, w["g_post1"], w["g_mix"], w_in, *[src for src, _ in side])


def _mix_prompt_kernel(layer, n_seq_tiles, sinks_ref, x1_ref, z_ref, conv_w_ref, conv_b_ref, wgate_ref, ba_ref,
                       bx_ref, lam_ref, g_attn_ref, g_lru_ref, wo_ref, g_postmix_ref, g_pre2_ref,
                       wg_ref, wu_ref, wd_ref, g_post2_ref,
                       x3_ref, nk_ref, nv_ref, nconv_ref, nh_ref,
                       k_scr, vt_scr, conv_scr, h_scr, x2_scr, bias_scr, attn_scr, xr_scr, xc_scr,
                       gates_scr, hs_scr, act_scr):
    tile = x1_ref.shape[0]
    g_idx = pl.program_id(0)
    n_tiles = pl.num_programs(0) - 1
    s_idx = g_idx % n_seq_tiles

    @pl.when(g_idx == 0)
    def _():
        x2_scr[...] = jnp.zeros(x2_scr.shape, F32)
        kj = lax.broadcasted_iota(jnp.int32, (2 * WINDOW, N_HEADS * WINDOW), 0)
        qi = lax.broadcasted_iota(jnp.int32, (2 * WINDOW, N_HEADS * WINDOW), 1) & (WINDOW - 1)
        band = (kj > qi) & (kj <= qi + WINDOW)
        bias_scr[0] = jnp.where(band, 0.0, -jnp.inf)
        bias_scr[1] = jnp.where(band & (kj >= WINDOW), 0.0, -jnp.inf)

    @pl.when(s_idx == 0)
    def _():
        k_scr[0:WINDOW, :] = jnp.zeros((WINDOW, KV_WIDTH), BF16)
        vt_scr[:, 0:WINDOW] = jnp.zeros((KV_WIDTH, WINDOW), BF16)
        conv_scr[...] = jnp.zeros((CONV_W - 1, LRU_WIDTH), F32)
        h_scr[...] = jnp.zeros((1, LRU_WIDTH), F32)

    def mixer_steps():
        k_scr[WINDOW:, :] = z_ref[:, O_K:O_V].astype(BF16)
        vt_scr[:, WINDOW:] = z_ref[:, O_V:O_XR].T.astype(BF16)
        sink = _sink_row(sinks_ref, layer, WINDOW)
        for i in range(tile // WINDOW):
            q_exp = _expand_q(z_ref[WINDOW * i:WINDOW * (i + 1), 0:ATTN_WIDTH], WINDOW)
            keys = k_scr[WINDOW * i:WINDOW * (i + 2), :]
            vals_t = vt_scr[:, WINDOW * i:WINDOW * (i + 2)]
            s = lax.dot_general(keys, q_exp, (((1,), (1,)), ((), ())), preferred_element_type=F32)
            s = s + bias_scr[jnp.where(s_idx == 0, 1, 0) if i == 0 else 0]
            m = jnp.maximum(jnp.max(s, axis=0, keepdims=True), sink)
            p = jnp.exp2(s - m)
            denom = jnp.sum(p, axis=0, keepdims=True) + jnp.exp2(sink - m)
            yield
            attn_scr[WINDOW * i:WINDOW * (i + 1), :] = _attend_values_t(
                vals_t, p.astype(BF16), 1.0 / denom, WINDOW).T
            yield
        k_scr[0:WINDOW, :] = k_scr[tile:tile + WINDOW, :]
        vt_scr[:, 0:WINDOW] = vt_scr[:, tile:tile + WINDOW]

        for j in range(LRU_WIDTH // LANES):
            ch = slice(LANES * j, LANES * (j + 1))
            xr_scr[j] = z_ref[:, O_XR + LANES * j:O_XR + LANES * (j + 1)]
            carry_rows = [conv_scr[m:m + 1, ch] for m in range(CONV_W - 1)]
            xc_phases = _conv_phases(_phase_blocks(xr_scr, j), carry_rows, conv_w_ref[:, ch],
                                     conv_b_ref[:, ch])
            xc_scr[:, ch] = jnp.concatenate(xc_phases, axis=0)
        conv_scr[...] = z_ref[tile - (CONV_W - 1):, O_XR:O_YG]
        gates_scr[...] = _lru_gates(xc_scr[...], wgate_ref)
        yield

        groups = tile // SUBLANES
        h_last = []
        for j in range(LRU_WIDTH // LANES):
            ch = slice(LANES * j, LANES * (j + 1))
            gate_x_ch = slice(LRU_WIDTH + LANES * j, LRU_WIDTH + LANES * (j + 1))
            a, u = _lru_coeffs(xc_scr[:, ch], gates_scr[:, ch], gates_scr[:, gate_x_ch],
                               ba_ref[:, ch], bx_ref[:, ch], lam_ref[:, ch])
            yield
            h_phases = _scan_phases(a, u, h_scr[:, ch])
            for k in range(SUBLANES):
                hs_scr[j, pl.ds(k, groups, stride=SUBLANES), :] = h_phases[k]
            h_last.append(h_phases[SUBLANES - 1][groups - 1:groups, :])
            yield
        h_last = jnp.concatenate(h_last, axis=1)
        h_scr[...] = h_last
        hs = jnp.concatenate([hs_scr[j] for j in range(LRU_WIDTH // LANES)], axis=1)
        mixed = _mix_normed(attn_scr[...], hs, z_ref[:, O_YG:], g_attn_ref[...], g_lru_ref[...])
        yield
        x2 = _mix_project(x1_ref[...], mixed, wo_ref, g_postmix_ref[...])
        return x2, h_last

    def ffn_steps():
        return _half_ffn_steps(x2_scr[...], g_pre2_ref[...], wg_ref, wu_ref, wd_ref,
                               g_post2_ref[...], act_scr)

    @pl.when(g_idx < n_tiles)
    def _():
        x3, (x2, h_last) = _interleave(ffn_steps(), mixer_steps())
        x3_ref[...] = x3
        x2_scr[...] = x2
        nk_ref[0] = z_ref[tile - WINDOW:, O_K:O_V]
        nv_ref[0] = z_ref[tile - WINDOW:, O_V:O_XR]
        nconv_ref[0] = z_ref[tile - (CONV_W - 1):, O_XR:O_YG]
        nh_ref[0] = h_last

    @pl.when(g_idx == n_tiles)
    def _():
        x3_ref[...] = _run(ffn_steps())


def _mix_weight_specs(layer, big):
    lspec = lambda *shape: _layer_spec(shape, layer, 1)
    w_o, wg, wu, wd = big
    return [
        lspec(CONV_W, LRU_WIDTH), lspec(1, LRU_WIDTH), lspec(LRU_WIDTH, 2 * LRU_WIDTH),
        lspec(1, LRU_WIDTH), lspec(1, LRU_WIDTH), lspec(1, LRU_WIDTH),
        lspec(1, ATTN_WIDTH), lspec(1, LRU_WIDTH), _resident(w_o.shape), lspec(1, D_MODEL),
        lspec(1, D_MODEL), _resident(wg.shape), _resident(wu.shape), _resident(wd.shape),
        lspec(1, D_MODEL),
    ]


def _mix_weights(w, big):
    w_o, wg, wu, wd = big
    return (w["conv_w"], w["conv_b"], w["w_gate"], w["gate_a_b"], w["gate_x_b"], w["lam"],
            w["g_attn"], w["g_lru"], w_o, w["g_postmix"], w["g_pre2"], wg, wu, wd, w["g_post2"])


def _mix_prompt(x1, z, w, big, layer, batch, seq, tile):
    n_s = seq // tile
    n_tiles = batch * n_s
    mix_tile = lambda g: jnp.minimum(g, n_tiles - 1)
    tok = lambda width: pl.BlockSpec((tile, width), lambda g: (mix_tile(g), 0))
    per_batch = lambda rows, width: pl.BlockSpec((1, rows, width), lambda g: (mix_tile(g) // n_s, 0, 0))
    return pl.pallas_call(
        functools.partial(_mix_prompt_kernel, layer, n_s),
        grid=(n_tiles + 1,),
        in_specs=[pl.BlockSpec(memory_space=pltpu.SMEM), tok(D_MODEL), tok(IN_WIDTH)]
                 + _mix_weight_specs(layer, big),
        out_specs=[
            pl.BlockSpec((tile, D_MODEL), lambda g: (jnp.maximum(g - 1, 0), 0)),
            per_batch(WINDOW, KV_WIDTH), per_batch(WINDOW, KV_WIDTH),
            per_batch(CONV_W - 1, LRU_WIDTH), per_batch(1, LRU_WIDTH),
        ],
        out_shape=[
            jax.ShapeDtypeStruct((batch * seq, D_MODEL), F32),
            jax.ShapeDtypeStruct((batch, WINDOW, KV_WIDTH), F32),
            jax.ShapeDtypeStruct((batch, WINDOW, KV_WIDTH), F32),
            jax.ShapeDtypeStruct((batch, CONV_W - 1, LRU_WIDTH), F32),
            jax.ShapeDtypeStruct((batch, 1, LRU_WIDTH), F32),
        ],
        scratch_shapes=[
            pltpu.VMEM((WINDOW + tile, KV_WIDTH), BF16),
            pltpu.VMEM((KV_WIDTH, WINDOW + tile), BF16),
            pltpu.VMEM((CONV_W - 1, LRU_WIDTH), F32),
            pltpu.VMEM((1, LRU_WIDTH), F32),
            pltpu.VMEM((tile, D_MODEL), F32),
            pltpu.VMEM((2, 2 * WINDOW, N_HEADS * WINDOW), F32),
            pltpu.VMEM((tile, ATTN_WIDTH), F32),
            pltpu.VMEM((LRU_WIDTH // LANES, tile, LANES), F32),
            pltpu.VMEM((tile, LRU_WIDTH), F32),
            pltpu.VMEM((tile, 2 * LRU_WIDTH), F32),
            pltpu.VMEM((LRU_WIDTH // LANES, tile, LANES), F32),
            pltpu.VMEM((tile, D_FF), BF16),
        ],
        compiler_params=pltpu.CompilerParams(
            dimension_semantics=("arbitrary",), vmem_limit_bytes=VMEM_LIMIT_BYTES),
        name="mix_prompt",
    )(w["sinks"], x1, z, *_mix_weights(w, big))


def _attn_sample_kernel(layer, sinks_ref, zq_ref, ck_ref, cv_ref, attn_ref):
    dec_seq, g = zq_ref.shape[0], zq_ref.shape[1]
    rows = dec_seq * g
    cache_len = ck_ref.shape[1]
    zq = zq_ref[...].reshape(rows, zq_ref.shape[2])
    q_exp = _expand_q(zq[:, 0:ATTN_WIDTH], rows)
    k_new = zq[:, O_K:O_V].astype(BF16)
    v_new = zq[:, O_V:O_XR].astype(BF16)
    k_cache = ck_ref[...].reshape(g * cache_len, KV_WIDTH).astype(BF16)
    v_cache = cv_ref[...].reshape(g * cache_len, KV_WIDTH).astype(BF16)
    contract_last = (((1,), (1,)), ((), ()))
    s_c = lax.dot_general(q_exp, k_cache, contract_last, preferred_element_type=F32)
    s_n = lax.dot_general(q_exp, k_new, contract_last, preferred_element_type=F32)

    def row_ids(shape):
        r = lax.broadcasted_iota(jnp.int32, shape, 0)
        return (r % rows) // g, r % g

    t_q, b_q = row_ids(s_c.shape)
    c = lax.broadcasted_iota(jnp.int32, s_c.shape, 1)
    mask_c = ((c // cache_len) == b_q) & ((c % cache_len) > t_q)
    t_q, b_q = row_ids(s_n.shape)
    c = lax.broadcasted_iota(jnp.int32, s_n.shape, 1)
    mask_n = ((c % g) == b_q) & ((c // g) <= t_q)
    s_c = jnp.where(mask_c, s_c, -jnp.inf)
    s_n = jnp.where(mask_n, s_n, -jnp.inf)

    sink = _sink_column(sinks_ref, layer, rows)
    m = jnp.maximum(jnp.maximum(jnp.max(s_c, axis=-1, keepdims=True),
                                jnp.max(s_n, axis=-1, keepdims=True)), sink)
    p_c = jnp.exp2(s_c - m)
    p_n = jnp.exp2(s_n - m)
    denom = (jnp.sum(p_c, axis=-1, keepdims=True) + jnp.sum(p_n, axis=-1, keepdims=True)
             + jnp.exp2(sink - m))
    o = (jnp.dot(p_c.astype(BF16), v_cache, preferred_element_type=F32)
         + jnp.dot(p_n.astype(BF16), v_new, preferred_element_type=F32)) / denom
    attn_ref[...] = _collect_heads(o, rows).reshape(dec_seq, g, ATTN_WIDTH)


def _attn_sample(z3, cache_k, cache_v, sinks, layer):
    dec_seq, dec_batch, _ = z3.shape
    cache_len = cache_k.shape[2]
    g = SAMPLE_GROUP
    cache_spec = pl.BlockSpec((None, g, cache_len, KV_WIDTH), lambda i: (layer, i, 0, 0))
    return pl.pallas_call(
        functools.partial(_attn_sample_kernel, layer),
        grid=(dec_batch // g,),
        in_specs=[
            pl.BlockSpec(memory_space=pltpu.SMEM),
            pl.BlockSpec((dec_seq, g, O_XR), lambda i: (0, i, 0)),
            cache_spec, cache_spec,
        ],
        out_specs=pl.BlockSpec((dec_seq, g, ATTN_WIDTH), lambda i: (0, i, 0)),
        out_shape=jax.ShapeDtypeStruct((dec_seq, dec_batch, ATTN_WIDTH), F32),
        compiler_params=pltpu.CompilerParams(
            dimension_semantics=("arbitrary",), vmem_limit_bytes=VMEM_LIMIT_BYTES),
        name="attn_sample",
    )(sinks, z3, cache_k, cache_v)


def _shift_caches_kernel(kvnew_ref, ck_ref, cv_ref, nk_ref, nv_ref):
    dec_seq = kvnew_ref.shape[1]
    cache_len = ck_ref.shape[1]
    nk_ref[:, 0:cache_len - dec_seq, :] = ck_ref[:, dec_seq:, :]
    nv_ref[:, 0:cache_len - dec_seq, :] = cv_ref[:, dec_seq:, :]
    nk_ref[:, cache_len - dec_seq:, :] = kvnew_ref[:, :, 0:KV_WIDTH]
    nv_ref[:, cache_len - dec_seq:, :] = kvnew_ref[:, :, KV_WIDTH:]


def _shift_caches(kvnew, cache_k, cache_v):
    depth, dec_batch, cache_len, _ = cache_k.shape
    dec_seq = kvnew.shape[2]
    g = SHIFT_GROUP
    spec = lambda rows, width: pl.BlockSpec((None, g, rows, width), lambda l, i: (l, i, 0, 0))
    return pl.pallas_call(
        _shift_caches_kernel,
        grid=(depth, dec_batch // g),
        in_specs=[spec(dec_seq, 2 * KV_WIDTH), spec(cache_len, KV_WIDTH), spec(cache_len, KV_WIDTH)],
        out_specs=[spec(cache_len, KV_WIDTH), spec(cache_len, KV_WIDTH)],
        out_shape=[jax.ShapeDtypeStruct(cache_k.shape, F32), jax.ShapeDtypeStruct(cache_v.shape, F32)],
        compiler_params=pltpu.CompilerParams(dimension_semantics=("arbitrary", "arbitrary")),
        name="shift_caches",
    )(kvnew, cache_k, cache_v)


def _mix_sample_kernel(x1_ref, z_ref, attn_ref, sconv_ref, sh_ref, conv_w_ref, conv_b_ref, wgate_ref,
                       ba_ref, bx_ref, lam_ref, g_attn_ref, g_lru_ref, wo_ref, g_postmix_ref,
                       g_pre2_ref, wg_ref, wu_ref, wd_ref, g_post2_ref,
                       x3_ref, nconv_ref, nh_ref, act_scr):
    nb = sh_ref.shape[0]
    dec_seq = x1_ref.shape[0] // nb
    xr = z_ref[:, O_XR:O_YG]
    xp = [sconv_ref[j] for j in range(CONV_W - 1)] + [xr[nb * t:nb * (t + 1)] for t in range(dec_seq)]
    xc_steps = []
    for t in range(dec_seq):
        acc = conv_b_ref[...] + xp[t] * conv_w_ref[0:1, :]
        for j in range(1, CONV_W):
            acc = acc + xp[t + j] * conv_w_ref[j:j + 1, :]
        xc_steps.append(acc)
    for j in range(CONV_W - 1):
        nconv_ref[j] = xp[dec_seq + j]
    xc = jnp.concatenate(xc_steps, axis=0)
    gates = _lru_gates(xc, wgate_ref)
    a, u = _lru_coeffs(xc, gates[:, :LRU_WIDTH], gates[:, LRU_WIDTH:], ba_ref[...], bx_ref[...],
                       lam_ref[...])
    h = sh_ref[...]
    hs_steps = []
    for t in range(dec_seq):
        h = a[nb * t:nb * (t + 1)] * h + u[nb * t:nb * (t + 1)]
        hs_steps.append(h)
    nh_ref[...] = h
    hs = jnp.concatenate(hs_steps, axis=0)
    x2 = _mix_out(x1_ref[...], attn_ref[...], hs, z_ref[:, O_YG:], g_attn_ref[...], g_lru_ref[...],
                  wo_ref, g_postmix_ref[...])
    x3_ref[...] = _half_ffn(x2, g_pre2_ref[...], wg_ref, wu_ref, wd_ref, g_post2_ref[...], act_scr)


def _mix_sample(x1, z, attn, sconv, state_h, w, big, layer):
    n = x1.shape[0]
    nb = state_h.shape[1]
    full = lambda *shape: pl.BlockSpec(shape, lambda i: (0,) * len(shape))
    return pl.pallas_call(
        _mix_sample_kernel,
        grid=(1,),
        in_specs=[
            full(n, D_MODEL), full(n, IN_WIDTH), full(n, ATTN_WIDTH),
            full(CONV_W - 1, nb, LRU_WIDTH), _layer_spec((nb, LRU_WIDTH), layer, 1),
        ] + _mix_weight_specs(layer, big),
        out_specs=[full(n, D_MODEL), full(CONV_W - 1, nb, LRU_WIDTH), full(nb, LRU_WIDTH)],
        out_shape=[
            jax.ShapeDtypeStruct((n, D_MODEL), F32),
            jax.ShapeDtypeStruct((CONV_W - 1, nb, LRU_WIDTH), F32),
            jax.ShapeDtypeStruct((nb, LRU_WIDTH), F32),
        ],
        scratch_shapes=[pltpu.VMEM((n, D_FF), BF16)],
        compiler_params=pltpu.CompilerParams(
            dimension_semantics=("arbitrary",), vmem_limit_bytes=VMEM_LIMIT_BYTES),
        name="mix_sample",
    )(x1, z, attn, sconv, state_h, *_mix_weights(w, big))


def _rope_tables(pos):
    half = ROT_DIM // 2
    inv = ROPE_THETA ** (-(jnp.arange(half, dtype=F32) * 2.0) / ROT_DIM)
    ang = pos.astype(F32)[:, None] * inv[None, :]
    cos, sin = jnp.cos(ang), jnp.sin(ang)
    n = pos.shape[0]
    rest = jnp.zeros((n, HEAD_DIM - ROT_DIM), F32)
    zero = jnp.zeros((n, half), F32)
    per_head = lambda parts: jnp.tile(jnp.concatenate(parts, axis=-1), (1, LANES // HEAD_DIM))
    return (per_head([cos, cos, rest + 1.0]), per_head([-sin, zero, rest]), per_head([zero, sin, rest]))


def _block_diag(w):
    eye = jnp.eye(LRU_HEADS, dtype=w.dtype)
    return jnp.einsum("lhij,hg->lhigj", w, eye).reshape(w.shape[0], LRU_WIDTH, LRU_WIDTH)


def kernel(x_prompt, x_sample, cache_k, cache_v, state_conv, state_h, norm_pre_ffn1, ffn1_w_gate, ffn1_w_up, ffn1_w_down, norm_post_ffn1, norm_pre_mix, w_in, sinks, conv_w, conv_b, gate_a_w, gate_a_b, gate_x_w, gate_x_b, lru_lambda, attn_out_norm, lru_out_norm, w_o, norm_post_mix, norm_pre_ffn2, ffn2_w_gate, ffn2_w_up, ffn2_w_down, norm_post_ffn2):
    batch, seq, _ = x_prompt.shape
    dec_batch, dec_seq, _ = x_sample.shape
    depth = w_in.shape[0]
    cache_len = cache_k.shape[2]
    n_s = dec_seq * dec_batch
    assert seq % FFN_TILE == 0 and n_s % FFN_TILE == 0
    assert seq % MIX_TILE == 0 and MIX_TILE % WINDOW == 0
    assert dec_batch % SAMPLE_GROUP == 0 and dec_batch % SHIFT_GROUP == 0

    tabs_p = _rope_tables(jnp.arange(seq))
    tabs_s = _rope_tables(jnp.repeat(PAST_LEN + jnp.arange(dec_seq), dec_batch))
    xp = x_prompt.reshape(batch * seq, D_MODEL)
    xs = jnp.swapaxes(x_sample, 0, 1).reshape(n_s, D_MODEL)
    rows = lambda v: v.reshape(depth, 1, -1)

    w = dict(
        g_pre1=rows(norm_pre_ffn1), g_post1=rows(norm_post_ffn1), g_mix=rows(norm_pre_mix),
        sinks=sinks, conv_w=conv_w, conv_b=rows(conv_b),
        w_gate=jnp.concatenate([_block_diag(gate_a_w), _block_diag(gate_x_w)], axis=2).astype(BF16),
        gate_a_b=rows(gate_a_b), gate_x_b=rows(gate_x_b), lam=rows(lru_lambda),
        g_attn=rows(attn_out_norm), g_lru=rows(lru_out_norm),
        g_postmix=rows(norm_post_mix), g_pre2=rows(norm_pre_ffn2), g_post2=rows(norm_post_ffn2),
    )
    ffn_in_src = (ffn1_w_gate, ffn1_w_up, ffn1_w_down, w_in)
    mix_src = (w_o, ffn2_w_gate, ffn2_w_up, ffn2_w_down)
    ck = cache_k.reshape(depth, dec_batch, cache_len, KV_WIDTH)
    cv = cache_v.reshape(depth, dec_batch, cache_len, KV_WIDTH)

    outs = [[] for _ in range(6)]
    kvnew = []
    ffn_in_big = [_to_bf16(s, 0) for s in ffn_in_src]
    for l in range(depth):
        side = [(s, l) for s in mix_src] + [(s, l + 1) for s in ffn_in_src if l + 1 < depth]
        x1p, zp, *cast = _ffn_in(xp, tabs_p, seq // FFN_TILE, w, ffn_in_big, l, FFN_TILE, side=side)
        mix_big, next_ffn_in_big = cast[:len(mix_src)], cast[len(mix_src):]
        xp, nk, nv, nc, nh = _mix_prompt(x1p, zp, w, mix_big, l, batch, seq, MIX_TILE)

        x1s, zs = _ffn_in(xs, tabs_s, n_s // FFN_TILE, w, ffn_in_big, l, FFN_TILE)
        z3 = zs.reshape(dec_seq, dec_batch, IN_WIDTH)
        kvnew.append(jnp.swapaxes(z3[:, :, O_K:O_XR], 0, 1))
        attn_s = _attn_sample(z3, ck, cv, sinks, l)
        xs, ncs, nhs = _mix_sample(x1s, zs, attn_s.reshape(n_s, ATTN_WIDTH),
                                   jnp.swapaxes(state_conv[l], 0, 1), state_h, w, mix_big, l)
        ffn_in_big = next_ffn_in_big
        kv_p = (batch, WINDOW, N_KV_HEADS, HEAD_DIM)
        for acc, val in zip(outs, (nk.reshape(kv_p), nv.reshape(kv_p), nc,
                                   nh.reshape(batch, LRU_WIDTH), jnp.swapaxes(ncs, 0, 1), nhs)):
            acc.append(val)

    nks, nvs = _shift_caches(jnp.stack(kvnew), ck, cv)
    y_prompt = xp.reshape(batch, seq, D_MODEL)
    y_sample = jnp.swapaxes(xs.reshape(dec_seq, dec_batch, D_MODEL), 0, 1)
    kp, vp, cp, hp, cs, hs = (jnp.stack(o) for o in outs)
    return (y_prompt, y_sample, kp, vp, cp, hp, nks.reshape(cache_k.shape), nvs.reshape(cache_v.shape),
            cs, hs)
```

```python
import functools

import jax
import jax.numpy as jnp
import numpy as np
from jax import lax
from jax.experimental import pallas as pl
from jax.experimental.pallas import tpu as pltpu

D_MODEL = 1024
PAST_LEN = 16384
N_HEADS = 8
HEAD_DIM = 64
N_KV_HEADS = 2
GROUP = N_HEADS // N_KV_HEADS
ATTN_WIDTH = N_HEADS * HEAD_DIM
KV_WIDTH = N_KV_HEADS * HEAD_DIM
WINDOW = 128
ROPE_THETA = 500000.0
ROT_DIM = HEAD_DIM // 4
LRU_WIDTH = D_MODEL // 2
LRU_HEADS = 8
LRU_BLOCK = LRU_WIDTH // LRU_HEADS
CONV_W = 4
LRU_C = 8.0
D_FF = 2816
IN_WIDTH = ATTN_WIDTH + 2 * KV_WIDTH + 2 * LRU_WIDTH
EPS = 1e-6

LANES = 128
SUBLANES = 8
FF_CHUNK = 256
N_FF_CHUNKS = D_FF // FF_CHUNK
FFN_TILE = 512
MIX_TILE = 512
SAMPLE_GROUP = 8
SHIFT_GROUP = 16
CAST_ROWS = 256
BF16_SUBLANES = 16
V7X_VMEM_BYTES = 64 * 1024 * 1024
VMEM_LIMIT_BYTES = V7X_VMEM_BYTES * 7 // 8

O_K = ATTN_WIDTH
O_V = O_K + KV_WIDTH
O_XR = O_V + KV_WIDTH
O_YG = O_XR + LRU_WIDTH

F32 = jnp.float32
BF16 = jnp.bfloat16
LOG2_E = np.float32(np.log2(np.e))


def _rmsnorm(x, g):
    return (x * lax.rsqrt(jnp.mean(x * x, axis=-1, keepdims=True) + EPS)) * g


def _half_ffn_steps(x, g_pre, wg_ref, wu_ref, wd_ref, g_post, act_scr):
    h = _rmsnorm(x, g_pre).astype(BF16)
    for c in range(N_FF_CHUNKS):
        cols = slice(FF_CHUNK * c, FF_CHUNK * (c + 1))
        g = jnp.dot(h, wg_ref[:, cols], preferred_element_type=F32)
        u = jnp.dot(h, wu_ref[:, cols], preferred_element_type=F32)
        act_scr[:, cols] = ((g * jax.nn.sigmoid(g)) * u).astype(BF16)
        yield
    ys = []
    for c in range(D_MODEL // FF_CHUNK):
        ys.append(jnp.dot(act_scr[...], wd_ref[:, FF_CHUNK * c:FF_CHUNK * (c + 1)],
                          preferred_element_type=F32))
        yield
    return x + _rmsnorm(jnp.concatenate(ys, axis=1), 0.5 * g_post)


def _advance(gen, done):
    if gen in done:
        return
    try:
        next(gen)
    except StopIteration as stop:
        done[gen] = stop.value


def _interleave(first, second):
    done = {}
    while len(done) < 2:
        _advance(first, done)
        _advance(second, done)
    return done[first], done[second]


def _run(gen):
    done = {}
    while gen not in done:
        _advance(gen, done)
    return done[gen]


def _half_ffn(*args):
    return _run(_half_ffn_steps(*args))


def _gelu_tanh_times(x, y):
    c = np.float32(np.sqrt(2.0 / np.pi))
    t = jnp.tanh(x * ((x * x) * (np.float32(0.044715) * c) + c))
    w = (0.5 * x) * y
    return t * w + w


def _softplus(x):
    return jnp.maximum(x, 0.0) + jnp.log1p(jnp.exp(-jnp.abs(x)))


def _lru_gates(xc, wgate_ref):
    return jnp.dot(xc.astype(BF16), wgate_ref[...], preferred_element_type=F32)


def _lru_coeffs(xc, gate_a, gate_x, ba, bx, lam):
    r = jax.nn.sigmoid(gate_a + ba)
    ig = jax.nn.sigmoid(gate_x + bx)
    a = jnp.exp2(r * ((-LRU_C * LOG2_E) * _softplus(-lam)))
    v = 1.0 - a * a
    u = jnp.where(v > 0.0, v * lax.rsqrt(v), 0.0) * (ig * xc)
    return a, u


def _scan_rows(a, u):
    n = a.shape[0]
    row = lax.broadcasted_iota(jnp.int32, a.shape, 0)
    d = 1
    while d < n:
        keep = row >= d
        u = jnp.where(keep, a * pltpu.roll(u, d, 0) + u, u)
        if 2 * d < n:
            a = jnp.where(keep, a * pltpu.roll(a, d, 0), a)
        d *= 2
    return u


def _phase_blocks(slab_ref, j):
    groups = slab_ref.shape[1] // SUBLANES
    return [slab_ref[j, pl.ds(k, groups, stride=SUBLANES), :] for k in range(SUBLANES)]


def _prev_group(block, carry_row):
    first = lax.broadcasted_iota(jnp.int32, block.shape, 0) == 0
    return jnp.where(first, carry_row, pltpu.roll(block, 1, 0))


def _conv_phases(x_phases, carry_rows, w, b):
    wrapped = {SUBLANES - m: _prev_group(x_phases[SUBLANES - m], carry_rows[CONV_W - 1 - m])
               for m in range(1, CONV_W)}
    out = []
    for k in range(SUBLANES):
        acc = b
        for tap in range(CONV_W):
            src = k - (CONV_W - 1 - tap)
            x = x_phases[src] if src >= 0 else wrapped[src + SUBLANES]
            acc = acc + x * w[tap:tap + 1, :]
        out.append(acc)
    return out


def _scan_phases(a, u, h0):
    groups = a.shape[0] // SUBLANES
    blk = lambda v, k: v[groups * k:groups * (k + 1)]
    h_zero = [blk(u, 0)]
    a_prod = [blk(a, 0)]
    for k in range(1, SUBLANES):
        h_zero.append(blk(a, k) * h_zero[-1] + blk(u, k))
        a_prod.append(blk(a, k) * a_prod[-1])
    first = lax.broadcasted_iota(jnp.int32, a_prod[-1].shape, 0) == 0
    h_end = _scan_rows(a_prod[-1], jnp.where(first, a_prod[-1] * h0 + h_zero[-1], h_zero[-1]))
    h_in = _prev_group(h_end, h0)
    return [h_zero[k] + a_prod[k] * h_in for k in range(SUBLANES)]


def _expand_q(q, rows):
    lane = lax.broadcasted_iota(jnp.int32, (rows, LANES), 1)
    pieces = []
    for h in range(N_HEADS):
        kv = h // GROUP
        src = q[:, LANES * (h // 2):LANES * (h // 2 + 1)]
        if (h % 2) != kv:
            src = pltpu.roll(src, HEAD_DIM, 1)
        in_half = (lane >= HEAD_DIM) if kv == 1 else (lane < HEAD_DIM)
        pieces.append(jnp.where(in_half, src, 0.0))
    return jnp.concatenate(pieces, axis=0).astype(BF16)


def _collect_heads(o, rows):
    lane = lax.broadcasted_iota(jnp.int32, (rows, LANES), 1)
    groups = []
    for j in range(N_HEADS // 2):
        kv = (2 * j) // GROUP
        even = o[rows * (2 * j):rows * (2 * j + 1)]
        odd = o[rows * (2 * j + 1):rows * (2 * j + 2)]
        if kv == 1:
            even = pltpu.roll(even, HEAD_DIM, 1)
        else:
            odd = pltpu.roll(odd, HEAD_DIM, 1)
        groups.append(jnp.where(lane < HEAD_DIM, even, odd))
    return jnp.concatenate(groups, axis=1)


def _collect_heads_t(o_t, rows):
    return jnp.concatenate(
        [o_t[HEAD_DIM * (h // GROUP):HEAD_DIM * (h // GROUP + 1), rows * h:rows * (h + 1)]
         for h in range(N_HEADS)], axis=0)


def _sink_row(sinks_ref, layer, rows):
    head = lax.broadcasted_iota(jnp.int32, (1, N_HEADS * rows), 1) // rows
    out = jnp.full((1, N_HEADS * rows), sinks_ref[layer, 0], F32)
    for h in range(1, N_HEADS):
        out = jnp.where(head == h, sinks_ref[layer, h], out)
    return out * LOG2_E


def _sink_column(sinks_ref, layer, rows):
    return jnp.concatenate([jnp.full((rows, 1), sinks_ref[layer, h], F32) for h in range(N_HEADS)],
                           axis=0) * LOG2_E


def _mix_normed(attn, hs, yg, g_attn, g_lru):
    lru = _gelu_tanh_times(yg, hs)
    return jnp.concatenate([_rmsnorm(attn, g_attn), _rmsnorm(lru, g_lru)], axis=-1).astype(BF16)


def _mix_project(x1, mixed, wo_ref, g_postmix):
    y = jnp.dot(mixed, wo_ref[...], preferred_element_type=F32)
    return x1 + _rmsnorm(y, g_postmix)


def _mix_out(x1, attn, hs, yg, g_attn, g_lru, wo_ref, g_postmix):
    return _mix_project(x1, _mix_normed(attn, hs, yg, g_attn, g_lru), wo_ref, g_postmix)


def _layer_spec(shape, layer, grid_rank):
    zeros = (0,) * len(shape)
    return pl.BlockSpec((None,) + tuple(shape), lambda *_: (layer,) + zeros,
                        pipeline_mode=pl.Buffered(1))


def _cast_kernel(w_ref, o_ref):
    o_ref[...] = w_ref[...].astype(BF16)


def _to_bf16(w, layer):
    _, rows, cols = w.shape
    assert rows % CAST_ROWS == 0
    return pl.pallas_call(
        _cast_kernel,
        grid=(rows // CAST_ROWS,),
        in_specs=[pl.BlockSpec((None, CAST_ROWS, cols), lambda i: (layer, i, 0))],
        out_specs=pl.BlockSpec((CAST_ROWS, cols), lambda i: (i, 0)),
        out_shape=jax.ShapeDtypeStruct((rows, cols), BF16),
        compiler_params=pltpu.CompilerParams(dimension_semantics=("arbitrary",)),
        name="cast_bf16",
    )(w)


def _side_block_rows(rows, n_tiles):
    for k in range(BF16_SUBLANES, rows + 1, BF16_SUBLANES):
        if rows % k == 0 and rows // k <= n_tiles:
            return k
    raise ValueError(f"cannot split {rows} rows over {n_tiles} steps")


def _side_cast_specs(sources, n_tiles):
    in_specs, out_specs, out_shapes = [], [], []
    for w, layer in sources:
        _, rows, cols = w.shape
        br = _side_block_rows(rows, n_tiles)
        block = functools.partial(jnp.minimum, rows // br - 1)
        in_specs.append(pl.BlockSpec((None, br, cols),
                                     lambda g, layer=layer, block=block: (layer, block(g), 0)))
        out_specs.append(pl.BlockSpec((br, cols), lambda g, block=block: (block(g), 0)))
        out_shapes.append(jax.ShapeDtypeStruct((rows, cols), BF16))
    return in_specs, out_specs, out_shapes


def _side_cast(src_refs, dst_refs):
    for src, dst in zip(src_refs, dst_refs):
        dst[...] = src[...].astype(BF16)


def _resident(shape):
    return pl.BlockSpec(shape, lambda *_: (0,) * len(shape))


def _in_proj_steps(x1, g_mix, win_ref, cos, sin_up, sin_dn, z_ref):
    h = _rmsnorm(x1, g_mix).astype(BF16)
    half = ROT_DIM // 2
    for c in range(IN_WIDTH // FF_CHUNK):
        zc = jnp.dot(h, win_ref[:, FF_CHUNK * c:FF_CHUNK * (c + 1)], preferred_element_type=F32)
        for k in range(FF_CHUNK // LANES):
            j = (FF_CHUNK // LANES) * c + k
            zj = zc[:, LANES * k:LANES * (k + 1)]
            if j < O_V // LANES:
                zj = (zj * cos + pltpu.roll(zj, LANES - half, 1) * sin_up
                      + pltpu.roll(zj, half, 1) * sin_dn)
            if j < ATTN_WIDTH // LANES:
                zj = zj * (LOG2_E * HEAD_DIM ** -0.5)
            z_ref[:, LANES * j:LANES * (j + 1)] = zj
        yield


N_LRU_WEIGHTS = 6


def _lru_weight_specs(layer):
    lspec = lambda *shape: _layer_spec(shape, layer, 1)
    return [lspec(CONV_W, LRU_WIDTH), lspec(1, LRU_WIDTH), lspec(LRU_WIDTH, 2 * LRU_WIDTH),
            lspec(1, LRU_WIDTH), lspec(1, LRU_WIDTH), lspec(1, LRU_WIDTH)]


def _lru_weights(w):
    return (w["conv_w"], w["conv_b"], w["w_gate"], w["gate_a_b"], w["gate_x_b"], w["lam"])


def _lru_prompt_steps(z_ref, lru_refs, conv_scr, h_scr, xr_scr, xc_scr, gates_scr, hs_scr):
    conv_w_ref, conv_b_ref, wgate_ref, ba_ref, bx_ref, lam_ref = lru_refs
    tile = z_ref.shape[0]
    groups = tile // SUBLANES
    n_slabs = LRU_WIDTH // LANES
    for j in range(n_slabs):
        ch = slice(LANES * j, LANES * (j + 1))
        xr_scr[j] = z_ref[:, O_XR + LANES * j:O_XR + LANES * (j + 1)]
        carry_rows = [conv_scr[m:m + 1, ch] for m in range(CONV_W - 1)]
        xc_phases = _conv_phases(_phase_blocks(xr_scr, j), carry_rows, conv_w_ref[:, ch],
                                 conv_b_ref[:, ch])
        xc_scr[:, ch] = jnp.concatenate(xc_phases, axis=0)
    last_rows = z_ref[tile - (CONV_W - 1):, O_XR:O_YG]
    conv_scr[...] = last_rows
    gates_scr[...] = _lru_gates(xc_scr[...], wgate_ref)
    yield
    h_last = []
    for j in range(n_slabs):
        ch = slice(LANES * j, LANES * (j + 1))
        gate_x_ch = slice(LRU_WIDTH + LANES * j, LRU_WIDTH + LANES * (j + 1))
        a, u = _lru_coeffs(xc_scr[:, ch], gates_scr[:, ch], gates_scr[:, gate_x_ch],
                           ba_ref[:, ch], bx_ref[:, ch], lam_ref[:, ch])
        yield
        h_phases = _scan_phases(a, u, h_scr[:, ch])
        for k in range(SUBLANES):
            hs_scr[j, pl.ds(k, groups, stride=SUBLANES), :] = h_phases[k]
        h_last.append(h_phases[SUBLANES - 1][groups - 1:groups, :])
        z_ref[:, O_XR + LANES * j:O_XR + LANES * (j + 1)] = hs_scr[j]
        yield
    h_last = jnp.concatenate(h_last, axis=1)
    h_scr[...] = h_last
    return last_rows, h_last


def _ffn_in_kernel(n_seq_tiles, n_side, *refs):
    pipelined = n_seq_tiles > 0
    (x_ref, cos_ref, sin_up_ref, sin_dn_ref, g_pre_ref, wg_ref, wu_ref, wd_ref, g_post_ref,
     g_mix_ref, win_ref) = refs[:11]
    n_lru = N_LRU_WEIGHTS if pipelined else 0
    lru_refs = refs[11:11 + n_lru]
    n_in = 11 + n_lru + n_side
    side_src = refs[11 + n_lru:n_in]
    x1_ref, z_ref = refs[n_in:n_in + 2]
    state_out = refs[n_in + 2:n_in + 2 + (2 if pipelined else 0)]
    n_out = 2 + len(state_out) + n_side
    side_dst = refs[n_in + 2 + len(state_out):n_in + n_out]
    act_scr, *stage_scr = refs[n_in + n_out:]

    def ffn_steps():
        return _half_ffn_steps(x_ref[...], g_pre_ref[...], wg_ref, wu_ref, wd_ref, g_post_ref[...],
                               act_scr)

    def in_proj_steps(x1):
        return _in_proj_steps(x1, g_mix_ref[...], win_ref, cos_ref[...], sin_up_ref[...],
                              sin_dn_ref[...], z_ref)

    if not pipelined:
        x1 = _run(ffn_steps())
        x1_ref[...] = x1
        _run(in_proj_steps(x1))
        return
    x1_scr, conv_scr, h_scr, xr_scr, xc_scr, gates_scr, hs_scr = stage_scr
    nconv_ref, nh_ref = state_out
    g_idx = pl.program_id(0)
    n_tiles = pl.num_programs(0) - 1

    @pl.when(g_idx == 0)
    def _():
        x1_scr[...] = jnp.zeros(x1_scr.shape, F32)

    @pl.when(jnp.maximum(g_idx - 1, 0) % n_seq_tiles == 0)
    def _():
        conv_scr[...] = jnp.zeros(conv_scr.shape, F32)
        h_scr[...] = jnp.zeros(h_scr.shape, F32)

    def stage_b_steps():
        yield from in_proj_steps(x1_scr[...])
        last_rows, h_last = yield from _lru_prompt_steps(z_ref, lru_refs, conv_scr, h_scr, xr_scr,
                                                         xc_scr, gates_scr, hs_scr)
        nconv_ref[0] = last_rows
        nh_ref[0] = h_last

    @pl.when(g_idx < n_tiles)
    def _():
        x1, _ = _interleave(ffn_steps(), stage_b_steps())
        x1_ref[...] = x1
        x1_scr[...] = x1
        _side_cast(side_src, side_dst)

    @pl.when(g_idx == n_tiles)
    def _():
        _run(stage_b_steps())


def _ffn_in_sample(x, tabs, w, big, layer):
    n = x.shape[0]
    lspec = lambda *shape: _layer_spec(shape, layer, 1)
    full = lambda *shape: pl.BlockSpec(shape, lambda g: (0,) * len(shape))
    wg, wu, wd, w_in = big
    return pl.pallas_call(
        functools.partial(_ffn_in_kernel, 0, 0),
        grid=(1,),
        in_specs=[
            full(n, D_MODEL), full(n, LANES), full(n, LANES), full(n, LANES),
            lspec(1, D_MODEL), _resident(wg.shape), _resident(wu.shape), _resident(wd.shape),
            lspec(1, D_MODEL), lspec(1, D_MODEL), _resident(w_in.shape),
        ],
        out_specs=[full(n, D_MODEL), full(n, IN_WIDTH)],
        out_shape=[jax.ShapeDtypeStruct((n, D_MODEL), F32), jax.ShapeDtypeStruct((n, IN_WIDTH), F32)],
        scratch_shapes=[pltpu.VMEM((n, D_FF), BF16)],
        compiler_params=pltpu.CompilerParams(
            dimension_semantics=("arbitrary",), vmem_limit_bytes=VMEM_LIMIT_BYTES),
        name="ffn_in_sample",
    )(x, *tabs, w["g_pre1"], wg, wu, wd, w["g_post1"], w["g_mix"], w_in)


def _ffn_in_prompt(x, tabs, w, big, layer, batch, seq, tile, side=()):
    n_s = seq // tile
    n = batch * n_s
    ffn_tile = lambda g: jnp.minimum(g, n - 1)
    proj_tile = lambda g: jnp.maximum(g - 1, 0)
    lspec = lambda *shape: _layer_spec(shape, layer, 1)
    tab_spec = pl.BlockSpec((tile, LANES), lambda g: (proj_tile(g) % n_s, 0))
    per_batch = lambda rows: pl.BlockSpec((1, rows, LRU_WIDTH), lambda g: (proj_tile(g) // n_s, 0, 0))
    side_in, side_out, side_shapes = _side_cast_specs(side, n)
    wg, wu, wd, w_in = big
    slabs = (LRU_WIDTH // LANES, tile, LANES)
    return pl.pallas_call(
        functools.partial(_ffn_in_kernel, n_s, len(side)),
        grid=(n + 1,),
        in_specs=[
            pl.BlockSpec((tile, D_MODEL), lambda g: (ffn_tile(g), 0)),
            tab_spec, tab_spec, tab_spec,
            lspec(1, D_MODEL), _resident(wg.shape), _resident(wu.shape), _resident(wd.shape),
            lspec(1, D_MODEL), lspec(1, D_MODEL), _resident(w_in.shape),
        ] + _lru_weight_specs(layer) + side_in,
        out_specs=[
            pl.BlockSpec((tile, D_MODEL), lambda g: (ffn_tile(g), 0)),
            pl.BlockSpec((tile, IN_WIDTH), lambda g: (proj_tile(g), 0)),
            per_batch(CONV_W - 1), per_batch(1),
        ] + side_out,
        out_shape=[
            jax.ShapeDtypeStruct((n * tile, D_MODEL), F32),
            jax.ShapeDtypeStruct((n * tile, IN_WIDTH), F32),
            jax.ShapeDtypeStruct((batch, CONV_W - 1, LRU_WIDTH), F32),
            jax.ShapeDtypeStruct((batch, 1, LRU_WIDTH), F32),
        ] + side_shapes,
        scratch_shapes=[
            pltpu.VMEM((tile, D_FF), BF16),
            pltpu.VMEM((tile, D_MODEL), F32),
            pltpu.VMEM((CONV_W - 1, LRU_WIDTH), F32),
            pltpu.VMEM((1, LRU_WIDTH), F32),
            pltpu.VMEM(slabs, F32),
            pltpu.VMEM((tile, LRU_WIDTH), F32),
            pltpu.VMEM((tile, 2 * LRU_WIDTH), F32),
            pltpu.VMEM(slabs, F32),
        ],
        compiler_params=pltpu.CompilerParams(
            dimension_semantics=("arbitrary",), vmem_limit_bytes=VMEM_LIMIT_BYTES),
        name="ffn_in",
    )(x, *tabs, w["g_pre1"], wg, wu, wd, w["g_post1"], w["g_mix"], w_in, *_lru_weights(w),
      *[src for src, _ in side])


def _mix_prompt_kernel(layer, n_seq_tiles, n_side, *refs):
    (sinks_ref, x1_ref, z_ref, g_attn_ref, g_lru_ref, wo_ref, g_postmix_ref, g_pre2_ref, wg_ref,
     wu_ref, wd_ref, g_post2_ref) = refs[:12]
    side_src = refs[12:12 + n_side]
    x3_ref, nk_ref, nv_ref = refs[12 + n_side:15 + n_side]
    side_dst = refs[15 + n_side:15 + 2 * n_side]
    k_scr, vt_scr, x2_scr, bias_scr, attn_scr, act_scr = refs[15 + 2 * n_side:]
    tile = x1_ref.shape[0]
    g_idx = pl.program_id(0)
    n_tiles = pl.num_programs(0) - 1
    s_idx = g_idx % n_seq_tiles

    @pl.when(g_idx == 0)
    def _():
        x2_scr[...] = jnp.zeros(x2_scr.shape, F32)
        kj = lax.broadcasted_iota(jnp.int32, (2 * WINDOW, N_HEADS * WINDOW), 0)
        qi = lax.broadcasted_iota(jnp.int32, (2 * WINDOW, N_HEADS * WINDOW), 1) & (WINDOW - 1)
        band = (kj > qi) & (kj <= qi + WINDOW)
        bias_scr[0] = jnp.where(band, 0.0, -jnp.inf)
        bias_scr[1] = jnp.where(band & (kj >= WINDOW), 0.0, -jnp.inf)

    @pl.when(s_idx == 0)
    def _():
        k_scr[0:WINDOW, :] = jnp.zeros((WINDOW, KV_WIDTH), BF16)
        vt_scr[:, 0:WINDOW] = jnp.zeros((KV_WIDTH, WINDOW), BF16)

    def mixer_steps():
        k_scr[WINDOW:, :] = z_ref[:, O_K:O_V].astype(BF16)
        vt_scr[:, WINDOW:] = z_ref[:, O_V:O_XR].T.astype(BF16)
        sink = _sink_row(sinks_ref, layer, WINDOW)
        for i in range(tile // WINDOW):
            q_exp = _expand_q(z_ref[WINDOW * i:WINDOW * (i + 1), 0:ATTN_WIDTH], WINDOW)
            keys = k_scr[WINDOW * i:WINDOW * (i + 2), :]
            vals_t = vt_scr[:, WINDOW * i:WINDOW * (i + 2)]
            s = lax.dot_general(keys, q_exp, (((1,), (1,)), ((), ())), preferred_element_type=F32)
            s = s + bias_scr[jnp.where(s_idx == 0, 1, 0) if i == 0 else 0]
            m = jnp.maximum(jnp.max(s, axis=0, keepdims=True), sink)
            p = jnp.exp2(s - m)
            denom = jnp.sum(p, axis=0, keepdims=True) + jnp.exp2(sink - m)
            yield
            o_t = jnp.dot(vals_t, p.astype(BF16), preferred_element_type=F32) * (1.0 / denom)
            attn_scr[WINDOW * i:WINDOW * (i + 1), :] = _collect_heads_t(o_t, WINDOW).T
            yield
        k_scr[0:WINDOW, :] = k_scr[tile:tile + WINDOW, :]
        vt_scr[:, 0:WINDOW] = vt_scr[:, tile:tile + WINDOW]
        mixed = _mix_normed(attn_scr[...], z_ref[:, O_XR:O_YG], z_ref[:, O_YG:], g_attn_ref[...],
                            g_lru_ref[...])
        yield
        return _mix_project(x1_ref[...], mixed, wo_ref, g_postmix_ref[...])

    def ffn_steps():
        return _half_ffn_steps(x2_scr[...], g_pre2_ref[...], wg_ref, wu_ref, wd_ref,
                               g_post2_ref[...], act_scr)

    @pl.when(g_idx < n_tiles)
    def _():
        x3, x2 = _interleave(ffn_steps(), mixer_steps())
        x3_ref[...] = x3
        x2_scr[...] = x2
        nk_ref[0] = z_ref[tile - WINDOW:, O_K:O_V]
        nv_ref[0] = z_ref[tile - WINDOW:, O_V:O_XR]
        _side_cast(side_src, side_dst)

    @pl.when(g_idx == n_tiles)
    def _():
        x3_ref[...] = _run(ffn_steps())


def _tail_weight_specs(layer, big):
    lspec = lambda *shape: _layer_spec(shape, layer, 1)
    w_o, wg, wu, wd = big
    return [
        lspec(1, ATTN_WIDTH), lspec(1, LRU_WIDTH), _resident(w_o.shape), lspec(1, D_MODEL),
        lspec(1, D_MODEL), _resident(wg.shape), _resident(wu.shape), _resident(wd.shape),
        lspec(1, D_MODEL),
    ]


def _tail_weights(w, big):
    w_o, wg, wu, wd = big
    return (w["g_attn"], w["g_lru"], w_o, w["g_postmix"], w["g_pre2"], wg, wu, wd, w["g_post2"])


def _mix_prompt(x1, z, w, big, layer, batch, seq, tile, side=()):
    n_s = seq // tile
    n_tiles = batch * n_s
    mix_tile = lambda g: jnp.minimum(g, n_tiles - 1)
    tok = lambda width: pl.BlockSpec((tile, width), lambda g: (mix_tile(g), 0))
    per_batch = lambda rows, width: pl.BlockSpec((1, rows, width), lambda g: (mix_tile(g) // n_s, 0, 0))
    side_in, side_out, side_shapes = _side_cast_specs(side, n_tiles)
    return pl.pallas_call(
        functools.partial(_mix_prompt_kernel, layer, n_s, len(side)),
        grid=(n_tiles + 1,),
        in_specs=[pl.BlockSpec(memory_space=pltpu.SMEM), tok(D_MODEL), tok(IN_WIDTH)]
                 + _tail_weight_specs(layer, big) + side_in,
        out_specs=[
            pl.BlockSpec((tile, D_MODEL), lambda g: (jnp.maximum(g - 1, 0), 0)),
            per_batch(WINDOW, KV_WIDTH), per_batch(WINDOW, KV_WIDTH),
        ] + side_out,
        out_shape=[
            jax.ShapeDtypeStruct((batch * seq, D_MODEL), F32),
            jax.ShapeDtypeStruct((batch, WINDOW, KV_WIDTH), F32),
            jax.ShapeDtypeStruct((batch, WINDOW, KV_WIDTH), F32),
        ] + side_shapes,
        scratch_shapes=[
            pltpu.VMEM((WINDOW + tile, KV_WIDTH), BF16),
            pltpu.VMEM((KV_WIDTH, WINDOW + tile), BF16),
            pltpu.VMEM((tile, D_MODEL), F32),
            pltpu.VMEM((2, 2 * WINDOW, N_HEADS * WINDOW), F32),
            pltpu.VMEM((tile, ATTN_WIDTH), F32),
            pltpu.VMEM((tile, D_FF), BF16),
        ],
        compiler_params=pltpu.CompilerParams(
            dimension_semantics=("arbitrary",), vmem_limit_bytes=VMEM_LIMIT_BYTES),
        name="mix_prompt",
    )(w["sinks"], x1, z, *_tail_weights(w, big), *[src for src, _ in side])


def _attn_sample_kernel(layer, sinks_ref, zq_ref, ck_ref, cv_ref, attn_ref):
    dec_seq, g = zq_ref.shape[0], zq_ref.shape[1]
    rows = dec_seq * g
    cache_len = ck_ref.shape[1]
    zq = zq_ref[...].reshape(rows, zq_ref.shape[2])
    q_exp = _expand_q(zq[:, 0:ATTN_WIDTH], rows)
    k_new = zq[:, O_K:O_V].astype(BF16)
    v_new = zq[:, O_V:O_XR].astype(BF16)
    k_cache = ck_ref[...].reshape(g * cache_len, KV_WIDTH).astype(BF16)
    v_cache = cv_ref[...].reshape(g * cache_len, KV_WIDTH).astype(BF16)
    contract_last = (((1,), (1,)), ((), ()))
    s_c = lax.dot_general(q_exp, k_cache, contract_last, preferred_element_type=F32)
    s_n = lax.dot_general(q_exp, k_new, contract_last, preferred_element_type=F32)

    def row_ids(shape):
        r = lax.broadcasted_iota(jnp.int32, shape, 0)
        return (r % rows) // g, r % g

    t_q, b_q = row_ids(s_c.shape)
    c = lax.broadcasted_iota(jnp.int32, s_c.shape, 1)
    mask_c = ((c // cache_len) == b_q) & ((c % cache_len) > t_q)
    t_q, b_q = row_ids(s_n.shape)
    c = lax.broadcasted_iota(jnp.int32, s_n.shape, 1)
    mask_n = ((c % g) == b_q) & ((c // g) <= t_q)
    s_c = jnp.where(mask_c, s_c, -jnp.inf)
    s_n = jnp.where(mask_n, s_n, -jnp.inf)

    sink = _sink_column(sinks_ref, layer, rows)
    m = jnp.maximum(jnp.maximum(jnp.max(s_c, axis=-1, keepdims=True),
                                jnp.max(s_n, axis=-1, keepdims=True)), sink)
    p_c = jnp.exp2(s_c - m)
    p_n = jnp.exp2(s_n - m)
    denom = (jnp.sum(p_c, axis=-1, keepdims=True) + jnp.sum(p_n, axis=-1, keepdims=True)
             + jnp.exp2(sink - m))
    o = (jnp.dot(p_c.astype(BF16), v_cache, preferred_element_type=F32)
         + jnp.dot(p_n.astype(BF16), v_new, preferred_element_type=F32)) / denom
    attn_ref[...] = _collect_heads(o, rows).reshape(dec_seq, g, ATTN_WIDTH)


def _attn_sample(z3, cache_k, cache_v, sinks, layer):
    dec_seq, dec_batch, _ = z3.shape
    cache_len = cache_k.shape[2]
    g = SAMPLE_GROUP
    cache_spec = pl.BlockSpec((None, g, cache_len, KV_WIDTH), lambda i: (layer, i, 0, 0))
    return pl.pallas_call(
        functools.partial(_attn_sample_kernel, layer),
        grid=(dec_batch // g,),
        in_specs=[
            pl.BlockSpec(memory_space=pltpu.SMEM),
            pl.BlockSpec((dec_seq, g, O_XR), lambda i: (0, i, 0)),
            cache_spec, cache_spec,
        ],
        out_specs=pl.BlockSpec((dec_seq, g, ATTN_WIDTH), lambda i: (0, i, 0)),
        out_shape=jax.ShapeDtypeStruct((dec_seq, dec_batch, ATTN_WIDTH), F32),
        compiler_params=pltpu.CompilerParams(
            dimension_semantics=("arbitrary",), vmem_limit_bytes=VMEM_LIMIT_BYTES),
        name="attn_sample",
    )(sinks, z3, cache_k, cache_v)


def _shift_caches_kernel(kvnew_ref, ck_ref, cv_ref, nk_ref, nv_ref):
    dec_seq = kvnew_ref.shape[1]
    cache_len = ck_ref.shape[1]
    nk_ref[:, 0:cache_len - dec_seq, :] = ck_ref[:, dec_seq:, :]
    nv_ref[:, 0:cache_len - dec_seq, :] = cv_ref[:, dec_seq:, :]
    nk_ref[:, cache_len - dec_seq:, :] = kvnew_ref[:, :, 0:KV_WIDTH]
    nv_ref[:, cache_len - dec_seq:, :] = kvnew_ref[:, :, KV_WIDTH:]


def _shift_caches(kvnew, cache_k, cache_v):
    depth, dec_batch, cache_len, _ = cache_k.shape
    dec_seq = kvnew.shape[2]
    g = SHIFT_GROUP
    spec = lambda rows, width: pl.BlockSpec((None, g, rows, width), lambda l, i: (l, i, 0, 0))
    return pl.pallas_call(
        _shift_caches_kernel,
        grid=(depth, dec_batch // g),
        in_specs=[spec(dec_seq, 2 * KV_WIDTH), spec(cache_len, KV_WIDTH), spec(cache_len, KV_WIDTH)],
        out_specs=[spec(cache_len, KV_WIDTH), spec(cache_len, KV_WIDTH)],
        out_shape=[jax.ShapeDtypeStruct(cache_k.shape, F32), jax.ShapeDtypeStruct(cache_v.shape, F32)],
        compiler_params=pltpu.CompilerParams(dimension_semantics=("arbitrary", "arbitrary")),
        name="shift_caches",
    )(kvnew, cache_k, cache_v)


def _mix_sample_kernel(x1_ref, z_ref, attn_ref, sconv_ref, sh_ref, conv_w_ref, conv_b_ref, wgate_ref,
                       ba_ref, bx_ref, lam_ref, g_attn_ref, g_lru_ref, wo_ref, g_postmix_ref,
                       g_pre2_ref, wg_ref, wu_ref, wd_ref, g_post2_ref,
                       x3_ref, nconv_ref, nh_ref, act_scr):
    nb = sh_ref.shape[0]
    dec_seq = x1_ref.shape[0] // nb
    xr = z_ref[:, O_XR:O_YG]
    xp = [sconv_ref[j] for j in range(CONV_W - 1)] + [xr[nb * t:nb * (t + 1)] for t in range(dec_seq)]
    xc_steps = []
    for t in range(dec_seq):
        acc = conv_b_ref[...] + xp[t] * conv_w_ref[0:1, :]
        for j in range(1, CONV_W):
            acc = acc + xp[t + j] * conv_w_ref[j:j + 1, :]
        xc_steps.append(acc)
    for j in range(CONV_W - 1):
        nconv_ref[j] = xp[dec_seq + j]
    xc = jnp.concatenate(xc_steps, axis=0)
    gates = _lru_gates(xc, wgate_ref)
    a, u = _lru_coeffs(xc, gates[:, :LRU_WIDTH], gates[:, LRU_WIDTH:], ba_ref[...], bx_ref[...],
                       lam_ref[...])
    h = sh_ref[...]
    hs_steps = []
    for t in range(dec_seq):
        h = a[nb * t:nb * (t + 1)] * h + u[nb * t:nb * (t + 1)]
        hs_steps.append(h)
    nh_ref[...] = h
    hs = jnp.concatenate(hs_steps, axis=0)
    x2 = _mix_out(x1_ref[...], attn_ref[...], hs, z_ref[:, O_YG:], g_attn_ref[...], g_lru_ref[...],
                  wo_ref, g_postmix_ref[...])
    x3_ref[...] = _half_ffn(x2, g_pre2_ref[...], wg_ref, wu_ref, wd_ref, g_post2_ref[...], act_scr)


def _mix_sample(x1, z, attn, sconv, state_h, w, big, layer):
    n = x1.shape[0]
    nb = state_h.shape[1]
    full = lambda *shape: pl.BlockSpec(shape, lambda i: (0,) * len(shape))
    return pl.pallas_call(
        _mix_sample_kernel,
        grid=(1,),
        in_specs=[
            full(n, D_MODEL), full(n, IN_WIDTH), full(n, ATTN_WIDTH),
            full(CONV_W - 1, nb, LRU_WIDTH), _layer_spec((nb, LRU_WIDTH), layer, 1),
        ] + _lru_weight_specs(layer) + _tail_weight_specs(layer, big),
        out_specs=[full(n, D_MODEL), full(CONV_W - 1, nb, LRU_WIDTH), full(nb, LRU_WIDTH)],
        out_shape=[
            jax.ShapeDtypeStruct((n, D_MODEL), F32),
            jax.ShapeDtypeStruct((CONV_W - 1, nb, LRU_WIDTH), F32),
            jax.ShapeDtypeStruct((nb, LRU_WIDTH), F32),
        ],
        scratch_shapes=[pltpu.VMEM((n, D_FF), BF16)],
        compiler_params=pltpu.CompilerParams(
            dimension_semantics=("arbitrary",), vmem_limit_bytes=VMEM_LIMIT_BYTES),
        name="mix_sample",
    )(x1, z, attn, sconv, state_h, *_lru_weights(w), *_tail_weights(w, big))


def _rope_tables(pos):
    half = ROT_DIM // 2
    inv = ROPE_THETA ** (-(jnp.arange(half, dtype=F32) * 2.0) / ROT_DIM)
    ang = pos.astype(F32)[:, None] * inv[None, :]
    cos, sin = jnp.cos(ang), jnp.sin(ang)
    n = pos.shape[0]
    rest = jnp.zeros((n, HEAD_DIM - ROT_DIM), F32)
    zero = jnp.zeros((n, half), F32)
    per_head = lambda parts: jnp.tile(jnp.concatenate(parts, axis=-1), (1, LANES // HEAD_DIM))
    return (per_head([cos, cos, rest + 1.0]), per_head([-sin, zero, rest]), per_head([zero, sin, rest]))


def _block_diag(w):
    eye = jnp.eye(LRU_HEADS, dtype=w.dtype)
    return jnp.einsum("lhij,hg->lhigj", w, eye).reshape(w.shape[0], LRU_WIDTH, LRU_WIDTH)


def kernel(x_prompt, x_sample, cache_k, cache_v, state_conv, state_h, norm_pre_ffn1, ffn1_w_gate, ffn1_w_up, ffn1_w_down, norm_post_ffn1, norm_pre_mix, w_in, sinks, conv_w, conv_b, gate_a_w, gate_a_b, gate_x_w, gate_x_b, lru_lambda, attn_out_norm, lru_out_norm, w_o, norm_post_mix, norm_pre_ffn2, ffn2_w_gate, ffn2_w_up, ffn2_w_down, norm_post_ffn2):
    batch, seq, _ = x_prompt.shape
    dec_batch, dec_seq, _ = x_sample.shape
    depth = w_in.shape[0]
    cache_len = cache_k.shape[2]
    n_s = dec_seq * dec_batch
    assert seq % FFN_TILE == 0 and n_s % FFN_TILE == 0
    assert seq % MIX_TILE == 0 and MIX_TILE % WINDOW == 0
    assert dec_batch % SAMPLE_GROUP == 0 and dec_batch % SHIFT_GROUP == 0

    tabs_p = _rope_tables(jnp.arange(seq))
    tabs_s = _rope_tables(jnp.repeat(PAST_LEN + jnp.arange(dec_seq), dec_batch))
    xp = x_prompt.reshape(batch * seq, D_MODEL)
    xs = jnp.swapaxes(x_sample, 0, 1).reshape(n_s, D_MODEL)
    rows = lambda v: v.reshape(depth, 1, -1)

    w = dict(
        g_pre1=rows(norm_pre_ffn1), g_post1=rows(norm_post_ffn1), g_mix=rows(norm_pre_mix),
        sinks=sinks, conv_w=conv_w, conv_b=rows(conv_b),
        w_gate=jnp.concatenate([_block_diag(gate_a_w), _block_diag(gate_x_w)], axis=2).astype(BF16),
        gate_a_b=rows(gate_a_b), gate_x_b=rows(gate_x_b), lam=rows(lru_lambda),
        g_attn=rows(attn_out_norm), g_lru=rows(lru_out_norm),
        g_postmix=rows(norm_post_mix), g_pre2=rows(norm_pre_ffn2), g_post2=rows(norm_post_ffn2),
    )
    ffn_in_src = (ffn1_w_gate, ffn1_w_up, ffn1_w_down, w_in)
    mix_src = (w_o, ffn2_w_gate, ffn2_w_up, ffn2_w_down)
    ck = cache_k.reshape(depth, dec_batch, cache_len, KV_WIDTH)
    cv = cache_v.reshape(depth, dec_batch, cache_len, KV_WIDTH)

    outs = [[] for _ in range(6)]
    kvnew = []
    ffn_in_big = [_to_bf16(s, 0) for s in ffn_in_src]
    for l in range(depth):
        x1p, zp, nc, nh, *mix_big = _ffn_in_prompt(xp, tabs_p, w, ffn_in_big, l, batch, seq, FFN_TILE,
                                                   side=[(s, l) for s in mix_src])
        xp, nk, nv, *next_ffn_in_big = _mix_prompt(
            x1p, zp, w, mix_big, l, batch, seq, MIX_TILE,
            side=[(s, l + 1) for s in ffn_in_src if l + 1 < depth])

        x1s, zs = _ffn_in_sample(xs, tabs_s, w, ffn_in_big, l)
        z3 = zs.reshape(dec_seq, dec_batch, IN_WIDTH)
        kvnew.append(jnp.swapaxes(z3[:, :, O_K:O_XR], 0, 1))
        attn_s = _attn_sample(z3, ck, cv, sinks, l)
        xs, ncs, nhs = _mix_sample(x1s, zs, attn_s.reshape(n_s, ATTN_WIDTH),
                                   jnp.swapaxes(state_conv[l], 0, 1), state_h, w, mix_big, l)
        ffn_in_big = next_ffn_in_big
        kv_p = (batch, WINDOW, N_KV_HEADS, HEAD_DIM)
        for acc, val in zip(outs, (nk.reshape(kv_p), nv.reshape(kv_p), nc,
                                   nh.reshape(batch, LRU_WIDTH), jnp.swapaxes(ncs, 0, 1), nhs)):
            acc.append(val)

    nks, nvs = _shift_caches(jnp.stack(kvnew), ck, cv)
    y_prompt = xp.reshape(batch, seq, D_MODEL)
    y_sample = jnp.swapaxes(xs.reshape(dec_seq, dec_batch, D_MODEL), 0, 1)
    kp, vp, cp, hp, cs, hs = (jnp.stack(o) for o in outs)
    return (y_prompt, y_sample, kp, vp, cp, hp, nks.reshape(cache_k.shape), nvs.reshape(cache_v.shape),
            cs, hs)
```

```python
import functools

import jax
import jax.numpy as jnp
import numpy as np
from jax import lax
from jax.experimental import pallas as pl
from jax.experimental.pallas import tpu as pltpu

D_MODEL = 1024
PAST_LEN = 16384
N_HEADS = 8
HEAD_DIM = 64
N_KV_HEADS = 2
GROUP = N_HEADS // N_KV_HEADS
ATTN_WIDTH = N_HEADS * HEAD_DIM
KV_WIDTH = N_KV_HEADS * HEAD_DIM
WINDOW = 128
ROPE_THETA = 500000.0
ROT_DIM = HEAD_DIM // 4
LRU_WIDTH = D_MODEL // 2
LRU_HEADS = 8
LRU_BLOCK = LRU_WIDTH // LRU_HEADS
CONV_W = 4
LRU_C = 8.0
D_FF = 2816
IN_WIDTH = ATTN_WIDTH + 2 * KV_WIDTH + 2 * LRU_WIDTH
EPS = 1e-6

LANES = 128
SUBLANES = 8
FF_CHUNK = 256
GATE_TILE = 256
N_FF_CHUNKS = D_FF // FF_CHUNK
FFN_TILE = 512
MIX_TILE = 512
SAMPLE_GROUP = 8
SHIFT_GROUP = 16
CAST_ROWS = 256
BF16_SUBLANES = 16
V7X_VMEM_BYTES = 64 * 1024 * 1024
VMEM_LIMIT_BYTES = V7X_VMEM_BYTES * 7 // 8

O_K = ATTN_WIDTH
O_V = O_K + KV_WIDTH
O_XR = O_V + KV_WIDTH
O_YG = O_XR + LRU_WIDTH

F32 = jnp.float32
BF16 = jnp.bfloat16
LOG2_E = np.float32(np.log2(np.e))


def _rmsnorm(x, g):
    return (x * lax.rsqrt(jnp.mean(x * x, axis=-1, keepdims=True) + EPS)) * g


def _half_ffn_steps(x, g_pre, wg_ref, wu_ref, wd_ref, g_post, act_scr):
    h = _rmsnorm(x, g_pre).astype(BF16)
    for c in range(N_FF_CHUNKS):
        cols = slice(FF_CHUNK * c, FF_CHUNK * (c + 1))
        g = jnp.dot(h, wg_ref[:, cols], preferred_element_type=F32)
        u = jnp.dot(h, wu_ref[:, cols], preferred_element_type=F32)
        act_scr[:, cols] = ((g * jax.nn.sigmoid(g)) * u).astype(BF16)
        yield
    ys = []
    for c in range(D_MODEL // FF_CHUNK):
        ys.append(jnp.dot(act_scr[...], wd_ref[:, FF_CHUNK * c:FF_CHUNK * (c + 1)],
                          preferred_element_type=F32))
        yield
    return x + _rmsnorm(jnp.concatenate(ys, axis=1), 0.5 * g_post)


def _advance(gen, done):
    if gen in done:
        return
    try:
        next(gen)
    except StopIteration as stop:
        done[gen] = stop.value


def _interleave(first, second):
    done = {}
    while len(done) < 2:
        _advance(first, done)
        _advance(second, done)
    return done[first], done[second]


def _run(gen):
    done = {}
    while gen not in done:
        _advance(gen, done)
    return done[gen]


def _half_ffn(*args):
    return _run(_half_ffn_steps(*args))


def _gelu_tanh_times(x, y):
    c = np.float32(np.sqrt(2.0 / np.pi))
    t = jnp.tanh(x * ((x * x) * (np.float32(0.044715) * c) + c))
    w = (0.5 * x) * y
    return t * w + w


def _softplus(x):
    return jnp.maximum(x, 0.0) + jnp.log1p(jnp.exp(-jnp.abs(x)))


def _lru_gates(xc, wgate_ref):
    x = xc.astype(BF16)
    parts = []
    for gate in range(2):
        for t in range(LRU_WIDTH // GATE_TILE):
            rows = slice(GATE_TILE * t, GATE_TILE * (t + 1))
            cols = slice(LRU_WIDTH * gate + GATE_TILE * t, LRU_WIDTH * gate + GATE_TILE * (t + 1))
            parts.append(jnp.dot(x[:, rows], wgate_ref[rows, cols], preferred_element_type=F32))
    return jnp.concatenate(parts, axis=1)


def _lru_coeffs(xc, gate_a, gate_x, ba, bx, lam):
    r = jax.nn.sigmoid(gate_a + ba)
    ig = jax.nn.sigmoid(gate_x + bx)
    a = jnp.exp2(r * ((-LRU_C * LOG2_E) * _softplus(-lam)))
    v = 1.0 - a * a
    u = jnp.where(v > 0.0, v * lax.rsqrt(v), 0.0) * (ig * xc)
    return a, u


def _scan_rows(a, u):
    n = a.shape[0]
    row = lax.broadcasted_iota(jnp.int32, a.shape, 0)
    d = 1
    while d < n:
        keep = row >= d
        u = jnp.where(keep, a * pltpu.roll(u, d, 0) + u, u)
        if 2 * d < n:
            a = jnp.where(keep, a * pltpu.roll(a, d, 0), a)
        d *= 2
    return u


def _phase_blocks(slab_ref, j):
    groups = slab_ref.shape[1] // SUBLANES
    return [slab_ref[j, pl.ds(k, groups, stride=SUBLANES), :] for k in range(SUBLANES)]


def _prev_group(block, carry_row):
    first = lax.broadcasted_iota(jnp.int32, block.shape, 0) == 0
    return jnp.where(first, carry_row, pltpu.roll(block, 1, 0))


def _conv_phases(x_phases, carry_rows, w, b):
    wrapped = {SUBLANES - m: _prev_group(x_phases[SUBLANES - m], carry_rows[CONV_W - 1 - m])
               for m in range(1, CONV_W)}
    out = []
    for k in range(SUBLANES):
        acc = b
        for tap in range(CONV_W):
            src = k - (CONV_W - 1 - tap)
            x = x_phases[src] if src >= 0 else wrapped[src + SUBLANES]
            acc = acc + x * w[tap:tap + 1, :]
        out.append(acc)
    return out


def _scan_phases(a, u, h0):
    groups = a.shape[0] // SUBLANES
    blk = lambda v, k: v[groups * k:groups * (k + 1)]
    h_zero = [blk(u, 0)]
    a_prod = [blk(a, 0)]
    for k in range(1, SUBLANES):
        h_zero.append(blk(a, k) * h_zero[-1] + blk(u, k))
        a_prod.append(blk(a, k) * a_prod[-1])
    first = lax.broadcasted_iota(jnp.int32, a_prod[-1].shape, 0) == 0
    h_end = _scan_rows(a_prod[-1], jnp.where(first, a_prod[-1] * h0 + h_zero[-1], h_zero[-1]))
    h_in = _prev_group(h_end, h0)
    return [h_zero[k] + a_prod[k] * h_in for k in range(SUBLANES)]


def _expand_q(q, rows):
    lane = lax.broadcasted_iota(jnp.int32, (rows, LANES), 1)
    pieces = []
    for h in range(N_HEADS):
        kv = h // GROUP
        src = q[:, LANES * (h // 2):LANES * (h // 2 + 1)]
        if (h % 2) != kv:
            src = pltpu.roll(src, HEAD_DIM, 1)
        in_half = (lane >= HEAD_DIM) if kv == 1 else (lane < HEAD_DIM)
        pieces.append(jnp.where(in_half, src, 0.0))
    return jnp.concatenate(pieces, axis=0).astype(BF16)


def _collect_heads(o, rows):
    lane = lax.broadcasted_iota(jnp.int32, (rows, LANES), 1)
    groups = []
    for j in range(N_HEADS // 2):
        kv = (2 * j) // GROUP
        even = o[rows * (2 * j):rows * (2 * j + 1)]
        odd = o[rows * (2 * j + 1):rows * (2 * j + 2)]
        if kv == 1:
            even = pltpu.roll(even, HEAD_DIM, 1)
        else:
            odd = pltpu.roll(odd, HEAD_DIM, 1)
        groups.append(jnp.where(lane < HEAD_DIM, even, odd))
    return jnp.concatenate(groups, axis=1)


def _collect_heads_t(o_t, rows):
    return jnp.concatenate(
        [o_t[HEAD_DIM * (h // GROUP):HEAD_DIM * (h // GROUP + 1), rows * h:rows * (h + 1)]
         for h in range(N_HEADS)], axis=0)


def _sink_row(sinks_ref, layer, rows):
    head = lax.broadcasted_iota(jnp.int32, (1, N_HEADS * rows), 1) // rows
    out = jnp.full((1, N_HEADS * rows), sinks_ref[layer, 0], F32)
    for h in range(1, N_HEADS):
        out = jnp.where(head == h, sinks_ref[layer, h], out)
    return out * LOG2_E


def _sink_column(sinks_ref, layer, rows):
    return jnp.concatenate([jnp.full((rows, 1), sinks_ref[layer, h], F32) for h in range(N_HEADS)],
                           axis=0) * LOG2_E


def _mix_normed(attn, hs, yg, g_attn, g_lru):
    lru = _gelu_tanh_times(yg, hs)
    return jnp.concatenate([_rmsnorm(attn, g_attn), _rmsnorm(lru, g_lru)], axis=-1).astype(BF16)


def _mix_project(x1, mixed, wo_ref, g_postmix):
    y = jnp.dot(mixed, wo_ref[...], preferred_element_type=F32)
    return x1 + _rmsnorm(y, g_postmix)


def _mix_out(x1, attn, hs, yg, g_attn, g_lru, wo_ref, g_postmix):
    return _mix_project(x1, _mix_normed(attn, hs, yg, g_attn, g_lru), wo_ref, g_postmix)


def _layer_spec(shape, layer, grid_rank):
    zeros = (0,) * len(shape)
    return pl.BlockSpec((None,) + tuple(shape), lambda *_: (layer,) + zeros,
                        pipeline_mode=pl.Buffered(1))


def _cast_kernel(w_ref, o_ref):
    o_ref[...] = w_ref[...].astype(BF16)


def _to_bf16(w, layer):
    _, rows, cols = w.shape
    assert rows % CAST_ROWS == 0
    return pl.pallas_call(
        _cast_kernel,
        grid=(rows // CAST_ROWS,),
        in_specs=[pl.BlockSpec((None, CAST_ROWS, cols), lambda i: (layer, i, 0))],
        out_specs=pl.BlockSpec((CAST_ROWS, cols), lambda i: (i, 0)),
        out_shape=jax.ShapeDtypeStruct((rows, cols), BF16),
        compiler_params=pltpu.CompilerParams(dimension_semantics=("arbitrary",)),
        name="cast_bf16",
    )(w)


def _side_block_rows(rows, n_tiles):
    for k in range(BF16_SUBLANES, rows + 1, BF16_SUBLANES):
        if rows % k == 0 and rows // k <= n_tiles:
            return k
    raise ValueError(f"cannot split {rows} rows over {n_tiles} steps")


def _side_cast_specs(sources, n_tiles):
    in_specs, out_specs, out_shapes = [], [], []
    for w, layer in sources:
        _, rows, cols = w.shape
        br = _side_block_rows(rows, n_tiles)
        block = functools.partial(jnp.minimum, rows // br - 1)
        in_specs.append(pl.BlockSpec((None, br, cols),
                                     lambda g, layer=layer, block=block: (layer, block(g), 0)))
        out_specs.append(pl.BlockSpec((br, cols), lambda g, block=block: (block(g), 0)))
        out_shapes.append(jax.ShapeDtypeStruct((rows, cols), BF16))
    return in_specs, out_specs, out_shapes


def _side_cast(src_refs, dst_refs):
    for src, dst in zip(src_refs, dst_refs):
        dst[...] = src[...].astype(BF16)


def _resident(shape):
    return pl.BlockSpec(shape, lambda *_: (0,) * len(shape))


def _in_proj_steps(x1, g_mix, win_ref, cos, sin_up, sin_dn, z_ref):
    h = _rmsnorm(x1, g_mix).astype(BF16)
    half = ROT_DIM // 2
    for c in range(IN_WIDTH // FF_CHUNK):
        zc = jnp.dot(h, win_ref[:, FF_CHUNK * c:FF_CHUNK * (c + 1)], preferred_element_type=F32)
        for k in range(FF_CHUNK // LANES):
            j = (FF_CHUNK // LANES) * c + k
            zj = zc[:, LANES * k:LANES * (k + 1)]
            if j < O_V // LANES:
                zj = (zj * cos + pltpu.roll(zj, LANES - half, 1) * sin_up
                      + pltpu.roll(zj, half, 1) * sin_dn)
            if j < ATTN_WIDTH // LANES:
                zj = zj * (LOG2_E * HEAD_DIM ** -0.5)
            z_ref[:, LANES * j:LANES * (j + 1)] = zj
        yield


N_LRU_WEIGHTS = 6


def _lru_weight_specs(layer):
    lspec = lambda *shape: _layer_spec(shape, layer, 1)
    return [lspec(CONV_W, LRU_WIDTH), lspec(1, LRU_WIDTH), lspec(LRU_WIDTH, 2 * LRU_WIDTH),
            lspec(1, LRU_WIDTH), lspec(1, LRU_WIDTH), lspec(1, LRU_WIDTH)]


def _lru_weights(w):
    return (w["conv_w"], w["conv_b"], w["w_gate"], w["gate_a_b"], w["gate_x_b"], w["lam"])


def _lru_prompt_steps(z_ref, lru_refs, conv_scr, h_scr, xr_scr, xc_scr, gates_scr, hs_scr):
    conv_w_ref, conv_b_ref, wgate_ref, ba_ref, bx_ref, lam_ref = lru_refs
    tile = z_ref.shape[0]
    groups = tile // SUBLANES
    n_slabs = LRU_WIDTH // LANES
    for j in range(n_slabs):
        ch = slice(LANES * j, LANES * (j + 1))
        xr_scr[j] = z_ref[:, O_XR + LANES * j:O_XR + LANES * (j + 1)]
        carry_rows = [conv_scr[m:m + 1, ch] for m in range(CONV_W - 1)]
        xc_phases = _conv_phases(_phase_blocks(xr_scr, j), carry_rows, conv_w_ref[:, ch],
                                 conv_b_ref[:, ch])
        xc_scr[:, ch] = jnp.concatenate(xc_phases, axis=0)
    last_rows = z_ref[tile - (CONV_W - 1):, O_XR:O_YG]
    conv_scr[...] = last_rows
    gates_scr[...] = _lru_gates(xc_scr[...], wgate_ref)
    yield
    h_last = []
    for j in range(n_slabs):
        ch = slice(LANES * j, LANES * (j + 1))
        gate_x_ch = slice(LRU_WIDTH + LANES * j, LRU_WIDTH + LANES * (j + 1))
        a, u = _lru_coeffs(xc_scr[:, ch], gates_scr[:, ch], gates_scr[:, gate_x_ch],
                           ba_ref[:, ch], bx_ref[:, ch], lam_ref[:, ch])
        yield
        h_phases = _scan_phases(a, u, h_scr[:, ch])
        for k in range(SUBLANES):
            hs_scr[j, pl.ds(k, groups, stride=SUBLANES), :] = h_phases[k]
        h_last.append(h_phases[SUBLANES - 1][groups - 1:groups, :])
        z_ref[:, O_XR + LANES * j:O_XR + LANES * (j + 1)] = hs_scr[j]
        yield
    h_last = jnp.concatenate(h_last, axis=1)
    h_scr[...] = h_last
    return last_rows, h_last


def _ffn_in_kernel(n_seq_tiles, n_side, *refs):
    pipelined = n_seq_tiles > 0
    (x_ref, cos_ref, sin_up_ref, sin_dn_ref, g_pre_ref, wg_ref, wu_ref, wd_ref, g_post_ref,
     g_mix_ref, win_ref) = refs[:11]
    n_lru = N_LRU_WEIGHTS if pipelined else 0
    lru_refs = refs[11:11 + n_lru]
    n_in = 11 + n_lru + n_side
    side_src = refs[11 + n_lru:n_in]
    x1_ref, z_ref = refs[n_in:n_in + 2]
    state_out = refs[n_in + 2:n_in + 2 + (2 if pipelined else 0)]
    n_out = 2 + len(state_out) + n_side
    side_dst = refs[n_in + 2 + len(state_out):n_in + n_out]
    act_scr, *stage_scr = refs[n_in + n_out:]

    def ffn_steps():
        return _half_ffn_steps(x_ref[...], g_pre_ref[...], wg_ref, wu_ref, wd_ref, g_post_ref[...],
                               act_scr)

    def in_proj_steps(x1):
        return _in_proj_steps(x1, g_mix_ref[...], win_ref, cos_ref[...], sin_up_ref[...],
                              sin_dn_ref[...], z_ref)

    if not pipelined:
        x1 = _run(ffn_steps())
        x1_ref[...] = x1
        _run(in_proj_steps(x1))
        return
    x1_scr, conv_scr, h_scr, xr_scr, xc_scr, gates_scr, hs_scr = stage_scr
    nconv_ref, nh_ref = state_out
    g_idx = pl.program_id(0)
    n_tiles = pl.num_programs(0) - 1

    @pl.when(g_idx == 0)
    def _():
        x1_scr[...] = jnp.zeros(x1_scr.shape, F32)

    @pl.when(jnp.maximum(g_idx - 1, 0) % n_seq_tiles == 0)
    def _():
        conv_scr[...] = jnp.zeros(conv_scr.shape, F32)
        h_scr[...] = jnp.zeros(h_scr.shape, F32)

    def stage_b_steps():
        yield from in_proj_steps(x1_scr[...])
        last_rows, h_last = yield from _lru_prompt_steps(z_ref, lru_refs, conv_scr, h_scr, xr_scr,
                                                         xc_scr, gates_scr, hs_scr)
        nconv_ref[0] = last_rows
        nh_ref[0] = h_last

    @pl.when(g_idx < n_tiles)
    def _():
        x1, _ = _interleave(ffn_steps(), stage_b_steps())
        x1_ref[...] = x1
        x1_scr[...] = x1
        _side_cast(side_src, side_dst)

    @pl.when(g_idx == n_tiles)
    def _():
        _run(stage_b_steps())


def _ffn_in_sample(x, tabs, w, big, layer):
    n = x.shape[0]
    lspec = lambda *shape: _layer_spec(shape, layer, 1)
    full = lambda *shape: pl.BlockSpec(shape, lambda g: (0,) * len(shape))
    wg, wu, wd, w_in = big
    return pl.pallas_call(
        functools.partial(_ffn_in_kernel, 0, 0),
        grid=(1,),
        in_specs=[
            full(n, D_MODEL), full(n, LANES), full(n, LANES), full(n, LANES),
            lspec(1, D_MODEL), _resident(wg.shape), _resident(wu.shape), _resident(wd.shape),
            lspec(1, D_MODEL), lspec(1, D_MODEL), _resident(w_in.shape),
        ],
        out_specs=[full(n, D_MODEL), full(n, IN_WIDTH)],
        out_shape=[jax.ShapeDtypeStruct((n, D_MODEL), F32), jax.ShapeDtypeStruct((n, IN_WIDTH), F32)],
        scratch_shapes=[pltpu.VMEM((n, D_FF), BF16)],
        compiler_params=pltpu.CompilerParams(
            dimension_semantics=("arbitrary",), vmem_limit_bytes=VMEM_LIMIT_BYTES),
        name="ffn_in_sample",
    )(x, *tabs, w["g_pre1"], wg, wu, wd, w["g_post1"], w["g_mix"], w_in)


def _ffn_in_prompt(x, tabs, w, big, layer, batch, seq, tile, side=()):
    n_s = seq // tile
    n = batch * n_s
    ffn_tile = lambda g: jnp.minimum(g, n - 1)
    proj_tile = lambda g: jnp.maximum(g - 1, 0)
    lspec = lambda *shape: _layer_spec(shape, layer, 1)
    tab_spec = pl.BlockSpec((tile, LANES), lambda g: (proj_tile(g) % n_s, 0))
    per_batch = lambda rows: pl.BlockSpec((1, rows, LRU_WIDTH), lambda g: (proj_tile(g) // n_s, 0, 0))
    side_in, side_out, side_shapes = _side_cast_specs(side, n)
    wg, wu, wd, w_in = big
    slabs = (LRU_WIDTH // LANES, tile, LANES)
    return pl.pallas_call(
        functools.partial(_ffn_in_kernel, n_s, len(side)),
        grid=(n + 1,),
        in_specs=[
            pl.BlockSpec((tile, D_MODEL), lambda g: (ffn_tile(g), 0)),
            tab_spec, tab_spec, tab_spec,
            lspec(1, D_MODEL), _resident(wg.shape), _resident(wu.shape), _resident(wd.shape),
            lspec(1, D_MODEL), lspec(1, D_MODEL), _resident(w_in.shape),
        ] + _lru_weight_specs(layer) + side_in,
        out_specs=[
            pl.BlockSpec((tile, D_MODEL), lambda g: (ffn_tile(g), 0)),
            pl.BlockSpec((tile, IN_WIDTH), lambda g: (proj_tile(g), 0)),
            per_batch(CONV_W - 1), per_batch(1),
        ] + side_out,
        out_shape=[
            jax.ShapeDtypeStruct((n * tile, D_MODEL), F32),
            jax.ShapeDtypeStruct((n * tile, IN_WIDTH), F32),
            jax.ShapeDtypeStruct((batch, CONV_W - 1, LRU_WIDTH), F32),
            jax.ShapeDtypeStruct((batch, 1, LRU_WIDTH), F32),
        ] + side_shapes,
        scratch_shapes=[
            pltpu.VMEM((tile, D_FF), BF16),
            pltpu.VMEM((tile, D_MODEL), F32),
            pltpu.VMEM((CONV_W - 1, LRU_WIDTH), F32),
            pltpu.VMEM((1, LRU_WIDTH), F32),
            pltpu.VMEM(slabs, F32),
            pltpu.VMEM((tile, LRU_WIDTH), F32),
            pltpu.VMEM((tile, 2 * LRU_WIDTH), F32),
            pltpu.VMEM(slabs, F32),
        ],
        compiler_params=pltpu.CompilerParams(
            dimension_semantics=("arbitrary",), vmem_limit_bytes=VMEM_LIMIT_BYTES),
        name="ffn_in",
    )(x, *tabs, w["g_pre1"], wg, wu, wd, w["g_post1"], w["g_mix"], w_in, *_lru_weights(w),
      *[src for src, _ in side])


def _mix_prompt_kernel(layer, n_seq_tiles, n_side, *refs):
    (sinks_ref, x1_ref, z_ref, g_attn_ref, g_lru_ref, wo_ref, g_postmix_ref, g_pre2_ref, wg_ref,
     wu_ref, wd_ref, g_post2_ref) = refs[:12]
    side_src = refs[12:12 + n_side]
    x3_ref, nk_ref, nv_ref = refs[12 + n_side:15 + n_side]
    side_dst = refs[15 + n_side:15 + 2 * n_side]
    k_scr, vt_scr, x2_scr, bias_scr, attn_scr, act_scr = refs[15 + 2 * n_side:]
    tile = x1_ref.shape[0]
    g_idx = pl.program_id(0)
    n_tiles = pl.num_programs(0) - 1
    s_idx = g_idx % n_seq_tiles

    @pl.when(g_idx == 0)
    def _():
        x2_scr[...] = jnp.zeros(x2_scr.shape, F32)
        kj = lax.broadcasted_iota(jnp.int32, (2 * WINDOW, N_HEADS * WINDOW), 0)
        qi = lax.broadcasted_iota(jnp.int32, (2 * WINDOW, N_HEADS * WINDOW), 1) & (WINDOW - 1)
        band = (kj > qi) & (kj <= qi + WINDOW)
        bias_scr[0] = jnp.where(band, 0.0, -jnp.inf)
        bias_scr[1] = jnp.where(band & (kj >= WINDOW), 0.0, -jnp.inf)

    @pl.when(s_idx == 0)
    def _():
        k_scr[0:WINDOW, :] = jnp.zeros((WINDOW, KV_WIDTH), BF16)
        vt_scr[:, 0:WINDOW] = jnp.zeros((KV_WIDTH, WINDOW), BF16)

    def mixer_steps():
        k_scr[WINDOW:, :] = z_ref[:, O_K:O_V].astype(BF16)
        vt_scr[:, WINDOW:] = z_ref[:, O_V:O_XR].T.astype(BF16)
        sink = _sink_row(sinks_ref, layer, WINDOW)
        for i in range(tile // WINDOW):
            q_exp = _expand_q(z_ref[WINDOW * i:WINDOW * (i + 1), 0:ATTN_WIDTH], WINDOW)
            keys = k_scr[WINDOW * i:WINDOW * (i + 2), :]
            vals_t = vt_scr[:, WINDOW * i:WINDOW * (i + 2)]
            s = lax.dot_general(keys, q_exp, (((1,), (1,)), ((), ())), preferred_element_type=F32)
            s = s + bias_scr[jnp.where(s_idx == 0, 1, 0) if i == 0 else 0]
            m = jnp.maximum(jnp.max(s, axis=0, keepdims=True), sink)
            p = jnp.exp2(s - m)
            denom = jnp.sum(p, axis=0, keepdims=True) + jnp.exp2(sink - m)
            yield
            o_t = jnp.dot(vals_t, p.astype(BF16), preferred_element_type=F32) * (1.0 / denom)
            attn_scr[WINDOW * i:WINDOW * (i + 1), :] = _collect_heads_t(o_t, WINDOW).T
            yield
        k_scr[0:WINDOW, :] = k_scr[tile:tile + WINDOW, :]
        vt_scr[:, 0:WINDOW] = vt_scr[:, tile:tile + WINDOW]
        mixed = _mix_normed(attn_scr[...], z_ref[:, O_XR:O_YG], z_ref[:, O_YG:], g_attn_ref[...],
                            g_lru_ref[...])
        yield
        return _mix_project(x1_ref[...], mixed, wo_ref, g_postmix_ref[...])

    def ffn_steps():
        return _half_ffn_steps(x2_scr[...], g_pre2_ref[...], wg_ref, wu_ref, wd_ref,
                               g_post2_ref[...], act_scr)

    @pl.when(g_idx < n_tiles)
    def _():
        x3, x2 = _interleave(ffn_steps(), mixer_steps())
        x3_ref[...] = x3
        x2_scr[...] = x2
        nk_ref[0] = z_ref[tile - WINDOW:, O_K:O_V]
        nv_ref[0] = z_ref[tile - WINDOW:, O_V:O_XR]
        _side_cast(side_src, side_dst)

    @pl.when(g_idx == n_tiles)
    def _():
        x3_ref[...] = _run(ffn_steps())


def _tail_weight_specs(layer, big):
    lspec = lambda *shape: _layer_spec(shape, layer, 1)
    w_o, wg, wu, wd = big
    return [
        lspec(1, ATTN_WIDTH), lspec(1, LRU_WIDTH), _resident(w_o.shape), lspec(1, D_MODEL),
        lspec(1, D_MODEL), _resident(wg.shape), _resident(wu.shape), _resident(wd.shape),
        lspec(1, D_MODEL),
    ]


def _tail_weights(w, big):
    w_o, wg, wu, wd = big
    return (w["g_attn"], w["g_lru"], w_o, w["g_postmix"], w["g_pre2"], wg, wu, wd, w["g_post2"])


def _mix_prompt(x1, z, w, big, layer, batch, seq, tile, side=()):
    n_s = seq // tile
    n_tiles = batch * n_s
    mix_tile = lambda g: jnp.minimum(g, n_tiles - 1)
    tok = lambda width: pl.BlockSpec((tile, width), lambda g: (mix_tile(g), 0))
    per_batch = lambda rows, width: pl.BlockSpec((1, rows, width), lambda g: (mix_tile(g) // n_s, 0, 0))
    side_in, side_out, side_shapes = _side_cast_specs(side, n_tiles)
    return pl.pallas_call(
        functools.partial(_mix_prompt_kernel, layer, n_s, len(side)),
        grid=(n_tiles + 1,),
        in_specs=[pl.BlockSpec(memory_space=pltpu.SMEM), tok(D_MODEL), tok(IN_WIDTH)]
                 + _tail_weight_specs(layer, big) + side_in,
        out_specs=[
            pl.BlockSpec((tile, D_MODEL), lambda g: (jnp.maximum(g - 1, 0), 0)),
            per_batch(WINDOW, KV_WIDTH), per_batch(WINDOW, KV_WIDTH),
        ] + side_out,
        out_shape=[
            jax.ShapeDtypeStruct((batch * seq, D_MODEL), F32),
            jax.ShapeDtypeStruct((batch, WINDOW, KV_WIDTH), F32),
            jax.ShapeDtypeStruct((batch, WINDOW, KV_WIDTH), F32),
        ] + side_shapes,
        scratch_shapes=[
            pltpu.VMEM((WINDOW + tile, KV_WIDTH), BF16),
            pltpu.VMEM((KV_WIDTH, WINDOW + tile), BF16),
            pltpu.VMEM((tile, D_MODEL), F32),
            pltpu.VMEM((2, 2 * WINDOW, N_HEADS * WINDOW), F32),
            pltpu.VMEM((tile, ATTN_WIDTH), F32),
            pltpu.VMEM((tile, D_FF), BF16),
        ],
        compiler_params=pltpu.CompilerParams(
            dimension_semantics=("arbitrary",), vmem_limit_bytes=VMEM_LIMIT_BYTES),
        name="mix_prompt",
    )(w["sinks"], x1, z, *_tail_weights(w, big), *[src for src, _ in side])


def _attn_sample_kernel(layer, sinks_ref, zq_ref, ck_ref, cv_ref, attn_ref):
    dec_seq, g = zq_ref.shape[0], zq_ref.shape[1]
    rows = dec_seq * g
    cache_len = ck_ref.shape[1]
    zq = zq_ref[...].reshape(rows, zq_ref.shape[2])
    q_exp = _expand_q(zq[:, 0:ATTN_WIDTH], rows)
    k_new = zq[:, O_K:O_V].astype(BF16)
    v_new = zq[:, O_V:O_XR].astype(BF16)
    k_cache = ck_ref[...].reshape(g * cache_len, KV_WIDTH).astype(BF16)
    v_cache = cv_ref[...].reshape(g * cache_len, KV_WIDTH).astype(BF16)
    contract_last = (((1,), (1,)), ((), ()))
    s_c = lax.dot_general(q_exp, k_cache, contract_last, preferred_element_type=F32)
    s_n = lax.dot_general(q_exp, k_new, contract_last, preferred_element_type=F32)

    def row_ids(shape):
        r = lax.broadcasted_iota(jnp.int32, shape, 0)
        return (r % rows) // g, r % g

    t_q, b_q = row_ids(s_c.shape)
    c = lax.broadcasted_iota(jnp.int32, s_c.shape, 1)
    mask_c = ((c // cache_len) == b_q) & ((c % cache_len) > t_q)
    t_q, b_q = row_ids(s_n.shape)
    c = lax.broadcasted_iota(jnp.int32, s_n.shape, 1)
    mask_n = ((c % g) == b_q) & ((c // g) <= t_q)
    s_c = jnp.where(mask_c, s_c, -jnp.inf)
    s_n = jnp.where(mask_n, s_n, -jnp.inf)

    sink = _sink_column(sinks_ref, layer, rows)
    m = jnp.maximum(jnp.maximum(jnp.max(s_c, axis=-1, keepdims=True),
                                jnp.max(s_n, axis=-1, keepdims=True)), sink)
    p_c = jnp.exp2(s_c - m)
    p_n = jnp.exp2(s_n - m)
    denom = (jnp.sum(p_c, axis=-1, keepdims=True) + jnp.sum(p_n, axis=-1, keepdims=True)
             + jnp.exp2(sink - m))
    o = (jnp.dot(p_c.astype(BF16), v_cache, preferred_element_type=F32)
         + jnp.dot(p_n.astype(BF16), v_new, preferred_element_type=F32)) / denom
    attn_ref[...] = _collect_heads(o, rows).reshape(dec_seq, g, ATTN_WIDTH)


def _attn_sample(z3, cache_k, cache_v, sinks, layer):
    dec_seq, dec_batch, _ = z3.shape
    cache_len = cache_k.shape[2]
    g = SAMPLE_GROUP
    cache_spec = pl.BlockSpec((None, g, cache_len, KV_WIDTH), lambda i: (layer, i, 0, 0))
    return pl.pallas_call(
        functools.partial(_attn_sample_kernel, layer),
        grid=(dec_batch // g,),
        in_specs=[
            pl.BlockSpec(memory_space=pltpu.SMEM),
            pl.BlockSpec((dec_seq, g, O_XR), lambda i: (0, i, 0)),
            cache_spec, cache_spec,
        ],
        out_specs=pl.BlockSpec((dec_seq, g, ATTN_WIDTH), lambda i: (0, i, 0)),
        out_shape=jax.ShapeDtypeStruct((dec_seq, dec_batch, ATTN_WIDTH), F32),
        compiler_params=pltpu.CompilerParams(
            dimension_semantics=("arbitrary",), vmem_limit_bytes=VMEM_LIMIT_BYTES),
        name="attn_sample",
    )(sinks, z3, cache_k, cache_v)


def _shift_caches_kernel(kvnew_ref, ck_ref, cv_ref, nk_ref, nv_ref):
    dec_seq = kvnew_ref.shape[1]
    cache_len = ck_ref.shape[1]
    nk_ref[:, 0:cache_len - dec_seq, :] = ck_ref[:, dec_seq:, :]
    nv_ref[:, 0:cache_len - dec_seq, :] = cv_ref[:, dec_seq:, :]
    nk_ref[:, cache_len - dec_seq:, :] = kvnew_ref[:, :, 0:KV_WIDTH]
    nv_ref[:, cache_len - dec_seq:, :] = kvnew_ref[:, :, KV_WIDTH:]


def _shift_caches(kvnew, cache_k, cache_v):
    depth, dec_batch, cache_len, _ = cache_k.shape
    dec_seq = kvnew.shape[2]
    g = SHIFT_GROUP
    spec = lambda rows, width: pl.BlockSpec((None, g, rows, width), lambda l, i: (l, i, 0, 0))
    return pl.pallas_call(
        _shift_caches_kernel,
        grid=(depth, dec_batch // g),
        in_specs=[spec(dec_seq, 2 * KV_WIDTH), spec(cache_len, KV_WIDTH), spec(cache_len, KV_WIDTH)],
        out_specs=[spec(cache_len, KV_WIDTH), spec(cache_len, KV_WIDTH)],
        out_shape=[jax.ShapeDtypeStruct(cache_k.shape, F32), jax.ShapeDtypeStruct(cache_v.shape, F32)],
        compiler_params=pltpu.CompilerParams(dimension_semantics=("arbitrary", "arbitrary")),
        name="shift_caches",
    )(kvnew, cache_k, cache_v)


def _mix_sample_kernel(x1_ref, z_ref, attn_ref, sconv_ref, sh_ref, conv_w_ref, conv_b_ref, wgate_ref,
                       ba_ref, bx_ref, lam_ref, g_attn_ref, g_lru_ref, wo_ref, g_postmix_ref,
                       g_pre2_ref, wg_ref, wu_ref, wd_ref, g_post2_ref,
                       x3_ref, nconv_ref, nh_ref, act_scr):
    nb = sh_ref.shape[0]
    dec_seq = x1_ref.shape[0] // nb
    xr = z_ref[:, O_XR:O_YG]
    xp = [sconv_ref[j] for j in range(CONV_W - 1)] + [xr[nb * t:nb * (t + 1)] for t in range(dec_seq)]
    xc_steps = []
    for t in range(dec_seq):
        acc = conv_b_ref[...] + xp[t] * conv_w_ref[0:1, :]
        for j in range(1, CONV_W):
            acc = acc + xp[t + j] * conv_w_ref[j:j + 1, :]
        xc_steps.append(acc)
    for j in range(CONV_W - 1):
        nconv_ref[j] = xp[dec_seq + j]
    xc = jnp.concatenate(xc_steps, axis=0)
    gates = _lru_gates(xc, wgate_ref)
    a, u = _lru_coeffs(xc, gates[:, :LRU_WIDTH], gates[:, LRU_WIDTH:], ba_ref[...], bx_ref[...],
                       lam_ref[...])
    h = sh_ref[...]
    hs_steps = []
    for t in range(dec_seq):
        h = a[nb * t:nb * (t + 1)] * h + u[nb * t:nb * (t + 1)]
        hs_steps.append(h)
    nh_ref[...] = h
    hs = jnp.concatenate(hs_steps, axis=0)
    x2 = _mix_out(x1_ref[...], attn_ref[...], hs, z_ref[:, O_YG:], g_attn_ref[...], g_lru_ref[...],
                  wo_ref, g_postmix_ref[...])
    x3_ref[...] = _half_ffn(x2, g_pre2_ref[...], wg_ref, wu_ref, wd_ref, g_post2_ref[...], act_scr)


def _mix_sample(x1, z, attn, sconv, state_h, w, big, layer):
    n = x1.shape[0]
    nb = state_h.shape[1]
    full = lambda *shape: pl.BlockSpec(shape, lambda i: (0,) * len(shape))
    return pl.pallas_call(
        _mix_sample_kernel,
        grid=(1,),
        in_specs=[
            full(n, D_MODEL), full(n, IN_WIDTH), full(n, ATTN_WIDTH),
            full(CONV_W - 1, nb, LRU_WIDTH), _layer_spec((nb, LRU_WIDTH), layer, 1),
        ] + _lru_weight_specs(layer) + _tail_weight_specs(layer, big),
        out_specs=[full(n, D_MODEL), full(CONV_W - 1, nb, LRU_WIDTH), full(nb, LRU_WIDTH)],
        out_shape=[
            jax.ShapeDtypeStruct((n, D_MODEL), F32),
            jax.ShapeDtypeStruct((CONV_W - 1, nb, LRU_WIDTH), F32),
            jax.ShapeDtypeStruct((nb, LRU_WIDTH), F32),
        ],
        scratch_shapes=[pltpu.VMEM((n, D_FF), BF16)],
        compiler_params=pltpu.CompilerParams(
            dimension_semantics=("arbitrary",), vmem_limit_bytes=VMEM_LIMIT_BYTES),
        name="mix_sample",
    )(x1, z, attn, sconv, state_h, *_lru_weights(w), *_tail_weights(w, big))


def _rope_tables(pos):
    half = ROT_DIM // 2
    inv = ROPE_THETA ** (-(jnp.arange(half, dtype=F32) * 2.0) / ROT_DIM)
    ang = pos.astype(F32)[:, None] * inv[None, :]
    cos, sin = jnp.cos(ang), jnp.sin(ang)
    n = pos.shape[0]
    rest = jnp.zeros((n, HEAD_DIM - ROT_DIM), F32)
    zero = jnp.zeros((n, half), F32)
    per_head = lambda parts: jnp.tile(jnp.concatenate(parts, axis=-1), (1, LANES // HEAD_DIM))
    return (per_head([cos, cos, rest + 1.0]), per_head([-sin, zero, rest]), per_head([zero, sin, rest]))


def _block_diag(w):
    eye = jnp.eye(LRU_HEADS, dtype=w.dtype)
    return jnp.einsum("lhij,hg->lhigj", w, eye).reshape(w.shape[0], LRU_WIDTH, LRU_WIDTH)


def kernel(x_prompt, x_sample, cache_k, cache_v, state_conv, state_h, norm_pre_ffn1, ffn1_w_gate, ffn1_w_up, ffn1_w_down, norm_post_ffn1, norm_pre_mix, w_in, sinks, conv_w, conv_b, gate_a_w, gate_a_b, gate_x_w, gate_x_b, lru_lambda, attn_out_norm, lru_out_norm, w_o, norm_post_mix, norm_pre_ffn2, ffn2_w_gate, ffn2_w_up, ffn2_w_down, norm_post_ffn2):
    batch, seq, _ = x_prompt.shape
    dec_batch, dec_seq, _ = x_sample.shape
    depth = w_in.shape[0]
    cache_len = cache_k.shape[2]
    n_s = dec_seq * dec_batch
    assert seq % FFN_TILE == 0 and n_s % FFN_TILE == 0
    assert seq % MIX_TILE == 0 and MIX_TILE % WINDOW == 0
    assert dec_batch % SAMPLE_GROUP == 0 and dec_batch % SHIFT_GROUP == 0

    tabs_p = _rope_tables(jnp.arange(seq))
    tabs_s = _rope_tables(jnp.repeat(PAST_LEN + jnp.arange(dec_seq), dec_batch))
    xp = x_prompt.reshape(batch * seq, D_MODEL)
    xs = jnp.swapaxes(x_sample, 0, 1).reshape(n_s, D_MODEL)
    rows = lambda v: v.reshape(depth, 1, -1)

    w = dict(
        g_pre1=rows(norm_pre_ffn1), g_post1=rows(norm_post_ffn1), g_mix=rows(norm_pre_mix),
        sinks=sinks, conv_w=conv_w, conv_b=rows(conv_b),
        w_gate=jnp.concatenate([_block_diag(gate_a_w), _block_diag(gate_x_w)], axis=2).astype(BF16),
        gate_a_b=rows(gate_a_b), gate_x_b=rows(gate_x_b), lam=rows(lru_lambda),
        g_attn=rows(attn_out_norm), g_lru=rows(lru_out_norm),
        g_postmix=rows(norm_post_mix), g_pre2=rows(norm_pre_ffn2), g_post2=rows(norm_post_ffn2),
    )
    ffn_in_src = (ffn1_w_gate, ffn1_w_up, ffn1_w_down, w_in)
    mix_src = (w_o, ffn2_w_gate, ffn2_w_up, ffn2_w_down)
    ck = cache_k.reshape(depth, dec_batch, cache_len, KV_WIDTH)
    cv = cache_v.reshape(depth, dec_batch, cache_len, KV_WIDTH)

    outs = [[] for _ in range(6)]
    kvnew = []
    ffn_in_big = [_to_bf16(s, 0) for s in ffn_in_src]
    for l in range(depth):
        x1p, zp, nc, nh, *mix_big = _ffn_in_prompt(xp, tabs_p, w, ffn_in_big, l, batch, seq, FFN_TILE,
                                                   side=[(s, l) for s in mix_src])
        xp, nk, nv, *next_ffn_in_big = _mix_prompt(
            x1p, zp, w, mix_big, l, batch, seq, MIX_TILE,
            side=[(s, l + 1) for s in ffn_in_src if l + 1 < depth])

        x1s, zs = _ffn_in_sample(xs, tabs_s, w, ffn_in_big, l)
        z3 = zs.reshape(dec_seq, dec_batch, IN_WIDTH)
        kvnew.append(jnp.swapaxes(z3[:, :, O_K:O_XR], 0, 1))
        attn_s = _attn_sample(z3, ck, cv, sinks, l)
        xs, ncs, nhs = _mix_sample(x1s, zs, attn_s.reshape(n_s, ATTN_WIDTH),
                                   jnp.swapaxes(state_conv[l], 0, 1), state_h, w, mix_big, l)
        ffn_in_big = next_ffn_in_big
        kv_p = (batch, WINDOW, N_KV_HEADS, HEAD_DIM)
        for acc, val in zip(outs, (nk.reshape(kv_p), nv.reshape(kv_p), nc,
                                   nh.reshape(batch, LRU_WIDTH), jnp.swapaxes(ncs, 0, 1), nhs)):
            acc.append(val)

    nks, nvs = _shift_caches(jnp.stack(kvnew), ck, cv)
    y_prompt = xp.reshape(batch, seq, D_MODEL)
    y_sample = jnp.swapaxes(xs.reshape(dec_seq, dec_batch, D_MODEL), 0, 1)
    kp, vp, cp, hp, cs, hs = (jnp.stack(o) for o in outs)
    return (y_prompt, y_sample, kp, vp, cp, hp, nks.reshape(cache_k.shape), nvs.reshape(cache_v.shape),
            cs, hs)
```

```python
import functools

import jax
import jax.numpy as jnp
import numpy as np
from jax import lax
from jax.experimental import pallas as pl
from jax.experimental.pallas import tpu as pltpu

D_MODEL = 1024
PAST_LEN = 16384
N_HEADS = 8
HEAD_DIM = 64
N_KV_HEADS = 2
GROUP = N_HEADS // N_KV_HEADS
ATTN_WIDTH = N_HEADS * HEAD_DIM
KV_WIDTH = N_KV_HEADS * HEAD_DIM
WINDOW = 128
ROPE_THETA = 500000.0
ROT_DIM = HEAD_DIM // 4
LRU_WIDTH = D_MODEL // 2
LRU_HEADS = 8
LRU_BLOCK = LRU_WIDTH // LRU_HEADS
CONV_W = 4
LRU_C = 8.0
D_FF = 2816
IN_WIDTH = ATTN_WIDTH + 2 * KV_WIDTH + 2 * LRU_WIDTH
EPS = 1e-6

LANES = 128
SUBLANES = 8
FF_CHUNK = 256
N_FF_CHUNKS = D_FF // FF_CHUNK
FFN_TILE = 512
MIX_TILE = 512
SAMPLE_GROUP = 8
SHIFT_GROUP = 16
CAST_ROWS = 256
BF16_SUBLANES = 16
V7X_VMEM_BYTES = 64 * 1024 * 1024
VMEM_LIMIT_BYTES = V7X_VMEM_BYTES * 7 // 8

O_K = ATTN_WIDTH
O_V = O_K + KV_WIDTH
O_XR = O_V + KV_WIDTH
O_YG = O_XR + LRU_WIDTH

F32 = jnp.float32
BF16 = jnp.bfloat16
LOG2_E = np.float32(np.log2(np.e))


def _rmsnorm(x, g):
    return (x * lax.rsqrt(jnp.mean(x * x, axis=-1, keepdims=True) + EPS)) * g


def _half_ffn_steps(x, g_pre, wg_ref, wu_ref, wd_ref, g_post, act_scr):
    h = _rmsnorm(x, g_pre).astype(BF16)
    for c in range(N_FF_CHUNKS):
        cols = slice(FF_CHUNK * c, FF_CHUNK * (c + 1))
        g = jnp.dot(h, wg_ref[:, cols], preferred_element_type=F32)
        u = jnp.dot(h, wu_ref[:, cols], preferred_element_type=F32)
        act_scr[:, cols] = ((g * jax.nn.sigmoid(g)) * u).astype(BF16)
        yield
    ys = []
    for c in range(D_MODEL // FF_CHUNK):
        ys.append(jnp.dot(act_scr[...], wd_ref[:, FF_CHUNK * c:FF_CHUNK * (c + 1)],
                          preferred_element_type=F32))
        yield
    return x + _rmsnorm(jnp.concatenate(ys, axis=1), 0.5 * g_post)


def _advance(gen, done):
    if gen in done:
        return
    try:
        next(gen)
    except StopIteration as stop:
        done[gen] = stop.value


def _interleave(first, second):
    done = {}
    while len(done) < 2:
        _advance(first, done)
        _advance(second, done)
    return done[first], done[second]


def _run(gen):
    done = {}
    while gen not in done:
        _advance(gen, done)
    return done[gen]


def _half_ffn(*args):
    return _run(_half_ffn_steps(*args))


def _gelu_tanh_times(x, y):
    c = np.float32(np.sqrt(2.0 / np.pi))
    t = jnp.tanh(x * ((x * x) * (np.float32(0.044715) * c) + c))
    w = (0.5 * x) * y
    return t * w + w


def _softplus(x):
    return jnp.maximum(x, 0.0) + jnp.log1p(jnp.exp(-jnp.abs(x)))


def _lru_gates(xc, wgate_ref):
    return jnp.dot(xc.astype(BF16), wgate_ref[...], preferred_element_type=F32)


def _lru_coeffs(xc, gate_a, gate_x, ba, bx, lam):
    r = jax.nn.sigmoid(gate_a + ba)
    ig = jax.nn.sigmoid(gate_x + bx)
    a = jnp.exp2(r * ((-LRU_C * LOG2_E) * _softplus(-lam)))
    v = 1.0 - a * a
    u = jnp.where(v > 0.0, v * lax.rsqrt(v), 0.0) * (ig * xc)
    return a, u


def _scan_rows(a, u):
    n = a.shape[0]
    row = lax.broadcasted_iota(jnp.int32, a.shape, 0)
    d = 1
    while d < n:
        keep = row >= d
        u = jnp.where(keep, a * pltpu.roll(u, d, 0) + u, u)
        if 2 * d < n:
            a = jnp.where(keep, a * pltpu.roll(a, d, 0), a)
        d *= 2
    return u


def _phase_blocks(slab_ref, j):
    groups = slab_ref.shape[1] // SUBLANES
    return [slab_ref[j, pl.ds(k, groups, stride=SUBLANES), :] for k in range(SUBLANES)]


def _prev_group(block, carry_row):
    first = lax.broadcasted_iota(jnp.int32, block.shape, 0) == 0
    return jnp.where(first, carry_row, pltpu.roll(block, 1, 0))


def _conv_phases(x_phases, carry_rows, w, b):
    wrapped = {SUBLANES - m: _prev_group(x_phases[SUBLANES - m], carry_rows[CONV_W - 1 - m])
               for m in range(1, CONV_W)}
    out = []
    for k in range(SUBLANES):
        acc = b
        for tap in range(CONV_W):
            src = k - (CONV_W - 1 - tap)
            x = x_phases[src] if src >= 0 else wrapped[src + SUBLANES]
            acc = acc + x * w[tap:tap + 1, :]
        out.append(acc)
    return out


def _scan_phases(a, u, h0):
    groups = a.shape[0] // SUBLANES
    blk = lambda v, k: v[groups * k:groups * (k + 1)]
    h_zero = [blk(u, 0)]
    a_prod = [blk(a, 0)]
    for k in range(1, SUBLANES):
        h_zero.append(blk(a, k) * h_zero[-1] + blk(u, k))
        a_prod.append(blk(a, k) * a_prod[-1])
    first = lax.broadcasted_iota(jnp.int32, a_prod[-1].shape, 0) == 0
    h_end = _scan_rows(a_prod[-1], jnp.where(first, a_prod[-1] * h0 + h_zero[-1], h_zero[-1]))
    h_in = _prev_group(h_end, h0)
    return [h_zero[k] + a_prod[k] * h_in for k in range(SUBLANES)]


def _expand_q(q, rows):
    lane = lax.broadcasted_iota(jnp.int32, (rows, LANES), 1)
    pieces = []
    for h in range(N_HEADS):
        kv = h // GROUP
        src = q[:, LANES * (h // 2):LANES * (h // 2 + 1)]
        if (h % 2) != kv:
            src = pltpu.roll(src, HEAD_DIM, 1)
        in_half = (lane >= HEAD_DIM) if kv == 1 else (lane < HEAD_DIM)
        pieces.append(jnp.where(in_half, src, 0.0))
    return jnp.concatenate(pieces, axis=0).astype(BF16)


def _collect_heads(o, rows):
    lane = lax.broadcasted_iota(jnp.int32, (rows, LANES), 1)
    groups = []
    for j in range(N_HEADS // 2):
        kv = (2 * j) // GROUP
        even = o[rows * (2 * j):rows * (2 * j + 1)]
        odd = o[rows * (2 * j + 1):rows * (2 * j + 2)]
        if kv == 1:
            even = pltpu.roll(even, HEAD_DIM, 1)
        else:
            odd = pltpu.roll(odd, HEAD_DIM, 1)
        groups.append(jnp.where(lane < HEAD_DIM, even, odd))
    return jnp.concatenate(groups, axis=1)


def _collect_heads_t(o_t, rows):
    return jnp.concatenate(
        [o_t[HEAD_DIM * (h // GROUP):HEAD_DIM * (h // GROUP + 1), rows * h:rows * (h + 1)]
         for h in range(N_HEADS)], axis=0)


def _sink_row(sinks_ref, layer, rows):
    head = lax.broadcasted_iota(jnp.int32, (1, N_HEADS * rows), 1) // rows
    out = jnp.full((1, N_HEADS * rows), sinks_ref[layer, 0], F32)
    for h in range(1, N_HEADS):
        out = jnp.where(head == h, sinks_ref[layer, h], out)
    return out * LOG2_E


def _sink_column(sinks_ref, layer, rows):
    return jnp.concatenate([jnp.full((rows, 1), sinks_ref[layer, h], F32) for h in range(N_HEADS)],
                           axis=0) * LOG2_E


def _mix_normed(attn, hs, yg, g_attn, g_lru):
    lru = _gelu_tanh_times(yg, hs)
    return jnp.concatenate([_rmsnorm(attn, g_attn), _rmsnorm(lru, g_lru)], axis=-1).astype(BF16)


def _mix_project(x1, mixed, wo_ref, g_postmix):
    y = jnp.dot(mixed, wo_ref[...], preferred_element_type=F32)
    return x1 + _rmsnorm(y, g_postmix)


def _mix_out(x1, attn, hs, yg, g_attn, g_lru, wo_ref, g_postmix):
    return _mix_project(x1, _mix_normed(attn, hs, yg, g_attn, g_lru), wo_ref, g_postmix)


def _layer_spec(shape, layer, grid_rank):
    zeros = (0,) * len(shape)
    return pl.BlockSpec((None,) + tuple(shape), lambda *_: (layer,) + zeros,
                        pipeline_mode=pl.Buffered(1))


def _cast_kernel(w_ref, o_ref):
    o_ref[...] = w_ref[...].astype(BF16)


def _to_bf16(w, layer):
    _, rows, cols = w.shape
    assert rows % CAST_ROWS == 0
    return pl.pallas_call(
        _cast_kernel,
        grid=(rows // CAST_ROWS,),
        in_specs=[pl.BlockSpec((None, CAST_ROWS, cols), lambda i: (layer, i, 0))],
        out_specs=pl.BlockSpec((CAST_ROWS, cols), lambda i: (i, 0)),
        out_shape=jax.ShapeDtypeStruct((rows, cols), BF16),
        compiler_params=pltpu.CompilerParams(dimension_semantics=("arbitrary",)),
        name="cast_bf16",
    )(w)


def _side_block_rows(rows, n_tiles):
    for k in range(BF16_SUBLANES, rows + 1, BF16_SUBLANES):
        if rows % k == 0 and rows // k <= n_tiles:
            return k
    raise ValueError(f"cannot split {rows} rows over {n_tiles} steps")


def _side_cast_specs(sources, n_tiles):
    in_specs, out_specs, out_shapes = [], [], []
    for w, layer in sources:
        _, rows, cols = w.shape
        br = _side_block_rows(rows, n_tiles)
        block = functools.partial(jnp.minimum, rows // br - 1)
        in_specs.append(pl.BlockSpec((None, br, cols),
                                     lambda g, layer=layer, block=block: (layer, block(g), 0)))
        out_specs.append(pl.BlockSpec((br, cols), lambda g, block=block: (block(g), 0)))
        out_shapes.append(jax.ShapeDtypeStruct((rows, cols), BF16))
    return in_specs, out_specs, out_shapes


def _side_cast(src_refs, dst_refs):
    for src, dst in zip(src_refs, dst_refs):
        dst[...] = src[...].astype(BF16)


def _resident(shape):
    return pl.BlockSpec(shape, lambda *_: (0,) * len(shape))


def _in_proj_steps(x1, g_mix, win_ref, cos, sin_up, sin_dn, z_ref):
    h = _rmsnorm(x1, g_mix).astype(BF16)
    half = ROT_DIM // 2
    for c in range(IN_WIDTH // FF_CHUNK):
        zc = jnp.dot(h, win_ref[:, FF_CHUNK * c:FF_CHUNK * (c + 1)], preferred_element_type=F32)
        for k in range(FF_CHUNK // LANES):
            j = (FF_CHUNK // LANES) * c + k
            zj = zc[:, LANES * k:LANES * (k + 1)]
            if j < O_V // LANES:
                zj = (zj * cos + pltpu.roll(zj, LANES - half, 1) * sin_up
                      + pltpu.roll(zj, half, 1) * sin_dn)
            if j < ATTN_WIDTH // LANES:
                zj = zj * (LOG2_E * HEAD_DIM ** -0.5)
            z_ref[:, LANES * j:LANES * (j + 1)] = zj
        yield


N_LRU_WEIGHTS = 6


def _lru_weight_specs(layer):
    lspec = lambda *shape: _layer_spec(shape, layer, 1)
    return [lspec(CONV_W, LRU_WIDTH), lspec(1, LRU_WIDTH), lspec(LRU_WIDTH, 2 * LRU_WIDTH),
            lspec(1, LRU_WIDTH), lspec(1, LRU_WIDTH), lspec(1, LRU_WIDTH)]


def _lru_weights(w):
    return (w["conv_w"], w["conv_b"], w["w_gate"], w["gate_a_b"], w["gate_x_b"], w["lam"])


def _lru_prompt_steps(z_ref, lru_refs, conv_scr, h_scr, xr_scr, xc_scr, gates_scr, hs_scr):
    conv_w_ref, conv_b_ref, wgate_ref, ba_ref, bx_ref, lam_ref = lru_refs
    tile = z_ref.shape[0]
    groups = tile // SUBLANES
    n_slabs = LRU_WIDTH // LANES
    for j in range(n_slabs):
        ch = slice(LANES * j, LANES * (j + 1))
        xr_scr[j] = z_ref[:, O_XR + LANES * j:O_XR + LANES * (j + 1)]
        carry_rows = [conv_scr[m:m + 1, ch] for m in range(CONV_W - 1)]
        xc_phases = _conv_phases(_phase_blocks(xr_scr, j), carry_rows, conv_w_ref[:, ch],
                                 conv_b_ref[:, ch])
        xc_scr[:, ch] = jnp.concatenate(xc_phases, axis=0)
    last_rows = z_ref[tile - (CONV_W - 1):, O_XR:O_YG]
    conv_scr[...] = last_rows
    gates_scr[...] = _lru_gates(xc_scr[...], wgate_ref)
    yield
    h_last = []
    for j in range(n_slabs):
        ch = slice(LANES * j, LANES * (j + 1))
        gate_x_ch = slice(LRU_WIDTH + LANES * j, LRU_WIDTH + LANES * (j + 1))
        a, u = _lru_coeffs(xc_scr[:, ch], gates_scr[:, ch], gates_scr[:, gate_x_ch],
                           ba_ref[:, ch], bx_ref[:, ch], lam_ref[:, ch])
        yield
        h_phases = _scan_phases(a, u, h_scr[:, ch])
        for k in range(SUBLANES):
            hs_scr[j, pl.ds(k, groups, stride=SUBLANES), :] = h_phases[k]
        h_last.append(h_phases[SUBLANES - 1][groups - 1:groups, :])
        z_ref[:, O_XR + LANES * j:O_XR + LANES * (j + 1)] = hs_scr[j]
        yield
    h_last = jnp.concatenate(h_last, axis=1)
    h_scr[...] = h_last
    return last_rows, h_last


def _ffn_in_kernel(n_seq_tiles, n_side, *refs):
    pipelined = n_seq_tiles > 0
    (x_ref, cos_ref, sin_up_ref, sin_dn_ref, g_pre_ref, wg_ref, wu_ref, wd_ref, g_post_ref,
     g_mix_ref, win_ref) = refs[:11]
    n_lru = N_LRU_WEIGHTS if pipelined else 0
    lru_refs = refs[11:11 + n_lru]
    n_in = 11 + n_lru + n_side
    side_src = refs[11 + n_lru:n_in]
    x1_ref, z_ref = refs[n_in:n_in + 2]
    state_out = refs[n_in + 2:n_in + 2 + (2 if pipelined else 0)]
    n_out = 2 + len(state_out) + n_side
    side_dst = refs[n_in + 2 + len(state_out):n_in + n_out]
    act_scr, *stage_scr = refs[n_in + n_out:]

    def ffn_steps():
        return _half_ffn_steps(x_ref[...], g_pre_ref[...], wg_ref, wu_ref, wd_ref, g_post_ref[...],
                               act_scr)

    def in_proj_steps(x1):
        return _in_proj_steps(x1, g_mix_ref[...], win_ref, cos_ref[...], sin_up_ref[...],
                              sin_dn_ref[...], z_ref)

    if not pipelined:
        x1 = _run(ffn_steps())
        x1_ref[...] = x1
        _run(in_proj_steps(x1))
        return
    x1_scr, conv_scr, h_scr, xr_scr, xc_scr, gates_scr, hs_scr = stage_scr
    nconv_ref, nh_ref = state_out
    g_idx = pl.program_id(0)
    n_tiles = pl.num_programs(0) - 1

    @pl.when(jnp.maximum(g_idx - 1, 0) % n_seq_tiles == 0)
    def _():
        conv_scr[...] = jnp.zeros(conv_scr.shape, F32)
        h_scr[...] = jnp.zeros(h_scr.shape, F32)

    def stage_b_steps():
        yield from in_proj_steps(x1_scr[...])
        last_rows, h_last = yield from _lru_prompt_steps(z_ref, lru_refs, conv_scr, h_scr, xr_scr,
                                                         xc_scr, gates_scr, hs_scr)
        nconv_ref[0] = last_rows
        nh_ref[0] = h_last

    @pl.when(g_idx == 0)
    def _():
        x1 = _run(ffn_steps())
        x1_ref[...] = x1
        x1_scr[...] = x1
        _side_cast(side_src, side_dst)

    @pl.when((g_idx > 0) & (g_idx < n_tiles))
    def _():
        x1, _ = _interleave(ffn_steps(), stage_b_steps())
        x1_ref[...] = x1
        x1_scr[...] = x1
        _side_cast(side_src, side_dst)

    @pl.when(g_idx == n_tiles)
    def _():
        _run(stage_b_steps())


def _ffn_in_sample(x, tabs, w, big, layer):
    n = x.shape[0]
    lspec = lambda *shape: _layer_spec(shape, layer, 1)
    full = lambda *shape: pl.BlockSpec(shape, lambda g: (0,) * len(shape))
    wg, wu, wd, w_in = big
    return pl.pallas_call(
        functools.partial(_ffn_in_kernel, 0, 0),
        grid=(1,),
        in_specs=[
            full(n, D_MODEL), full(n, LANES), full(n, LANES), full(n, LANES),
            lspec(1, D_MODEL), _resident(wg.shape), _resident(wu.shape), _resident(wd.shape),
            lspec(1, D_MODEL), lspec(1, D_MODEL), _resident(w_in.shape),
        ],
        out_specs=[full(n, D_MODEL), full(n, IN_WIDTH)],
        out_shape=[jax.ShapeDtypeStruct((n, D_MODEL), F32), jax.ShapeDtypeStruct((n, IN_WIDTH), F32)],
        scratch_shapes=[pltpu.VMEM((n, D_FF), BF16)],
        compiler_params=pltpu.CompilerParams(
            dimension_semantics=("arbitrary",), vmem_limit_bytes=VMEM_LIMIT_BYTES),
        name="ffn_in_sample",
    )(x, *tabs, w["g_pre1"], wg, wu, wd, w["g_post1"], w["g_mix"], w_in)


def _ffn_in_prompt(x, tabs, w, big, layer, batch, seq, tile, side=()):
    n_s = seq // tile
    n = batch * n_s
    ffn_tile = lambda g: jnp.minimum(g, n - 1)
    proj_tile = lambda g: jnp.maximum(g - 1, 0)
    lspec = lambda *shape: _layer_spec(shape, layer, 1)
    tab_spec = pl.BlockSpec((tile, LANES), lambda g: (proj_tile(g) % n_s, 0))
    per_batch = lambda rows: pl.BlockSpec((1, rows, LRU_WIDTH), lambda g: (proj_tile(g) // n_s, 0, 0))
    side_in, side_out, side_shapes = _side_cast_specs(side, n)
    wg, wu, wd, w_in = big
    slabs = (LRU_WIDTH // LANES, tile, LANES)
    return pl.pallas_call(
        functools.partial(_ffn_in_kernel, n_s, len(side)),
        grid=(n + 1,),
        in_specs=[
            pl.BlockSpec((tile, D_MODEL), lambda g: (ffn_tile(g), 0)),
            tab_spec, tab_spec, tab_spec,
            lspec(1, D_MODEL), _resident(wg.shape), _resident(wu.shape), _resident(wd.shape),
            lspec(1, D_MODEL), lspec(1, D_MODEL), _resident(w_in.shape),
        ] + _lru_weight_specs(layer) + side_in,
        out_specs=[
            pl.BlockSpec((tile, D_MODEL), lambda g: (ffn_tile(g), 0)),
            pl.BlockSpec((tile, IN_WIDTH), lambda g: (proj_tile(g), 0)),
            per_batch(CONV_W - 1), per_batch(1),
        ] + side_out,
        out_shape=[
            jax.ShapeDtypeStruct((n * tile, D_MODEL), F32),
            jax.ShapeDtypeStruct((n * tile, IN_WIDTH), F32),
            jax.ShapeDtypeStruct((batch, CONV_W - 1, LRU_WIDTH), F32),
            jax.ShapeDtypeStruct((batch, 1, LRU_WIDTH), F32),
        ] + side_shapes,
        scratch_shapes=[
            pltpu.VMEM((tile, D_FF), BF16),
            pltpu.VMEM((tile, D_MODEL), F32),
            pltpu.VMEM((CONV_W - 1, LRU_WIDTH), F32),
            pltpu.VMEM((1, LRU_WIDTH), F32),
            pltpu.VMEM(slabs, F32),
            pltpu.VMEM((tile, LRU_WIDTH), F32),
            pltpu.VMEM((tile, 2 * LRU_WIDTH), F32),
            pltpu.VMEM(slabs, F32),
        ],
        compiler_params=pltpu.CompilerParams(
            dimension_semantics=("arbitrary",), vmem_limit_bytes=VMEM_LIMIT_BYTES),
        name="ffn_in",
    )(x, *tabs, w["g_pre1"], wg, wu, wd, w["g_post1"], w["g_mix"], w_in, *_lru_weights(w),
      *[src for src, _ in side])


def _mix_prompt_kernel(layer, n_seq_tiles, n_side, *refs):
    (sinks_ref, x1_ref, z_ref, g_attn_ref, g_lru_ref, wo_ref, g_postmix_ref, g_pre2_ref, wg_ref,
     wu_ref, wd_ref, g_post2_ref) = refs[:12]
    side_src = refs[12:12 + n_side]
    x3_ref, nk_ref, nv_ref = refs[12 + n_side:15 + n_side]
    side_dst = refs[15 + n_side:15 + 2 * n_side]
    k_scr, vt_scr, x2_scr, bias_scr, attn_scr, act_scr = refs[15 + 2 * n_side:]
    tile = x1_ref.shape[0]
    g_idx = pl.program_id(0)
    n_tiles = pl.num_programs(0) - 1
    s_idx = g_idx % n_seq_tiles

    @pl.when(g_idx == 0)
    def _():
        kj = lax.broadcasted_iota(jnp.int32, (2 * WINDOW, N_HEADS * WINDOW), 0)
        qi = lax.broadcasted_iota(jnp.int32, (2 * WINDOW, N_HEADS * WINDOW), 1) & (WINDOW - 1)
        band = (kj > qi) & (kj <= qi + WINDOW)
        bias_scr[0] = jnp.where(band, 0.0, -jnp.inf)
        bias_scr[1] = jnp.where(band & (kj >= WINDOW), 0.0, -jnp.inf)

    @pl.when(s_idx == 0)
    def _():
        k_scr[0:WINDOW, :] = jnp.zeros((WINDOW, KV_WIDTH), BF16)
        vt_scr[:, 0:WINDOW] = jnp.zeros((KV_WIDTH, WINDOW), BF16)

    def mixer_steps():
        k_scr[WINDOW:, :] = z_ref[:, O_K:O_V].astype(BF16)
        vt_scr[:, WINDOW:] = z_ref[:, O_V:O_XR].T.astype(BF16)
        sink = _sink_row(sinks_ref, layer, WINDOW)
        for i in range(tile // WINDOW):
            q_exp = _expand_q(z_ref[WINDOW * i:WINDOW * (i + 1), 0:ATTN_WIDTH], WINDOW)
            keys = k_scr[WINDOW * i:WINDOW * (i + 2), :]
            vals_t = vt_scr[:, WINDOW * i:WINDOW * (i + 2)]
            s = lax.dot_general(keys, q_exp, (((1,), (1,)), ((), ())), preferred_element_type=F32)
            s = s + bias_scr[jnp.where(s_idx == 0, 1, 0) if i == 0 else 0]
            m = jnp.maximum(jnp.max(s, axis=0, keepdims=True), sink)
            p = jnp.exp2(s - m)
            denom = jnp.sum(p, axis=0, keepdims=True) + jnp.exp2(sink - m)
            yield
            o_t = jnp.dot(vals_t, p.astype(BF16), preferred_element_type=F32) * (1.0 / denom)
            attn_scr[WINDOW * i:WINDOW * (i + 1), :] = _collect_heads_t(o_t, WINDOW).T
            yield
        k_scr[0:WINDOW, :] = k_scr[tile:tile + WINDOW, :]
        vt_scr[:, 0:WINDOW] = vt_scr[:, tile:tile + WINDOW]
        mixed = _mix_normed(attn_scr[...], z_ref[:, O_XR:O_YG], z_ref[:, O_YG:], g_attn_ref[...],
                            g_lru_ref[...])
        yield
        return _mix_project(x1_ref[...], mixed, wo_ref, g_postmix_ref[...])

    def ffn_steps():
        return _half_ffn_steps(x2_scr[...], g_pre2_ref[...], wg_ref, wu_ref, wd_ref,
                               g_post2_ref[...], act_scr)

    def finish_mixer(x2):
        x2_scr[...] = x2
        nk_ref[0] = z_ref[tile - WINDOW:, O_K:O_V]
        nv_ref[0] = z_ref[tile - WINDOW:, O_V:O_XR]
        _side_cast(side_src, side_dst)

    @pl.when(g_idx == 0)
    def _():
        finish_mixer(_run(mixer_steps()))

    @pl.when((g_idx > 0) & (g_idx < n_tiles))
    def _():
        x3, x2 = _interleave(ffn_steps(), mixer_steps())
        x3_ref[...] = x3
        finish_mixer(x2)

    @pl.when(g_idx == n_tiles)
    def _():
        x3_ref[...] = _run(ffn_steps())


def _tail_weight_specs(layer, big):
    lspec = lambda *shape: _layer_spec(shape, layer, 1)
    w_o, wg, wu, wd = big
    return [
        lspec(1, ATTN_WIDTH), lspec(1, LRU_WIDTH), _resident(w_o.shape), lspec(1, D_MODEL),
        lspec(1, D_MODEL), _resident(wg.shape), _resident(wu.shape), _resident(wd.shape),
        lspec(1, D_MODEL),
    ]


def _tail_weights(w, big):
    w_o, wg, wu, wd = big
    return (w["g_attn"], w["g_lru"], w_o, w["g_postmix"], w["g_pre2"], wg, wu, wd, w["g_post2"])


def _mix_prompt(x1, z, w, big, layer, batch, seq, tile, side=()):
    n_s = seq // tile
    n_tiles = batch * n_s
    mix_tile = lambda g: jnp.minimum(g, n_tiles - 1)
    tok = lambda width: pl.BlockSpec((tile, width), lambda g: (mix_tile(g), 0))
    per_batch = lambda rows, width: pl.BlockSpec((1, rows, width), lambda g: (mix_tile(g) // n_s, 0, 0))
    side_in, side_out, side_shapes = _side_cast_specs(side, n_tiles)
    return pl.pallas_call(
        functools.partial(_mix_prompt_kernel, layer, n_s, len(side)),
        grid=(n_tiles + 1,),
        in_specs=[pl.BlockSpec(memory_space=pltpu.SMEM), tok(D_MODEL), tok(IN_WIDTH)]
                 + _tail_weight_specs(layer, big) + side_in,
        out_specs=[
            pl.BlockSpec((tile, D_MODEL), lambda g: (jnp.maximum(g - 1, 0), 0)),
            per_batch(WINDOW, KV_WIDTH), per_batch(WINDOW, KV_WIDTH),
        ] + side_out,
        out_shape=[
            jax.ShapeDtypeStruct((batch * seq, D_MODEL), F32),
            jax.ShapeDtypeStruct((batch, WINDOW, KV_WIDTH), F32),
            jax.ShapeDtypeStruct((batch, WINDOW, KV_WIDTH), F32),
        ] + side_shapes,
        scratch_shapes=[
            pltpu.VMEM((WINDOW + tile, KV_WIDTH), BF16),
            pltpu.VMEM((KV_WIDTH, WINDOW + tile), BF16),
            pltpu.VMEM((tile, D_MODEL), F32),
            pltpu.VMEM((2, 2 * WINDOW, N_HEADS * WINDOW), F32),
            pltpu.VMEM((tile, ATTN_WIDTH), F32),
            pltpu.VMEM((tile, D_FF), BF16),
        ],
        compiler_params=pltpu.CompilerParams(
            dimension_semantics=("arbitrary",), vmem_limit_bytes=VMEM_LIMIT_BYTES),
        name="mix_prompt",
    )(w["sinks"], x1, z, *_tail_weights(w, big), *[src for src, _ in side])


def _attn_sample_kernel(layer, sinks_ref, zq_ref, ck_ref, cv_ref, attn_ref):
    dec_seq, g = zq_ref.shape[0], zq_ref.shape[1]
    rows = dec_seq * g
    cache_len = ck_ref.shape[1]
    zq = zq_ref[...].reshape(rows, zq_ref.shape[2])
    q_exp = _expand_q(zq[:, 0:ATTN_WIDTH], rows)
    k_new = zq[:, O_K:O_V].astype(BF16)
    v_new = zq[:, O_V:O_XR].astype(BF16)
    k_cache = ck_ref[...].reshape(g * cache_len, KV_WIDTH).astype(BF16)
    v_cache = cv_ref[...].reshape(g * cache_len, KV_WIDTH).astype(BF16)
    contract_last = (((1,), (1,)), ((), ()))
    s_c = lax.dot_general(q_exp, k_cache, contract_last, preferred_element_type=F32)
    s_n = lax.dot_general(q_exp, k_new, contract_last, preferred_element_type=F32)

    def row_ids(shape):
        r = lax.broadcasted_iota(jnp.int32, shape, 0)
        return (r % rows) // g, r % g

    t_q, b_q = row_ids(s_c.shape)
    c = lax.broadcasted_iota(jnp.int32, s_c.shape, 1)
    mask_c = ((c // cache_len) == b_q) & ((c % cache_len) > t_q)
    t_q, b_q = row_ids(s_n.shape)
    c = lax.broadcasted_iota(jnp.int32, s_n.shape, 1)
    mask_n = ((c % g) == b_q) & ((c // g) <= t_q)
    s_c = jnp.where(mask_c, s_c, -jnp.inf)
    s_n = jnp.where(mask_n, s_n, -jnp.inf)

    sink = _sink_column(sinks_ref, layer, rows)
    m = jnp.maximum(jnp.maximum(jnp.max(s_c, axis=-1, keepdims=True),
                                jnp.max(s_n, axis=-1, keepdims=True)), sink)
    p_c = jnp.exp2(s_c - m)
    p_n = jnp.exp2(s_n - m)
    denom = (jnp.sum(p_c, axis=-1, keepdims=True) + jnp.sum(p_n, axis=-1, keepdims=True)
             + jnp.exp2(sink - m))
    o = (jnp.dot(p_c.astype(BF16), v_cache, preferred_element_type=F32)
         + jnp.dot(p_n.astype(BF16), v_new, preferred_element_type=F32)) / denom
    attn_ref[...] = _collect_heads(o, rows).reshape(dec_seq, g, ATTN_WIDTH)


def _attn_sample(z3, cache_k, cache_v, sinks, layer):
    dec_seq, dec_batch, _ = z3.shape
    cache_len = cache_k.shape[2]
    g = SAMPLE_GROUP
    cache_spec = pl.BlockSpec((None, g, cache_len, KV_WIDTH), lambda i: (layer, i, 0, 0))
    return pl.pallas_call(
        functools.partial(_attn_sample_kernel, layer),
        grid=(dec_batch // g,),
        in_specs=[
            pl.BlockSpec(memory_space=pltpu.SMEM),
            pl.BlockSpec((dec_seq, g, O_XR), lambda i: (0, i, 0)),
            cache_spec, cache_spec,
        ],
        out_specs=pl.BlockSpec((dec_seq, g, ATTN_WIDTH), lambda i: (0, i, 0)),
        out_shape=jax.ShapeDtypeStruct((dec_seq, dec_batch, ATTN_WIDTH), F32),
        compiler_params=pltpu.CompilerParams(
            dimension_semantics=("arbitrary",), vmem_limit_bytes=VMEM_LIMIT_BYTES),
        name="attn_sample",
    )(sinks, z3, cache_k, cache_v)


def _shift_caches_kernel(kvnew_ref, ck_ref, cv_ref, nk_ref, nv_ref):
    dec_seq = kvnew_ref.shape[1]
    cache_len = ck_ref.shape[1]
    nk_ref[:, 0:cache_len - dec_seq, :] = ck_ref[:, dec_seq:, :]
    nv_ref[:, 0:cache_len - dec_seq, :] = cv_ref[:, dec_seq:, :]
    nk_ref[:, cache_len - dec_seq:, :] = kvnew_ref[:, :, 0:KV_WIDTH]
    nv_ref[:, cache_len - dec_seq:, :] = kvnew_ref[:, :, KV_WIDTH:]


def _shift_caches(kvnew, cache_k, cache_v):
    depth, dec_batch, cache_len, _ = cache_k.shape
    dec_seq = kvnew.shape[2]
    g = SHIFT_GROUP
    spec = lambda rows, width: pl.BlockSpec((None, g, rows, width), lambda l, i: (l, i, 0, 0))
    return pl.pallas_call(
        _shift_caches_kernel,
        grid=(depth, dec_batch // g),
        in_specs=[spec(dec_seq, 2 * KV_WIDTH), spec(cache_len, KV_WIDTH), spec(cache_len, KV_WIDTH)],
        out_specs=[spec(cache_len, KV_WIDTH), spec(cache_len, KV_WIDTH)],
        out_shape=[jax.ShapeDtypeStruct(cache_k.shape, F32), jax.ShapeDtypeStruct(cache_v.shape, F32)],
        compiler_params=pltpu.CompilerParams(dimension_semantics=("arbitrary", "arbitrary")),
        name="shift_caches",
    )(kvnew, cache_k, cache_v)


def _mix_sample_kernel(x1_ref, z_ref, attn_ref, sconv_ref, sh_ref, conv_w_ref, conv_b_ref, wgate_ref,
                       ba_ref, bx_ref, lam_ref, g_attn_ref, g_lru_ref, wo_ref, g_postmix_ref,
                       g_pre2_ref, wg_ref, wu_ref, wd_ref, g_post2_ref,
                       x3_ref, nconv_ref, nh_ref, act_scr):
    nb = sh_ref.shape[0]
    dec_seq = x1_ref.shape[0] // nb
    xr = z_ref[:, O_XR:O_YG]
    xp = [sconv_ref[j] for j in range(CONV_W - 1)] + [xr[nb * t:nb * (t + 1)] for t in range(dec_seq)]
    xc_steps = []
    for t in range(dec_seq):
        acc = conv_b_ref[...] + xp[t] * conv_w_ref[0:1, :]
        for j in range(1, CONV_W):
            acc = acc + xp[t + j] * conv_w_ref[j:j + 1, :]
        xc_steps.append(acc)
    for j in range(CONV_W - 1):
        nconv_ref[j] = xp[dec_seq + j]
    xc = jnp.concatenate(xc_steps, axis=0)
    gates = _lru_gates(xc, wgate_ref)
    a, u = _lru_coeffs(xc, gates[:, :LRU_WIDTH], gates[:, LRU_WIDTH:], ba_ref[...], bx_ref[...],
                       lam_ref[...])
    h = sh_ref[...]
    hs_steps = []
    for t in range(dec_seq):
        h = a[nb * t:nb * (t + 1)] * h + u[nb * t:nb * (t + 1)]
        hs_steps.append(h)
    nh_ref[...] = h
    hs = jnp.concatenate(hs_steps, axis=0)
    x2 = _mix_out(x1_ref[...], attn_ref[...], hs, z_ref[:, O_YG:], g_attn_ref[...], g_lru_ref[...],
                  wo_ref, g_postmix_ref[...])
    x3_ref[...] = _half_ffn(x2, g_pre2_ref[...], wg_ref, wu_ref, wd_ref, g_post2_ref[...], act_scr)


def _mix_sample(x1, z, attn, sconv, state_h, w, big, layer):
    n = x1.shape[0]
    nb = state_h.shape[1]
    full = lambda *shape: pl.BlockSpec(shape, lambda i: (0,) * len(shape))
    return pl.pallas_call(
        _mix_sample_kernel,
        grid=(1,),
        in_specs=[
            full(n, D_MODEL), full(n, IN_WIDTH), full(n, ATTN_WIDTH),
            full(CONV_W - 1, nb, LRU_WIDTH), _layer_spec((nb, LRU_WIDTH), layer, 1),
        ] + _lru_weight_specs(layer) + _tail_weight_specs(layer, big),
        out_specs=[full(n, D_MODEL), full(CONV_W - 1, nb, LRU_WIDTH), full(nb, LRU_WIDTH)],
        out_shape=[
            jax.ShapeDtypeStruct((n, D_MODEL), F32),
            jax.ShapeDtypeStruct((CONV_W - 1, nb, LRU_WIDTH), F32),
            jax.ShapeDtypeStruct((nb, LRU_WIDTH), F32),
        ],
        scratch_shapes=[pltpu.VMEM((n, D_FF), BF16)],
        compiler_params=pltpu.CompilerParams(
            dimension_semantics=("arbitrary",), vmem_limit_bytes=VMEM_LIMIT_BYTES),
        name="mix_sample",
    )(x1, z, attn, sconv, state_h, *_lru_weights(w), *_tail_weights(w, big))


def _rope_tables(pos):
    half = ROT_DIM // 2
    inv = ROPE_THETA ** (-(jnp.arange(half, dtype=F32) * 2.0) / ROT_DIM)
    ang = pos.astype(F32)[:, None] * inv[None, :]
    cos, sin = jnp.cos(ang), jnp.sin(ang)
    n = pos.shape[0]
    rest = jnp.zeros((n, HEAD_DIM - ROT_DIM), F32)
    zero = jnp.zeros((n, half), F32)
    per_head = lambda parts: jnp.tile(jnp.concatenate(parts, axis=-1), (1, LANES // HEAD_DIM))
    return (per_head([cos, cos, rest + 1.0]), per_head([-sin, zero, rest]), per_head([zero, sin, rest]))


def _block_diag(w):
    eye = jnp.eye(LRU_HEADS, dtype=w.dtype)
    return jnp.einsum("lhij,hg->lhigj", w, eye).reshape(w.shape[0], LRU_WIDTH, LRU_WIDTH)


def kernel(x_prompt, x_sample, cache_k, cache_v, state_conv, state_h, norm_pre_ffn1, ffn1_w_gate, ffn1_w_up, ffn1_w_down, norm_post_ffn1, norm_pre_mix, w_in, sinks, conv_w, conv_b, gate_a_w, gate_a_b, gate_x_w, gate_x_b, lru_lambda, attn_out_norm, lru_out_norm, w_o, norm_post_mix, norm_pre_ffn2, ffn2_w_gate, ffn2_w_up, ffn2_w_down, norm_post_ffn2):
    batch, seq, _ = x_prompt.shape
    dec_batch, dec_seq, _ = x_sample.shape
    depth = w_in.shape[0]
    cache_len = cache_k.shape[2]
    n_s = dec_seq * dec_batch
    assert seq % FFN_TILE == 0 and n_s % FFN_TILE == 0
    assert seq % MIX_TILE == 0 and MIX_TILE % WINDOW == 0
    assert dec_batch % SAMPLE_GROUP == 0 and dec_batch % SHIFT_GROUP == 0

    tabs_p = _rope_tables(jnp.arange(seq))
    tabs_s = _rope_tables(jnp.repeat(PAST_LEN + jnp.arange(dec_seq), dec_batch))
    xp = x_prompt.reshape(batch * seq, D_MODEL)
    xs = jnp.swapaxes(x_sample, 0, 1).reshape(n_s, D_MODEL)
    rows = lambda v: v.reshape(depth, 1, -1)

    w = dict(
        g_pre1=rows(norm_pre_ffn1), g_post1=rows(norm_post_ffn1), g_mix=rows(norm_pre_mix),
        sinks=sinks, conv_w=conv_w, conv_b=rows(conv_b),
        w_gate=jnp.concatenate([_block_diag(gate_a_w), _block_diag(gate_x_w)], axis=2).astype(BF16),
        gate_a_b=rows(gate_a_b), gate_x_b=rows(gate_x_b), lam=rows(lru_lambda),
        g_attn=rows(attn_out_norm), g_lru=rows(lru_out_norm),
        g_postmix=rows(norm_post_mix), g_pre2=rows(norm_pre_ffn2), g_post2=rows(norm_post_ffn2),
    )
    ffn_in_src = (ffn1_w_gate, ffn1_w_up, ffn1_w_down, w_in)
    mix_src = (w_o, ffn2_w_gate, ffn2_w_up, ffn2_w_down)
    ck = cache_k.reshape(depth, dec_batch, cache_len, KV_WIDTH)
    cv = cache_v.reshape(depth, dec_batch, cache_len, KV_WIDTH)

    outs = [[] for _ in range(6)]
    kvnew = []
    ffn_in_big = [_to_bf16(s, 0) for s in ffn_in_src]
    for l in range(depth):
        x1p, zp, nc, nh, *mix_big = _ffn_in_prompt(xp, tabs_p, w, ffn_in_big, l, batch, seq, FFN_TILE,
                                                   side=[(s, l) for s in mix_src])
        xp, nk, nv, *next_ffn_in_big = _mix_prompt(
            x1p, zp, w, mix_big, l, batch, seq, MIX_TILE,
            side=[(s, l + 1) for s in ffn_in_src if l + 1 < depth])

        x1s, zs = _ffn_in_sample(xs, tabs_s, w, ffn_in_big, l)
        z3 = zs.reshape(dec_seq, dec_batch, IN_WIDTH)
        kvnew.append(jnp.swapaxes(z3[:, :, O_K:O_XR], 0, 1))
        attn_s = _attn_sample(z3, ck, cv, sinks, l)
        xs, ncs, nhs = _mix_sample(x1s, zs, attn_s.reshape(n_s, ATTN_WIDTH),
                                   jnp.swapaxes(state_conv[l], 0, 1), state_h, w, mix_big, l)
        ffn_in_big = next_ffn_in_big
        kv_p = (batch, WINDOW, N_KV_HEADS, HEAD_DIM)
        for acc, val in zip(outs, (nk.reshape(kv_p), nv.reshape(kv_p), nc,
                                   nh.reshape(batch, LRU_WIDTH), jnp.swapaxes(ncs, 0, 1), nhs)):
            acc.append(val)

    nks, nvs = _shift_caches(jnp.stack(kvnew), ck, cv)
    y_prompt = xp.reshape(batch, seq, D_MODEL)
    y_sample = jnp.swapaxes(xs.reshape(dec_seq, dec_batch, D_MODEL), 0, 1)
    kp, vp, cp, hp, cs, hs = (jnp.stack(o) for o in outs)
    return (y_prompt, y_sample, kp, vp, cp, hp, nks.reshape(cache_k.shape), nvs.reshape(cache_v.shape),
            cs, hs)
```

```python
import functools

import jax
import jax.numpy as jnp
import numpy as np
from jax import lax
from jax.experimental import pallas as pl
from jax.experimental.pallas import tpu as pltpu

D_MODEL = 1024
PAST_LEN = 16384
N_HEADS = 8
HEAD_DIM = 64
N_KV_HEADS = 2
GROUP = N_HEADS // N_KV_HEADS
ATTN_WIDTH = N_HEADS * HEAD_DIM
KV_WIDTH = N_KV_HEADS * HEAD_DIM
WINDOW = 128
ROPE_THETA = 500000.0
ROT_DIM = HEAD_DIM // 4
LRU_WIDTH = D_MODEL // 2
LRU_HEADS = 8
LRU_BLOCK = LRU_WIDTH // LRU_HEADS
CONV_W = 4
LRU_C = 8.0
D_FF = 2816
IN_WIDTH = ATTN_WIDTH + 2 * KV_WIDTH + 2 * LRU_WIDTH
EPS = 1e-6

LANES = 128
SUBLANES = 8
FF_CHUNK = 256
N_FF_CHUNKS = D_FF // FF_CHUNK
FFN_TILE = 512
MIX_TILE = 512
SAMPLE_GROUP = 8
SHIFT_GROUP = 16
CAST_ROWS = 256
BF16_SUBLANES = 16
V7X_VMEM_BYTES = 64 * 1024 * 1024
VMEM_LIMIT_BYTES = V7X_VMEM_BYTES * 7 // 8

O_K = ATTN_WIDTH
O_V = O_K + KV_WIDTH
O_XR = O_V + KV_WIDTH
O_YG = O_XR + LRU_WIDTH

F32 = jnp.float32
BF16 = jnp.bfloat16
LOG2_E = np.float32(np.log2(np.e))


def _rmsnorm(x, g):
    return (x * lax.rsqrt(jnp.mean(x * x, axis=-1, keepdims=True) + EPS)) * g


def _half_ffn_steps(x, g_pre, wg_ref, wu_ref, wd_ref, g_post, act_scr):
    h = _rmsnorm(x, g_pre).astype(BF16)
    for c in range(N_FF_CHUNKS):
        cols = slice(FF_CHUNK * c, FF_CHUNK * (c + 1))
        g = jnp.dot(h, wg_ref[:, cols], preferred_element_type=F32)
        u = jnp.dot(h, wu_ref[:, cols], preferred_element_type=F32)
        act_scr[:, cols] = ((g * jax.nn.sigmoid(g)) * u).astype(BF16)
        yield
    ys = []
    for c in range(D_MODEL // FF_CHUNK):
        ys.append(jnp.dot(act_scr[...], wd_ref[:, FF_CHUNK * c:FF_CHUNK * (c + 1)],
                          preferred_element_type=F32))
        yield
    return x + _rmsnorm(jnp.concatenate(ys, axis=1), 0.5 * g_post)


def _advance(gen, done):
    if gen in done:
        return
    try:
        next(gen)
    except StopIteration as stop:
        done[gen] = stop.value


def _interleave(first, second):
    done = {}
    while len(done) < 2:
        _advance(first, done)
        _advance(second, done)
    return done[first], done[second]


def _run(gen):
    done = {}
    while gen not in done:
        _advance(gen, done)
    return done[gen]


def _half_ffn(*args):
    return _run(_half_ffn_steps(*args))


def _gelu_tanh_times(x, y):
    c = np.float32(np.sqrt(2.0 / np.pi))
    t = jnp.tanh(x * ((x * x) * (np.float32(0.044715) * c) + c))
    w = (0.5 * x) * y
    return t * w + w


def _softplus(x):
    return jnp.maximum(x, 0.0) + jnp.log1p(jnp.exp(-jnp.abs(x)))


def _lru_gates(xc, wgate_ref):
    return jnp.dot(xc.astype(BF16), wgate_ref[...], preferred_element_type=F32)


def _lru_coeffs(xc, gate_a, gate_x, ba, bx, lam):
    r = jax.nn.sigmoid(gate_a + ba)
    ig = jax.nn.sigmoid(gate_x + bx)
    a = jnp.exp2(r * ((-LRU_C * LOG2_E) * _softplus(-lam)))
    v = 1.0 - a * a
    u = jnp.where(v > 0.0, v * lax.rsqrt(v), 0.0) * (ig * xc)
    return a, u


def _scan_rows(a, u):
    n = a.shape[0]
    row = lax.broadcasted_iota(jnp.int32, a.shape, 0)
    d = 1
    while d < n:
        keep = row >= d
        u = jnp.where(keep, a * pltpu.roll(u, d, 0) + u, u)
        if 2 * d < n:
            a = jnp.where(keep, a * pltpu.roll(a, d, 0), a)
        d *= 2
    return u


def _phase_blocks(slab_ref, j):
    groups = slab_ref.shape[1] // SUBLANES
    return [slab_ref[j, pl.ds(k, groups, stride=SUBLANES), :] for k in range(SUBLANES)]


def _prev_group(block, carry_row):
    first = lax.broadcasted_iota(jnp.int32, block.shape, 0) == 0
    return jnp.where(first, carry_row, pltpu.roll(block, 1, 0))


def _conv_phases(x_phases, carry_rows, w, b):
    wrapped = {SUBLANES - m: _prev_group(x_phases[SUBLANES - m], carry_rows[CONV_W - 1 - m])
               for m in range(1, CONV_W)}
    out = []
    for k in range(SUBLANES):
        acc = b
        for tap in range(CONV_W):
            src = k - (CONV_W - 1 - tap)
            x = x_phases[src] if src >= 0 else wrapped[src + SUBLANES]
            acc = acc + x * w[tap:tap + 1, :]
        out.append(acc)
    return out


def _scan_phases(a, u, h0):
    groups = a.shape[0] // SUBLANES
    blk = lambda v, k: v[groups * k:groups * (k + 1)]
    h_zero = [blk(u, 0)]
    a_prod = [blk(a, 0)]
    for k in range(1, SUBLANES):
        h_zero.append(blk(a, k) * h_zero[-1] + blk(u, k))
        a_prod.append(blk(a, k) * a_prod[-1])
    first = lax.broadcasted_iota(jnp.int32, a_prod[-1].shape, 0) == 0
    h_end = _scan_rows(a_prod[-1], jnp.where(first, a_prod[-1] * h0 + h_zero[-1], h_zero[-1]))
    h_in = _prev_group(h_end, h0)
    return [h_zero[k] + a_prod[k] * h_in for k in range(SUBLANES)]


def _expand_q(q, rows):
    lane = lax.broadcasted_iota(jnp.int32, (rows, LANES), 1)
    pieces = []
    for h in range(N_HEADS):
        kv = h // GROUP
        src = q[:, LANES * (h // 2):LANES * (h // 2 + 1)]
        if (h % 2) != kv:
            src = pltpu.roll(src, HEAD_DIM, 1)
        in_half = (lane >= HEAD_DIM) if kv == 1 else (lane < HEAD_DIM)
        pieces.append(jnp.where(in_half, src, 0.0))
    return jnp.concatenate(pieces, axis=0).astype(BF16)


def _collect_heads(o, rows):
    lane = lax.broadcasted_iota(jnp.int32, (rows, LANES), 1)
    groups = []
    for j in range(N_HEADS // 2):
        kv = (2 * j) // GROUP
        even = o[rows * (2 * j):rows * (2 * j + 1)]
        odd = o[rows * (2 * j + 1):rows * (2 * j + 2)]
        if kv == 1:
            even = pltpu.roll(even, HEAD_DIM, 1)
        else:
            odd = pltpu.roll(odd, HEAD_DIM, 1)
        groups.append(jnp.where(lane < HEAD_DIM, even, odd))
    return jnp.concatenate(groups, axis=1)


def _collect_heads_t(o_t, rows):
    return jnp.concatenate(
        [o_t[HEAD_DIM * (h // GROUP):HEAD_DIM * (h // GROUP + 1), rows * h:rows * (h + 1)]
         for h in range(N_HEADS)], axis=0)


def _sink_row(sinks_ref, layer, rows):
    head = lax.broadcasted_iota(jnp.int32, (1, N_HEADS * rows), 1) // rows
    out = jnp.full((1, N_HEADS * rows), sinks_ref[layer, 0], F32)
    for h in range(1, N_HEADS):
        out = jnp.where(head == h, sinks_ref[layer, h], out)
    return out * LOG2_E


def _sink_column(sinks_ref, layer, rows):
    return jnp.concatenate([jnp.full((rows, 1), sinks_ref[layer, h], F32) for h in range(N_HEADS)],
                           axis=0) * LOG2_E


def _mix_normed(attn, hs, yg, g_attn, g_lru):
    lru = _gelu_tanh_times(yg, hs)
    return jnp.concatenate([_rmsnorm(attn, g_attn), _rmsnorm(lru, g_lru)], axis=-1).astype(BF16)


def _mix_project(x1, mixed, wo_ref, g_postmix):
    y = jnp.dot(mixed, wo_ref[...], preferred_element_type=F32)
    return x1 + _rmsnorm(y, g_postmix)


def _mix_out(x1, attn, hs, yg, g_attn, g_lru, wo_ref, g_postmix):
    return _mix_project(x1, _mix_normed(attn, hs, yg, g_attn, g_lru), wo_ref, g_postmix)


def _layer_spec(shape, layer, grid_rank):
    zeros = (0,) * len(shape)
    return pl.BlockSpec((None,) + tuple(shape), lambda *_: (layer,) + zeros,
                        pipeline_mode=pl.Buffered(1))


def _cast_kernel(w_ref, o_ref):
    o_ref[...] = w_ref[...].astype(BF16)


def _to_bf16(w, layer):
    _, rows, cols = w.shape
    assert rows % CAST_ROWS == 0
    return pl.pallas_call(
        _cast_kernel,
        grid=(rows // CAST_ROWS,),
        in_specs=[pl.BlockSpec((None, CAST_ROWS, cols), lambda i: (layer, i, 0))],
        out_specs=pl.BlockSpec((CAST_ROWS, cols), lambda i: (i, 0)),
        out_shape=jax.ShapeDtypeStruct((rows, cols), BF16),
        compiler_params=pltpu.CompilerParams(dimension_semantics=("arbitrary",)),
        name="cast_bf16",
    )(w)


def _side_block_rows(rows, n_tiles):
    for k in range(BF16_SUBLANES, rows + 1, BF16_SUBLANES):
        if rows % k == 0 and rows // k <= n_tiles:
            return k
    raise ValueError(f"cannot split {rows} rows over {n_tiles} steps")


def _side_cast_specs(sources, n_tiles):
    in_specs, out_specs, out_shapes = [], [], []
    for w, layer in sources:
        _, rows, cols = w.shape
        br = _side_block_rows(rows, n_tiles)
        block = functools.partial(jnp.minimum, rows // br - 1)
        in_specs.append(pl.BlockSpec((None, br, cols),
                                     lambda g, layer=layer, block=block: (layer, block(g), 0)))
        out_specs.append(pl.BlockSpec((br, cols), lambda g, block=block: (block(g), 0)))
        out_shapes.append(jax.ShapeDtypeStruct((rows, cols), BF16))
    return in_specs, out_specs, out_shapes


def _side_cast(src_refs, dst_refs):
    for src, dst in zip(src_refs, dst_refs):
        dst[...] = src[...].astype(BF16)


def _resident(shape):
    return pl.BlockSpec(shape, lambda *_: (0,) * len(shape))


def _in_proj_steps(x1, g_mix, win_ref, cos, sin_up, sin_dn, z_ref):
    h = _rmsnorm(x1, g_mix).astype(BF16)
    half = ROT_DIM // 2
    for c in range(IN_WIDTH // FF_CHUNK):
        zc = jnp.dot(h, win_ref[:, FF_CHUNK * c:FF_CHUNK * (c + 1)], preferred_element_type=F32)
        for k in range(FF_CHUNK // LANES):
            j = (FF_CHUNK // LANES) * c + k
            zj = zc[:, LANES * k:LANES * (k + 1)]
            if j < O_V // LANES:
                zj = (zj * cos + pltpu.roll(zj, LANES - half, 1) * sin_up
                      + pltpu.roll(zj, half, 1) * sin_dn)
            if j < ATTN_WIDTH // LANES:
                zj = zj * (LOG2_E * HEAD_DIM ** -0.5)
            z_ref[:, LANES * j:LANES * (j + 1)] = zj
        yield


N_LRU_WEIGHTS = 6


def _lru_weight_specs(layer):
    lspec = lambda *shape: _layer_spec(shape, layer, 1)
    return [lspec(CONV_W, LRU_WIDTH), lspec(1, LRU_WIDTH), lspec(LRU_WIDTH, 2 * LRU_WIDTH),
            lspec(1, LRU_WIDTH), lspec(1, LRU_WIDTH), lspec(1, LRU_WIDTH)]


def _lru_weights(w):
    return (w["conv_w"], w["conv_b"], w["w_gate"], w["gate_a_b"], w["gate_x_b"], w["lam"])


def _lru_prompt_steps(z_ref, lru_refs, conv_scr, h_scr, xr_scr, xc_scr, gates_scr, hs_scr):
    conv_w_ref, conv_b_ref, wgate_ref, ba_ref, bx_ref, lam_ref = lru_refs
    tile = z_ref.shape[0]
    groups = tile // SUBLANES
    n_slabs = LRU_WIDTH // LANES
    for j in range(n_slabs):
        ch = slice(LANES * j, LANES * (j + 1))
        xr_scr[j] = z_ref[:, O_XR + LANES * j:O_XR + LANES * (j + 1)]
        carry_rows = [conv_scr[m:m + 1, ch] for m in range(CONV_W - 1)]
        xc_phases = _conv_phases(_phase_blocks(xr_scr, j), carry_rows, conv_w_ref[:, ch],
                                 conv_b_ref[:, ch])
        xc_scr[:, ch] = jnp.concatenate(xc_phases, axis=0)
    last_rows = z_ref[tile - (CONV_W - 1):, O_XR:O_YG]
    conv_scr[...] = last_rows
    gates_scr[...] = _lru_gates(xc_scr[...], wgate_ref)
    yield
    h_last = []
    for j in range(n_slabs):
        ch = slice(LANES * j, LANES * (j + 1))
        gate_x_ch = slice(LRU_WIDTH + LANES * j, LRU_WIDTH + LANES * (j + 1))
        a, u = _lru_coeffs(xc_scr[:, ch], gates_scr[:, ch], gates_scr[:, gate_x_ch],
                           ba_ref[:, ch], bx_ref[:, ch], lam_ref[:, ch])
        yield
        h_phases = _scan_phases(a, u, h_scr[:, ch])
        for k in range(SUBLANES):
            hs_scr[j, pl.ds(k, groups, stride=SUBLANES), :] = h_phases[k]
        h_last.append(h_phases[SUBLANES - 1][groups - 1:groups, :])
        z_ref[:, O_XR + LANES * j:O_XR + LANES * (j + 1)] = hs_scr[j]
        yield
    h_last = jnp.concatenate(h_last, axis=1)
    h_scr[...] = h_last
    return last_rows, h_last


def _ffn_in_kernel(n_seq_tiles, n_side, *refs):
    pipelined = n_seq_tiles > 0
    (x_ref, cos_ref, sin_up_ref, sin_dn_ref, g_pre_ref, wg_ref, wu_ref, wd_ref, g_post_ref,
     g_mix_ref, win_ref) = refs[:11]
    n_lru = N_LRU_WEIGHTS if pipelined else 0
    lru_refs = refs[11:11 + n_lru]
    n_in = 11 + n_lru + n_side
    side_src = refs[11 + n_lru:n_in]
    x1_ref, z_ref = refs[n_in:n_in + 2]
    state_out = refs[n_in + 2:n_in + 2 + (2 if pipelined else 0)]
    n_out = 2 + len(state_out) + n_side
    side_dst = refs[n_in + 2 + len(state_out):n_in + n_out]
    act_scr, *stage_scr = refs[n_in + n_out:]

    def ffn_steps():
        return _half_ffn_steps(x_ref[...], g_pre_ref[...], wg_ref, wu_ref, wd_ref, g_post_ref[...],
                               act_scr)

    def in_proj_steps(x1):
        return _in_proj_steps(x1, g_mix_ref[...], win_ref, cos_ref[...], sin_up_ref[...],
                              sin_dn_ref[...], z_ref)

    if not pipelined:
        x1 = _run(ffn_steps())
        x1_ref[...] = x1
        _run(in_proj_steps(x1))
        return
    x1_scr, conv_scr, h_scr, xr_scr, xc_scr, gates_scr, hs_scr = stage_scr
    nconv_ref, nh_ref = state_out
    g_idx = pl.program_id(0)
    n_tiles = pl.num_programs(0) - 1

    @pl.when(jnp.maximum(g_idx - 1, 0) % n_seq_tiles == 0)
    def _():
        conv_scr[...] = jnp.zeros(conv_scr.shape, F32)
        h_scr[...] = jnp.zeros(h_scr.shape, F32)

    def stage_b_steps():
        yield from in_proj_steps(x1_scr[...])
        last_rows, h_last = yield from _lru_prompt_steps(z_ref, lru_refs, conv_scr, h_scr, xr_scr,
                                                         xc_scr, gates_scr, hs_scr)
        nconv_ref[0] = last_rows
        nh_ref[0] = h_last

    @pl.when(g_idx == 0)
    def _():
        x1_scr[...] = jnp.zeros(x1_scr.shape, F32)

    @pl.when(g_idx < n_tiles)
    def _():
        x1, _ = _interleave(ffn_steps(), stage_b_steps())
        x1_ref[...] = x1
        x1_scr[...] = x1
        _side_cast(side_src, side_dst)

    @pl.when(g_idx == n_tiles)
    def _():
        _run(stage_b_steps())


def _ffn_in_sample(x, tabs, w, big, layer):
    n = x.shape[0]
    lspec = lambda *shape: _layer_spec(shape, layer, 1)
    full = lambda *shape: pl.BlockSpec(shape, lambda g: (0,) * len(shape))
    wg, wu, wd, w_in = big
    return pl.pallas_call(
        functools.partial(_ffn_in_kernel, 0, 0),
        grid=(1,),
        in_specs=[
            full(n, D_MODEL), full(n, LANES), full(n, LANES), full(n, LANES),
            lspec(1, D_MODEL), _resident(wg.shape), _resident(wu.shape), _resident(wd.shape),
            lspec(1, D_MODEL), lspec(1, D_MODEL), _resident(w_in.shape),
        ],
        out_specs=[full(n, D_MODEL), full(n, IN_WIDTH)],
        out_shape=[jax.ShapeDtypeStruct((n, D_MODEL), F32), jax.ShapeDtypeStruct((n, IN_WIDTH), F32)],
        scratch_shapes=[pltpu.VMEM((n, D_FF), BF16)],
        compiler_params=pltpu.CompilerParams(
            dimension_semantics=("arbitrary",), vmem_limit_bytes=VMEM_LIMIT_BYTES),
        name="ffn_in_sample",
    )(x, *tabs, w["g_pre1"], wg, wu, wd, w["g_post1"], w["g_mix"], w_in)


def _ffn_in_prompt(x, tabs, w, big, layer, batch, seq, tile, side=()):
    n_s = seq // tile
    n = batch * n_s
    ffn_tile = lambda g: jnp.minimum(g, n - 1)
    proj_tile = lambda g: jnp.maximum(g - 1, 0)
    lspec = lambda *shape: _layer_spec(shape, layer, 1)
    tab_spec = pl.BlockSpec((tile, LANES), lambda g: (proj_tile(g) % n_s, 0))
    per_batch = lambda rows: pl.BlockSpec((1, rows, LRU_WIDTH), lambda g: (proj_tile(g) // n_s, 0, 0))
    side_in, side_out, side_shapes = _side_cast_specs(side, n)
    wg, wu, wd, w_in = big
    slabs = (LRU_WIDTH // LANES, tile, LANES)
    return pl.pallas_call(
        functools.partial(_ffn_in_kernel, n_s, len(side)),
        grid=(n + 1,),
        in_specs=[
            pl.BlockSpec((tile, D_MODEL), lambda g: (ffn_tile(g), 0)),
            tab_spec, tab_spec, tab_spec,
            lspec(1, D_MODEL), _resident(wg.shape), _resident(wu.shape), _resident(wd.shape),
            lspec(1, D_MODEL), lspec(1, D_MODEL), _resident(w_in.shape),
        ] + _lru_weight_specs(layer) + side_in,
        out_specs=[
            pl.BlockSpec((tile, D_MODEL), lambda g: (ffn_tile(g), 0)),
            pl.BlockSpec((tile, IN_WIDTH), lambda g: (proj_tile(g), 0)),
            per_batch(CONV_W - 1), per_batch(1),
        ] + side_out,
        out_shape=[
            jax.ShapeDtypeStruct((n * tile, D_MODEL), F32),
            jax.ShapeDtypeStruct((n * tile, IN_WIDTH), F32),
            jax.ShapeDtypeStruct((batch, CONV_W - 1, LRU_WIDTH), F32),
            jax.ShapeDtypeStruct((batch, 1, LRU_WIDTH), F32),
        ] + side_shapes,
        scratch_shapes=[
            pltpu.VMEM((tile, D_FF), BF16),
            pltpu.VMEM((tile, D_MODEL), F32),
            pltpu.VMEM((CONV_W - 1, LRU_WIDTH), F32),
            pltpu.VMEM((1, LRU_WIDTH), F32),
            pltpu.VMEM(slabs, F32),
            pltpu.VMEM((tile, LRU_WIDTH), F32),
            pltpu.VMEM((tile, 2 * LRU_WIDTH), F32),
            pltpu.VMEM(slabs, F32),
        ],
        compiler_params=pltpu.CompilerParams(
            dimension_semantics=("arbitrary",), vmem_limit_bytes=VMEM_LIMIT_BYTES),
        name="ffn_in",
    )(x, *tabs, w["g_pre1"], wg, wu, wd, w["g_post1"], w["g_mix"], w_in, *_lru_weights(w),
      *[src for src, _ in side])


def _mix_prompt_kernel(layer, n_seq_tiles, n_side, *refs):
    (sinks_ref, x1_ref, z_ref, g_attn_ref, g_lru_ref, wo_ref, g_postmix_ref, g_pre2_ref, wg_ref,
     wu_ref, wd_ref, g_post2_ref) = refs[:12]
    side_src = refs[12:12 + n_side]
    x3_ref, nk_ref, nv_ref = refs[12 + n_side:15 + n_side]
    side_dst = refs[15 + n_side:15 + 2 * n_side]
    k_scr, vt_scr, x2_scr, bias_scr, attn_scr, act_scr = refs[15 + 2 * n_side:]
    tile = x1_ref.shape[0]
    g_idx = pl.program_id(0)
    n_tiles = pl.num_programs(0) - 1
    s_idx = g_idx % n_seq_tiles

    @pl.when(g_idx == 0)
    def _():
        kj = lax.broadcasted_iota(jnp.int32, (2 * WINDOW, N_HEADS * WINDOW), 0)
        qi = lax.broadcasted_iota(jnp.int32, (2 * WINDOW, N_HEADS * WINDOW), 1) & (WINDOW - 1)
        band = (kj > qi) & (kj <= qi + WINDOW)
        bias_scr[0] = jnp.where(band, 0.0, -jnp.inf)
        bias_scr[1] = jnp.where(band & (kj >= WINDOW), 0.0, -jnp.inf)

    @pl.when(s_idx == 0)
    def _():
        k_scr[0:WINDOW, :] = jnp.zeros((WINDOW, KV_WIDTH), BF16)
        vt_scr[:, 0:WINDOW] = jnp.zeros((KV_WIDTH, WINDOW), BF16)

    def mixer_steps():
        k_scr[WINDOW:, :] = z_ref[:, O_K:O_V].astype(BF16)
        vt_scr[:, WINDOW:] = z_ref[:, O_V:O_XR].T.astype(BF16)
        sink = _sink_row(sinks_ref, layer, WINDOW)
        for i in range(tile // WINDOW):
            q_exp = _expand_q(z_ref[WINDOW * i:WINDOW * (i + 1), 0:ATTN_WIDTH], WINDOW)
            keys = k_scr[WINDOW * i:WINDOW * (i + 2), :]
            vals_t = vt_scr[:, WINDOW * i:WINDOW * (i + 2)]
            s = lax.dot_general(keys, q_exp, (((1,), (1,)), ((), ())), preferred_element_type=F32)
            s = s + bias_scr[jnp.where(s_idx == 0, 1, 0) if i == 0 else 0]
            m = jnp.maximum(jnp.max(s, axis=0, keepdims=True), sink)
            p = jnp.exp2(s - m)
            denom = jnp.sum(p, axis=0, keepdims=True) + jnp.exp2(sink - m)
            yield
            o_t = jnp.dot(vals_t, p.astype(BF16), preferred_element_type=F32) * (1.0 / denom)
            attn_scr[WINDOW * i:WINDOW * (i + 1), :] = _collect_heads_t(o_t, WINDOW).T
            yield
        k_scr[0:WINDOW, :] = k_scr[tile:tile + WINDOW, :]
        vt_scr[:, 0:WINDOW] = vt_scr[:, tile:tile + WINDOW]
        mixed = _mix_normed(attn_scr[...], z_ref[:, O_XR:O_YG], z_ref[:, O_YG:], g_attn_ref[...],
                            g_lru_ref[...])
        yield
        return _mix_project(x1_ref[...], mixed, wo_ref, g_postmix_ref[...])

    def ffn_steps():
        return _half_ffn_steps(x2_scr[...], g_pre2_ref[...], wg_ref, wu_ref, wd_ref,
                               g_post2_ref[...], act_scr)

    def finish_mixer(x2):
        x2_scr[...] = x2
        nk_ref[0] = z_ref[tile - WINDOW:, O_K:O_V]
        nv_ref[0] = z_ref[tile - WINDOW:, O_V:O_XR]
        _side_cast(side_src, side_dst)

    @pl.when(g_idx == 0)
    def _():
        finish_mixer(_run(mixer_steps()))

    @pl.when((g_idx > 0) & (g_idx < n_tiles))
    def _():
        x3, x2 = _interleave(ffn_steps(), mixer_steps())
        x3_ref[...] = x3
        finish_mixer(x2)

    @pl.when(g_idx == n_tiles)
    def _():
        x3_ref[...] = _run(ffn_steps())


def _tail_weight_specs(layer, big):
    lspec = lambda *shape: _layer_spec(shape, layer, 1)
    w_o, wg, wu, wd = big
    return [
        lspec(1, ATTN_WIDTH), lspec(1, LRU_WIDTH), _resident(w_o.shape), lspec(1, D_MODEL),
        lspec(1, D_MODEL), _resident(wg.shape), _resident(wu.shape), _resident(wd.shape),
        lspec(1, D_MODEL),
    ]


def _tail_weights(w, big):
    w_o, wg, wu, wd = big
    return (w["g_attn"], w["g_lru"], w_o, w["g_postmix"], w["g_pre2"], wg, wu, wd, w["g_post2"])


def _mix_prompt(x1, z, w, big, layer, batch, seq, tile, side=()):
    n_s = seq // tile
    n_tiles = batch * n_s
    mix_tile = lambda g: jnp.minimum(g, n_tiles - 1)
    tok = lambda width: pl.BlockSpec((tile, width), lambda g: (mix_tile(g), 0))
    per_batch = lambda rows, width: pl.BlockSpec((1, rows, width), lambda g: (mix_tile(g) // n_s, 0, 0))
    side_in, side_out, side_shapes = _side_cast_specs(side, n_tiles)
    return pl.pallas_call(
        functools.partial(_mix_prompt_kernel, layer, n_s, len(side)),
        grid=(n_tiles + 1,),
        in_specs=[pl.BlockSpec(memory_space=pltpu.SMEM), tok(D_MODEL), tok(IN_WIDTH)]
                 + _tail_weight_specs(layer, big) + side_in,
        out_specs=[
            pl.BlockSpec((tile, D_MODEL), lambda g: (jnp.maximum(g - 1, 0), 0)),
            per_batch(WINDOW, KV_WIDTH), per_batch(WINDOW, KV_WIDTH),
        ] + side_out,
        out_shape=[
            jax.ShapeDtypeStruct((batch * seq, D_MODEL), F32),
            jax.ShapeDtypeStruct((batch, WINDOW, KV_WIDTH), F32),
            jax.ShapeDtypeStruct((batch, WINDOW, KV_WIDTH), F32),
        ] + side_shapes,
        scratch_shapes=[
            pltpu.VMEM((WINDOW + tile, KV_WIDTH), BF16),
            pltpu.VMEM((KV_WIDTH, WINDOW + tile), BF16),
            pltpu.VMEM((tile, D_MODEL), F32),
            pltpu.VMEM((2, 2 * WINDOW, N_HEADS * WINDOW), F32),
            pltpu.VMEM((tile, ATTN_WIDTH), F32),
            pltpu.VMEM((tile, D_FF), BF16),
        ],
        compiler_params=pltpu.CompilerParams(
            dimension_semantics=("arbitrary",), vmem_limit_bytes=VMEM_LIMIT_BYTES),
        name="mix_prompt",
    )(w["sinks"], x1, z, *_tail_weights(w, big), *[src for src, _ in side])


def _attn_sample_kernel(layer, sinks_ref, zq_ref, ck_ref, cv_ref, attn_ref):
    dec_seq, g = zq_ref.shape[0], zq_ref.shape[1]
    rows = dec_seq * g
    cache_len = ck_ref.shape[1]
    zq = zq_ref[...].reshape(rows, zq_ref.shape[2])
    q_exp = _expand_q(zq[:, 0:ATTN_WIDTH], rows)
    k_new = zq[:, O_K:O_V].astype(BF16)
    v_new = zq[:, O_V:O_XR].astype(BF16)
    k_cache = ck_ref[...].reshape(g * cache_len, KV_WIDTH).astype(BF16)
    v_cache = cv_ref[...].reshape(g * cache_len, KV_WIDTH).astype(BF16)
    contract_last = (((1,), (1,)), ((), ()))
    s_c = lax.dot_general(q_exp, k_cache, contract_last, preferred_element_type=F32)
    s_n = lax.dot_general(q_exp, k_new, contract_last, preferred_element_type=F32)

    def row_ids(shape):
        r = lax.broadcasted_iota(jnp.int32, shape, 0)
        return (r % rows) // g, r % g

    t_q, b_q = row_ids(s_c.shape)
    c = lax.broadcasted_iota(jnp.int32, s_c.shape, 1)
    mask_c = ((c // cache_len) == b_q) & ((c % cache_len) > t_q)
    t_q, b_q = row_ids(s_n.shape)
    c = lax.broadcasted_iota(jnp.int32, s_n.shape, 1)
    mask_n = ((c % g) == b_q) & ((c // g) <= t_q)
    s_c = jnp.where(mask_c, s_c, -jnp.inf)
    s_n = jnp.where(mask_n, s_n, -jnp.inf)

    sink = _sink_column(sinks_ref, layer, rows)
    m = jnp.maximum(jnp.maximum(jnp.max(s_c, axis=-1, keepdims=True),
                                jnp.max(s_n, axis=-1, keepdims=True)), sink)
    p_c = jnp.exp2(s_c - m)
    p_n = jnp.exp2(s_n - m)
    denom = (jnp.sum(p_c, axis=-1, keepdims=True) + jnp.sum(p_n, axis=-1, keepdims=True)
             + jnp.exp2(sink - m))
    o = (jnp.dot(p_c.astype(BF16), v_cache, preferred_element_type=F32)
         + jnp.dot(p_n.astype(BF16), v_new, preferred_element_type=F32)) / denom
    attn_ref[...] = _collect_heads(o, rows).reshape(dec_seq, g, ATTN_WIDTH)


def _attn_sample(z3, cache_k, cache_v, sinks, layer):
    dec_seq, dec_batch, _ = z3.shape
    cache_len = cache_k.shape[2]
    g = SAMPLE_GROUP
    cache_spec = pl.BlockSpec((None, g, cache_len, KV_WIDTH), lambda i: (layer, i, 0, 0))
    return pl.pallas_call(
        functools.partial(_attn_sample_kernel, layer),
        grid=(dec_batch // g,),
        in_specs=[
            pl.BlockSpec(memory_space=pltpu.SMEM),
            pl.BlockSpec((dec_seq, g, O_XR), lambda i: (0, i, 0)),
            cache_spec, cache_spec,
        ],
        out_specs=pl.BlockSpec((dec_seq, g, ATTN_WIDTH), lambda i: (0, i, 0)),
        out_shape=jax.ShapeDtypeStruct((dec_seq, dec_batch, ATTN_WIDTH), F32),
        compiler_params=pltpu.CompilerParams(
            dimension_semantics=("arbitrary",), vmem_limit_bytes=VMEM_LIMIT_BYTES),
        name="attn_sample",
    )(sinks, z3, cache_k, cache_v)


def _shift_caches_kernel(kvnew_ref, ck_ref, cv_ref, nk_ref, nv_ref):
    dec_seq = kvnew_ref.shape[1]
    cache_len = ck_ref.shape[1]
    nk_ref[:, 0:cache_len - dec_seq, :] = ck_ref[:, dec_seq:, :]
    nv_ref[:, 0:cache_len - dec_seq, :] = cv_ref[:, dec_seq:, :]
    nk_ref[:, cache_len - dec_seq:, :] = kvnew_ref[:, :, 0:KV_WIDTH]
    nv_ref[:, cache_len - dec_seq:, :] = kvnew_ref[:, :, KV_WIDTH:]


def _shift_caches(kvnew, cache_k, cache_v):
    depth, dec_batch, cache_len, _ = cache_k.shape
    dec_seq = kvnew.shape[2]
    g = SHIFT_GROUP
    spec = lambda rows, width: pl.BlockSpec((None, g, rows, width), lambda l, i: (l, i, 0, 0))
    return pl.pallas_call(
        _shift_caches_kernel,
        grid=(depth, dec_batch // g),
        in_specs=[spec(dec_seq, 2 * KV_WIDTH), spec(cache_len, KV_WIDTH), spec(cache_len, KV_WIDTH)],
        out_specs=[spec(cache_len, KV_WIDTH), spec(cache_len, KV_WIDTH)],
        out_shape=[jax.ShapeDtypeStruct(cache_k.shape, F32), jax.ShapeDtypeStruct(cache_v.shape, F32)],
        compiler_params=pltpu.CompilerParams(dimension_semantics=("arbitrary", "arbitrary")),
        name="shift_caches",
    )(kvnew, cache_k, cache_v)


def _mix_sample_kernel(x1_ref, z_ref, attn_ref, sconv_ref, sh_ref, conv_w_ref, conv_b_ref, wgate_ref,
                       ba_ref, bx_ref, lam_ref, g_attn_ref, g_lru_ref, wo_ref, g_postmix_ref,
                       g_pre2_ref, wg_ref, wu_ref, wd_ref, g_post2_ref,
                       x3_ref, nconv_ref, nh_ref, act_scr):
    nb = sh_ref.shape[0]
    dec_seq = x1_ref.shape[0] // nb
    xr = z_ref[:, O_XR:O_YG]
    xp = [sconv_ref[j] for j in range(CONV_W - 1)] + [xr[nb * t:nb * (t + 1)] for t in range(dec_seq)]
    xc_steps = []
    for t in range(dec_seq):
        acc = conv_b_ref[...] + xp[t] * conv_w_ref[0:1, :]
        for j in range(1, CONV_W):
            acc = acc + xp[t + j] * conv_w_ref[j:j + 1, :]
        xc_steps.append(acc)
    for j in range(CONV_W - 1):
        nconv_ref[j] = xp[dec_seq + j]
    xc = jnp.concatenate(xc_steps, axis=0)
    gates = _lru_gates(xc, wgate_ref)
    a, u = _lru_coeffs(xc, gates[:, :LRU_WIDTH], gates[:, LRU_WIDTH:], ba_ref[...], bx_ref[...],
                       lam_ref[...])
    h = sh_ref[...]
    hs_steps = []
    for t in range(dec_seq):
        h = a[nb * t:nb * (t + 1)] * h + u[nb * t:nb * (t + 1)]
        hs_steps.append(h)
    nh_ref[...] = h
    hs = jnp.concatenate(hs_steps, axis=0)
    x2 = _mix_out(x1_ref[...], attn_ref[...], hs, z_ref[:, O_YG:], g_attn_ref[...], g_lru_ref[...],
                  wo_ref, g_postmix_ref[...])
    x3_ref[...] = _half_ffn(x2, g_pre2_ref[...], wg_ref, wu_ref, wd_ref, g_post2_ref[...], act_scr)


def _mix_sample(x1, z, attn, sconv, state_h, w, big, layer):
    n = x1.shape[0]
    nb = state_h.shape[1]
    full = lambda *shape: pl.BlockSpec(shape, lambda i: (0,) * len(shape))
    return pl.pallas_call(
        _mix_sample_kernel,
        grid=(1,),
        in_specs=[
            full(n, D_MODEL), full(n, IN_WIDTH), full(n, ATTN_WIDTH),
            full(CONV_W - 1, nb, LRU_WIDTH), _layer_spec((nb, LRU_WIDTH), layer, 1),
        ] + _lru_weight_specs(layer) + _tail_weight_specs(layer, big),
        out_specs=[full(n, D_MODEL), full(CONV_W - 1, nb, LRU_WIDTH), full(nb, LRU_WIDTH)],
        out_shape=[
            jax.ShapeDtypeStruct((n, D_MODEL), F32),
            jax.ShapeDtypeStruct((CONV_W - 1, nb, LRU_WIDTH), F32),
            jax.ShapeDtypeStruct((nb, LRU_WIDTH), F32),
        ],
        scratch_shapes=[pltpu.VMEM((n, D_FF), BF16)],
        compiler_params=pltpu.CompilerParams(
            dimension_semantics=("arbitrary",), vmem_limit_bytes=VMEM_LIMIT_BYTES),
        name="mix_sample",
    )(x1, z, attn, sconv, state_h, *_lru_weights(w), *_tail_weights(w, big))


def _rope_tables(pos):
    half = ROT_DIM // 2
    inv = ROPE_THETA ** (-(jnp.arange(half, dtype=F32) * 2.0) / ROT_DIM)
    ang = pos.astype(F32)[:, None] * inv[None, :]
    cos, sin = jnp.cos(ang), jnp.sin(ang)
    n = pos.shape[0]
    rest = jnp.zeros((n, HEAD_DIM - ROT_DIM), F32)
    zero = jnp.zeros((n, half), F32)
    per_head = lambda parts: jnp.tile(jnp.concatenate(parts, axis=-1), (1, LANES // HEAD_DIM))
    return (per_head([cos, cos, rest + 1.0]), per_head([-sin, zero, rest]), per_head([zero, sin, rest]))


def _block_diag(w):
    eye = jnp.eye(LRU_HEADS, dtype=w.dtype)
    return jnp.einsum("lhij,hg->lhigj", w, eye).reshape(w.shape[0], LRU_WIDTH, LRU_WIDTH)


def kernel(x_prompt, x_sample, cache_k, cache_v, state_conv, state_h, norm_pre_ffn1, ffn1_w_gate, ffn1_w_up, ffn1_w_down, norm_post_ffn1, norm_pre_mix, w_in, sinks, conv_w, conv_b, gate_a_w, gate_a_b, gate_x_w, gate_x_b, lru_lambda, attn_out_norm, lru_out_norm, w_o, norm_post_mix, norm_pre_ffn2, ffn2_w_gate, ffn2_w_up, ffn2_w_down, norm_post_ffn2):
    batch, seq, _ = x_prompt.shape
    dec_batch, dec_seq, _ = x_sample.shape
    depth = w_in.shape[0]
    cache_len = cache_k.shape[2]
    n_s = dec_seq * dec_batch
    assert seq % FFN_TILE == 0 and n_s % FFN_TILE == 0
    assert seq % MIX_TILE == 0 and MIX_TILE % WINDOW == 0
    assert dec_batch % SAMPLE_GROUP == 0 and dec_batch % SHIFT_GROUP == 0

    tabs_p = _rope_tables(jnp.arange(seq))
    tabs_s = _rope_tables(jnp.repeat(PAST_LEN + jnp.arange(dec_seq), dec_batch))
    xp = x_prompt.reshape(batch * seq, D_MODEL)
    xs = jnp.swapaxes(x_sample, 0, 1).reshape(n_s, D_MODEL)
    rows = lambda v: v.reshape(depth, 1, -1)

    w = dict(
        g_pre1=rows(norm_pre_ffn1), g_post1=rows(norm_post_ffn1), g_mix=rows(norm_pre_mix),
        sinks=sinks, conv_w=conv_w, conv_b=rows(conv_b),
        w_gate=jnp.concatenate([_block_diag(gate_a_w), _block_diag(gate_x_w)], axis=2).astype(BF16),
        gate_a_b=rows(gate_a_b), gate_x_b=rows(gate_x_b), lam=rows(lru_lambda),
        g_attn=rows(attn_out_norm), g_lru=rows(lru_out_norm),
        g_postmix=rows(norm_post_mix), g_pre2=rows(norm_pre_ffn2), g_post2=rows(norm_post_ffn2),
    )
    ffn_in_src = (ffn1_w_gate, ffn1_w_up, ffn1_w_down, w_in)
    mix_src = (w_o, ffn2_w_gate, ffn2_w_up, ffn2_w_down)
    ck = cache_k.reshape(depth, dec_batch, cache_len, KV_WIDTH)
    cv = cache_v.reshape(depth, dec_batch, cache_len, KV_WIDTH)

    outs = [[] for _ in range(6)]
    kvnew = []
    ffn_in_big = [_to_bf16(s, 0) for s in ffn_in_src]
    for l in range(depth):
        x1p, zp, nc, nh, *mix_big = _ffn_in_prompt(xp, tabs_p, w, ffn_in_big, l, batch, seq, FFN_TILE,
                                                   side=[(s, l) for s in mix_src])
        xp, nk, nv, *next_ffn_in_big = _mix_prompt(
            x1p, zp, w, mix_big, l, batch, seq, MIX_TILE,
            side=[(s, l + 1) for s in ffn_in_src if l + 1 < depth])

        x1s, zs = _ffn_in_sample(xs, tabs_s, w, ffn_in_big, l)
        z3 = zs.reshape(dec_seq, dec_batch, IN_WIDTH)
        kvnew.append(jnp.swapaxes(z3[:, :, O_K:O_XR], 0, 1))
        attn_s = _attn_sample(z3, ck, cv, sinks, l)
        xs, ncs, nhs = _mix_sample(x1s, zs, attn_s.reshape(n_s, ATTN_WIDTH),
                                   jnp.swapaxes(state_conv[l], 0, 1), state_h, w, mix_big, l)
        ffn_in_big = next_ffn_in_big
        kv_p = (batch, WINDOW, N_KV_HEADS, HEAD_DIM)
        for acc, val in zip(outs, (nk.reshape(kv_p), nv.reshape(kv_p), nc,
                                   nh.reshape(batch, LRU_WIDTH), jnp.swapaxes(ncs, 0, 1), nhs)):
            acc.append(val)

    nks, nvs = _shift_caches(jnp.stack(kvnew), ck, cv)
    y_prompt = xp.reshape(batch, seq, D_MODEL)
    y_sample = jnp.swapaxes(xs.reshape(dec_seq, dec_batch, D_MODEL), 0, 1)
    kp, vp, cp, hp, cs, hs = (jnp.stack(o) for o in outs)
    return (y_prompt, y_sample, kp, vp, cp, hp, nks.reshape(cache_k.shape), nvs.reshape(cache_v.shape),
            cs, hs)
```

```python
import functools

import jax
import jax.numpy as jnp
import numpy as np
from jax import lax
from jax.experimental import pallas as pl
from jax.experimental.pallas import tpu as pltpu

D_MODEL = 1024
PAST_LEN = 16384
N_HEADS = 8
HEAD_DIM = 64
N_KV_HEADS = 2
GROUP = N_HEADS // N_KV_HEADS
ATTN_WIDTH = N_HEADS * HEAD_DIM
KV_WIDTH = N_KV_HEADS * HEAD_DIM
WINDOW = 128
ROPE_THETA = 500000.0
ROT_DIM = HEAD_DIM // 4
LRU_WIDTH = D_MODEL // 2
LRU_HEADS = 8
LRU_BLOCK = LRU_WIDTH // LRU_HEADS
CONV_W = 4
LRU_C = 8.0
D_FF = 2816
IN_WIDTH = ATTN_WIDTH + 2 * KV_WIDTH + 2 * LRU_WIDTH
EPS = 1e-6

LANES = 128
SUBLANES = 8
FF_CHUNK = 256
N_FF_CHUNKS = D_FF // FF_CHUNK
FFN_TILE = 512
MIX_TILE = 512
SAMPLE_GROUP = 8
SHIFT_GROUP = 16
CAST_ROWS = 256
BF16_SUBLANES = 16
V7X_VMEM_BYTES = 64 * 1024 * 1024
VMEM_LIMIT_BYTES = V7X_VMEM_BYTES * 7 // 8

O_K = ATTN_WIDTH
O_V = O_K + KV_WIDTH
O_XR = O_V + KV_WIDTH
O_YG = O_XR + LRU_WIDTH

F32 = jnp.float32
BF16 = jnp.bfloat16
LOG2_E = np.float32(np.log2(np.e))


def _rmsnorm(x, g):
    return (x * lax.rsqrt(jnp.mean(x * x, axis=-1, keepdims=True) + EPS)) * g


def _half_ffn_steps(x, g_pre, wg_ref, wu_ref, wd_ref, g_post, act_scr):
    h = _rmsnorm(x, g_pre).astype(BF16)
    for c in range(N_FF_CHUNKS):
        cols = slice(FF_CHUNK * c, FF_CHUNK * (c + 1))
        g = jnp.dot(h, wg_ref[:, cols], preferred_element_type=F32)
        u = jnp.dot(h, wu_ref[:, cols], preferred_element_type=F32)
        act_scr[:, cols] = ((g * jax.nn.sigmoid(g)) * u).astype(BF16)
        yield
    ys = []
    for c in range(D_MODEL // FF_CHUNK):
        ys.append(jnp.dot(act_scr[...], wd_ref[:, FF_CHUNK * c:FF_CHUNK * (c + 1)],
                          preferred_element_type=F32))
        yield
    return x + _rmsnorm(jnp.concatenate(ys, axis=1), 0.5 * g_post)


def _advance(gen, done):
    if gen in done:
        return
    try:
        next(gen)
    except StopIteration as stop:
        done[gen] = stop.value


def _interleave(first, second):
    done = {}
    while len(done) < 2:
        _advance(first, done)
        _advance(second, done)
    return done[first], done[second]


def _run(gen):
    done = {}
    while gen not in done:
        _advance(gen, done)
    return done[gen]


def _half_ffn(*args):
    return _run(_half_ffn_steps(*args))


def _gelu_tanh_times(x, y):
    c = np.float32(np.sqrt(2.0 / np.pi))
    t = jnp.tanh(x * ((x * x) * (np.float32(0.044715) * c) + c))
    w = (0.5 * x) * y
    return t * w + w


def _softplus(x):
    return jnp.maximum(x, 0.0) + jnp.log1p(jnp.exp(-jnp.abs(x)))


def _lru_gates(xc, wgate_ref):
    return jnp.dot(xc.astype(BF16), wgate_ref[...], preferred_element_type=F32)


def _lru_coeffs(xc, gate_a, gate_x, ba, bx, lam):
    r = jax.nn.sigmoid(gate_a + ba)
    ig = jax.nn.sigmoid(gate_x + bx)
    a = jnp.exp2(r * ((-LRU_C * LOG2_E) * _softplus(-lam)))
    v = 1.0 - a * a
    u = jnp.where(v > 0.0, v * lax.rsqrt(v), 0.0) * (ig * xc)
    return a, u


def _scan_rows(a, u):
    n = a.shape[0]
    row = lax.broadcasted_iota(jnp.int32, a.shape, 0)
    d = 1
    while d < n:
        keep = row >= d
        u = jnp.where(keep, a * pltpu.roll(u, d, 0) + u, u)
        if 2 * d < n:
            a = jnp.where(keep, a * pltpu.roll(a, d, 0), a)
        d *= 2
    return u


def _phase_blocks(slab_ref, j):
    groups = slab_ref.shape[1] // SUBLANES
    return [slab_ref[j, pl.ds(k, groups, stride=SUBLANES), :] for k in range(SUBLANES)]


def _prev_group(block, carry_row):
    first = lax.broadcasted_iota(jnp.int32, block.shape, 0) == 0
    return jnp.where(first, carry_row, pltpu.roll(block, 1, 0))


def _conv_phases(x_phases, carry_rows, w, b):
    wrapped = {SUBLANES - m: _prev_group(x_phases[SUBLANES - m], carry_rows[CONV_W - 1 - m])
               for m in range(1, CONV_W)}
    out = []
    for k in range(SUBLANES):
        acc = b
        for tap in range(CONV_W):
            src = k - (CONV_W - 1 - tap)
            x = x_phases[src] if src >= 0 else wrapped[src + SUBLANES]
            acc = acc + x * w[tap:tap + 1, :]
        out.append(acc)
    return out


def _scan_phases(a, u, h0):
    groups = a.shape[0] // SUBLANES
    blk = lambda v, k: v[groups * k:groups * (k + 1)]
    h_zero = [blk(u, 0)]
    a_prod = [blk(a, 0)]
    for k in range(1, SUBLANES):
        h_zero.append(blk(a, k) * h_zero[-1] + blk(u, k))
        a_prod.append(blk(a, k) * a_prod[-1])
    first = lax.broadcasted_iota(jnp.int32, a_prod[-1].shape, 0) == 0
    h_end = _scan_rows(a_prod[-1], jnp.where(first, a_prod[-1] * h0 + h_zero[-1], h_zero[-1]))
    h_in = _prev_group(h_end, h0)
    return [h_zero[k] + a_prod[k] * h_in for k in range(SUBLANES)]


def _expand_q(q, rows):
    lane = lax.broadcasted_iota(jnp.int32, (rows, LANES), 1)
    pieces = []
    for h in range(N_HEADS):
        kv = h // GROUP
        src = q[:, LANES * (h // 2):LANES * (h // 2 + 1)]
        if (h % 2) != kv:
            src = pltpu.roll(src, HEAD_DIM, 1)
        in_half = (lane >= HEAD_DIM) if kv == 1 else (lane < HEAD_DIM)
        pieces.append(jnp.where(in_half, src, 0.0))
    return jnp.concatenate(pieces, axis=0).astype(BF16)


def _collect_heads(o, rows):
    lane = lax.broadcasted_iota(jnp.int32, (rows, LANES), 1)
    groups = []
    for j in range(N_HEADS // 2):
        kv = (2 * j) // GROUP
        even = o[rows * (2 * j):rows * (2 * j + 1)]
        odd = o[rows * (2 * j + 1):rows * (2 * j + 2)]
        if kv == 1:
            even = pltpu.roll(even, HEAD_DIM, 1)
        else:
            odd = pltpu.roll(odd, HEAD_DIM, 1)
        groups.append(jnp.where(lane < HEAD_DIM, even, odd))
    return jnp.concatenate(groups, axis=1)


def _collect_heads_t(o_t, rows):
    return jnp.concatenate(
        [o_t[HEAD_DIM * (h // GROUP):HEAD_DIM * (h // GROUP + 1), rows * h:rows * (h + 1)]
         for h in range(N_HEADS)], axis=0)


def _sink_row(sinks_ref, layer, rows):
    head = lax.broadcasted_iota(jnp.int32, (1, N_HEADS * rows), 1) // rows
    out = jnp.full((1, N_HEADS * rows), sinks_ref[layer, 0], F32)
    for h in range(1, N_HEADS):
        out = jnp.where(head == h, sinks_ref[layer, h], out)
    return out * LOG2_E


def _sink_column(sinks_ref, layer, rows):
    return jnp.concatenate([jnp.full((rows, 1), sinks_ref[layer, h], F32) for h in range(N_HEADS)],
                           axis=0) * LOG2_E


def _mix_normed(attn, hs, yg, g_attn, g_lru):
    lru = _gelu_tanh_times(yg, hs)
    return jnp.concatenate([_rmsnorm(attn, g_attn), _rmsnorm(lru, g_lru)], axis=-1).astype(BF16)


def _mix_project(x1, mixed, wo_ref, g_postmix):
    y = jnp.dot(mixed, wo_ref[...], preferred_element_type=F32)
    return x1 + _rmsnorm(y, g_postmix)


def _mix_out(x1, attn, hs, yg, g_attn, g_lru, wo_ref, g_postmix):
    return _mix_project(x1, _mix_normed(attn, hs, yg, g_attn, g_lru), wo_ref, g_postmix)


def _layer_spec(shape, layer):
    zeros = (0,) * len(shape)
    return pl.BlockSpec((None,) + tuple(shape), lambda *_: (layer,) + zeros,
                        pipeline_mode=pl.Buffered(1))


def _cast_kernel(w_ref, o_ref):
    o_ref[...] = w_ref[...].astype(BF16)


def _to_bf16(w, layer):
    _, rows, cols = w.shape
    assert rows % CAST_ROWS == 0
    return pl.pallas_call(
        _cast_kernel,
        grid=(rows // CAST_ROWS,),
        in_specs=[pl.BlockSpec((None, CAST_ROWS, cols), lambda i: (layer, i, 0))],
        out_specs=pl.BlockSpec((CAST_ROWS, cols), lambda i: (i, 0)),
        out_shape=jax.ShapeDtypeStruct((rows, cols), BF16),
        compiler_params=pltpu.CompilerParams(dimension_semantics=("arbitrary",)),
        name="cast_bf16",
    )(w)


def _side_block_rows(rows, n_tiles):
    for k in range(BF16_SUBLANES, rows + 1, BF16_SUBLANES):
        if rows % k == 0 and rows // k <= n_tiles:
            return k
    raise ValueError(f"cannot split {rows} rows over {n_tiles} steps")


def _side_cast_specs(sources, n_tiles):
    in_specs, out_specs, out_shapes = [], [], []
    for w, layer in sources:
        _, rows, cols = w.shape
        br = _side_block_rows(rows, n_tiles)
        block = functools.partial(jnp.minimum, rows // br - 1)
        in_specs.append(pl.BlockSpec((None, br, cols),
                                     lambda g, layer=layer, block=block: (layer, block(g), 0)))
        out_specs.append(pl.BlockSpec((br, cols), lambda g, block=block: (block(g), 0)))
        out_shapes.append(jax.ShapeDtypeStruct((rows, cols), BF16))
    return in_specs, out_specs, out_shapes


def _side_cast(src_refs, dst_refs):
    for src, dst in zip(src_refs, dst_refs):
        dst[...] = src[...].astype(BF16)


def _resident(shape):
    return pl.BlockSpec(shape, lambda *_: (0,) * len(shape))


def _in_proj_steps(x1, g_mix, win_ref, cos, sin_up, sin_dn, z_ref):
    h = _rmsnorm(x1, g_mix).astype(BF16)
    half = ROT_DIM // 2
    for c in range(IN_WIDTH // FF_CHUNK):
        zc = jnp.dot(h, win_ref[:, FF_CHUNK * c:FF_CHUNK * (c + 1)], preferred_element_type=F32)
        for k in range(FF_CHUNK // LANES):
            j = (FF_CHUNK // LANES) * c + k
            zj = zc[:, LANES * k:LANES * (k + 1)]
            if j < O_V // LANES:
                zj = (zj * cos + pltpu.roll(zj, LANES - half, 1) * sin_up
                      + pltpu.roll(zj, half, 1) * sin_dn)
            if j < ATTN_WIDTH // LANES:
                zj = zj * (LOG2_E * HEAD_DIM ** -0.5)
            z_ref[:, LANES * j:LANES * (j + 1)] = zj
        yield


N_LRU_WEIGHTS = 6


def _lru_weight_specs(layer):
    lspec = lambda *shape: _layer_spec(shape, layer)
    return [lspec(CONV_W, LRU_WIDTH), lspec(1, LRU_WIDTH), lspec(LRU_WIDTH, 2 * LRU_WIDTH),
            lspec(1, LRU_WIDTH), lspec(1, LRU_WIDTH), lspec(1, LRU_WIDTH)]


def _lru_weights(w):
    return (w["conv_w"], w["conv_b"], w["w_gate"], w["gate_a_b"], w["gate_x_b"], w["lam"])


def _lru_prompt_steps(z_ref, lru_refs, conv_scr, h_scr, xr_scr, xc_scr, gates_scr, hs_scr):
    conv_w_ref, conv_b_ref, wgate_ref, ba_ref, bx_ref, lam_ref = lru_refs
    tile = z_ref.shape[0]
    groups = tile // SUBLANES
    n_slabs = LRU_WIDTH // LANES
    for j in range(n_slabs):
        ch = slice(LANES * j, LANES * (j + 1))
        xr_scr[j] = z_ref[:, O_XR + LANES * j:O_XR + LANES * (j + 1)]
        carry_rows = [conv_scr[m:m + 1, ch] for m in range(CONV_W - 1)]
        xc_phases = _conv_phases(_phase_blocks(xr_scr, j), carry_rows, conv_w_ref[:, ch],
                                 conv_b_ref[:, ch])
        xc_scr[:, ch] = jnp.concatenate(xc_phases, axis=0)
    last_rows = z_ref[tile - (CONV_W - 1):, O_XR:O_YG]
    conv_scr[...] = last_rows
    gates_scr[...] = _lru_gates(xc_scr[...], wgate_ref)
    yield
    h_last = []
    for j in range(n_slabs):
        ch = slice(LANES * j, LANES * (j + 1))
        gate_x_ch = slice(LRU_WIDTH + LANES * j, LRU_WIDTH + LANES * (j + 1))
        a, u = _lru_coeffs(xc_scr[:, ch], gates_scr[:, ch], gates_scr[:, gate_x_ch],
                           ba_ref[:, ch], bx_ref[:, ch], lam_ref[:, ch])
        yield
        h_phases = _scan_phases(a, u, h_scr[:, ch])
        for k in range(SUBLANES):
            hs_scr[j, pl.ds(k, groups, stride=SUBLANES), :] = h_phases[k]
        h_last.append(h_phases[SUBLANES - 1][groups - 1:groups, :])
        z_ref[:, O_XR + LANES * j:O_XR + LANES * (j + 1)] = hs_scr[j]
        yield
    h_last = jnp.concatenate(h_last, axis=1)
    h_scr[...] = h_last
    return last_rows, h_last


def _ffn_in_kernel(n_seq_tiles, n_side, *refs):
    pipelined = n_seq_tiles > 0
    (x_ref, cos_ref, sin_up_ref, sin_dn_ref, g_pre_ref, wg_ref, wu_ref, wd_ref, g_post_ref,
     g_mix_ref, win_ref) = refs[:11]
    n_lru = N_LRU_WEIGHTS if pipelined else 0
    lru_refs = refs[11:11 + n_lru]
    n_in = 11 + n_lru + n_side
    side_src = refs[11 + n_lru:n_in]
    x1_ref, z_ref = refs[n_in:n_in + 2]
    state_out = refs[n_in + 2:n_in + 2 + (2 if pipelined else 0)]
    n_out = 2 + len(state_out) + n_side
    side_dst = refs[n_in + 2 + len(state_out):n_in + n_out]
    act_scr, *stage_scr = refs[n_in + n_out:]

    def ffn_steps():
        return _half_ffn_steps(x_ref[...], g_pre_ref[...], wg_ref, wu_ref, wd_ref, g_post_ref[...],
                               act_scr)

    def in_proj_steps(x1):
        return _in_proj_steps(x1, g_mix_ref[...], win_ref, cos_ref[...], sin_up_ref[...],
                              sin_dn_ref[...], z_ref)

    if not pipelined:
        x1 = _run(ffn_steps())
        x1_ref[...] = x1
        _run(in_proj_steps(x1))
        return
    x1_scr, conv_scr, h_scr, xr_scr, xc_scr, gates_scr, hs_scr = stage_scr
    nconv_ref, nh_ref = state_out
    g_idx = pl.program_id(0)
    n_tiles = pl.num_programs(0) - 1

    @pl.when(jnp.maximum(g_idx - 1, 0) % n_seq_tiles == 0)
    def _():
        conv_scr[...] = jnp.zeros(conv_scr.shape, F32)
        h_scr[...] = jnp.zeros(h_scr.shape, F32)

    def stage_b_steps():
        yield from in_proj_steps(x1_scr[...])
        last_rows, h_last = yield from _lru_prompt_steps(z_ref, lru_refs, conv_scr, h_scr, xr_scr,
                                                         xc_scr, gates_scr, hs_scr)
        nconv_ref[0] = last_rows
        nh_ref[0] = h_last

    @pl.when(g_idx == 0)
    def _():
        x1_scr[...] = jnp.zeros(x1_scr.shape, F32)

    @pl.when(g_idx < n_tiles)
    def _():
        x1, _ = _interleave(ffn_steps(), stage_b_steps())
        x1_ref[...] = x1
        x1_scr[...] = x1
        _side_cast(side_src, side_dst)

    @pl.when(g_idx == n_tiles)
    def _():
        _run(stage_b_steps())


def _ffn_in_sample(x, tabs, w, big, layer):
    n = x.shape[0]
    lspec = lambda *shape: _layer_spec(shape, layer)
    full = lambda *shape: pl.BlockSpec(shape, lambda g: (0,) * len(shape))
    wg, wu, wd, w_in = big
    return pl.pallas_call(
        functools.partial(_ffn_in_kernel, 0, 0),
        grid=(1,),
        in_specs=[
            full(n, D_MODEL), full(n, LANES), full(n, LANES), full(n, LANES),
            lspec(1, D_MODEL), _resident(wg.shape), _resident(wu.shape), _resident(wd.shape),
            lspec(1, D_MODEL), lspec(1, D_MODEL), _resident(w_in.shape),
        ],
        out_specs=[full(n, D_MODEL), full(n, IN_WIDTH)],
        out_shape=[jax.ShapeDtypeStruct((n, D_MODEL), F32), jax.ShapeDtypeStruct((n, IN_WIDTH), F32)],
        scratch_shapes=[pltpu.VMEM((n, D_FF), BF16)],
        compiler_params=pltpu.CompilerParams(
            dimension_semantics=("arbitrary",), vmem_limit_bytes=VMEM_LIMIT_BYTES),
        name="ffn_in_sample",
    )(x, *tabs, w["g_pre1"], wg, wu, wd, w["g_post1"], w["g_mix"], w_in)


def _ffn_in_prompt(x, tabs, w, big, layer, batch, seq, tile, side=()):
    n_s = seq // tile
    n = batch * n_s
    ffn_tile = lambda g: jnp.minimum(g, n - 1)
    proj_tile = lambda g: jnp.maximum(g - 1, 0)
    lspec = lambda *shape: _layer_spec(shape, layer)
    tab_spec = pl.BlockSpec((tile, LANES), lambda g: (proj_tile(g) % n_s, 0))
    per_batch = lambda rows: pl.BlockSpec((1, rows, LRU_WIDTH), lambda g: (proj_tile(g) // n_s, 0, 0))
    side_in, side_out, side_shapes = _side_cast_specs(side, n)
    wg, wu, wd, w_in = big
    slabs = (LRU_WIDTH // LANES, tile, LANES)
    return pl.pallas_call(
        functools.partial(_ffn_in_kernel, n_s, len(side)),
        grid=(n + 1,),
        in_specs=[
            pl.BlockSpec((tile, D_MODEL), lambda g: (ffn_tile(g), 0)),
            tab_spec, tab_spec, tab_spec,
            lspec(1, D_MODEL), _resident(wg.shape), _resident(wu.shape), _resident(wd.shape),
            lspec(1, D_MODEL), lspec(1, D_MODEL), _resident(w_in.shape),
        ] + _lru_weight_specs(layer) + side_in,
        out_specs=[
            pl.BlockSpec((tile, D_MODEL), lambda g: (ffn_tile(g), 0)),
            pl.BlockSpec((tile, IN_WIDTH), lambda g: (proj_tile(g), 0)),
            per_batch(CONV_W - 1), per_batch(1),
        ] + side_out,
        out_shape=[
            jax.ShapeDtypeStruct((n * tile, D_MODEL), F32),
            jax.ShapeDtypeStruct((n * tile, IN_WIDTH), F32),
            jax.ShapeDtypeStruct((batch, CONV_W - 1, LRU_WIDTH), F32),
            jax.ShapeDtypeStruct((batch, 1, LRU_WIDTH), F32),
        ] + side_shapes,
        scratch_shapes=[
            pltpu.VMEM((tile, D_FF), BF16),
            pltpu.VMEM((tile, D_MODEL), F32),
            pltpu.VMEM((CONV_W - 1, LRU_WIDTH), F32),
            pltpu.VMEM((1, LRU_WIDTH), F32),
            pltpu.VMEM(slabs, F32),
            pltpu.VMEM((tile, LRU_WIDTH), F32),
            pltpu.VMEM((tile, 2 * LRU_WIDTH), F32),
            pltpu.VMEM(slabs, F32),
        ],
        compiler_params=pltpu.CompilerParams(
            dimension_semantics=("arbitrary",), vmem_limit_bytes=VMEM_LIMIT_BYTES),
        name="ffn_in",
    )(x, *tabs, w["g_pre1"], wg, wu, wd, w["g_post1"], w["g_mix"], w_in, *_lru_weights(w),
      *[src for src, _ in side])


def _mix_prompt_kernel(layer, n_seq_tiles, n_side, *refs):
    (sinks_ref, x1_ref, z_ref, g_attn_ref, g_lru_ref, wo_ref, g_postmix_ref, g_pre2_ref, wg_ref,
     wu_ref, wd_ref, g_post2_ref) = refs[:12]
    side_src = refs[12:12 + n_side]
    x3_ref, nk_ref, nv_ref = refs[12 + n_side:15 + n_side]
    side_dst = refs[15 + n_side:15 + 2 * n_side]
    k_scr, vt_scr, x2_scr, bias_scr, attn_scr, act_scr = refs[15 + 2 * n_side:]
    tile = x1_ref.shape[0]
    g_idx = pl.program_id(0)
    n_tiles = pl.num_programs(0) - 1
    s_idx = g_idx % n_seq_tiles

    @pl.when(g_idx == 0)
    def _():
        kj = lax.broadcasted_iota(jnp.int32, (2 * WINDOW, N_HEADS * WINDOW), 0)
        qi = lax.broadcasted_iota(jnp.int32, (2 * WINDOW, N_HEADS * WINDOW), 1) & (WINDOW - 1)
        band = (kj > qi) & (kj <= qi + WINDOW)
        bias_scr[0] = jnp.where(band, 0.0, -jnp.inf)
        bias_scr[1] = jnp.where(band & (kj >= WINDOW), 0.0, -jnp.inf)

    @pl.when(s_idx == 0)
    def _():
        k_scr[0:WINDOW, :] = jnp.zeros((WINDOW, KV_WIDTH), BF16)
        vt_scr[:, 0:WINDOW] = jnp.zeros((KV_WIDTH, WINDOW), BF16)

    def mixer_steps():
        k_scr[WINDOW:, :] = z_ref[:, O_K:O_V].astype(BF16)
        vt_scr[:, WINDOW:] = z_ref[:, O_V:O_XR].T.astype(BF16)
        sink = _sink_row(sinks_ref, layer, WINDOW)
        for i in range(tile // WINDOW):
            q_exp = _expand_q(z_ref[WINDOW * i:WINDOW * (i + 1), 0:ATTN_WIDTH], WINDOW)
            keys = k_scr[WINDOW * i:WINDOW * (i + 2), :]
            vals_t = vt_scr[:, WINDOW * i:WINDOW * (i + 2)]
            s = lax.dot_general(keys, q_exp, (((1,), (1,)), ((), ())), preferred_element_type=F32)
            s = s + bias_scr[jnp.where(s_idx == 0, 1, 0) if i == 0 else 0]
            m = jnp.maximum(jnp.max(s, axis=0, keepdims=True), sink)
            p = jnp.exp2(s - m)
            denom = jnp.sum(p, axis=0, keepdims=True) + jnp.exp2(sink - m)
            yield
            o_t = jnp.dot(vals_t, p.astype(BF16), preferred_element_type=F32) * (1.0 / denom)
            attn_scr[WINDOW * i:WINDOW * (i + 1), :] = _collect_heads_t(o_t, WINDOW).T
            yield
        k_scr[0:WINDOW, :] = k_scr[tile:tile + WINDOW, :]
        vt_scr[:, 0:WINDOW] = vt_scr[:, tile:tile + WINDOW]
        mixed = _mix_normed(attn_scr[...], z_ref[:, O_XR:O_YG], z_ref[:, O_YG:], g_attn_ref[...],
                            g_lru_ref[...])
        yield
        return _mix_project(x1_ref[...], mixed, wo_ref, g_postmix_ref[...])

    def ffn_steps():
        return _half_ffn_steps(x2_scr[...], g_pre2_ref[...], wg_ref, wu_ref, wd_ref,
                               g_post2_ref[...], act_scr)

    def finish_mixer(x2):
        x2_scr[...] = x2
        nk_ref[0] = z_ref[tile - WINDOW:, O_K:O_V]
        nv_ref[0] = z_ref[tile - WINDOW:, O_V:O_XR]
        _side_cast(side_src, side_dst)

    @pl.when(g_idx == 0)
    def _():
        finish_mixer(_run(mixer_steps()))

    @pl.when((g_idx > 0) & (g_idx < n_tiles))
    def _():
        x3, x2 = _interleave(ffn_steps(), mixer_steps())
        x3_ref[...] = x3
        finish_mixer(x2)

    @pl.when(g_idx == n_tiles)
    def _():
        x3_ref[...] = _run(ffn_steps())


def _tail_weight_specs(layer, big):
    lspec = lambda *shape: _layer_spec(shape, layer)
    w_o, wg, wu, wd = big
    return [
        lspec(1, ATTN_WIDTH), lspec(1, LRU_WIDTH), _resident(w_o.shape), lspec(1, D_MODEL),
        lspec(1, D_MODEL), _resident(wg.shape), _resident(wu.shape), _resident(wd.shape),
        lspec(1, D_MODEL),
    ]


def _tail_weights(w, big):
    w_o, wg, wu, wd = big
    return (w["g_attn"], w["g_lru"], w_o, w["g_postmix"], w["g_pre2"], wg, wu, wd, w["g_post2"])


def _mix_prompt(x1, z, w, big, layer, batch, seq, tile, side=()):
    n_s = seq // tile
    n_tiles = batch * n_s
    mix_tile = lambda g: jnp.minimum(g, n_tiles - 1)
    tok = lambda width: pl.BlockSpec((tile, width), lambda g: (mix_tile(g), 0))
    per_batch = lambda rows, width: pl.BlockSpec((1, rows, width), lambda g: (mix_tile(g) // n_s, 0, 0))
    side_in, side_out, side_shapes = _side_cast_specs(side, n_tiles)
    return pl.pallas_call(
        functools.partial(_mix_prompt_kernel, layer, n_s, len(side)),
        grid=(n_tiles + 1,),
        in_specs=[pl.BlockSpec(memory_space=pltpu.SMEM), tok(D_MODEL), tok(IN_WIDTH)]
                 + _tail_weight_specs(layer, big) + side_in,
        out_specs=[
            pl.BlockSpec((tile, D_MODEL), lambda g: (jnp.maximum(g - 1, 0), 0)),
            per_batch(WINDOW, KV_WIDTH), per_batch(WINDOW, KV_WIDTH),
        ] + side_out,
        out_shape=[
            jax.ShapeDtypeStruct((batch * seq, D_MODEL), F32),
            jax.ShapeDtypeStruct((batch, WINDOW, KV_WIDTH), F32),
            jax.ShapeDtypeStruct((batch, WINDOW, KV_WIDTH), F32),
        ] + side_shapes,
        scratch_shapes=[
            pltpu.VMEM((WINDOW + tile, KV_WIDTH), BF16),
            pltpu.VMEM((KV_WIDTH, WINDOW + tile), BF16),
            pltpu.VMEM((tile, D_MODEL), F32),
            pltpu.VMEM((2, 2 * WINDOW, N_HEADS * WINDOW), F32),
            pltpu.VMEM((tile, ATTN_WIDTH), F32),
            pltpu.VMEM((tile, D_FF), BF16),
        ],
        compiler_params=pltpu.CompilerParams(
            dimension_semantics=("arbitrary",), vmem_limit_bytes=VMEM_LIMIT_BYTES),
        name="mix_prompt",
    )(w["sinks"], x1, z, *_tail_weights(w, big), *[src for src, _ in side])


def _attn_sample_kernel(layer, sinks_ref, zq_ref, ck_ref, cv_ref, attn_ref):
    dec_seq, g = zq_ref.shape[0], zq_ref.shape[1]
    rows = dec_seq * g
    cache_len = ck_ref.shape[1]
    zq = zq_ref[...].reshape(rows, zq_ref.shape[2])
    q_exp = _expand_q(zq[:, 0:ATTN_WIDTH], rows)
    k_new = zq[:, O_K:O_V].astype(BF16)
    v_new = zq[:, O_V:O_XR].astype(BF16)
    k_cache = ck_ref[...].reshape(g * cache_len, KV_WIDTH).astype(BF16)
    v_cache = cv_ref[...].reshape(g * cache_len, KV_WIDTH).astype(BF16)
    contract_last = (((1,), (1,)), ((), ()))
    s_c = lax.dot_general(q_exp, k_cache, contract_last, preferred_element_type=F32)
    s_n = lax.dot_general(q_exp, k_new, contract_last, preferred_element_type=F32)

    def row_ids(shape):
        r = lax.broadcasted_iota(jnp.int32, shape, 0)
        return (r % rows) // g, r % g

    t_q, b_q = row_ids(s_c.shape)
    c = lax.broadcasted_iota(jnp.int32, s_c.shape, 1)
    mask_c = ((c // cache_len) == b_q) & ((c % cache_len) > t_q)
    t_q, b_q = row_ids(s_n.shape)
    c = lax.broadcasted_iota(jnp.int32, s_n.shape, 1)
    mask_n = ((c % g) == b_q) & ((c // g) <= t_q)
    s_c = jnp.where(mask_c, s_c, -jnp.inf)
    s_n = jnp.where(mask_n, s_n, -jnp.inf)

    sink = _sink_column(sinks_ref, layer, rows)
    m = jnp.maximum(jnp.maximum(jnp.max(s_c, axis=-1, keepdims=True),
                                jnp.max(s_n, axis=-1, keepdims=True)), sink)
    p_c = jnp.exp2(s_c - m)
    p_n = jnp.exp2(s_n - m)
    denom = (jnp.sum(p_c, axis=-1, keepdims=True) + jnp.sum(p_n, axis=-1, keepdims=True)
             + jnp.exp2(sink - m))
    o = (jnp.dot(p_c.astype(BF16), v_cache, preferred_element_type=F32)
         + jnp.dot(p_n.astype(BF16), v_new, preferred_element_type=F32)) / denom
    attn_ref[...] = _collect_heads(o, rows).reshape(dec_seq, g, ATTN_WIDTH)


def _attn_sample(z3, cache_k, cache_v, sinks, layer):
    dec_seq, dec_batch, _ = z3.shape
    cache_len = cache_k.shape[2]
    g = SAMPLE_GROUP
    cache_spec = pl.BlockSpec((None, g, cache_len, KV_WIDTH), lambda i: (layer, i, 0, 0))
    return pl.pallas_call(
        functools.partial(_attn_sample_kernel, layer),
        grid=(dec_batch // g,),
        in_specs=[
            pl.BlockSpec(memory_space=pltpu.SMEM),
            pl.BlockSpec((dec_seq, g, O_XR), lambda i: (0, i, 0)),
            cache_spec, cache_spec,
        ],
        out_specs=pl.BlockSpec((dec_seq, g, ATTN_WIDTH), lambda i: (0, i, 0)),
        out_shape=jax.ShapeDtypeStruct((dec_seq, dec_batch, ATTN_WIDTH), F32),
        compiler_params=pltpu.CompilerParams(
            dimension_semantics=("arbitrary",), vmem_limit_bytes=VMEM_LIMIT_BYTES),
        name="attn_sample",
    )(sinks, z3, cache_k, cache_v)


def _shift_caches_kernel(kvnew_ref, ck_ref, cv_ref, nk_ref, nv_ref):
    dec_seq = kvnew_ref.shape[1]
    cache_len = ck_ref.shape[1]
    nk_ref[:, 0:cache_len - dec_seq, :] = ck_ref[:, dec_seq:, :]
    nv_ref[:, 0:cache_len - dec_seq, :] = cv_ref[:, dec_seq:, :]
    nk_ref[:, cache_len - dec_seq:, :] = kvnew_ref[:, :, 0:KV_WIDTH]
    nv_ref[:, cache_len - dec_seq:, :] = kvnew_ref[:, :, KV_WIDTH:]


def _shift_caches(kvnew, cache_k, cache_v):
    depth, dec_batch, cache_len, _ = cache_k.shape
    dec_seq = kvnew.shape[2]
    g = SHIFT_GROUP
    spec = lambda rows, width: pl.BlockSpec((None, g, rows, width), lambda l, i: (l, i, 0, 0))
    return pl.pallas_call(
        _shift_caches_kernel,
        grid=(depth, dec_batch // g),
        in_specs=[spec(dec_seq, 2 * KV_WIDTH), spec(cache_len, KV_WIDTH), spec(cache_len, KV_WIDTH)],
        out_specs=[spec(cache_len, KV_WIDTH), spec(cache_len, KV_WIDTH)],
        out_shape=[jax.ShapeDtypeStruct(cache_k.shape, F32), jax.ShapeDtypeStruct(cache_v.shape, F32)],
        compiler_params=pltpu.CompilerParams(dimension_semantics=("arbitrary", "arbitrary")),
        name="shift_caches",
    )(kvnew, cache_k, cache_v)


def _mix_sample_kernel(x1_ref, z_ref, attn_ref, sconv_ref, sh_ref, conv_w_ref, conv_b_ref, wgate_ref,
                       ba_ref, bx_ref, lam_ref, g_attn_ref, g_lru_ref, wo_ref, g_postmix_ref,
                       g_pre2_ref, wg_ref, wu_ref, wd_ref, g_post2_ref,
                       x3_ref, nconv_ref, nh_ref, act_scr):
    nb = sh_ref.shape[0]
    dec_seq = x1_ref.shape[0] // nb
    xr = z_ref[:, O_XR:O_YG]
    xp = [sconv_ref[j] for j in range(CONV_W - 1)] + [xr[nb * t:nb * (t + 1)] for t in range(dec_seq)]
    xc_steps = []
    for t in range(dec_seq):
        acc = conv_b_ref[...] + xp[t] * conv_w_ref[0:1, :]
        for j in range(1, CONV_W):
            acc = acc + xp[t + j] * conv_w_ref[j:j + 1, :]
        xc_steps.append(acc)
    for j in range(CONV_W - 1):
        nconv_ref[j] = xp[dec_seq + j]
    xc = jnp.concatenate(xc_steps, axis=0)
    gates = _lru_gates(xc, wgate_ref)
    a, u = _lru_coeffs(xc, gates[:, :LRU_WIDTH], gates[:, LRU_WIDTH:], ba_ref[...], bx_ref[...],
                       lam_ref[...])
    h = sh_ref[...]
    hs_steps = []
    for t in range(dec_seq):
        h = a[nb * t:nb * (t + 1)] * h + u[nb * t:nb * (t + 1)]
        hs_steps.append(h)
    nh_ref[...] = h
    hs = jnp.concatenate(hs_steps, axis=0)
    x2 = _mix_out(x1_ref[...], attn_ref[...], hs, z_ref[:, O_YG:], g_attn_ref[...], g_lru_ref[...],
                  wo_ref, g_postmix_ref[...])
    x3_ref[...] = _half_ffn(x2, g_pre2_ref[...], wg_ref, wu_ref, wd_ref, g_post2_ref[...], act_scr)


def _mix_sample(x1, z, attn, sconv, state_h, w, big, layer):
    n = x1.shape[0]
    nb = state_h.shape[1]
    full = lambda *shape: pl.BlockSpec(shape, lambda i: (0,) * len(shape))
    return pl.pallas_call(
        _mix_sample_kernel,
        grid=(1,),
        in_specs=[
            full(n, D_MODEL), full(n, IN_WIDTH), full(n, ATTN_WIDTH),
            full(CONV_W - 1, nb, LRU_WIDTH), _layer_spec((nb, LRU_WIDTH), layer),
        ] + _lru_weight_specs(layer) + _tail_weight_specs(layer, big),
        out_specs=[full(n, D_MODEL), full(CONV_W - 1, nb, LRU_WIDTH), full(nb, LRU_WIDTH)],
        out_shape=[
            jax.ShapeDtypeStruct((n, D_MODEL), F32),
            jax.ShapeDtypeStruct((CONV_W - 1, nb, LRU_WIDTH), F32),
            jax.ShapeDtypeStruct((nb, LRU_WIDTH), F32),
        ],
        scratch_shapes=[pltpu.VMEM((n, D_FF), BF16)],
        compiler_params=pltpu.CompilerParams(
            dimension_semantics=("arbitrary",), vmem_limit_bytes=VMEM_LIMIT_BYTES),
        name="mix_sample",
    )(x1, z, attn, sconv, state_h, *_lru_weights(w), *_tail_weights(w, big))


def _rope_tables(pos):
    half = ROT_DIM // 2
    inv = ROPE_THETA ** (-(jnp.arange(half, dtype=F32) * 2.0) / ROT_DIM)
    ang = pos.astype(F32)[:, None] * inv[None, :]
    cos, sin = jnp.cos(ang), jnp.sin(ang)
    n = pos.shape[0]
    rest = jnp.zeros((n, HEAD_DIM - ROT_DIM), F32)
    zero = jnp.zeros((n, half), F32)
    per_head = lambda parts: jnp.tile(jnp.concatenate(parts, axis=-1), (1, LANES // HEAD_DIM))
    return (per_head([cos, cos, rest + 1.0]), per_head([-sin, zero, rest]), per_head([zero, sin, rest]))


def _block_diag(w):
    eye = jnp.eye(LRU_HEADS, dtype=w.dtype)
    return jnp.einsum("lhij,hg->lhigj", w, eye).reshape(w.shape[0], LRU_WIDTH, LRU_WIDTH)


def kernel(x_prompt, x_sample, cache_k, cache_v, state_conv, state_h, norm_pre_ffn1, ffn1_w_gate, ffn1_w_up, ffn1_w_down, norm_post_ffn1, norm_pre_mix, w_in, sinks, conv_w, conv_b, gate_a_w, gate_a_b, gate_x_w, gate_x_b, lru_lambda, attn_out_norm, lru_out_norm, w_o, norm_post_mix, norm_pre_ffn2, ffn2_w_gate, ffn2_w_up, ffn2_w_down, norm_post_ffn2):
    batch, seq, _ = x_prompt.shape
    dec_batch, dec_seq, _ = x_sample.shape
    depth = w_in.shape[0]
    cache_len = cache_k.shape[2]
    n_s = dec_seq * dec_batch
    assert seq % FFN_TILE == 0 and n_s % FFN_TILE == 0
    assert seq % MIX_TILE == 0 and MIX_TILE % WINDOW == 0
    assert dec_batch % SAMPLE_GROUP == 0 and dec_batch % SHIFT_GROUP == 0

    tabs_p = _rope_tables(jnp.arange(seq))
    tabs_s = _rope_tables(jnp.repeat(PAST_LEN + jnp.arange(dec_seq), dec_batch))
    xp = x_prompt.reshape(batch * seq, D_MODEL)
    xs = jnp.swapaxes(x_sample, 0, 1).reshape(n_s, D_MODEL)
    rows = lambda v: v.reshape(depth, 1, -1)

    w = dict(
        g_pre1=rows(norm_pre_ffn1), g_post1=rows(norm_post_ffn1), g_mix=rows(norm_pre_mix),
        sinks=sinks, conv_w=conv_w, conv_b=rows(conv_b),
        w_gate=jnp.concatenate([_block_diag(gate_a_w), _block_diag(gate_x_w)], axis=2).astype(BF16),
        gate_a_b=rows(gate_a_b), gate_x_b=rows(gate_x_b), lam=rows(lru_lambda),
        g_attn=rows(attn_out_norm), g_lru=rows(lru_out_norm),
        g_postmix=rows(norm_post_mix), g_pre2=rows(norm_pre_ffn2), g_post2=rows(norm_post_ffn2),
    )
    ffn_in_src = (ffn1_w_gate, ffn1_w_up, ffn1_w_down, w_in)
    mix_src = (w_o, ffn2_w_gate, ffn2_w_up, ffn2_w_down)
    ck = cache_k.reshape(depth, dec_batch, cache_len, KV_WIDTH)
    cv = cache_v.reshape(depth, dec_batch, cache_len, KV_WIDTH)

    outs = [[] for _ in range(6)]
    kvnew = []
    ffn_in_big = [_to_bf16(s, 0) for s in ffn_in_src]
    for l in range(depth):
        x1p, zp, nc, nh, *mix_big = _ffn_in_prompt(xp, tabs_p, w, ffn_in_big, l, batch, seq, FFN_TILE,
                                                   side=[(s, l) for s in mix_src])
        xp, nk, nv, *next_ffn_in_big = _mix_prompt(
            x1p, zp, w, mix_big, l, batch, seq, MIX_TILE,
            side=[(s, l + 1) for s in ffn_in_src if l + 1 < depth])

        x1s, zs = _ffn_in_sample(xs, tabs_s, w, ffn_in_big, l)
        z3 = zs.reshape(dec_seq, dec_batch, IN_WIDTH)
        kvnew.append(jnp.swapaxes(z3[:, :, O_K:O_XR], 0, 1))
        attn_s = _attn_sample(z3, ck, cv, sinks, l)
        xs, ncs, nhs = _mix_sample(x1s, zs, attn_s.reshape(n_s, ATTN_WIDTH),
                                   jnp.swapaxes(state_conv[l], 0, 1), state_h, w, mix_big, l)
        ffn_in_big = next_ffn_in_big
        kv_p = (batch, WINDOW, N_KV_HEADS, HEAD_DIM)
        for acc, val in zip(outs, (nk.reshape(kv_p), nv.reshape(kv_p), nc,
                                   nh.reshape(batch, LRU_WIDTH), jnp.swapaxes(ncs, 0, 1), nhs)):
            acc.append(val)

    nks, nvs = _shift_caches(jnp.stack(kvnew), ck, cv)
    y_prompt = xp.reshape(batch, seq, D_MODEL)
    y_sample = jnp.swapaxes(xs.reshape(dec_seq, dec_batch, D_MODEL), 0, 1)
    kp, vp, cp, hp, cs, hs = (jnp.stack(o) for o in outs)
    return (y_prompt, y_sample, kp, vp, cp, hp, nks.reshape(cache_k.shape), nvs.reshape(cache_v.shape),
            cs, hs)
```

```python
import functools

import jax
import jax.numpy as jnp
import numpy as np
from jax import lax
from jax.experimental import pallas as pl
from jax.experimental.pallas import tpu as pltpu

D_MODEL = 1024
PAST_LEN = 16384
N_HEADS = 8
HEAD_DIM = 64
N_KV_HEADS = 2
GROUP = N_HEADS // N_KV_HEADS
ATTN_WIDTH = N_HEADS * HEAD_DIM
KV_WIDTH = N_KV_HEADS * HEAD_DIM
WINDOW = 128
ROPE_THETA = 500000.0
ROT_DIM = HEAD_DIM // 4
LRU_WIDTH = D_MODEL // 2
LRU_HEADS = 8
LRU_BLOCK = LRU_WIDTH // LRU_HEADS
CONV_W = 4
LRU_C = 8.0
D_FF = 2816
IN_WIDTH = ATTN_WIDTH + 2 * KV_WIDTH + 2 * LRU_WIDTH
EPS = 1e-6

LANES = 128
SUBLANES = 8
FF_CHUNK = 256
N_FF_CHUNKS = D_FF // FF_CHUNK
FFN_TILE = 512
MIX_TILE = 256
SAMPLE_GROUP = 8
SHIFT_GROUP = 16
CAST_ROWS = 256
BF16_SUBLANES = 16
V7X_VMEM_BYTES = 64 * 1024 * 1024
VMEM_LIMIT_BYTES = V7X_VMEM_BYTES * 7 // 8

O_K = ATTN_WIDTH
O_V = O_K + KV_WIDTH
O_XR = O_V + KV_WIDTH
O_YG = O_XR + LRU_WIDTH

F32 = jnp.float32
BF16 = jnp.bfloat16
LOG2_E = np.float32(np.log2(np.e))


def _rmsnorm(x, g):
    return (x * lax.rsqrt(jnp.mean(x * x, axis=-1, keepdims=True) + EPS)) * g


def _half_ffn_steps(x, g_pre, wg_ref, wu_ref, wd_ref, g_post, act_scr):
    h = _rmsnorm(x, g_pre).astype(BF16)
    for c in range(N_FF_CHUNKS):
        cols = slice(FF_CHUNK * c, FF_CHUNK * (c + 1))
        g = jnp.dot(h, wg_ref[:, cols], preferred_element_type=F32)
        u = jnp.dot(h, wu_ref[:, cols], preferred_element_type=F32)
        act_scr[:, cols] = ((g * jax.nn.sigmoid(g)) * u).astype(BF16)
        yield
    ys = []
    for c in range(D_MODEL // FF_CHUNK):
        ys.append(jnp.dot(act_scr[...], wd_ref[:, FF_CHUNK * c:FF_CHUNK * (c + 1)],
                          preferred_element_type=F32))
        yield
    return x + _rmsnorm(jnp.concatenate(ys, axis=1), 0.5 * g_post)


def _advance(gen, done):
    if gen in done:
        return
    try:
        next(gen)
    except StopIteration as stop:
        done[gen] = stop.value


def _interleave(first, second):
    done = {}
    while len(done) < 2:
        _advance(first, done)
        _advance(second, done)
    return done[first], done[second]


def _run(gen):
    done = {}
    while gen not in done:
        _advance(gen, done)
    return done[gen]


def _half_ffn(*args):
    return _run(_half_ffn_steps(*args))


def _gelu_tanh_times(x, y):
    c = np.float32(np.sqrt(2.0 / np.pi))
    t = jnp.tanh(x * ((x * x) * (np.float32(0.044715) * c) + c))
    w = (0.5 * x) * y
    return t * w + w


def _softplus(x):
    return jnp.maximum(x, 0.0) + jnp.log1p(jnp.exp(-jnp.abs(x)))


def _lru_gates(xc, wgate_ref):
    return jnp.dot(xc.astype(BF16), wgate_ref[...], preferred_element_type=F32)


def _lru_coeffs(xc, gate_a, gate_x, ba, bx, lam):
    r = jax.nn.sigmoid(gate_a + ba)
    ig = jax.nn.sigmoid(gate_x + bx)
    a = jnp.exp2(r * ((-LRU_C * LOG2_E) * _softplus(-lam)))
    v = 1.0 - a * a
    u = jnp.where(v > 0.0, v * lax.rsqrt(v), 0.0) * (ig * xc)
    return a, u


def _scan_rows(a, u):
    n = a.shape[0]
    row = lax.broadcasted_iota(jnp.int32, a.shape, 0)
    d = 1
    while d < n:
        keep = row >= d
        u = jnp.where(keep, a * pltpu.roll(u, d, 0) + u, u)
        if 2 * d < n:
            a = jnp.where(keep, a * pltpu.roll(a, d, 0), a)
        d *= 2
    return u


def _phase_blocks(slab_ref, j):
    groups = slab_ref.shape[1] // SUBLANES
    return [slab_ref[j, pl.ds(k, groups, stride=SUBLANES), :] for k in range(SUBLANES)]


def _prev_group(block, carry_row):
    first = lax.broadcasted_iota(jnp.int32, block.shape, 0) == 0
    return jnp.where(first, carry_row, pltpu.roll(block, 1, 0))


def _conv_phases(x_phases, carry_rows, w, b):
    wrapped = {SUBLANES - m: _prev_group(x_phases[SUBLANES - m], carry_rows[CONV_W - 1 - m])
               for m in range(1, CONV_W)}
    out = []
    for k in range(SUBLANES):
        acc = b
        for tap in range(CONV_W):
            src = k - (CONV_W - 1 - tap)
            x = x_phases[src] if src >= 0 else wrapped[src + SUBLANES]
            acc = acc + x * w[tap:tap + 1, :]
        out.append(acc)
    return out


def _scan_phases(a, u, h0):
    groups = a.shape[0] // SUBLANES
    blk = lambda v, k: v[groups * k:groups * (k + 1)]
    h_zero = [blk(u, 0)]
    a_prod = [blk(a, 0)]
    for k in range(1, SUBLANES):
        h_zero.append(blk(a, k) * h_zero[-1] + blk(u, k))
        a_prod.append(blk(a, k) * a_prod[-1])
    first = lax.broadcasted_iota(jnp.int32, a_prod[-1].shape, 0) == 0
    h_end = _scan_rows(a_prod[-1], jnp.where(first, a_prod[-1] * h0 + h_zero[-1], h_zero[-1]))
    h_in = _prev_group(h_end, h0)
    return [h_zero[k] + a_prod[k] * h_in for k in range(SUBLANES)]


def _expand_q(q, rows):
    lane = lax.broadcasted_iota(jnp.int32, (rows, LANES), 1)
    pieces = []
    for h in range(N_HEADS):
        kv = h // GROUP
        src = q[:, LANES * (h // 2):LANES * (h // 2 + 1)]
        if (h % 2) != kv:
            src = pltpu.roll(src, HEAD_DIM, 1)
        in_half = (lane >= HEAD_DIM) if kv == 1 else (lane < HEAD_DIM)
        pieces.append(jnp.where(in_half, src, 0.0))
    return jnp.concatenate(pieces, axis=0).astype(BF16)


def _collect_heads(o, rows):
    lane = lax.broadcasted_iota(jnp.int32, (rows, LANES), 1)
    groups = []
    for j in range(N_HEADS // 2):
        kv = (2 * j) // GROUP
        even = o[rows * (2 * j):rows * (2 * j + 1)]
        odd = o[rows * (2 * j + 1):rows * (2 * j + 2)]
        if kv == 1:
            even = pltpu.roll(even, HEAD_DIM, 1)
        else:
            odd = pltpu.roll(odd, HEAD_DIM, 1)
        groups.append(jnp.where(lane < HEAD_DIM, even, odd))
    return jnp.concatenate(groups, axis=1)


def _collect_heads_t(o_t, rows):
    return jnp.concatenate(
        [o_t[HEAD_DIM * (h // GROUP):HEAD_DIM * (h // GROUP + 1), rows * h:rows * (h + 1)]
         for h in range(N_HEADS)], axis=0)


def _sink_row(sinks_ref, layer, rows):
    head = lax.broadcasted_iota(jnp.int32, (1, N_HEADS * rows), 1) // rows
    out = jnp.full((1, N_HEADS * rows), sinks_ref[layer, 0], F32)
    for h in range(1, N_HEADS):
        out = jnp.where(head == h, sinks_ref[layer, h], out)
    return out * LOG2_E


def _sink_column(sinks_ref, layer, rows):
    return jnp.concatenate([jnp.full((rows, 1), sinks_ref[layer, h], F32) for h in range(N_HEADS)],
                           axis=0) * LOG2_E


def _mix_normed(attn, hs, yg, g_attn, g_lru):
    lru = _gelu_tanh_times(yg, hs)
    return jnp.concatenate([_rmsnorm(attn, g_attn), _rmsnorm(lru, g_lru)], axis=-1).astype(BF16)


def _mix_project(x1, mixed, wo_ref, g_postmix):
    y = jnp.dot(mixed, wo_ref[...], preferred_element_type=F32)
    return x1 + _rmsnorm(y, g_postmix)


def _mix_out(x1, attn, hs, yg, g_attn, g_lru, wo_ref, g_postmix):
    return _mix_project(x1, _mix_normed(attn, hs, yg, g_attn, g_lru), wo_ref, g_postmix)


def _layer_spec(shape, layer):
    zeros = (0,) * len(shape)
    return pl.BlockSpec((None,) + tuple(shape), lambda *_: (layer,) + zeros,
                        pipeline_mode=pl.Buffered(1))


def _cast_kernel(w_ref, o_ref):
    o_ref[...] = w_ref[...].astype(BF16)


def _to_bf16(w, layer):
    _, rows, cols = w.shape
    assert rows % CAST_ROWS == 0
    return pl.pallas_call(
        _cast_kernel,
        grid=(rows // CAST_ROWS,),
        in_specs=[pl.BlockSpec((None, CAST_ROWS, cols), lambda i: (layer, i, 0))],
        out_specs=pl.BlockSpec((CAST_ROWS, cols), lambda i: (i, 0)),
        out_shape=jax.ShapeDtypeStruct((rows, cols), BF16),
        compiler_params=pltpu.CompilerParams(dimension_semantics=("arbitrary",)),
        name="cast_bf16",
    )(w)


def _side_block_rows(rows, n_tiles):
    for k in range(BF16_SUBLANES, rows + 1, BF16_SUBLANES):
        if rows % k == 0 and rows // k <= n_tiles:
            return k
    raise ValueError(f"cannot split {rows} rows over {n_tiles} steps")


def _side_cast_specs(sources, n_tiles):
    in_specs, out_specs, out_shapes = [], [], []
    for w, layer in sources:
        _, rows, cols = w.shape
        br = _side_block_rows(rows, n_tiles)
        block = functools.partial(jnp.minimum, rows // br - 1)
        in_specs.append(pl.BlockSpec((None, br, cols),
                                     lambda g, layer=layer, block=block: (layer, block(g), 0)))
        out_specs.append(pl.BlockSpec((br, cols), lambda g, block=block: (block(g), 0)))
        out_shapes.append(jax.ShapeDtypeStruct((rows, cols), BF16))
    return in_specs, out_specs, out_shapes


def _side_cast(src_refs, dst_refs):
    for src, dst in zip(src_refs, dst_refs):
        dst[...] = src[...].astype(BF16)


def _resident(shape):
    return pl.BlockSpec(shape, lambda *_: (0,) * len(shape))


def _in_proj_steps(x1, g_mix, win_ref, cos, sin_up, sin_dn, z_ref):
    h = _rmsnorm(x1, g_mix).astype(BF16)
    half = ROT_DIM // 2
    for c in range(IN_WIDTH // FF_CHUNK):
        zc = jnp.dot(h, win_ref[:, FF_CHUNK * c:FF_CHUNK * (c + 1)], preferred_element_type=F32)
        for k in range(FF_CHUNK // LANES):
            j = (FF_CHUNK // LANES) * c + k
            zj = zc[:, LANES * k:LANES * (k + 1)]
            if j < O_V // LANES:
                zj = (zj * cos + pltpu.roll(zj, LANES - half, 1) * sin_up
                      + pltpu.roll(zj, half, 1) * sin_dn)
            if j < ATTN_WIDTH // LANES:
                zj = zj * (LOG2_E * HEAD_DIM ** -0.5)
            z_ref[:, LANES * j:LANES * (j + 1)] = zj
        yield


N_LRU_WEIGHTS = 6


def _lru_weight_specs(layer):
    lspec = lambda *shape: _layer_spec(shape, layer)
    return [lspec(CONV_W, LRU_WIDTH), lspec(1, LRU_WIDTH), lspec(LRU_WIDTH, 2 * LRU_WIDTH),
            lspec(1, LRU_WIDTH), lspec(1, LRU_WIDTH), lspec(1, LRU_WIDTH)]


def _lru_weights(w):
    return (w["conv_w"], w["conv_b"], w["w_gate"], w["gate_a_b"], w["gate_x_b"], w["lam"])


def _lru_prompt_steps(z_ref, lru_refs, conv_scr, h_scr, xr_scr, xc_scr, gates_scr, hs_scr):
    conv_w_ref, conv_b_ref, wgate_ref, ba_ref, bx_ref, lam_ref = lru_refs
    tile = z_ref.shape[0]
    groups = tile // SUBLANES
    n_slabs = LRU_WIDTH // LANES
    for j in range(n_slabs):
        ch = slice(LANES * j, LANES * (j + 1))
        xr_scr[j] = z_ref[:, O_XR + LANES * j:O_XR + LANES * (j + 1)]
        carry_rows = [conv_scr[m:m + 1, ch] for m in range(CONV_W - 1)]
        xc_phases = _conv_phases(_phase_blocks(xr_scr, j), carry_rows, conv_w_ref[:, ch],
                                 conv_b_ref[:, ch])
        xc_scr[:, ch] = jnp.concatenate(xc_phases, axis=0)
    last_rows = z_ref[tile - (CONV_W - 1):, O_XR:O_YG]
    conv_scr[...] = last_rows
    gates_scr[...] = _lru_gates(xc_scr[...], wgate_ref)
    yield
    h_last = []
    for j in range(n_slabs):
        ch = slice(LANES * j, LANES * (j + 1))
        gate_x_ch = slice(LRU_WIDTH + LANES * j, LRU_WIDTH + LANES * (j + 1))
        a, u = _lru_coeffs(xc_scr[:, ch], gates_scr[:, ch], gates_scr[:, gate_x_ch],
                           ba_ref[:, ch], bx_ref[:, ch], lam_ref[:, ch])
        yield
        h_phases = _scan_phases(a, u, h_scr[:, ch])
        for k in range(SUBLANES):
            hs_scr[j, pl.ds(k, groups, stride=SUBLANES), :] = h_phases[k]
        h_last.append(h_phases[SUBLANES - 1][groups - 1:groups, :])
        z_ref[:, O_XR + LANES * j:O_XR + LANES * (j + 1)] = hs_scr[j]
        yield
    h_last = jnp.concatenate(h_last, axis=1)
    h_scr[...] = h_last
    return last_rows, h_last


def _ffn_in_kernel(n_seq_tiles, n_side, *refs):
    pipelined = n_seq_tiles > 0
    (x_ref, cos_ref, sin_up_ref, sin_dn_ref, g_pre_ref, wg_ref, wu_ref, wd_ref, g_post_ref,
     g_mix_ref, win_ref) = refs[:11]
    n_lru = N_LRU_WEIGHTS if pipelined else 0
    lru_refs = refs[11:11 + n_lru]
    n_in = 11 + n_lru + n_side
    side_src = refs[11 + n_lru:n_in]
    x1_ref, z_ref = refs[n_in:n_in + 2]
    state_out = refs[n_in + 2:n_in + 2 + (2 if pipelined else 0)]
    n_out = 2 + len(state_out) + n_side
    side_dst = refs[n_in + 2 + len(state_out):n_in + n_out]
    act_scr, *stage_scr = refs[n_in + n_out:]

    def ffn_steps():
        return _half_ffn_steps(x_ref[...], g_pre_ref[...], wg_ref, wu_ref, wd_ref, g_post_ref[...],
                               act_scr)

    def in_proj_steps(x1):
        return _in_proj_steps(x1, g_mix_ref[...], win_ref, cos_ref[...], sin_up_ref[...],
                              sin_dn_ref[...], z_ref)

    if not pipelined:
        x1 = _run(ffn_steps())
        x1_ref[...] = x1
        _run(in_proj_steps(x1))
        return
    x1_scr, conv_scr, h_scr, xr_scr, xc_scr, gates_scr, hs_scr = stage_scr
    nconv_ref, nh_ref = state_out
    g_idx = pl.program_id(0)
    n_tiles = pl.num_programs(0) - 1

    @pl.when(jnp.maximum(g_idx - 1, 0) % n_seq_tiles == 0)
    def _():
        conv_scr[...] = jnp.zeros(conv_scr.shape, F32)
        h_scr[...] = jnp.zeros(h_scr.shape, F32)

    def stage_b_steps():
        yield from in_proj_steps(x1_scr[...])
        last_rows, h_last = yield from _lru_prompt_steps(z_ref, lru_refs, conv_scr, h_scr, xr_scr,
                                                         xc_scr, gates_scr, hs_scr)
        nconv_ref[0] = last_rows
        nh_ref[0] = h_last

    @pl.when(g_idx == 0)
    def _():
        x1_scr[...] = jnp.zeros(x1_scr.shape, F32)

    @pl.when(g_idx < n_tiles)
    def _():
        x1, _ = _interleave(ffn_steps(), stage_b_steps())
        x1_ref[...] = x1
        x1_scr[...] = x1
        _side_cast(side_src, side_dst)

    @pl.when(g_idx == n_tiles)
    def _():
        _run(stage_b_steps())


def _ffn_in_sample(x, tabs, w, big, layer):
    n = x.shape[0]
    lspec = lambda *shape: _layer_spec(shape, layer)
    full = lambda *shape: pl.BlockSpec(shape, lambda g: (0,) * len(shape))
    wg, wu, wd, w_in = big
    return pl.pallas_call(
        functools.partial(_ffn_in_kernel, 0, 0),
        grid=(1,),
        in_specs=[
            full(n, D_MODEL), full(n, LANES), full(n, LANES), full(n, LANES),
            lspec(1, D_MODEL), _resident(wg.shape), _resident(wu.shape), _resident(wd.shape),
            lspec(1, D_MODEL), lspec(1, D_MODEL), _resident(w_in.shape),
        ],
        out_specs=[full(n, D_MODEL), full(n, IN_WIDTH)],
        out_shape=[jax.ShapeDtypeStruct((n, D_MODEL), F32), jax.ShapeDtypeStruct((n, IN_WIDTH), F32)],
        scratch_shapes=[pltpu.VMEM((n, D_FF), BF16)],
        compiler_params=pltpu.CompilerParams(
            dimension_semantics=("arbitrary",), vmem_limit_bytes=VMEM_LIMIT_BYTES),
        name="ffn_in_sample",
    )(x, *tabs, w["g_pre1"], wg, wu, wd, w["g_post1"], w["g_mix"], w_in)


def _ffn_in_prompt(x, tabs, w, big, layer, batch, seq, tile, side=()):
    n_s = seq // tile
    n = batch * n_s
    ffn_tile = lambda g: jnp.minimum(g, n - 1)
    proj_tile = lambda g: jnp.maximum(g - 1, 0)
    lspec = lambda *shape: _layer_spec(shape, layer)
    tab_spec = pl.BlockSpec((tile, LANES), lambda g: (proj_tile(g) % n_s, 0))
    per_batch = lambda rows: pl.BlockSpec((1, rows, LRU_WIDTH), lambda g: (proj_tile(g) // n_s, 0, 0))
    side_in, side_out, side_shapes = _side_cast_specs(side, n)
    wg, wu, wd, w_in = big
    slabs = (LRU_WIDTH // LANES, tile, LANES)
    return pl.pallas_call(
        functools.partial(_ffn_in_kernel, n_s, len(side)),
        grid=(n + 1,),
        in_specs=[
            pl.BlockSpec((tile, D_MODEL), lambda g: (ffn_tile(g), 0)),
            tab_spec, tab_spec, tab_spec,
            lspec(1, D_MODEL), _resident(wg.shape), _resident(wu.shape), _resident(wd.shape),
            lspec(1, D_MODEL), lspec(1, D_MODEL), _resident(w_in.shape),
        ] + _lru_weight_specs(layer) + side_in,
        out_specs=[
            pl.BlockSpec((tile, D_MODEL), lambda g: (ffn_tile(g), 0)),
            pl.BlockSpec((tile, IN_WIDTH), lambda g: (proj_tile(g), 0)),
            per_batch(CONV_W - 1), per_batch(1),
        ] + side_out,
        out_shape=[
            jax.ShapeDtypeStruct((n * tile, D_MODEL), F32),
            jax.ShapeDtypeStruct((n * tile, IN_WIDTH), F32),
            jax.ShapeDtypeStruct((batch, CONV_W - 1, LRU_WIDTH), F32),
            jax.ShapeDtypeStruct((batch, 1, LRU_WIDTH), F32),
        ] + side_shapes,
        scratch_shapes=[
            pltpu.VMEM((tile, D_FF), BF16),
            pltpu.VMEM((tile, D_MODEL), F32),
            pltpu.VMEM((CONV_W - 1, LRU_WIDTH), F32),
            pltpu.VMEM((1, LRU_WIDTH), F32),
            pltpu.VMEM(slabs, F32),
            pltpu.VMEM((tile, LRU_WIDTH), F32),
            pltpu.VMEM((tile, 2 * LRU_WIDTH), F32),
            pltpu.VMEM(slabs, F32),
        ],
        compiler_params=pltpu.CompilerParams(
            dimension_semantics=("arbitrary",), vmem_limit_bytes=VMEM_LIMIT_BYTES),
        name="ffn_in",
    )(x, *tabs, w["g_pre1"], wg, wu, wd, w["g_post1"], w["g_mix"], w_in, *_lru_weights(w),
      *[src for src, _ in side])


def _mix_prompt_kernel(layer, n_seq_tiles, n_side, *refs):
    (sinks_ref, x1_ref, z_ref, g_attn_ref, g_lru_ref, wo_ref, g_postmix_ref, g_pre2_ref, wg_ref,
     wu_ref, wd_ref, g_post2_ref) = refs[:12]
    side_src = refs[12:12 + n_side]
    x3_ref, nk_ref, nv_ref = refs[12 + n_side:15 + n_side]
    side_dst = refs[15 + n_side:15 + 2 * n_side]
    k_scr, vt_scr, x2_scr, bias_scr, attn_scr, act_scr = refs[15 + 2 * n_side:]
    tile = x1_ref.shape[0]
    g_idx = pl.program_id(0)
    n_tiles = pl.num_programs(0) - 1
    s_idx = g_idx % n_seq_tiles

    @pl.when(g_idx == 0)
    def _():
        kj = lax.broadcasted_iota(jnp.int32, (2 * WINDOW, N_HEADS * WINDOW), 0)
        qi = lax.broadcasted_iota(jnp.int32, (2 * WINDOW, N_HEADS * WINDOW), 1) & (WINDOW - 1)
        band = (kj > qi) & (kj <= qi + WINDOW)
        bias_scr[0] = jnp.where(band, 0.0, -jnp.inf)
        bias_scr[1] = jnp.where(band & (kj >= WINDOW), 0.0, -jnp.inf)

    @pl.when(s_idx == 0)
    def _():
        k_scr[0:WINDOW, :] = jnp.zeros((WINDOW, KV_WIDTH), BF16)
        vt_scr[:, 0:WINDOW] = jnp.zeros((KV_WIDTH, WINDOW), BF16)

    def mixer_steps():
        k_scr[WINDOW:, :] = z_ref[:, O_K:O_V].astype(BF16)
        vt_scr[:, WINDOW:] = z_ref[:, O_V:O_XR].T.astype(BF16)
        sink = _sink_row(sinks_ref, layer, WINDOW)
        for i in range(tile // WINDOW):
            q_exp = _expand_q(z_ref[WINDOW * i:WINDOW * (i + 1), 0:ATTN_WIDTH], WINDOW)
            keys = k_scr[WINDOW * i:WINDOW * (i + 2), :]
            vals_t = vt_scr[:, WINDOW * i:WINDOW * (i + 2)]
            s = lax.dot_general(keys, q_exp, (((1,), (1,)), ((), ())), preferred_element_type=F32)
            s = s + bias_scr[jnp.where(s_idx == 0, 1, 0) if i == 0 else 0]
            m = jnp.maximum(jnp.max(s, axis=0, keepdims=True), sink)
            p = jnp.exp2(s - m)
            denom = jnp.sum(p, axis=0, keepdims=True) + jnp.exp2(sink - m)
            yield
            o_t = jnp.dot(vals_t, p.astype(BF16), preferred_element_type=F32) * (1.0 / denom)
            attn_scr[WINDOW * i:WINDOW * (i + 1), :] = _collect_heads_t(o_t, WINDOW).T
            yield
        k_scr[0:WINDOW, :] = k_scr[tile:tile + WINDOW, :]
        vt_scr[:, 0:WINDOW] = vt_scr[:, tile:tile + WINDOW]
        mixed = _mix_normed(attn_scr[...], z_ref[:, O_XR:O_YG], z_ref[:, O_YG:], g_attn_ref[...],
                            g_lru_ref[...])
        yield
        return _mix_project(x1_ref[...], mixed, wo_ref, g_postmix_ref[...])

    def ffn_steps():
        return _half_ffn_steps(x2_scr[...], g_pre2_ref[...], wg_ref, wu_ref, wd_ref,
                               g_post2_ref[...], act_scr)

    def finish_mixer(x2):
        x2_scr[...] = x2
        nk_ref[0] = z_ref[tile - WINDOW:, O_K:O_V]
        nv_ref[0] = z_ref[tile - WINDOW:, O_V:O_XR]
        _side_cast(side_src, side_dst)

    @pl.when(g_idx == 0)
    def _():
        finish_mixer(_run(mixer_steps()))

    @pl.when((g_idx > 0) & (g_idx < n_tiles))
    def _():
        x3, x2 = _interleave(ffn_steps(), mixer_steps())
        x3_ref[...] = x3
        finish_mixer(x2)

    @pl.when(g_idx == n_tiles)
    def _():
        x3_ref[...] = _run(ffn_steps())


def _tail_weight_specs(layer, big):
    lspec = lambda *shape: _layer_spec(shape, layer)
    w_o, wg, wu, wd = big
    return [
        lspec(1, ATTN_WIDTH), lspec(1, LRU_WIDTH), _resident(w_o.shape), lspec(1, D_MODEL),
        lspec(1, D_MODEL), _resident(wg.shape), _resident(wu.shape), _resident(wd.shape),
        lspec(1, D_MODEL),
    ]


def _tail_weights(w, big):
    w_o, wg, wu, wd = big
    return (w["g_attn"], w["g_lru"], w_o, w["g_postmix"], w["g_pre2"], wg, wu, wd, w["g_post2"])


def _mix_prompt(x1, z, w, big, layer, batch, seq, tile, side=()):
    n_s = seq // tile
    n_tiles = batch * n_s
    mix_tile = lambda g: jnp.minimum(g, n_tiles - 1)
    tok = lambda width: pl.BlockSpec((tile, width), lambda g: (mix_tile(g), 0))
    per_batch = lambda rows, width: pl.BlockSpec((1, rows, width), lambda g: (mix_tile(g) // n_s, 0, 0))
    side_in, side_out, side_shapes = _side_cast_specs(side, n_tiles)
    return pl.pallas_call(
        functools.partial(_mix_prompt_kernel, layer, n_s, len(side)),
        grid=(n_tiles + 1,),
        in_specs=[pl.BlockSpec(memory_space=pltpu.SMEM), tok(D_MODEL), tok(IN_WIDTH)]
                 + _tail_weight_specs(layer, big) + side_in,
        out_specs=[
            pl.BlockSpec((tile, D_MODEL), lambda g: (jnp.maximum(g - 1, 0), 0)),
            per_batch(WINDOW, KV_WIDTH), per_batch(WINDOW, KV_WIDTH),
        ] + side_out,
        out_shape=[
            jax.ShapeDtypeStruct((batch * seq, D_MODEL), F32),
            jax.ShapeDtypeStruct((batch, WINDOW, KV_WIDTH), F32),
            jax.ShapeDtypeStruct((batch, WINDOW, KV_WIDTH), F32),
        ] + side_shapes,
        scratch_shapes=[
            pltpu.VMEM((WINDOW + tile, KV_WIDTH), BF16),
            pltpu.VMEM((KV_WIDTH, WINDOW + tile), BF16),
            pltpu.VMEM((tile, D_MODEL), F32),
            pltpu.VMEM((2, 2 * WINDOW, N_HEADS * WINDOW), F32),
            pltpu.VMEM((tile, ATTN_WIDTH), F32),
            pltpu.VMEM((tile, D_FF), BF16),
        ],
        compiler_params=pltpu.CompilerParams(
            dimension_semantics=("arbitrary",), vmem_limit_bytes=VMEM_LIMIT_BYTES),
        name="mix_prompt",
    )(w["sinks"], x1, z, *_tail_weights(w, big), *[src for src, _ in side])


def _attn_sample_kernel(layer, sinks_ref, zq_ref, ck_ref, cv_ref, attn_ref):
    dec_seq, g = zq_ref.shape[0], zq_ref.shape[1]
    rows = dec_seq * g
    cache_len = ck_ref.shape[1]
    zq = zq_ref[...].reshape(rows, zq_ref.shape[2])
    q_exp = _expand_q(zq[:, 0:ATTN_WIDTH], rows)
    k_new = zq[:, O_K:O_V].astype(BF16)
    v_new = zq[:, O_V:O_XR].astype(BF16)
    k_cache = ck_ref[...].reshape(g * cache_len, KV_WIDTH).astype(BF16)
    v_cache = cv_ref[...].reshape(g * cache_len, KV_WIDTH).astype(BF16)
    contract_last = (((1,), (1,)), ((), ()))
    s_c = lax.dot_general(q_exp, k_cache, contract_last, preferred_element_type=F32)
    s_n = lax.dot_general(q_exp, k_new, contract_last, preferred_element_type=F32)

    def row_ids(shape):
        r = lax.broadcasted_iota(jnp.int32, shape, 0)
        return (r % rows) // g, r % g

    t_q, b_q = row_ids(s_c.shape)
    c = lax.broadcasted_iota(jnp.int32, s_c.shape, 1)
    mask_c = ((c // cache_len) == b_q) & ((c % cache_len) > t_q)
    t_q, b_q = row_ids(s_n.shape)
    c = lax.broadcasted_iota(jnp.int32, s_n.shape, 1)
    mask_n = ((c % g) == b_q) & ((c // g) <= t_q)
    s_c = jnp.where(mask_c, s_c, -jnp.inf)
    s_n = jnp.where(mask_n, s_n, -jnp.inf)

    sink = _sink_column(sinks_ref, layer, rows)
    m = jnp.maximum(jnp.maximum(jnp.max(s_c, axis=-1, keepdims=True),
                                jnp.max(s_n, axis=-1, keepdims=True)), sink)
    p_c = jnp.exp2(s_c - m)
    p_n = jnp.exp2(s_n - m)
    denom = (jnp.sum(p_c, axis=-1, keepdims=True) + jnp.sum(p_n, axis=-1, keepdims=True)
             + jnp.exp2(sink - m))
    o = (jnp.dot(p_c.astype(BF16), v_cache, preferred_element_type=F32)
         + jnp.dot(p_n.astype(BF16), v_new, preferred_element_type=F32)) / denom
    attn_ref[...] = _collect_heads(o, rows).reshape(dec_seq, g, ATTN_WIDTH)


def _attn_sample(z3, cache_k, cache_v, sinks, layer):
    dec_seq, dec_batch, _ = z3.shape
    cache_len = cache_k.shape[2]
    g = SAMPLE_GROUP
    cache_spec = pl.BlockSpec((None, g, cache_len, KV_WIDTH), lambda i: (layer, i, 0, 0))
    return pl.pallas_call(
        functools.partial(_attn_sample_kernel, layer),
        grid=(dec_batch // g,),
        in_specs=[
            pl.BlockSpec(memory_space=pltpu.SMEM),
            pl.BlockSpec((dec_seq, g, O_XR), lambda i: (0, i, 0)),
            cache_spec, cache_spec,
        ],
        out_specs=pl.BlockSpec((dec_seq, g, ATTN_WIDTH), lambda i: (0, i, 0)),
        out_shape=jax.ShapeDtypeStruct((dec_seq, dec_batch, ATTN_WIDTH), F32),
        compiler_params=pltpu.CompilerParams(
            dimension_semantics=("arbitrary",), vmem_limit_bytes=VMEM_LIMIT_BYTES),
        name="attn_sample",
    )(sinks, z3, cache_k, cache_v)


def _shift_caches_kernel(kvnew_ref, ck_ref, cv_ref, nk_ref, nv_ref):
    dec_seq = kvnew_ref.shape[1]
    cache_len = ck_ref.shape[1]
    nk_ref[:, 0:cache_len - dec_seq, :] = ck_ref[:, dec_seq:, :]
    nv_ref[:, 0:cache_len - dec_seq, :] = cv_ref[:, dec_seq:, :]
    nk_ref[:, cache_len - dec_seq:, :] = kvnew_ref[:, :, 0:KV_WIDTH]
    nv_ref[:, cache_len - dec_seq:, :] = kvnew_ref[:, :, KV_WIDTH:]


def _shift_caches(kvnew, cache_k, cache_v):
    depth, dec_batch, cache_len, _ = cache_k.shape
    dec_seq = kvnew.shape[2]
    g = SHIFT_GROUP
    spec = lambda rows, width: pl.BlockSpec((None, g, rows, width), lambda l, i: (l, i, 0, 0))
    return pl.pallas_call(
        _shift_caches_kernel,
        grid=(depth, dec_batch // g),
        in_specs=[spec(dec_seq, 2 * KV_WIDTH), spec(cache_len, KV_WIDTH), spec(cache_len, KV_WIDTH)],
        out_specs=[spec(cache_len, KV_WIDTH), spec(cache_len, KV_WIDTH)],
        out_shape=[jax.ShapeDtypeStruct(cache_k.shape, F32), jax.ShapeDtypeStruct(cache_v.shape, F32)],
        compiler_params=pltpu.CompilerParams(dimension_semantics=("arbitrary", "arbitrary")),
        name="shift_caches",
    )(kvnew, cache_k, cache_v)


def _mix_sample_kernel(x1_ref, z_ref, attn_ref, sconv_ref, sh_ref, conv_w_ref, conv_b_ref, wgate_ref,
                       ba_ref, bx_ref, lam_ref, g_attn_ref, g_lru_ref, wo_ref, g_postmix_ref,
                       g_pre2_ref, wg_ref, wu_ref, wd_ref, g_post2_ref,
                       x3_ref, nconv_ref, nh_ref, act_scr):
    nb = sh_ref.shape[0]
    dec_seq = x1_ref.shape[0] // nb
    xr = z_ref[:, O_XR:O_YG]
    xp = [sconv_ref[j] for j in range(CONV_W - 1)] + [xr[nb * t:nb * (t + 1)] for t in range(dec_seq)]
    xc_steps = []
    for t in range(dec_seq):
        acc = conv_b_ref[...] + xp[t] * conv_w_ref[0:1, :]
        for j in range(1, CONV_W):
            acc = acc + xp[t + j] * conv_w_ref[j:j + 1, :]
        xc_steps.append(acc)
    for j in range(CONV_W - 1):
        nconv_ref[j] = xp[dec_seq + j]
    xc = jnp.concatenate(xc_steps, axis=0)
    gates = _lru_gates(xc, wgate_ref)
    a, u = _lru_coeffs(xc, gates[:, :LRU_WIDTH], gates[:, LRU_WIDTH:], ba_ref[...], bx_ref[...],
                       lam_ref[...])
    h = sh_ref[...]
    hs_steps = []
    for t in range(dec_seq):
        h = a[nb * t:nb * (t + 1)] * h + u[nb * t:nb * (t + 1)]
        hs_steps.append(h)
    nh_ref[...] = h
    hs = jnp.concatenate(hs_steps, axis=0)
    x2 = _mix_out(x1_ref[...], attn_ref[...], hs, z_ref[:, O_YG:], g_attn_ref[...], g_lru_ref[...],
                  wo_ref, g_postmix_ref[...])
    x3_ref[...] = _half_ffn(x2, g_pre2_ref[...], wg_ref, wu_ref, wd_ref, g_post2_ref[...], act_scr)


def _mix_sample(x1, z, attn, sconv, state_h, w, big, layer):
    n = x1.shape[0]
    nb = state_h.shape[1]
    full = lambda *shape: pl.BlockSpec(shape, lambda i: (0,) * len(shape))
    return pl.pallas_call(
        _mix_sample_kernel,
        grid=(1,),
        in_specs=[
            full(n, D_MODEL), full(n, IN_WIDTH), full(n, ATTN_WIDTH),
            full(CONV_W - 1, nb, LRU_WIDTH), _layer_spec((nb, LRU_WIDTH), layer),
        ] + _lru_weight_specs(layer) + _tail_weight_specs(layer, big),
        out_specs=[full(n, D_MODEL), full(CONV_W - 1, nb, LRU_WIDTH), full(nb, LRU_WIDTH)],
        out_shape=[
            jax.ShapeDtypeStruct((n, D_MODEL), F32),
            jax.ShapeDtypeStruct((CONV_W - 1, nb, LRU_WIDTH), F32),
            jax.ShapeDtypeStruct((nb, LRU_WIDTH), F32),
        ],
        scratch_shapes=[pltpu.VMEM((n, D_FF), BF16)],
        compiler_params=pltpu.CompilerParams(
            dimension_semantics=("arbitrary",), vmem_limit_bytes=VMEM_LIMIT_BYTES),
        name="mix_sample",
    )(x1, z, attn, sconv, state_h, *_lru_weights(w), *_tail_weights(w, big))


def _rope_tables(pos):
    half = ROT_DIM // 2
    inv = ROPE_THETA ** (-(jnp.arange(half, dtype=F32) * 2.0) / ROT_DIM)
    ang = pos.astype(F32)[:, None] * inv[None, :]
    cos, sin = jnp.cos(ang), jnp.sin(ang)
    n = pos.shape[0]
    rest = jnp.zeros((n, HEAD_DIM - ROT_DIM), F32)
    zero = jnp.zeros((n, half), F32)
    per_head = lambda parts: jnp.tile(jnp.concatenate(parts, axis=-1), (1, LANES // HEAD_DIM))
    return (per_head([cos, cos, rest + 1.0]), per_head([-sin, zero, rest]), per_head([zero, sin, rest]))


def _block_diag(w):
    eye = jnp.eye(LRU_HEADS, dtype=w.dtype)
    return jnp.einsum("lhij,hg->lhigj", w, eye).reshape(w.shape[0], LRU_WIDTH, LRU_WIDTH)


def kernel(x_prompt, x_sample, cache_k, cache_v, state_conv, state_h, norm_pre_ffn1, ffn1_w_gate, ffn1_w_up, ffn1_w_down, norm_post_ffn1, norm_pre_mix, w_in, sinks, conv_w, conv_b, gate_a_w, gate_a_b, gate_x_w, gate_x_b, lru_lambda, attn_out_norm, lru_out_norm, w_o, norm_post_mix, norm_pre_ffn2, ffn2_w_gate, ffn2_w_up, ffn2_w_down, norm_post_ffn2):
    batch, seq, _ = x_prompt.shape
    dec_batch, dec_seq, _ = x_sample.shape
    depth = w_in.shape[0]
    cache_len = cache_k.shape[2]
    n_s = dec_seq * dec_batch
    assert seq % FFN_TILE == 0 and n_s % FFN_TILE == 0
    assert seq % MIX_TILE == 0 and MIX_TILE % WINDOW == 0
    assert dec_batch % SAMPLE_GROUP == 0 and dec_batch % SHIFT_GROUP == 0

    tabs_p = _rope_tables(jnp.arange(seq))
    tabs_s = _rope_tables(jnp.repeat(PAST_LEN + jnp.arange(dec_seq), dec_batch))
    xp = x_prompt.reshape(batch * seq, D_MODEL)
    xs = jnp.swapaxes(x_sample, 0, 1).reshape(n_s, D_MODEL)
    rows = lambda v: v.reshape(depth, 1, -1)

    w = dict(
        g_pre1=rows(norm_pre_ffn1), g_post1=rows(norm_post_ffn1), g_mix=rows(norm_pre_mix),
        sinks=sinks, conv_w=conv_w, conv_b=rows(conv_b),
        w_gate=jnp.concatenate([_block_diag(gate_a_w), _block_diag(gate_x_w)], axis=2).astype(BF16),
        gate_a_b=rows(gate_a_b), gate_x_b=rows(gate_x_b), lam=rows(lru_lambda),
        g_attn=rows(attn_out_norm), g_lru=rows(lru_out_norm),
        g_postmix=rows(norm_post_mix), g_pre2=rows(norm_pre_ffn2), g_post2=rows(norm_post_ffn2),
    )
    ffn_in_src = (ffn1_w_gate, ffn1_w_up, ffn1_w_down, w_in)
    mix_src = (w_o, ffn2_w_gate, ffn2_w_up, ffn2_w_down)
    ck = cache_k.reshape(depth, dec_batch, cache_len, KV_WIDTH)
    cv = cache_v.reshape(depth, dec_batch, cache_len, KV_WIDTH)

    outs = [[] for _ in range(6)]
    kvnew = []
    ffn_in_big = [_to_bf16(s, 0) for s in ffn_in_src]
    for l in range(depth):
        x1p, zp, nc, nh, *mix_big = _ffn_in_prompt(xp, tabs_p, w, ffn_in_big, l, batch, seq, FFN_TILE,
                                                   side=[(s, l) for s in mix_src])
        xp, nk, nv, *next_ffn_in_big = _mix_prompt(
            x1p, zp, w, mix_big, l, batch, seq, MIX_TILE,
            side=[(s, l + 1) for s in ffn_in_src if l + 1 < depth])

        x1s, zs = _ffn_in_sample(xs, tabs_s, w, ffn_in_big, l)
        z3 = zs.reshape(dec_seq, dec_batch, IN_WIDTH)
        kvnew.append(jnp.swapaxes(z3[:, :, O_K:O_XR], 0, 1))
        attn_s = _attn_sample(z3, ck, cv, sinks, l)
        xs, ncs, nhs = _mix_sample(x1s, zs, attn_s.reshape(n_s, ATTN_WIDTH),
                                   jnp.swapaxes(state_conv[l], 0, 1), state_h, w, mix_big, l)
        ffn_in_big = next_ffn_in_big
        kv_p = (batch, WINDOW, N_KV_HEADS, HEAD_DIM)
        for acc, val in zip(outs, (nk.reshape(kv_p), nv.reshape(kv_p), nc,
                                   nh.reshape(batch, LRU_WIDTH), jnp.swapaxes(ncs, 0, 1), nhs)):
            acc.append(val)

    nks, nvs = _shift_caches(jnp.stack(kvnew), ck, cv)
    y_prompt = xp.reshape(batch, seq, D_MODEL)
    y_sample = jnp.swapaxes(xs.reshape(dec_seq, dec_batch, D_MODEL), 0, 1)
    kp, vp, cp, hp, cs, hs = (jnp.stack(o) for o in outs)
    return (y_prompt, y_sample, kp, vp, cp, hp, nks.reshape(cache_k.shape), nvs.reshape(cache_v.shape),
            cs, hs)
```

```python
import functools

import jax
import jax.numpy as jnp
import numpy as np
from jax import lax
from jax.experimental import pallas as pl
from jax.experimental.pallas import tpu as pltpu

D_MODEL = 1024
PAST_LEN = 16384
N_HEADS = 8
HEAD_DIM = 64
N_KV_HEADS = 2
GROUP = N_HEADS // N_KV_HEADS
ATTN_WIDTH = N_HEADS * HEAD_DIM
KV_WIDTH = N_KV_HEADS * HEAD_DIM
WINDOW = 128
ROPE_THETA = 500000.0
ROT_DIM = HEAD_DIM // 4
LRU_WIDTH = D_MODEL // 2
LRU_HEADS = 8
LRU_BLOCK = LRU_WIDTH // LRU_HEADS
CONV_W = 4
LRU_C = 8.0
D_FF = 2816
IN_WIDTH = ATTN_WIDTH + 2 * KV_WIDTH + 2 * LRU_WIDTH
EPS = 1e-6

LANES = 128
SUBLANES = 8
FF_CHUNK = 256
N_FF_CHUNKS = D_FF // FF_CHUNK
FFN_TILE = 512
MIX_TILE = 512
SAMPLE_GROUP = 8
SHIFT_GROUP = 32
CAST_MAX_ROWS = 704
BF16_SUBLANES = 16
V7X_VMEM_BYTES = 64 * 1024 * 1024
VMEM_LIMIT_BYTES = V7X_VMEM_BYTES * 7 // 8

O_K = ATTN_WIDTH
O_V = O_K + KV_WIDTH
O_XR = O_V + KV_WIDTH
O_YG = O_XR + LRU_WIDTH

F32 = jnp.float32
BF16 = jnp.bfloat16
LOG2_E = np.float32(np.log2(np.e))


def _rmsnorm(x, g):
    return (x * lax.rsqrt(jnp.mean(x * x, axis=-1, keepdims=True) + EPS)) * g


def _half_ffn_steps(x, g_pre, wg_ref, wu_ref, wd_ref, g_post, act_scr):
    h = _rmsnorm(x, g_pre).astype(BF16)
    for c in range(N_FF_CHUNKS):
        cols = slice(FF_CHUNK * c, FF_CHUNK * (c + 1))
        g = jnp.dot(h, wg_ref[:, cols], preferred_element_type=F32)
        u = jnp.dot(h, wu_ref[:, cols], preferred_element_type=F32)
        act_scr[:, cols] = ((g * jax.nn.sigmoid(g)) * u).astype(BF16)
        yield
    ys = []
    for c in range(D_MODEL // FF_CHUNK):
        ys.append(jnp.dot(act_scr[...], wd_ref[:, FF_CHUNK * c:FF_CHUNK * (c + 1)],
                          preferred_element_type=F32))
        yield
    return x + _rmsnorm(jnp.concatenate(ys, axis=1), 0.5 * g_post)


def _advance(gen, done):
    if gen in done:
        return
    try:
        next(gen)
    except StopIteration as stop:
        done[gen] = stop.value


def _interleave(first, second):
    done = {}
    while len(done) < 2:
        _advance(first, done)
        _advance(second, done)
    return done[first], done[second]


def _run(gen):
    done = {}
    while gen not in done:
        _advance(gen, done)
    return done[gen]


def _half_ffn(*args):
    return _run(_half_ffn_steps(*args))


def _gelu_tanh_times(x, y):
    c = np.float32(np.sqrt(2.0 / np.pi))
    t = jnp.tanh(x * ((x * x) * (np.float32(0.044715) * c) + c))
    w = (0.5 * x) * y
    return t * w + w


def _softplus(x):
    return jnp.maximum(x, 0.0) + jnp.log1p(jnp.exp(-jnp.abs(x)))


def _lru_gates(xc, wgate_ref):
    return jnp.dot(xc.astype(BF16), wgate_ref[...], preferred_element_type=F32)


def _lru_coeffs(xc, gate_a, gate_x, ba, bx, lam):
    r = jax.nn.sigmoid(gate_a + ba)
    ig = jax.nn.sigmoid(gate_x + bx)
    a = jnp.exp2(r * ((-LRU_C * LOG2_E) * _softplus(-lam)))
    v = 1.0 - a * a
    u = jnp.where(v > 0.0, v * lax.rsqrt(v), 0.0) * (ig * xc)
    return a, u


def _scan_rows(a, u):
    n = a.shape[0]
    row = lax.broadcasted_iota(jnp.int32, a.shape, 0)
    d = 1
    while d < n:
        keep = row >= d
        u = jnp.where(keep, a * pltpu.roll(u, d, 0) + u, u)
        if 2 * d < n:
            a = jnp.where(keep, a * pltpu.roll(a, d, 0), a)
        d *= 2
    return u


def _phase_blocks(slab_ref, j):
    groups = slab_ref.shape[1] // SUBLANES
    return [slab_ref[j, pl.ds(k, groups, stride=SUBLANES), :] for k in range(SUBLANES)]


def _prev_group(block, carry_row):
    first = lax.broadcasted_iota(jnp.int32, block.shape, 0) == 0
    return jnp.where(first, carry_row, pltpu.roll(block, 1, 0))


def _conv_phases(x_phases, carry_rows, w, b):
    wrapped = {SUBLANES - m: _prev_group(x_phases[SUBLANES - m], carry_rows[CONV_W - 1 - m])
               for m in range(1, CONV_W)}
    out = []
    for k in range(SUBLANES):
        acc = b
        for tap in range(CONV_W):
            src = k - (CONV_W - 1 - tap)
            x = x_phases[src] if src >= 0 else wrapped[src + SUBLANES]
            acc = acc + x * w[tap:tap + 1, :]
        out.append(acc)
    return out


def _scan_phases(a, u, h0):
    groups = a.shape[0] // SUBLANES
    blk = lambda v, k: v[groups * k:groups * (k + 1)]
    h_zero = [blk(u, 0)]
    a_prod = [blk(a, 0)]
    for k in range(1, SUBLANES):
        h_zero.append(blk(a, k) * h_zero[-1] + blk(u, k))
        a_prod.append(blk(a, k) * a_prod[-1])
    first = lax.broadcasted_iota(jnp.int32, a_prod[-1].shape, 0) == 0
    h_end = _scan_rows(a_prod[-1], jnp.where(first, a_prod[-1] * h0 + h_zero[-1], h_zero[-1]))
    h_in = _prev_group(h_end, h0)
    return [h_zero[k] + a_prod[k] * h_in for k in range(SUBLANES)]


def _expand_q(q, rows):
    lane = lax.broadcasted_iota(jnp.int32, (rows, LANES), 1)
    pieces = []
    for h in range(N_HEADS):
        kv = h // GROUP
        src = q[:, LANES * (h // 2):LANES * (h // 2 + 1)]
        if (h % 2) != kv:
            src = pltpu.roll(src, HEAD_DIM, 1)
        in_half = (lane >= HEAD_DIM) if kv == 1 else (lane < HEAD_DIM)
        pieces.append(jnp.where(in_half, src, 0.0))
    return jnp.concatenate(pieces, axis=0).astype(BF16)


def _collect_heads(o, rows):
    lane = lax.broadcasted_iota(jnp.int32, (rows, LANES), 1)
    groups = []
    for j in range(N_HEADS // 2):
        kv = (2 * j) // GROUP
        even = o[rows * (2 * j):rows * (2 * j + 1)]
        odd = o[rows * (2 * j + 1):rows * (2 * j + 2)]
        if kv == 1:
            even = pltpu.roll(even, HEAD_DIM, 1)
        else:
            odd = pltpu.roll(odd, HEAD_DIM, 1)
        groups.append(jnp.where(lane < HEAD_DIM, even, odd))
    return jnp.concatenate(groups, axis=1)


def _collect_heads_t(o_t, rows):
    return jnp.concatenate(
        [o_t[HEAD_DIM * (h // GROUP):HEAD_DIM * (h // GROUP + 1), rows * h:rows * (h + 1)]
         for h in range(N_HEADS)], axis=0)


def _sink_row(sinks_ref, layer, rows):
    head = lax.broadcasted_iota(jnp.int32, (1, N_HEADS * rows), 1) // rows
    out = jnp.full((1, N_HEADS * rows), sinks_ref[layer, 0], F32)
    for h in range(1, N_HEADS):
        out = jnp.where(head == h, sinks_ref[layer, h], out)
    return out * LOG2_E


def _sink_column(sinks_ref, layer, rows):
    return jnp.concatenate([jnp.full((rows, 1), sinks_ref[layer, h], F32) for h in range(N_HEADS)],
                           axis=0) * LOG2_E


def _mix_normed(attn, hs, yg, g_attn, g_lru):
    lru = _gelu_tanh_times(yg, hs)
    return jnp.concatenate([_rmsnorm(attn, g_attn), _rmsnorm(lru, g_lru)], axis=-1).astype(BF16)


def _mix_project(x1, mixed, wo_ref, g_postmix):
    y = jnp.dot(mixed, wo_ref[...], preferred_element_type=F32)
    return x1 + _rmsnorm(y, g_postmix)


def _mix_out(x1, attn, hs, yg, g_attn, g_lru, wo_ref, g_postmix):
    return _mix_project(x1, _mix_normed(attn, hs, yg, g_attn, g_lru), wo_ref, g_postmix)


def _layer_spec(shape, layer):
    zeros = (0,) * len(shape)
    return pl.BlockSpec((None,) + tuple(shape), lambda *_: (layer,) + zeros,
                        pipeline_mode=pl.Buffered(1))


def _cast_kernel(w_ref, o_ref):
    o_ref[...] = w_ref[...].astype(BF16)


def _to_bf16(w, layer):
    _, rows, cols = w.shape
    block = next(k for k in range(CAST_MAX_ROWS, 0, -BF16_SUBLANES) if rows % k == 0)
    return pl.pallas_call(
        _cast_kernel,
        grid=(rows // block,),
        in_specs=[pl.BlockSpec((None, block, cols), lambda i: (layer, i, 0))],
        out_specs=pl.BlockSpec((block, cols), lambda i: (i, 0)),
        out_shape=jax.ShapeDtypeStruct((rows, cols), BF16),
        compiler_params=pltpu.CompilerParams(dimension_semantics=("arbitrary",)),
        name="cast_bf16",
    )(w)


def _side_block_rows(rows, n_tiles):
    for k in range(BF16_SUBLANES, rows + 1, BF16_SUBLANES):
        if rows % k == 0 and rows // k <= n_tiles:
            return k
    raise ValueError(f"cannot split {rows} rows over {n_tiles} steps")


def _side_cast_specs(sources, n_tiles):
    in_specs, out_specs, out_shapes = [], [], []
    for w, layer in sources:
        _, rows, cols = w.shape
        br = _side_block_rows(rows, n_tiles)
        block = functools.partial(jnp.minimum, rows // br - 1)
        in_specs.append(pl.BlockSpec((None, br, cols),
                                     lambda g, layer=layer, block=block: (layer, block(g), 0)))
        out_specs.append(pl.BlockSpec((br, cols), lambda g, block=block: (block(g), 0)))
        out_shapes.append(jax.ShapeDtypeStruct((rows, cols), BF16))
    return in_specs, out_specs, out_shapes


def _side_cast(src_refs, dst_refs):
    for src, dst in zip(src_refs, dst_refs):
        dst[...] = src[...].astype(BF16)


def _resident(shape):
    return pl.BlockSpec(shape, lambda *_: (0,) * len(shape))


def _in_proj_steps(x1, g_mix, win_ref, cos, sin_up, sin_dn, z_ref):
    h = _rmsnorm(x1, g_mix).astype(BF16)
    half = ROT_DIM // 2
    for c in range(IN_WIDTH // FF_CHUNK):
        zc = jnp.dot(h, win_ref[:, FF_CHUNK * c:FF_CHUNK * (c + 1)], preferred_element_type=F32)
        for k in range(FF_CHUNK // LANES):
            j = (FF_CHUNK // LANES) * c + k
            zj = zc[:, LANES * k:LANES * (k + 1)]
            if j < O_V // LANES:
                zj = (zj * cos + pltpu.roll(zj, LANES - half, 1) * sin_up
                      + pltpu.roll(zj, half, 1) * sin_dn)
            if j < ATTN_WIDTH // LANES:
                zj = zj * (LOG2_E * HEAD_DIM ** -0.5)
            z_ref[:, LANES * j:LANES * (j + 1)] = zj
        yield


N_LRU_WEIGHTS = 6


def _lru_weight_specs(layer):
    lspec = lambda *shape: _layer_spec(shape, layer)
    return [lspec(CONV_W, LRU_WIDTH), lspec(1, LRU_WIDTH), lspec(LRU_WIDTH, 2 * LRU_WIDTH),
            lspec(1, LRU_WIDTH), lspec(1, LRU_WIDTH), lspec(1, LRU_WIDTH)]


def _lru_weights(w):
    return (w["conv_w"], w["conv_b"], w["w_gate"], w["gate_a_b"], w["gate_x_b"], w["lam"])


def _lru_prompt_steps(z_ref, lru_refs, conv_scr, h_scr, xr_scr, xc_scr, gates_scr, hs_scr):
    conv_w_ref, conv_b_ref, wgate_ref, ba_ref, bx_ref, lam_ref = lru_refs
    tile = z_ref.shape[0]
    groups = tile // SUBLANES
    n_slabs = LRU_WIDTH // LANES
    for j in range(n_slabs):
        ch = slice(LANES * j, LANES * (j + 1))
        xr_scr[j] = z_ref[:, O_XR + LANES * j:O_XR + LANES * (j + 1)]
        carry_rows = [conv_scr[m:m + 1, ch] for m in range(CONV_W - 1)]
        xc_phases = _conv_phases(_phase_blocks(xr_scr, j), carry_rows, conv_w_ref[:, ch],
                                 conv_b_ref[:, ch])
        xc_scr[:, ch] = jnp.concatenate(xc_phases, axis=0)
    last_rows = z_ref[tile - (CONV_W - 1):, O_XR:O_YG]
    conv_scr[...] = last_rows
    gates_scr[...] = _lru_gates(xc_scr[...], wgate_ref)
    yield
    h_last = []
    for j in range(n_slabs):
        ch = slice(LANES * j, LANES * (j + 1))
        gate_x_ch = slice(LRU_WIDTH + LANES * j, LRU_WIDTH + LANES * (j + 1))
        a, u = _lru_coeffs(xc_scr[:, ch], gates_scr[:, ch], gates_scr[:, gate_x_ch],
                           ba_ref[:, ch], bx_ref[:, ch], lam_ref[:, ch])
        yield
        h_phases = _scan_phases(a, u, h_scr[:, ch])
        for k in range(SUBLANES):
            hs_scr[j, pl.ds(k, groups, stride=SUBLANES), :] = h_phases[k]
        h_last.append(h_phases[SUBLANES - 1][groups - 1:groups, :])
        z_ref[:, O_XR + LANES * j:O_XR + LANES * (j + 1)] = hs_scr[j]
        yield
    h_last = jnp.concatenate(h_last, axis=1)
    h_scr[...] = h_last
    return last_rows, h_last


def _ffn_in_kernel(n_seq_tiles, n_side, *refs):
    pipelined = n_seq_tiles > 0
    (x_ref, cos_ref, sin_up_ref, sin_dn_ref, g_pre_ref, wg_ref, wu_ref, wd_ref, g_post_ref,
     g_mix_ref, win_ref) = refs[:11]
    n_lru = N_LRU_WEIGHTS if pipelined else 0
    lru_refs = refs[11:11 + n_lru]
    n_in = 11 + n_lru + n_side
    side_src = refs[11 + n_lru:n_in]
    x1_ref, z_ref = refs[n_in:n_in + 2]
    state_out = refs[n_in + 2:n_in + 2 + (2 if pipelined else 0)]
    n_out = 2 + len(state_out) + n_side
    side_dst = refs[n_in + 2 + len(state_out):n_in + n_out]
    act_scr, *stage_scr = refs[n_in + n_out:]

    def ffn_steps():
        return _half_ffn_steps(x_ref[...], g_pre_ref[...], wg_ref, wu_ref, wd_ref, g_post_ref[...],
                               act_scr)

    def in_proj_steps(x1):
        return _in_proj_steps(x1, g_mix_ref[...], win_ref, cos_ref[...], sin_up_ref[...],
                              sin_dn_ref[...], z_ref)

    if not pipelined:
        x1 = _run(ffn_steps())
        x1_ref[...] = x1
        _run(in_proj_steps(x1))
        return
    x1_scr, conv_scr, h_scr, xr_scr, xc_scr, gates_scr, hs_scr = stage_scr
    nconv_ref, nh_ref = state_out
    g_idx = pl.program_id(0)
    n_tiles = pl.num_programs(0) - 1

    @pl.when(jnp.maximum(g_idx - 1, 0) % n_seq_tiles == 0)
    def _():
        conv_scr[...] = jnp.zeros(conv_scr.shape, F32)
        h_scr[...] = jnp.zeros(h_scr.shape, F32)

    def stage_b_steps():
        yield from in_proj_steps(x1_scr[...])
        last_rows, h_last = yield from _lru_prompt_steps(z_ref, lru_refs, conv_scr, h_scr, xr_scr,
                                                         xc_scr, gates_scr, hs_scr)
        nconv_ref[0] = last_rows
        nh_ref[0] = h_last

    @pl.when(g_idx == 0)
    def _():
        x1_scr[...] = jnp.zeros(x1_scr.shape, F32)

    @pl.when(g_idx < n_tiles)
    def _():
        x1, _ = _interleave(ffn_steps(), stage_b_steps())
        x1_ref[...] = x1
        x1_scr[...] = x1
        _side_cast(side_src, side_dst)

    @pl.when(g_idx == n_tiles)
    def _():
        _run(stage_b_steps())


def _ffn_in_sample(x, tabs, w, big, layer):
    n = x.shape[0]
    lspec = lambda *shape: _layer_spec(shape, layer)
    full = lambda *shape: pl.BlockSpec(shape, lambda g: (0,) * len(shape))
    wg, wu, wd, w_in = big
    return pl.pallas_call(
        functools.partial(_ffn_in_kernel, 0, 0),
        grid=(1,),
        in_specs=[
            full(n, D_MODEL), full(n, LANES), full(n, LANES), full(n, LANES),
            lspec(1, D_MODEL), _resident(wg.shape), _resident(wu.shape), _resident(wd.shape),
            lspec(1, D_MODEL), lspec(1, D_MODEL), _resident(w_in.shape),
        ],
        out_specs=[full(n, D_MODEL), full(n, IN_WIDTH)],
        out_shape=[jax.ShapeDtypeStruct((n, D_MODEL), F32), jax.ShapeDtypeStruct((n, IN_WIDTH), F32)],
        scratch_shapes=[pltpu.VMEM((n, D_FF), BF16)],
        compiler_params=pltpu.CompilerParams(
            dimension_semantics=("arbitrary",), vmem_limit_bytes=VMEM_LIMIT_BYTES),
        name="ffn_in_sample",
    )(x, *tabs, w["g_pre1"], wg, wu, wd, w["g_post1"], w["g_mix"], w_in)


def _ffn_in_prompt(x, tabs, w, big, layer, batch, seq, tile, side=()):
    n_s = seq // tile
    n = batch * n_s
    ffn_tile = lambda g: jnp.minimum(g, n - 1)
    proj_tile = lambda g: jnp.maximum(g - 1, 0)
    lspec = lambda *shape: _layer_spec(shape, layer)
    tab_spec = pl.BlockSpec((tile, LANES), lambda g: (proj_tile(g) % n_s, 0))
    per_batch = lambda rows: pl.BlockSpec((1, rows, LRU_WIDTH), lambda g: (proj_tile(g) // n_s, 0, 0))
    side_in, side_out, side_shapes = _side_cast_specs(side, n)
    wg, wu, wd, w_in = big
    slabs = (LRU_WIDTH // LANES, tile, LANES)
    return pl.pallas_call(
        functools.partial(_ffn_in_kernel, n_s, len(side)),
        grid=(n + 1,),
        in_specs=[
            pl.BlockSpec((tile, D_MODEL), lambda g: (ffn_tile(g), 0)),
            tab_spec, tab_spec, tab_spec,
            lspec(1, D_MODEL), _resident(wg.shape), _resident(wu.shape), _resident(wd.shape),
            lspec(1, D_MODEL), lspec(1, D_MODEL), _resident(w_in.shape),
        ] + _lru_weight_specs(layer) + side_in,
        out_specs=[
            pl.BlockSpec((tile, D_MODEL), lambda g: (ffn_tile(g), 0)),
            pl.BlockSpec((tile, IN_WIDTH), lambda g: (proj_tile(g), 0)),
            per_batch(CONV_W - 1), per_batch(1),
        ] + side_out,
        out_shape=[
            jax.ShapeDtypeStruct((n * tile, D_MODEL), F32),
            jax.ShapeDtypeStruct((n * tile, IN_WIDTH), F32),
            jax.ShapeDtypeStruct((batch, CONV_W - 1, LRU_WIDTH), F32),
            jax.ShapeDtypeStruct((batch, 1, LRU_WIDTH), F32),
        ] + side_shapes,
        scratch_shapes=[
            pltpu.VMEM((tile, D_FF), BF16),
            pltpu.VMEM((tile, D_MODEL), F32),
            pltpu.VMEM((CONV_W - 1, LRU_WIDTH), F32),
            pltpu.VMEM((1, LRU_WIDTH), F32),
            pltpu.VMEM(slabs, F32),
            pltpu.VMEM((tile, LRU_WIDTH), F32),
            pltpu.VMEM((tile, 2 * LRU_WIDTH), F32),
            pltpu.VMEM(slabs, F32),
        ],
        compiler_params=pltpu.CompilerParams(
            dimension_semantics=("arbitrary",), vmem_limit_bytes=VMEM_LIMIT_BYTES),
        name="ffn_in",
    )(x, *tabs, w["g_pre1"], wg, wu, wd, w["g_post1"], w["g_mix"], w_in, *_lru_weights(w),
      *[src for src, _ in side])


def _mix_prompt_kernel(layer, n_seq_tiles, n_side, *refs):
    (sinks_ref, x1_ref, z_ref, g_attn_ref, g_lru_ref, wo_ref, g_postmix_ref, g_pre2_ref, wg_ref,
     wu_ref, wd_ref, g_post2_ref) = refs[:12]
    side_src = refs[12:12 + n_side]
    x3_ref, nk_ref, nv_ref = refs[12 + n_side:15 + n_side]
    side_dst = refs[15 + n_side:15 + 2 * n_side]
    k_scr, vt_scr, x2_scr, bias_scr, attn_scr, act_scr = refs[15 + 2 * n_side:]
    tile = x1_ref.shape[0]
    g_idx = pl.program_id(0)
    n_tiles = pl.num_programs(0) - 1
    s_idx = g_idx % n_seq_tiles

    @pl.when(g_idx == 0)
    def _():
        kj = lax.broadcasted_iota(jnp.int32, (2 * WINDOW, N_HEADS * WINDOW), 0)
        qi = lax.broadcasted_iota(jnp.int32, (2 * WINDOW, N_HEADS * WINDOW), 1) & (WINDOW - 1)
        band = (kj > qi) & (kj <= qi + WINDOW)
        bias_scr[0] = jnp.where(band, 0.0, -jnp.inf)
        bias_scr[1] = jnp.where(band & (kj >= WINDOW), 0.0, -jnp.inf)

    @pl.when(s_idx == 0)
    def _():
        k_scr[0:WINDOW, :] = jnp.zeros((WINDOW, KV_WIDTH), BF16)
        vt_scr[:, 0:WINDOW] = jnp.zeros((KV_WIDTH, WINDOW), BF16)

    def mixer_steps():
        k_scr[WINDOW:, :] = z_ref[:, O_K:O_V].astype(BF16)
        vt_scr[:, WINDOW:] = z_ref[:, O_V:O_XR].T.astype(BF16)
        sink = _sink_row(sinks_ref, layer, WINDOW)
        for i in range(tile // WINDOW):
            q_exp = _expand_q(z_ref[WINDOW * i:WINDOW * (i + 1), 0:ATTN_WIDTH], WINDOW)
            keys = k_scr[WINDOW * i:WINDOW * (i + 2), :]
            vals_t = vt_scr[:, WINDOW * i:WINDOW * (i + 2)]
            s = lax.dot_general(keys, q_exp, (((1,), (1,)), ((), ())), preferred_element_type=F32)
            s = s + bias_scr[jnp.where(s_idx == 0, 1, 0) if i == 0 else 0]
            m = jnp.maximum(jnp.max(s, axis=0, keepdims=True), sink)
            p = jnp.exp2(s - m)
            denom = jnp.sum(p, axis=0, keepdims=True) + jnp.exp2(sink - m)
            yield
            o_t = jnp.dot(vals_t, p.astype(BF16), preferred_element_type=F32) * (1.0 / denom)
            attn_scr[WINDOW * i:WINDOW * (i + 1), :] = _collect_heads_t(o_t, WINDOW).T
            yield
        k_scr[0:WINDOW, :] = k_scr[tile:tile + WINDOW, :]
        vt_scr[:, 0:WINDOW] = vt_scr[:, tile:tile + WINDOW]
        mixed = _mix_normed(attn_scr[...], z_ref[:, O_XR:O_YG], z_ref[:, O_YG:], g_attn_ref[...],
                            g_lru_ref[...])
        yield
        return _mix_project(x1_ref[...], mixed, wo_ref, g_postmix_ref[...])

    def ffn_steps():
        return _half_ffn_steps(x2_scr[...], g_pre2_ref[...], wg_ref, wu_ref, wd_ref,
                               g_post2_ref[...], act_scr)

    def finish_mixer(x2):
        x2_scr[...] = x2
        nk_ref[0] = z_ref[tile - WINDOW:, O_K:O_V]
        nv_ref[0] = z_ref[tile - WINDOW:, O_V:O_XR]
        _side_cast(side_src, side_dst)

    @pl.when(g_idx == 0)
    def _():
        finish_mixer(_run(mixer_steps()))

    @pl.when((g_idx > 0) & (g_idx < n_tiles))
    def _():
        x3, x2 = _interleave(ffn_steps(), mixer_steps())
        x3_ref[...] = x3
        finish_mixer(x2)

    @pl.when(g_idx == n_tiles)
    def _():
        x3_ref[...] = _run(ffn_steps())


def _tail_weight_specs(layer, big):
    lspec = lambda *shape: _layer_spec(shape, layer)
    w_o, wg, wu, wd = big
    return [
        lspec(1, ATTN_WIDTH), lspec(1, LRU_WIDTH), _resident(w_o.shape), lspec(1, D_MODEL),
        lspec(1, D_MODEL), _resident(wg.shape), _resident(wu.shape), _resident(wd.shape),
        lspec(1, D_MODEL),
    ]


def _tail_weights(w, big):
    w_o, wg, wu, wd = big
    return (w["g_attn"], w["g_lru"], w_o, w["g_postmix"], w["g_pre2"], wg, wu, wd, w["g_post2"])


def _mix_prompt(x1, z, w, big, layer, batch, seq, tile, side=()):
    n_s = seq // tile
    n_tiles = batch * n_s
    mix_tile = lambda g: jnp.minimum(g, n_tiles - 1)
    tok = lambda width: pl.BlockSpec((tile, width), lambda g: (mix_tile(g), 0))
    per_batch = lambda rows, width: pl.BlockSpec((1, rows, width), lambda g: (mix_tile(g) // n_s, 0, 0))
    side_in, side_out, side_shapes = _side_cast_specs(side, n_tiles)
    return pl.pallas_call(
        functools.partial(_mix_prompt_kernel, layer, n_s, len(side)),
        grid=(n_tiles + 1,),
        in_specs=[pl.BlockSpec(memory_space=pltpu.SMEM), tok(D_MODEL), tok(IN_WIDTH)]
                 + _tail_weight_specs(layer, big) + side_in,
        out_specs=[
            pl.BlockSpec((tile, D_MODEL), lambda g: (jnp.maximum(g - 1, 0), 0)),
            per_batch(WINDOW, KV_WIDTH), per_batch(WINDOW, KV_WIDTH),
        ] + side_out,
        out_shape=[
            jax.ShapeDtypeStruct((batch * seq, D_MODEL), F32),
            jax.ShapeDtypeStruct((batch, WINDOW, KV_WIDTH), F32),
            jax.ShapeDtypeStruct((batch, WINDOW, KV_WIDTH), F32),
        ] + side_shapes,
        scratch_shapes=[
            pltpu.VMEM((WINDOW + tile, KV_WIDTH), BF16),
            pltpu.VMEM((KV_WIDTH, WINDOW + tile), BF16),
            pltpu.VMEM((tile, D_MODEL), F32),
            pltpu.VMEM((2, 2 * WINDOW, N_HEADS * WINDOW), F32),
            pltpu.VMEM((tile, ATTN_WIDTH), F32),
            pltpu.VMEM((tile, D_FF), BF16),
        ],
        compiler_params=pltpu.CompilerParams(
            dimension_semantics=("arbitrary",), vmem_limit_bytes=VMEM_LIMIT_BYTES),
        name="mix_prompt",
    )(w["sinks"], x1, z, *_tail_weights(w, big), *[src for src, _ in side])


def _attn_sample_kernel(layer, sinks_ref, zq_ref, ck_ref, cv_ref, attn_ref):
    dec_seq, g = zq_ref.shape[0], zq_ref.shape[1]
    rows = dec_seq * g
    cache_len = ck_ref.shape[1]
    zq = zq_ref[...].reshape(rows, zq_ref.shape[2])
    q_exp = _expand_q(zq[:, 0:ATTN_WIDTH], rows)
    k_new = zq[:, O_K:O_V].astype(BF16)
    v_new = zq[:, O_V:O_XR].astype(BF16)
    k_cache = ck_ref[...].reshape(g * cache_len, KV_WIDTH).astype(BF16)
    v_cache = cv_ref[...].reshape(g * cache_len, KV_WIDTH).astype(BF16)
    contract_last = (((1,), (1,)), ((), ()))
    s_c = lax.dot_general(q_exp, k_cache, contract_last, preferred_element_type=F32)
    s_n = lax.dot_general(q_exp, k_new, contract_last, preferred_element_type=F32)

    def row_ids(shape):
        r = lax.broadcasted_iota(jnp.int32, shape, 0)
        return (r % rows) // g, r % g

    t_q, b_q = row_ids(s_c.shape)
    c = lax.broadcasted_iota(jnp.int32, s_c.shape, 1)
    mask_c = ((c // cache_len) == b_q) & ((c % cache_len) > t_q)
    t_q, b_q = row_ids(s_n.shape)
    c = lax.broadcasted_iota(jnp.int32, s_n.shape, 1)
    mask_n = ((c % g) == b_q) & ((c // g) <= t_q)
    s_c = jnp.where(mask_c, s_c, -jnp.inf)
    s_n = jnp.where(mask_n, s_n, -jnp.inf)

    sink = _sink_column(sinks_ref, layer, rows)
    m = jnp.maximum(jnp.maximum(jnp.max(s_c, axis=-1, keepdims=True),
                                jnp.max(s_n, axis=-1, keepdims=True)), sink)
    p_c = jnp.exp2(s_c - m)
    p_n = jnp.exp2(s_n - m)
    denom = (jnp.sum(p_c, axis=-1, keepdims=True) + jnp.sum(p_n, axis=-1, keepdims=True)
             + jnp.exp2(sink - m))
    o = (jnp.dot(p_c.astype(BF16), v_cache, preferred_element_type=F32)
         + jnp.dot(p_n.astype(BF16), v_new, preferred_element_type=F32)) / denom
    attn_ref[...] = _collect_heads(o, rows).reshape(dec_seq, g, ATTN_WIDTH)


def _attn_sample(z3, cache_k, cache_v, sinks, layer):
    dec_seq, dec_batch, _ = z3.shape
    cache_len = cache_k.shape[2]
    g = SAMPLE_GROUP
    cache_spec = pl.BlockSpec((None, g, cache_len, KV_WIDTH), lambda i: (layer, i, 0, 0))
    return pl.pallas_call(
        functools.partial(_attn_sample_kernel, layer),
        grid=(dec_batch // g,),
        in_specs=[
            pl.BlockSpec(memory_space=pltpu.SMEM),
            pl.BlockSpec((dec_seq, g, O_XR), lambda i: (0, i, 0)),
            cache_spec, cache_spec,
        ],
        out_specs=pl.BlockSpec((dec_seq, g, ATTN_WIDTH), lambda i: (0, i, 0)),
        out_shape=jax.ShapeDtypeStruct((dec_seq, dec_batch, ATTN_WIDTH), F32),
        compiler_params=pltpu.CompilerParams(
            dimension_semantics=("arbitrary",), vmem_limit_bytes=VMEM_LIMIT_BYTES),
        name="attn_sample",
    )(sinks, z3, cache_k, cache_v)


def _shift_caches_kernel(kvnew_ref, ck_ref, cv_ref, nk_ref, nv_ref):
    dec_seq = kvnew_ref.shape[1]
    cache_len = ck_ref.shape[1]
    nk_ref[:, 0:cache_len - dec_seq, :] = ck_ref[:, dec_seq:, :]
    nv_ref[:, 0:cache_len - dec_seq, :] = cv_ref[:, dec_seq:, :]
    nk_ref[:, cache_len - dec_seq:, :] = kvnew_ref[:, :, 0:KV_WIDTH]
    nv_ref[:, cache_len - dec_seq:, :] = kvnew_ref[:, :, KV_WIDTH:]


def _shift_caches(kvnew, cache_k, cache_v):
    depth, dec_batch, cache_len, _ = cache_k.shape
    dec_seq = kvnew.shape[2]
    g = SHIFT_GROUP
    spec = lambda rows, width: pl.BlockSpec((None, g, rows, width), lambda l, i: (l, i, 0, 0))
    return pl.pallas_call(
        _shift_caches_kernel,
        grid=(depth, dec_batch // g),
        in_specs=[spec(dec_seq, 2 * KV_WIDTH), spec(cache_len, KV_WIDTH), spec(cache_len, KV_WIDTH)],
        out_specs=[spec(cache_len, KV_WIDTH), spec(cache_len, KV_WIDTH)],
        out_shape=[jax.ShapeDtypeStruct(cache_k.shape, F32), jax.ShapeDtypeStruct(cache_v.shape, F32)],
        compiler_params=pltpu.CompilerParams(dimension_semantics=("arbitrary", "arbitrary")),
        name="shift_caches",
    )(kvnew, cache_k, cache_v)


def _mix_sample_kernel(x1_ref, z_ref, attn_ref, sconv_ref, sh_ref, conv_w_ref, conv_b_ref, wgate_ref,
                       ba_ref, bx_ref, lam_ref, g_attn_ref, g_lru_ref, wo_ref, g_postmix_ref,
                       g_pre2_ref, wg_ref, wu_ref, wd_ref, g_post2_ref,
                       x3_ref, nconv_ref, nh_ref, act_scr):
    nb = sh_ref.shape[0]
    dec_seq = x1_ref.shape[0] // nb
    xr = z_ref[:, O_XR:O_YG]
    xp = [sconv_ref[j] for j in range(CONV_W - 1)] + [xr[nb * t:nb * (t + 1)] for t in range(dec_seq)]
    xc_steps = []
    for t in range(dec_seq):
        acc = conv_b_ref[...] + xp[t] * conv_w_ref[0:1, :]
        for j in range(1, CONV_W):
            acc = acc + xp[t + j] * conv_w_ref[j:j + 1, :]
        xc_steps.append(acc)
    for j in range(CONV_W - 1):
        nconv_ref[j] = xp[dec_seq + j]
    xc = jnp.concatenate(xc_steps, axis=0)
    gates = _lru_gates(xc, wgate_ref)
    a, u = _lru_coeffs(xc, gates[:, :LRU_WIDTH], gates[:, LRU_WIDTH:], ba_ref[...], bx_ref[...],
                       lam_ref[...])
    h = sh_ref[...]
    hs_steps = []
    for t in range(dec_seq):
        h = a[nb * t:nb * (t + 1)] * h + u[nb * t:nb * (t + 1)]
        hs_steps.append(h)
    nh_ref[...] = h
    hs = jnp.concatenate(hs_steps, axis=0)
    x2 = _mix_out(x1_ref[...], attn_ref[...], hs, z_ref[:, O_YG:], g_attn_ref[...], g_lru_ref[...],
                  wo_ref, g_postmix_ref[...])
    x3_ref[...] = _half_ffn(x2, g_pre2_ref[...], wg_ref, wu_ref, wd_ref, g_post2_ref[...], act_scr)


def _mix_sample(x1, z, attn, sconv, state_h, w, big, layer):
    n = x1.shape[0]
    nb = state_h.shape[1]
    full = lambda *shape: pl.BlockSpec(shape, lambda i: (0,) * len(shape))
    return pl.pallas_call(
        _mix_sample_kernel,
        grid=(1,),
        in_specs=[
            full(n, D_MODEL), full(n, IN_WIDTH), full(n, ATTN_WIDTH),
            full(CONV_W - 1, nb, LRU_WIDTH), _layer_spec((nb, LRU_WIDTH), layer),
        ] + _lru_weight_specs(layer) + _tail_weight_specs(layer, big),
        out_specs=[full(n, D_MODEL), full(CONV_W - 1, nb, LRU_WIDTH), full(nb, LRU_WIDTH)],
        out_shape=[
            jax.ShapeDtypeStruct((n, D_MODEL), F32),
            jax.ShapeDtypeStruct((CONV_W - 1, nb, LRU_WIDTH), F32),
            jax.ShapeDtypeStruct((nb, LRU_WIDTH), F32),
        ],
        scratch_shapes=[pltpu.VMEM((n, D_FF), BF16)],
        compiler_params=pltpu.CompilerParams(
            dimension_semantics=("arbitrary",), vmem_limit_bytes=VMEM_LIMIT_BYTES),
        name="mix_sample",
    )(x1, z, attn, sconv, state_h, *_lru_weights(w), *_tail_weights(w, big))


def _rope_tables(pos):
    half = ROT_DIM // 2
    inv = ROPE_THETA ** (-(jnp.arange(half, dtype=F32) * 2.0) / ROT_DIM)
    ang = pos.astype(F32)[:, None] * inv[None, :]
    cos, sin = jnp.cos(ang), jnp.sin(ang)
    n = pos.shape[0]
    rest = jnp.zeros((n, HEAD_DIM - ROT_DIM), F32)
    zero = jnp.zeros((n, half), F32)
    per_head = lambda parts: jnp.tile(jnp.concatenate(parts, axis=-1), (1, LANES // HEAD_DIM))
    return (per_head([cos, cos, rest + 1.0]), per_head([-sin, zero, rest]), per_head([zero, sin, rest]))


def _block_diag(w):
    eye = jnp.eye(LRU_HEADS, dtype=w.dtype)
    return jnp.einsum("lhij,hg->lhigj", w, eye).reshape(w.shape[0], LRU_WIDTH, LRU_WIDTH)


def kernel(x_prompt, x_sample, cache_k, cache_v, state_conv, state_h, norm_pre_ffn1, ffn1_w_gate, ffn1_w_up, ffn1_w_down, norm_post_ffn1, norm_pre_mix, w_in, sinks, conv_w, conv_b, gate_a_w, gate_a_b, gate_x_w, gate_x_b, lru_lambda, attn_out_norm, lru_out_norm, w_o, norm_post_mix, norm_pre_ffn2, ffn2_w_gate, ffn2_w_up, ffn2_w_down, norm_post_ffn2):
    batch, seq, _ = x_prompt.shape
    dec_batch, dec_seq, _ = x_sample.shape
    depth = w_in.shape[0]
    cache_len = cache_k.shape[2]
    n_s = dec_seq * dec_batch
    assert seq % FFN_TILE == 0 and n_s % FFN_TILE == 0
    assert seq % MIX_TILE == 0 and MIX_TILE % WINDOW == 0
    assert dec_batch % SAMPLE_GROUP == 0 and dec_batch % SHIFT_GROUP == 0

    tabs_p = _rope_tables(jnp.arange(seq))
    tabs_s = _rope_tables(jnp.repeat(PAST_LEN + jnp.arange(dec_seq), dec_batch))
    xp = x_prompt.reshape(batch * seq, D_MODEL)
    xs = jnp.swapaxes(x_sample, 0, 1).reshape(n_s, D_MODEL)
    rows = lambda v: v.reshape(depth, 1, -1)

    w = dict(
        g_pre1=rows(norm_pre_ffn1), g_post1=rows(norm_post_ffn1), g_mix=rows(norm_pre_mix),
        sinks=sinks, conv_w=conv_w, conv_b=rows(conv_b),
        w_gate=jnp.concatenate([_block_diag(gate_a_w), _block_diag(gate_x_w)], axis=2).astype(BF16),
        gate_a_b=rows(gate_a_b), gate_x_b=rows(gate_x_b), lam=rows(lru_lambda),
        g_attn=rows(attn_out_norm), g_lru=rows(lru_out_norm),
        g_postmix=rows(norm_post_mix), g_pre2=rows(norm_pre_ffn2), g_post2=rows(norm_post_ffn2),
    )
    ffn_in_src = (ffn1_w_gate, ffn1_w_up, ffn1_w_down, w_in)
    mix_src = (w_o, ffn2_w_gate, ffn2_w_up, ffn2_w_down)
    ck = cache_k.reshape(depth, dec_batch, cache_len, KV_WIDTH)
    cv = cache_v.reshape(depth, dec_batch, cache_len, KV_WIDTH)

    outs = [[] for _ in range(6)]
    kvnew = []
    ffn_in_big = [_to_bf16(s, 0) for s in ffn_in_src]
    for l in range(depth):
        x1p, zp, nc, nh, *mix_big = _ffn_in_prompt(xp, tabs_p, w, ffn_in_big, l, batch, seq, FFN_TILE,
                                                   side=[(s, l) for s in mix_src])
        xp, nk, nv, *next_ffn_in_big = _mix_prompt(
            x1p, zp, w, mix_big, l, batch, seq, MIX_TILE,
            side=[(s, l + 1) for s in ffn_in_src if l + 1 < depth])

        x1s, zs = _ffn_in_sample(xs, tabs_s, w, ffn_in_big, l)
        z3 = zs.reshape(dec_seq, dec_batch, IN_WIDTH)
        kvnew.append(jnp.swapaxes(z3[:, :, O_K:O_XR], 0, 1))
        attn_s = _attn_sample(z3, ck, cv, sinks, l)
        xs, ncs, nhs = _mix_sample(x1s, zs, attn_s.reshape(n_s, ATTN_WIDTH),
                                   jnp.swapaxes(state_conv[l], 0, 1), state_h, w, mix_big, l)
        ffn_in_big = next_ffn_in_big
        kv_p = (batch, WINDOW, N_KV_HEADS, HEAD_DIM)
        for acc, val in zip(outs, (nk.reshape(kv_p), nv.reshape(kv_p), nc,
                                   nh.reshape(batch, LRU_WIDTH), jnp.swapaxes(ncs, 0, 1), nhs)):
            acc.append(val)

    nks, nvs = _shift_caches(jnp.stack(kvnew), ck, cv)
    y_prompt = xp.reshape(batch, seq, D_MODEL)
    y_sample = jnp.swapaxes(xs.reshape(dec_seq, dec_batch, D_MODEL), 0, 1)
    kp, vp, cp, hp, cs, hs = (jnp.stack(o) for o in outs)
    return (y_prompt, y_sample, kp, vp, cp, hp, nks.reshape(cache_k.shape), nvs.reshape(cache_v.shape),
            cs, hs)
```

```python
import functools

import jax
import jax.numpy as jnp
import numpy as np
from jax import lax
from jax.experimental import pallas as pl
from jax.experimental.pallas import tpu as pltpu

D_MODEL = 1024
PAST_LEN = 16384
N_HEADS = 8
HEAD_DIM = 64
N_KV_HEADS = 2
GROUP = N_HEADS // N_KV_HEADS
ATTN_WIDTH = N_HEADS * HEAD_DIM
KV_WIDTH = N_KV_HEADS * HEAD_DIM
WINDOW = 128
ROPE_THETA = 500000.0
ROT_DIM = HEAD_DIM // 4
LRU_WIDTH = D_MODEL // 2
LRU_HEADS = 8
LRU_BLOCK = LRU_WIDTH // LRU_HEADS
CONV_W = 4
LRU_C = 8.0
D_FF = 2816
IN_WIDTH = ATTN_WIDTH + 2 * KV_WIDTH + 2 * LRU_WIDTH
EPS = 1e-6

LANES = 128
SUBLANES = 8
FF_CHUNK = 256
N_FF_CHUNKS = D_FF // FF_CHUNK
FFN_TILE = 512
MIX_TILE = 512
SAMPLE_GROUP = 8
CAST_MAX_ROWS = 704
BF16_SUBLANES = 16
V7X_VMEM_BYTES = 64 * 1024 * 1024
VMEM_LIMIT_BYTES = V7X_VMEM_BYTES * 7 // 8

O_K = ATTN_WIDTH
O_V = O_K + KV_WIDTH
O_XR = O_V + KV_WIDTH
O_YG = O_XR + LRU_WIDTH

F32 = jnp.float32
BF16 = jnp.bfloat16
LOG2_E = np.float32(np.log2(np.e))


def _rmsnorm(x, g):
    return (x * lax.rsqrt(jnp.mean(x * x, axis=-1, keepdims=True) + EPS)) * g


def _half_ffn_steps(x, g_pre, wg_ref, wu_ref, wd_ref, g_post, act_scr):
    h = _rmsnorm(x, g_pre).astype(BF16)
    for c in range(N_FF_CHUNKS):
        cols = slice(FF_CHUNK * c, FF_CHUNK * (c + 1))
        g = jnp.dot(h, wg_ref[:, cols], preferred_element_type=F32)
        u = jnp.dot(h, wu_ref[:, cols], preferred_element_type=F32)
        act_scr[:, cols] = ((g * jax.nn.sigmoid(g)) * u).astype(BF16)
        yield
    ys = []
    for c in range(D_MODEL // FF_CHUNK):
        ys.append(jnp.dot(act_scr[...], wd_ref[:, FF_CHUNK * c:FF_CHUNK * (c + 1)],
                          preferred_element_type=F32))
        yield
    return x + _rmsnorm(jnp.concatenate(ys, axis=1), 0.5 * g_post)


def _advance(gen, done):
    if gen in done:
        return
    try:
        next(gen)
    except StopIteration as stop:
        done[gen] = stop.value


def _interleave(first, second):
    done = {}
    while len(done) < 2:
        _advance(first, done)
        _advance(second, done)
    return done[first], done[second]


def _run(gen):
    done = {}
    while gen not in done:
        _advance(gen, done)
    return done[gen]


def _half_ffn(*args):
    return _run(_half_ffn_steps(*args))


def _gelu_tanh_times(x, y):
    c = np.float32(np.sqrt(2.0 / np.pi))
    t = jnp.tanh(x * ((x * x) * (np.float32(0.044715) * c) + c))
    w = (0.5 * x) * y
    return t * w + w


def _softplus(x):
    return jnp.maximum(x, 0.0) + jnp.log1p(jnp.exp(-jnp.abs(x)))


def _lru_gates(xc, wgate_ref):
    return jnp.dot(xc.astype(BF16), wgate_ref[...], preferred_element_type=F32)


def _lru_coeffs(xc, gate_a, gate_x, ba, bx, lam):
    r = jax.nn.sigmoid(gate_a + ba)
    ig = jax.nn.sigmoid(gate_x + bx)
    a = jnp.exp2(r * ((-LRU_C * LOG2_E) * _softplus(-lam)))
    v = 1.0 - a * a
    u = jnp.where(v > 0.0, v * lax.rsqrt(v), 0.0) * (ig * xc)
    return a, u


def _scan_rows(a, u):
    n = a.shape[0]
    row = lax.broadcasted_iota(jnp.int32, a.shape, 0)
    d = 1
    while d < n:
        keep = row >= d
        u = jnp.where(keep, a * pltpu.roll(u, d, 0) + u, u)
        if 2 * d < n:
            a = jnp.where(keep, a * pltpu.roll(a, d, 0), a)
        d *= 2
    return u


def _phase_blocks(slab_ref, j):
    groups = slab_ref.shape[1] // SUBLANES
    return [slab_ref[j, pl.ds(k, groups, stride=SUBLANES), :] for k in range(SUBLANES)]


def _prev_group(block, carry_row):
    first = lax.broadcasted_iota(jnp.int32, block.shape, 0) == 0
    return jnp.where(first, carry_row, pltpu.roll(block, 1, 0))


def _conv_phases(x_phases, carry_rows, w, b):
    wrapped = {SUBLANES - m: _prev_group(x_phases[SUBLANES - m], carry_rows[CONV_W - 1 - m])
               for m in range(1, CONV_W)}
    out = []
    for k in range(SUBLANES):
        acc = b
        for tap in range(CONV_W):
            src = k - (CONV_W - 1 - tap)
            x = x_phases[src] if src >= 0 else wrapped[src + SUBLANES]
            acc = acc + x * w[tap:tap + 1, :]
        out.append(acc)
    return out


def _scan_phases(a, u, h0):
    groups = a.shape[0] // SUBLANES
    blk = lambda v, k: v[groups * k:groups * (k + 1)]
    h_zero = [blk(u, 0)]
    a_prod = [blk(a, 0)]
    for k in range(1, SUBLANES):
        h_zero.append(blk(a, k) * h_zero[-1] + blk(u, k))
        a_prod.append(blk(a, k) * a_prod[-1])
    first = lax.broadcasted_iota(jnp.int32, a_prod[-1].shape, 0) == 0
    h_end = _scan_rows(a_prod[-1], jnp.where(first, a_prod[-1] * h0 + h_zero[-1], h_zero[-1]))
    h_in = _prev_group(h_end, h0)
    return [h_zero[k] + a_prod[k] * h_in for k in range(SUBLANES)]


def _expand_q(q, rows):
    lane = lax.broadcasted_iota(jnp.int32, (rows, LANES), 1)
    pieces = []
    for h in range(N_HEADS):
        kv = h // GROUP
        src = q[:, LANES * (h // 2):LANES * (h // 2 + 1)]
        if (h % 2) != kv:
            src = pltpu.roll(src, HEAD_DIM, 1)
        in_half = (lane >= HEAD_DIM) if kv == 1 else (lane < HEAD_DIM)
        pieces.append(jnp.where(in_half, src, 0.0))
    return jnp.concatenate(pieces, axis=0).astype(BF16)


def _collect_heads(o, rows):
    lane = lax.broadcasted_iota(jnp.int32, (rows, LANES), 1)
    groups = []
    for j in range(N_HEADS // 2):
        kv = (2 * j) // GROUP
        even = o[rows * (2 * j):rows * (2 * j + 1)]
        odd = o[rows * (2 * j + 1):rows * (2 * j + 2)]
        if kv == 1:
            even = pltpu.roll(even, HEAD_DIM, 1)
        else:
            odd = pltpu.roll(odd, HEAD_DIM, 1)
        groups.append(jnp.where(lane < HEAD_DIM, even, odd))
    return jnp.concatenate(groups, axis=1)


def _collect_heads_t(o_t, rows):
    return jnp.concatenate(
        [o_t[HEAD_DIM * (h // GROUP):HEAD_DIM * (h // GROUP + 1), rows * h:rows * (h + 1)]
         for h in range(N_HEADS)], axis=0)


def _sink_row(sinks_ref, layer, rows):
    head = lax.broadcasted_iota(jnp.int32, (1, N_HEADS * rows), 1) // rows
    out = jnp.full((1, N_HEADS * rows), sinks_ref[layer, 0], F32)
    for h in range(1, N_HEADS):
        out = jnp.where(head == h, sinks_ref[layer, h], out)
    return out * LOG2_E


def _sink_column(sinks_ref, layer, rows):
    return jnp.concatenate([jnp.full((rows, 1), sinks_ref[layer, h], F32) for h in range(N_HEADS)],
                           axis=0) * LOG2_E


def _mix_normed(attn, hs, yg, g_attn, g_lru):
    lru = _gelu_tanh_times(yg, hs)
    return jnp.concatenate([_rmsnorm(attn, g_attn), _rmsnorm(lru, g_lru)], axis=-1).astype(BF16)


def _mix_project(x1, mixed, wo_ref, g_postmix):
    y = jnp.dot(mixed, wo_ref[...], preferred_element_type=F32)
    return x1 + _rmsnorm(y, g_postmix)


def _mix_out(x1, attn, hs, yg, g_attn, g_lru, wo_ref, g_postmix):
    return _mix_project(x1, _mix_normed(attn, hs, yg, g_attn, g_lru), wo_ref, g_postmix)


def _layer_spec(shape, layer):
    zeros = (0,) * len(shape)
    return pl.BlockSpec((None,) + tuple(shape), lambda *_: (layer,) + zeros,
                        pipeline_mode=pl.Buffered(1))


def _cast_kernel(w_ref, o_ref):
    o_ref[...] = w_ref[...].astype(BF16)


def _to_bf16(w, layer):
    _, rows, cols = w.shape
    block = next(k for k in range(CAST_MAX_ROWS, 0, -BF16_SUBLANES) if rows % k == 0)
    return pl.pallas_call(
        _cast_kernel,
        grid=(rows // block,),
        in_specs=[pl.BlockSpec((None, block, cols), lambda i: (layer, i, 0))],
        out_specs=pl.BlockSpec((block, cols), lambda i: (i, 0)),
        out_shape=jax.ShapeDtypeStruct((rows, cols), BF16),
        compiler_params=pltpu.CompilerParams(dimension_semantics=("arbitrary",)),
        name="cast_bf16",
    )(w)


def _side_block_rows(rows, n_tiles):
    for k in range(BF16_SUBLANES, rows + 1, BF16_SUBLANES):
        if rows % k == 0 and rows // k <= n_tiles:
            return k
    raise ValueError(f"cannot split {rows} rows over {n_tiles} steps")


def _side_cast_specs(sources, n_tiles):
    in_specs, out_specs, out_shapes = [], [], []
    for w, layer in sources:
        _, rows, cols = w.shape
        br = _side_block_rows(rows, n_tiles)
        block = functools.partial(jnp.minimum, rows // br - 1)
        in_specs.append(pl.BlockSpec((None, br, cols),
                                     lambda g, layer=layer, block=block: (layer, block(g), 0)))
        out_specs.append(pl.BlockSpec((br, cols), lambda g, block=block: (block(g), 0)))
        out_shapes.append(jax.ShapeDtypeStruct((rows, cols), BF16))
    return in_specs, out_specs, out_shapes


def _side_cast(src_refs, dst_refs):
    for src, dst in zip(src_refs, dst_refs):
        dst[...] = src[...].astype(BF16)


def _resident(shape):
    return pl.BlockSpec(shape, lambda *_: (0,) * len(shape))


def _in_proj_steps(x1, g_mix, win_ref, cos, sin_up, sin_dn, z_ref):
    h = _rmsnorm(x1, g_mix).astype(BF16)
    half = ROT_DIM // 2
    for c in range(IN_WIDTH // FF_CHUNK):
        zc = jnp.dot(h, win_ref[:, FF_CHUNK * c:FF_CHUNK * (c + 1)], preferred_element_type=F32)
        for k in range(FF_CHUNK // LANES):
            j = (FF_CHUNK // LANES) * c + k
            zj = zc[:, LANES * k:LANES * (k + 1)]
            if j < O_V // LANES:
                zj = (zj * cos + pltpu.roll(zj, LANES - half, 1) * sin_up
                      + pltpu.roll(zj, half, 1) * sin_dn)
            if j < ATTN_WIDTH // LANES:
                zj = zj * (LOG2_E * HEAD_DIM ** -0.5)
            z_ref[:, LANES * j:LANES * (j + 1)] = zj
        yield


N_LRU_WEIGHTS = 6


def _lru_weight_specs(layer):
    lspec = lambda *shape: _layer_spec(shape, layer)
    return [lspec(CONV_W, LRU_WIDTH), lspec(1, LRU_WIDTH), lspec(LRU_WIDTH, 2 * LRU_WIDTH),
            lspec(1, LRU_WIDTH), lspec(1, LRU_WIDTH), lspec(1, LRU_WIDTH)]


def _lru_weights(w):
    return (w["conv_w"], w["conv_b"], w["w_gate"], w["gate_a_b"], w["gate_x_b"], w["lam"])


def _lru_prompt_steps(z_ref, lru_refs, conv_scr, h_scr, xr_scr, xc_scr, gates_scr, hs_scr):
    conv_w_ref, conv_b_ref, wgate_ref, ba_ref, bx_ref, lam_ref = lru_refs
    tile = z_ref.shape[0]
    groups = tile // SUBLANES
    n_slabs = LRU_WIDTH // LANES
    for j in range(n_slabs):
        ch = slice(LANES * j, LANES * (j + 1))
        xr_scr[j] = z_ref[:, O_XR + LANES * j:O_XR + LANES * (j + 1)]
        carry_rows = [conv_scr[m:m + 1, ch] for m in range(CONV_W - 1)]
        xc_phases = _conv_phases(_phase_blocks(xr_scr, j), carry_rows, conv_w_ref[:, ch],
                                 conv_b_ref[:, ch])
        xc_scr[:, ch] = jnp.concatenate(xc_phases, axis=0)
    last_rows = z_ref[tile - (CONV_W - 1):, O_XR:O_YG]
    conv_scr[...] = last_rows
    gates_scr[...] = _lru_gates(xc_scr[...], wgate_ref)
    yield
    h_last = []
    for j in range(n_slabs):
        ch = slice(LANES * j, LANES * (j + 1))
        gate_x_ch = slice(LRU_WIDTH + LANES * j, LRU_WIDTH + LANES * (j + 1))
        a, u = _lru_coeffs(xc_scr[:, ch], gates_scr[:, ch], gates_scr[:, gate_x_ch],
                           ba_ref[:, ch], bx_ref[:, ch], lam_ref[:, ch])
        yield
        h_phases = _scan_phases(a, u, h_scr[:, ch])
        for k in range(SUBLANES):
            hs_scr[j, pl.ds(k, groups, stride=SUBLANES), :] = h_phases[k]
        h_last.append(h_phases[SUBLANES - 1][groups - 1:groups, :])
        z_ref[:, O_XR + LANES * j:O_XR + LANES * (j + 1)] = hs_scr[j]
        yield
    h_last = jnp.concatenate(h_last, axis=1)
    h_scr[...] = h_last
    return last_rows, h_last


def _ffn_in_kernel(n_seq_tiles, n_side, *refs):
    pipelined = n_seq_tiles > 0
    (x_ref, cos_ref, sin_up_ref, sin_dn_ref, g_pre_ref, wg_ref, wu_ref, wd_ref, g_post_ref,
     g_mix_ref, win_ref) = refs[:11]
    n_lru = N_LRU_WEIGHTS if pipelined else 0
    lru_refs = refs[11:11 + n_lru]
    n_in = 11 + n_lru + n_side
    side_src = refs[11 + n_lru:n_in]
    x1_ref, z_ref = refs[n_in:n_in + 2]
    state_out = refs[n_in + 2:n_in + 2 + (2 if pipelined else 0)]
    n_out = 2 + len(state_out) + n_side
    side_dst = refs[n_in + 2 + len(state_out):n_in + n_out]
    act_scr, *stage_scr = refs[n_in + n_out:]

    def ffn_steps():
        return _half_ffn_steps(x_ref[...], g_pre_ref[...], wg_ref, wu_ref, wd_ref, g_post_ref[...],
                               act_scr)

    def in_proj_steps(x1):
        return _in_proj_steps(x1, g_mix_ref[...], win_ref, cos_ref[...], sin_up_ref[...],
                              sin_dn_ref[...], z_ref)

    if not pipelined:
        x1 = _run(ffn_steps())
        x1_ref[...] = x1
        _run(in_proj_steps(x1))
        return
    x1_scr, conv_scr, h_scr, xr_scr, xc_scr, gates_scr, hs_scr = stage_scr
    nconv_ref, nh_ref = state_out
    g_idx = pl.program_id(0)
    n_tiles = pl.num_programs(0) - 1

    @pl.when(jnp.maximum(g_idx - 1, 0) % n_seq_tiles == 0)
    def _():
        conv_scr[...] = jnp.zeros(conv_scr.shape, F32)
        h_scr[...] = jnp.zeros(h_scr.shape, F32)

    def stage_b_steps():
        yield from in_proj_steps(x1_scr[...])
        last_rows, h_last = yield from _lru_prompt_steps(z_ref, lru_refs, conv_scr, h_scr, xr_scr,
                                                         xc_scr, gates_scr, hs_scr)
        nconv_ref[0] = last_rows
        nh_ref[0] = h_last

    @pl.when(g_idx == 0)
    def _():
        x1_scr[...] = jnp.zeros(x1_scr.shape, F32)

    @pl.when(g_idx < n_tiles)
    def _():
        x1, _ = _interleave(ffn_steps(), stage_b_steps())
        x1_ref[...] = x1
        x1_scr[...] = x1
        _side_cast(side_src, side_dst)

    @pl.when(g_idx == n_tiles)
    def _():
        _run(stage_b_steps())


def _ffn_in_sample(x, tabs, w, big, layer):
    n = x.shape[0]
    lspec = lambda *shape: _layer_spec(shape, layer)
    full = lambda *shape: pl.BlockSpec(shape, lambda g: (0,) * len(shape))
    wg, wu, wd, w_in = big
    return pl.pallas_call(
        functools.partial(_ffn_in_kernel, 0, 0),
        grid=(1,),
        in_specs=[
            full(n, D_MODEL), full(n, LANES), full(n, LANES), full(n, LANES),
            lspec(1, D_MODEL), _resident(wg.shape), _resident(wu.shape), _resident(wd.shape),
            lspec(1, D_MODEL), lspec(1, D_MODEL), _resident(w_in.shape),
        ],
        out_specs=[full(n, D_MODEL), full(n, IN_WIDTH)],
        out_shape=[jax.ShapeDtypeStruct((n, D_MODEL), F32), jax.ShapeDtypeStruct((n, IN_WIDTH), F32)],
        scratch_shapes=[pltpu.VMEM((n, D_FF), BF16)],
        compiler_params=pltpu.CompilerParams(
            dimension_semantics=("arbitrary",), vmem_limit_bytes=VMEM_LIMIT_BYTES),
        name="ffn_in_sample",
    )(x, *tabs, w["g_pre1"], wg, wu, wd, w["g_post1"], w["g_mix"], w_in)


def _ffn_in_prompt(x, tabs, w, big, layer, batch, seq, tile, side=()):
    n_s = seq // tile
    n = batch * n_s
    ffn_tile = lambda g: jnp.minimum(g, n - 1)
    proj_tile = lambda g: jnp.maximum(g - 1, 0)
    lspec = lambda *shape: _layer_spec(shape, layer)
    tab_spec = pl.BlockSpec((tile, LANES), lambda g: (proj_tile(g) % n_s, 0))
    per_batch = lambda rows: pl.BlockSpec((1, rows, LRU_WIDTH), lambda g: (proj_tile(g) // n_s, 0, 0))
    side_in, side_out, side_shapes = _side_cast_specs(side, n)
    wg, wu, wd, w_in = big
    slabs = (LRU_WIDTH // LANES, tile, LANES)
    return pl.pallas_call(
        functools.partial(_ffn_in_kernel, n_s, len(side)),
        grid=(n + 1,),
        in_specs=[
            pl.BlockSpec((tile, D_MODEL), lambda g: (ffn_tile(g), 0)),
            tab_spec, tab_spec, tab_spec,
            lspec(1, D_MODEL), _resident(wg.shape), _resident(wu.shape), _resident(wd.shape),
            lspec(1, D_MODEL), lspec(1, D_MODEL), _resident(w_in.shape),
        ] + _lru_weight_specs(layer) + side_in,
        out_specs=[
            pl.BlockSpec((tile, D_MODEL), lambda g: (ffn_tile(g), 0)),
            pl.BlockSpec((tile, IN_WIDTH), lambda g: (proj_tile(g), 0)),
            per_batch(CONV_W - 1), per_batch(1),
        ] + side_out,
        out_shape=[
            jax.ShapeDtypeStruct((n * tile, D_MODEL), F32),
            jax.ShapeDtypeStruct((n * tile, IN_WIDTH), F32),
            jax.ShapeDtypeStruct((batch, CONV_W - 1, LRU_WIDTH), F32),
            jax.ShapeDtypeStruct((batch, 1, LRU_WIDTH), F32),
        ] + side_shapes,
        scratch_shapes=[
            pltpu.VMEM((tile, D_FF), BF16),
            pltpu.VMEM((tile, D_MODEL), F32),
            pltpu.VMEM((CONV_W - 1, LRU_WIDTH), F32),
            pltpu.VMEM((1, LRU_WIDTH), F32),
            pltpu.VMEM(slabs, F32),
            pltpu.VMEM((tile, LRU_WIDTH), F32),
            pltpu.VMEM((tile, 2 * LRU_WIDTH), F32),
            pltpu.VMEM(slabs, F32),
        ],
        compiler_params=pltpu.CompilerParams(
            dimension_semantics=("arbitrary",), vmem_limit_bytes=VMEM_LIMIT_BYTES),
        name="ffn_in",
    )(x, *tabs, w["g_pre1"], wg, wu, wd, w["g_post1"], w["g_mix"], w_in, *_lru_weights(w),
      *[src for src, _ in side])


def _mix_prompt_kernel(layer, n_seq_tiles, n_side, *refs):
    (sinks_ref, x1_ref, z_ref, g_attn_ref, g_lru_ref, wo_ref, g_postmix_ref, g_pre2_ref, wg_ref,
     wu_ref, wd_ref, g_post2_ref) = refs[:12]
    side_src = refs[12:12 + n_side]
    x3_ref, nk_ref, nv_ref = refs[12 + n_side:15 + n_side]
    side_dst = refs[15 + n_side:15 + 2 * n_side]
    k_scr, vt_scr, x2_scr, bias_scr, attn_scr, act_scr = refs[15 + 2 * n_side:]
    tile = x1_ref.shape[0]
    g_idx = pl.program_id(0)
    n_tiles = pl.num_programs(0) - 1
    s_idx = g_idx % n_seq_tiles

    @pl.when(g_idx == 0)
    def _():
        kj = lax.broadcasted_iota(jnp.int32, (2 * WINDOW, N_HEADS * WINDOW), 0)
        qi = lax.broadcasted_iota(jnp.int32, (2 * WINDOW, N_HEADS * WINDOW), 1) & (WINDOW - 1)
        band = (kj > qi) & (kj <= qi + WINDOW)
        bias_scr[0] = jnp.where(band, 0.0, -jnp.inf)
        bias_scr[1] = jnp.where(band & (kj >= WINDOW), 0.0, -jnp.inf)

    @pl.when(s_idx == 0)
    def _():
        k_scr[0:WINDOW, :] = jnp.zeros((WINDOW, KV_WIDTH), BF16)
        vt_scr[:, 0:WINDOW] = jnp.zeros((KV_WIDTH, WINDOW), BF16)

    def mixer_steps():
        k_scr[WINDOW:, :] = z_ref[:, O_K:O_V].astype(BF16)
        vt_scr[:, WINDOW:] = z_ref[:, O_V:O_XR].T.astype(BF16)
        sink = _sink_row(sinks_ref, layer, WINDOW)
        for i in range(tile // WINDOW):
            q_exp = _expand_q(z_ref[WINDOW * i:WINDOW * (i + 1), 0:ATTN_WIDTH], WINDOW)
            keys = k_scr[WINDOW * i:WINDOW * (i + 2), :]
            vals_t = vt_scr[:, WINDOW * i:WINDOW * (i + 2)]
            s = lax.dot_general(keys, q_exp, (((1,), (1,)), ((), ())), preferred_element_type=F32)
            s = s + bias_scr[jnp.where(s_idx == 0, 1, 0) if i == 0 else 0]
            m = jnp.maximum(jnp.max(s, axis=0, keepdims=True), sink)
            p = jnp.exp2(s - m)
            denom = jnp.sum(p, axis=0, keepdims=True) + jnp.exp2(sink - m)
            yield
            o_t = jnp.dot(vals_t, p.astype(BF16), preferred_element_type=F32) * (1.0 / denom)
            attn_scr[WINDOW * i:WINDOW * (i + 1), :] = _collect_heads_t(o_t, WINDOW).T
            yield
        k_scr[0:WINDOW, :] = k_scr[tile:tile + WINDOW, :]
        vt_scr[:, 0:WINDOW] = vt_scr[:, tile:tile + WINDOW]
        mixed = _mix_normed(attn_scr[...], z_ref[:, O_XR:O_YG], z_ref[:, O_YG:], g_attn_ref[...],
                            g_lru_ref[...])
        yield
        return _mix_project(x1_ref[...], mixed, wo_ref, g_postmix_ref[...])

    def ffn_steps():
        return _half_ffn_steps(x2_scr[...], g_pre2_ref[...], wg_ref, wu_ref, wd_ref,
                               g_post2_ref[...], act_scr)

    def finish_mixer(x2):
        x2_scr[...] = x2
        nk_ref[0] = z_ref[tile - WINDOW:, O_K:O_V]
        nv_ref[0] = z_ref[tile - WINDOW:, O_V:O_XR]
        _side_cast(side_src, side_dst)

    @pl.when(g_idx == 0)
    def _():
        finish_mixer(_run(mixer_steps()))

    @pl.when((g_idx > 0) & (g_idx < n_tiles))
    def _():
        x3, x2 = _interleave(ffn_steps(), mixer_steps())
        x3_ref[...] = x3
        finish_mixer(x2)

    @pl.when(g_idx == n_tiles)
    def _():
        x3_ref[...] = _run(ffn_steps())


def _tail_weight_specs(layer, big):
    lspec = lambda *shape: _layer_spec(shape, layer)
    w_o, wg, wu, wd = big
    return [
        lspec(1, ATTN_WIDTH), lspec(1, LRU_WIDTH), _resident(w_o.shape), lspec(1, D_MODEL),
        lspec(1, D_MODEL), _resident(wg.shape), _resident(wu.shape), _resident(wd.shape),
        lspec(1, D_MODEL),
    ]


def _tail_weights(w, big):
    w_o, wg, wu, wd = big
    return (w["g_attn"], w["g_lru"], w_o, w["g_postmix"], w["g_pre2"], wg, wu, wd, w["g_post2"])


def _mix_prompt(x1, z, w, big, layer, batch, seq, tile, side=()):
    n_s = seq // tile
    n_tiles = batch * n_s
    mix_tile = lambda g: jnp.minimum(g, n_tiles - 1)
    tok = lambda width: pl.BlockSpec((tile, width), lambda g: (mix_tile(g), 0))
    per_batch = lambda rows, width: pl.BlockSpec((1, rows, width), lambda g: (mix_tile(g) // n_s, 0, 0))
    side_in, side_out, side_shapes = _side_cast_specs(side, n_tiles)
    return pl.pallas_call(
        functools.partial(_mix_prompt_kernel, layer, n_s, len(side)),
        grid=(n_tiles + 1,),
        in_specs=[pl.BlockSpec(memory_space=pltpu.SMEM), tok(D_MODEL), tok(IN_WIDTH)]
                 + _tail_weight_specs(layer, big) + side_in,
        out_specs=[
            pl.BlockSpec((tile, D_MODEL), lambda g: (jnp.maximum(g - 1, 0), 0)),
            per_batch(WINDOW, KV_WIDTH), per_batch(WINDOW, KV_WIDTH),
        ] + side_out,
        out_shape=[
            jax.ShapeDtypeStruct((batch * seq, D_MODEL), F32),
            jax.ShapeDtypeStruct((batch, WINDOW, KV_WIDTH), F32),
            jax.ShapeDtypeStruct((batch, WINDOW, KV_WIDTH), F32),
        ] + side_shapes,
        scratch_shapes=[
            pltpu.VMEM((WINDOW + tile, KV_WIDTH), BF16),
            pltpu.VMEM((KV_WIDTH, WINDOW + tile), BF16),
            pltpu.VMEM((tile, D_MODEL), F32),
            pltpu.VMEM((2, 2 * WINDOW, N_HEADS * WINDOW), F32),
            pltpu.VMEM((tile, ATTN_WIDTH), F32),
            pltpu.VMEM((tile, D_FF), BF16),
        ],
        compiler_params=pltpu.CompilerParams(
            dimension_semantics=("arbitrary",), vmem_limit_bytes=VMEM_LIMIT_BYTES),
        name="mix_prompt",
    )(w["sinks"], x1, z, *_tail_weights(w, big), *[src for src, _ in side])


def _attn_sample_kernel(layer, sinks_ref, zq_ref, ck_ref, cv_ref, attn_ref):
    dec_seq, g = zq_ref.shape[0], zq_ref.shape[1]
    rows = dec_seq * g
    cache_len = ck_ref.shape[1]
    zq = zq_ref[...].reshape(rows, zq_ref.shape[2])
    q_exp = _expand_q(zq[:, 0:ATTN_WIDTH], rows)
    k_new = zq[:, O_K:O_V].astype(BF16)
    v_new = zq[:, O_V:O_XR].astype(BF16)
    k_cache = ck_ref[...].reshape(g * cache_len, KV_WIDTH).astype(BF16)
    v_cache = cv_ref[...].reshape(g * cache_len, KV_WIDTH).astype(BF16)
    contract_last = (((1,), (1,)), ((), ()))
    s_c = lax.dot_general(q_exp, k_cache, contract_last, preferred_element_type=F32)
    s_n = lax.dot_general(q_exp, k_new, contract_last, preferred_element_type=F32)

    def row_ids(shape):
        r = lax.broadcasted_iota(jnp.int32, shape, 0)
        return (r % rows) // g, r % g

    t_q, b_q = row_ids(s_c.shape)
    c = lax.broadcasted_iota(jnp.int32, s_c.shape, 1)
    mask_c = ((c // cache_len) == b_q) & ((c % cache_len) > t_q)
    t_q, b_q = row_ids(s_n.shape)
    c = lax.broadcasted_iota(jnp.int32, s_n.shape, 1)
    mask_n = ((c % g) == b_q) & ((c // g) <= t_q)
    s_c = jnp.where(mask_c, s_c, -jnp.inf)
    s_n = jnp.where(mask_n, s_n, -jnp.inf)

    sink = _sink_column(sinks_ref, layer, rows)
    m = jnp.maximum(jnp.maximum(jnp.max(s_c, axis=-1, keepdims=True),
                                jnp.max(s_n, axis=-1, keepdims=True)), sink)
    p_c = jnp.exp2(s_c - m)
    p_n = jnp.exp2(s_n - m)
    denom = (jnp.sum(p_c, axis=-1, keepdims=True) + jnp.sum(p_n, axis=-1, keepdims=True)
             + jnp.exp2(sink - m))
    o = (jnp.dot(p_c.astype(BF16), v_cache, preferred_element_type=F32)
         + jnp.dot(p_n.astype(BF16), v_new, preferred_element_type=F32)) / denom
    attn_ref[...] = _collect_heads(o, rows).reshape(dec_seq, g, ATTN_WIDTH)


def _attn_sample(z3, cache_k, cache_v, sinks, layer):
    dec_seq, dec_batch, _ = z3.shape
    cache_len = cache_k.shape[2]
    g = SAMPLE_GROUP
    cache_spec = pl.BlockSpec((None, g, cache_len, KV_WIDTH), lambda i: (layer, i, 0, 0))
    return pl.pallas_call(
        functools.partial(_attn_sample_kernel, layer),
        grid=(dec_batch // g,),
        in_specs=[
            pl.BlockSpec(memory_space=pltpu.SMEM),
            pl.BlockSpec((dec_seq, g, O_XR), lambda i: (0, i, 0)),
            cache_spec, cache_spec,
        ],
        out_specs=pl.BlockSpec((dec_seq, g, ATTN_WIDTH), lambda i: (0, i, 0)),
        out_shape=jax.ShapeDtypeStruct((dec_seq, dec_batch, ATTN_WIDTH), F32),
        compiler_params=pltpu.CompilerParams(
            dimension_semantics=("arbitrary",), vmem_limit_bytes=VMEM_LIMIT_BYTES),
        name="attn_sample",
    )(sinks, z3, cache_k, cache_v)


def _shift_caches_kernel(knew_ref, vnew_ref, ck_ref, cv_ref, nk_ref, nv_ref, sems):
    dec_seq = knew_ref.shape[2]
    keep = ck_ref.shape[2] - dec_seq
    old = lambda ref: ref.at[:, :, pl.ds(dec_seq, keep), :]
    head = lambda ref: ref.at[:, :, pl.ds(0, keep), :]
    tail = lambda ref: ref.at[:, :, pl.ds(keep, dec_seq), :]
    copies = [
        pltpu.make_async_copy(old(ck_ref), head(nk_ref), sems.at[0]),
        pltpu.make_async_copy(old(cv_ref), head(nv_ref), sems.at[1]),
        pltpu.make_async_copy(knew_ref, tail(nk_ref), sems.at[2]),
        pltpu.make_async_copy(vnew_ref, tail(nv_ref), sems.at[3]),
    ]
    for copy in copies:
        copy.start()
    for copy in copies:
        copy.wait()


def _shift_caches(kvnew, cache_k, cache_v):
    any_spec = pl.BlockSpec(memory_space=pl.ANY)
    return pl.pallas_call(
        _shift_caches_kernel,
        in_specs=[any_spec] * 4,
        out_specs=[any_spec] * 2,
        out_shape=[jax.ShapeDtypeStruct(cache_k.shape, F32), jax.ShapeDtypeStruct(cache_v.shape, F32)],
        scratch_shapes=[pltpu.SemaphoreType.DMA((4,))],
        name="shift_caches",
    )(kvnew[..., :KV_WIDTH], kvnew[..., KV_WIDTH:], cache_k, cache_v)


def _mix_sample_kernel(x1_ref, z_ref, attn_ref, sconv_ref, sh_ref, conv_w_ref, conv_b_ref, wgate_ref,
                       ba_ref, bx_ref, lam_ref, g_attn_ref, g_lru_ref, wo_ref, g_postmix_ref,
                       g_pre2_ref, wg_ref, wu_ref, wd_ref, g_post2_ref,
                       x3_ref, nconv_ref, nh_ref, act_scr):
    nb = sh_ref.shape[0]
    dec_seq = x1_ref.shape[0] // nb
    xr = z_ref[:, O_XR:O_YG]
    xp = [sconv_ref[j] for j in range(CONV_W - 1)] + [xr[nb * t:nb * (t + 1)] for t in range(dec_seq)]
    xc_steps = []
    for t in range(dec_seq):
        acc = conv_b_ref[...] + xp[t] * conv_w_ref[0:1, :]
        for j in range(1, CONV_W):
            acc = acc + xp[t + j] * conv_w_ref[j:j + 1, :]
        xc_steps.append(acc)
    for j in range(CONV_W - 1):
        nconv_ref[j] = xp[dec_seq + j]
    xc = jnp.concatenate(xc_steps, axis=0)
    gates = _lru_gates(xc, wgate_ref)
    a, u = _lru_coeffs(xc, gates[:, :LRU_WIDTH], gates[:, LRU_WIDTH:], ba_ref[...], bx_ref[...],
                       lam_ref[...])
    h = sh_ref[...]
    hs_steps = []
    for t in range(dec_seq):
        h = a[nb * t:nb * (t + 1)] * h + u[nb * t:nb * (t + 1)]
        hs_steps.append(h)
    nh_ref[...] = h
    hs = jnp.concatenate(hs_steps, axis=0)
    x2 = _mix_out(x1_ref[...], attn_ref[...], hs, z_ref[:, O_YG:], g_attn_ref[...], g_lru_ref[...],
                  wo_ref, g_postmix_ref[...])
    x3_ref[...] = _half_ffn(x2, g_pre2_ref[...], wg_ref, wu_ref, wd_ref, g_post2_ref[...], act_scr)


def _mix_sample(x1, z, attn, sconv, state_h, w, big, layer):
    n = x1.shape[0]
    nb = state_h.shape[1]
    full = lambda *shape: pl.BlockSpec(shape, lambda i: (0,) * len(shape))
    return pl.pallas_call(
        _mix_sample_kernel,
        grid=(1,),
        in_specs=[
            full(n, D_MODEL), full(n, IN_WIDTH), full(n, ATTN_WIDTH),
            full(CONV_W - 1, nb, LRU_WIDTH), _layer_spec((nb, LRU_WIDTH), layer),
        ] + _lru_weight_specs(layer) + _tail_weight_specs(layer, big),
        out_specs=[full(n, D_MODEL), full(CONV_W - 1, nb, LRU_WIDTH), full(nb, LRU_WIDTH)],
        out_shape=[
            jax.ShapeDtypeStruct((n, D_MODEL), F32),
            jax.ShapeDtypeStruct((CONV_W - 1, nb, LRU_WIDTH), F32),
            jax.ShapeDtypeStruct((nb, LRU_WIDTH), F32),
        ],
        scratch_shapes=[pltpu.VMEM((n, D_FF), BF16)],
        compiler_params=pltpu.CompilerParams(
            dimension_semantics=("arbitrary",), vmem_limit_bytes=VMEM_LIMIT_BYTES),
        name="mix_sample",
    )(x1, z, attn, sconv, state_h, *_lru_weights(w), *_tail_weights(w, big))


def _rope_tables(pos):
    half = ROT_DIM // 2
    inv = ROPE_THETA ** (-(jnp.arange(half, dtype=F32) * 2.0) / ROT_DIM)
    ang = pos.astype(F32)[:, None] * inv[None, :]
    cos, sin = jnp.cos(ang), jnp.sin(ang)
    n = pos.shape[0]
    rest = jnp.zeros((n, HEAD_DIM - ROT_DIM), F32)
    zero = jnp.zeros((n, half), F32)
    per_head = lambda parts: jnp.tile(jnp.concatenate(parts, axis=-1), (1, LANES // HEAD_DIM))
    return (per_head([cos, cos, rest + 1.0]), per_head([-sin, zero, rest]), per_head([zero, sin, rest]))


def _block_diag(w):
    eye = jnp.eye(LRU_HEADS, dtype=w.dtype)
    return jnp.einsum("lhij,hg->lhigj", w, eye).reshape(w.shape[0], LRU_WIDTH, LRU_WIDTH)


def kernel(x_prompt, x_sample, cache_k, cache_v, state_conv, state_h, norm_pre_ffn1, ffn1_w_gate, ffn1_w_up, ffn1_w_down, norm_post_ffn1, norm_pre_mix, w_in, sinks, conv_w, conv_b, gate_a_w, gate_a_b, gate_x_w, gate_x_b, lru_lambda, attn_out_norm, lru_out_norm, w_o, norm_post_mix, norm_pre_ffn2, ffn2_w_gate, ffn2_w_up, ffn2_w_down, norm_post_ffn2):
    batch, seq, _ = x_prompt.shape
    dec_batch, dec_seq, _ = x_sample.shape
    depth = w_in.shape[0]
    cache_len = cache_k.shape[2]
    n_s = dec_seq * dec_batch
    assert seq % FFN_TILE == 0 and n_s % FFN_TILE == 0
    assert seq % MIX_TILE == 0 and MIX_TILE % WINDOW == 0
    assert dec_batch % SAMPLE_GROUP == 0

    tabs_p = _rope_tables(jnp.arange(seq))
    tabs_s = _rope_tables(jnp.repeat(PAST_LEN + jnp.arange(dec_seq), dec_batch))
    xp = x_prompt.reshape(batch * seq, D_MODEL)
    xs = jnp.swapaxes(x_sample, 0, 1).reshape(n_s, D_MODEL)
    rows = lambda v: v.reshape(depth, 1, -1)

    w = dict(
        g_pre1=rows(norm_pre_ffn1), g_post1=rows(norm_post_ffn1), g_mix=rows(norm_pre_mix),
        sinks=sinks, conv_w=conv_w, conv_b=rows(conv_b),
        w_gate=jnp.concatenate([_block_diag(gate_a_w), _block_diag(gate_x_w)], axis=2).astype(BF16),
        gate_a_b=rows(gate_a_b), gate_x_b=rows(gate_x_b), lam=rows(lru_lambda),
        g_attn=rows(attn_out_norm), g_lru=rows(lru_out_norm),
        g_postmix=rows(norm_post_mix), g_pre2=rows(norm_pre_ffn2), g_post2=rows(norm_post_ffn2),
    )
    ffn_in_src = (ffn1_w_gate, ffn1_w_up, ffn1_w_down, w_in)
    mix_src = (w_o, ffn2_w_gate, ffn2_w_up, ffn2_w_down)
    ck = cache_k.reshape(depth, dec_batch, cache_len, KV_WIDTH)
    cv = cache_v.reshape(depth, dec_batch, cache_len, KV_WIDTH)

    outs = [[] for _ in range(6)]
    kvnew = []
    ffn_in_big = [_to_bf16(s, 0) for s in ffn_in_src]
    for l in range(depth):
        x1p, zp, nc, nh, *mix_big = _ffn_in_prompt(xp, tabs_p, w, ffn_in_big, l, batch, seq, FFN_TILE,
                                                   side=[(s, l) for s in mix_src])
        xp, nk, nv, *next_ffn_in_big = _mix_prompt(
            x1p, zp, w, mix_big, l, batch, seq, MIX_TILE,
            side=[(s, l + 1) for s in ffn_in_src if l + 1 < depth])

        x1s, zs = _ffn_in_sample(xs, tabs_s, w, ffn_in_big, l)
        z3 = zs.reshape(dec_seq, dec_batch, IN_WIDTH)
        kvnew.append(jnp.swapaxes(z3[:, :, O_K:O_XR], 0, 1))
        attn_s = _attn_sample(z3, ck, cv, sinks, l)
        xs, ncs, nhs = _mix_sample(x1s, zs, attn_s.reshape(n_s, ATTN_WIDTH),
                                   jnp.swapaxes(state_conv[l], 0, 1), state_h, w, mix_big, l)
        ffn_in_big = next_ffn_in_big
        kv_p = (batch, WINDOW, N_KV_HEADS, HEAD_DIM)
        for acc, val in zip(outs, (nk.reshape(kv_p), nv.reshape(kv_p), nc,
                                   nh.reshape(batch, LRU_WIDTH), jnp.swapaxes(ncs, 0, 1), nhs)):
            acc.append(val)

    nks, nvs = _shift_caches(jnp.stack(kvnew), ck, cv)
    y_prompt = xp.reshape(batch, seq, D_MODEL)
    y_sample = jnp.swapaxes(xs.reshape(dec_seq, dec_batch, D_MODEL), 0, 1)
    kp, vp, cp, hp, cs, hs = (jnp.stack(o) for o in outs)
    return (y_prompt, y_sample, kp, vp, cp, hp, nks.reshape(cache_k.shape), nvs.reshape(cache_v.shape),
            cs, hs)
```
